```python
import jax, jax.numpy as jnp
from jax import lax
import numpy as np

D_MODEL = 1024
BATCH = 16
SEQ = 4096
DEPTH = 1

EPS = 1e-6
NSA_HEADS = 8
NSA_KV_GROUPS = 2
NSA_REP = NSA_HEADS // NSA_KV_GROUPS
HEAD_DIM = 64
NSA_WIDTH = NSA_HEADS * HEAD_DIM
KV_WIDTH = NSA_KV_GROUPS * HEAD_DIM
CMP_STRIDE = 16
CMP_BLOCK = 2 * CMP_STRIDE
CMP_HIDDEN = 256
SEL_BLOCK = 64
SEL_TOP_N = 16
SEL_Q_BLOCK = 64
WINDOW = 512
WIN_Q_BLOCK = 128
ROPE_DIM = HEAD_DIM // 4
ROPE_THETA = 500000.0
SSD_HEADS = 8
SSD_HEAD_DIM = 64
SSD_WIDTH = SSD_HEADS * SSD_HEAD_DIM
SSD_GROUPS = 2
SSD_REP = SSD_HEADS // SSD_GROUPS
SSD_STATE = 128
SSD_CONV = 4
SSD_CHUNK = 128
SSD_CONV_DIM = SSD_WIDTH + 2 * SSD_GROUPS * SSD_STATE
MIX_WIDTH = NSA_WIDTH + SSD_WIDTH
IN_PROJ = NSA_WIDTH + 6 * KV_WIDTH + 3 * NSA_HEADS + SSD_WIDTH + SSD_CONV_DIM + SSD_HEADS
PEER_HEADS = 8
PEER_N_KEYS = 128
PEER_N_EXPERTS = PEER_N_KEYS * PEER_N_KEYS
PEER_KEY_DIM = 256
PEER_HALF = PEER_KEY_DIM // 2
PEER_TOPK = 16
PEER_TOK_BLOCK = 128

kernel_name = "hymba_nsa_ssd_peer_layer"


def rmsnorm(x, g):
    xf = x.astype(jnp.float32)
    y = xf * lax.rsqrt(jnp.mean(xf * xf, axis=-1, keepdims=True) + EPS)
    return (y * g.astype(jnp.float32)).astype(x.dtype)


def masked_softmax(s, mask):
    s = jnp.where(mask, s.astype(jnp.float32), -1e30)
    return jax.nn.softmax(s, axis=-1) * mask


def partial_rope(x, pos):
    half = ROPE_DIM // 2
    inv = jnp.power(ROPE_THETA, -jnp.arange(half, dtype=jnp.float32) * 2.0 / ROPE_DIM)
    ang = pos.astype(jnp.float32)[:, None] * inv[None, :]
    cos, sin = jnp.cos(ang).astype(x.dtype), jnp.sin(ang).astype(x.dtype)
    x1, x2, xp = x[..., :half], x[..., half:ROPE_DIM], x[..., ROPE_DIM:]
    return jnp.concatenate([x1 * cos - x2 * sin, x2 * cos + x1 * sin, xp], axis=-1)


def in_proj_splits():
    widths = [NSA_WIDTH] + [KV_WIDTH] * 6 + [3 * NSA_HEADS, SSD_WIDTH, SSD_CONV_DIM, SSD_HEADS]
    return [int(v) for v in np.cumsum(widths)[:-1]]


def compress(kv, pos_emb, w1, b1, w2):
    b, g, s, d = kv.shape
    chunks = kv.reshape(b, g, s // CMP_STRIDE, CMP_STRIDE, d)
    blocks = jnp.concatenate([chunks[:, :, :-1], chunks[:, :, 1:]], axis=3)
    blocks = (blocks + pos_emb).reshape(b, g, -1, CMP_BLOCK * d)
    return jax.nn.gelu(blocks @ w1 + b1) @ w2


def nsa_mixer(q, k_c, v_c, k_s, v_s, k_w, v_w, gates, cmp_k, cmp_v):
    b, g, r, s, d = q.shape
    scale = d ** -0.5
    t = jnp.arange(s)
    kc = compress(k_c, *cmp_k)
    vc = compress(v_c, *cmp_v)
    n_cmp = s // CMP_STRIDE - 1
    cmp_end = jnp.arange(n_cmp) * CMP_STRIDE + CMP_BLOCK - 1
    mask_c = cmp_end[None, :] <= t[:, None]
    p_cmp = masked_softmax(jnp.einsum('bgrtd,bgjd->bgrtj', q, kc) * scale, mask_c)
    o_cmp = jnp.einsum('bgrtj,bgjd->bgrtd', p_cmp.astype(vc.dtype), vc)
    n_sel = s // SEL_BLOCK
    n_top = min(SEL_TOP_N, n_sel)
    cs = np.arange(n_cmp) * CMP_STRIDE
    ss = np.arange(n_sel) * SEL_BLOCK
    overlap = (cs[:, None] < ss[None, :] + SEL_BLOCK) & (cs[:, None] + CMP_BLOCK > ss[None, :])
    overlap = jnp.asarray(overlap, jnp.float32)
    imp = jnp.einsum('bgrtj,ji->bgti', p_cmp, overlap)
    blk = jnp.arange(n_sel)
    cur = t // SEL_BLOCK
    forced = (blk[None, :] == 0) | (blk[None, :] == cur[:, None]) | (blk[None, :] == cur[:, None] - 1)
    valid = blk[None, :] <= cur[:, None]
    imp = jnp.where(forced, 1e9, jnp.where(valid, imp, -1.0))
    _, sel_idx = lax.top_k(imp, n_top)
    k_blk = k_s.reshape(b, g, n_sel, SEL_BLOCK, d)
    v_blk = v_s.reshape(b, g, n_sel, SEL_BLOCK, d)
    nqb = s // SEL_Q_BLOCK
    q_b = jnp.moveaxis(q.reshape(b, g, r, nqb, SEL_Q_BLOCK, d), 3, 0)
    idx_b = jnp.moveaxis(sel_idx.reshape(b, g, nqb, SEL_Q_BLOCK, n_top), 2, 0)
    t_b = t.reshape(nqb, SEL_Q_BLOCK)
    bi = jnp.arange(b)[:, None, None, None]
    gi = jnp.arange(g)[None, :, None, None]

    def sel_block(args):
        qb, ib, tb = args
        kg = k_blk[bi, gi, ib].reshape(b, g, SEL_Q_BLOCK, n_top * SEL_BLOCK, d)
        vg = v_blk[bi, gi, ib].reshape(b, g, SEL_Q_BLOCK, n_top * SEL_BLOCK, d)
        kpos = (ib[..., None] * SEL_BLOCK + jnp.arange(SEL_BLOCK)).reshape(b, g, SEL_Q_BLOCK, -1)
        mask = (kpos <= tb[:, None])[:, :, None]
        p = masked_softmax(jnp.einsum('bgrqd,bgqkd->bgrqk', qb, kg) * scale, mask)
        return jnp.einsum('bgrqk,bgqkd->bgrqd', p.astype(vg.dtype), vg)

    o_sel = lax.map(sel_block, (q_b, idx_b, t_b))
    o_sel = jnp.moveaxis(o_sel, 0, 3).reshape(b, g, r, s, d)
    nwb = s // WIN_Q_BLOCK
    n_ctx = WINDOW // WIN_Q_BLOCK + 1

    def band(kv):
        kp = jnp.pad(kv, ((0, 0), (0, 0), (WINDOW, 0), (0, 0))).reshape(b, g, nwb + n_ctx - 1, WIN_Q_BLOCK, d)
        return jnp.concatenate([kp[:, :, j:j + nwb] for j in range(n_ctx)], axis=3)

    kwb, vwb = band(k_w), band(v_w)
    qw = q.reshape(b, g, r, nwb, WIN_Q_BLOCK, d)
    tq = t.reshape(nwb, WIN_Q_BLOCK)
    tk = (jnp.arange(nwb)[:, None] - (n_ctx - 1)) * WIN_Q_BLOCK + jnp.arange(n_ctx * WIN_Q_BLOCK)[None, :]
    diff = tq[:, :, None] - tk[:, None, :]
    mask_w = (tk[:, None, :] >= 0) & (diff >= 0) & (diff < WINDOW)
    p_w = masked_softmax(jnp.einsum('bgrnqd,bgnkd->bgrnqk', qw, kwb) * scale, mask_w)
    o_win = jnp.einsum('bgrnqk,bgnkd->bgrnqd', p_w.astype(vwb.dtype), vwb).reshape(b, g, r, s, d)
    gr = gates.reshape(b, s, g, r, 3).transpose(0, 2, 3, 1, 4)
    o = gr[..., 0:1] * o_cmp + gr[..., 1:2] * o_sel + gr[..., 2:3] * o_win
    return o.transpose(0, 3, 1, 2, 4).reshape(b, s, g * r * d)


def ssd_mixer(z, xbc, dt, conv_w, conv_b, dt_bias, a_log, d_skip, norm_g):
    b, s, _ = xbc.shape
    xbc = lax.conv_general_dilated(xbc, conv_w[:, None, :], window_strides=(1,),
                                   padding=[(SSD_CONV - 1, 0)],
                                   dimension_numbers=('NWC', 'WIO', 'NWC'),
                                   feature_group_count=SSD_CONV_DIM) + conv_b
    xbc = jax.nn.silu(xbc)
    xs, bm, cm = jnp.split(xbc, [SSD_WIDTH, SSD_WIDTH + SSD_GROUPS * SSD_STATE], axis=-1)
    nc, l = s // SSD_CHUNK, SSD_CHUNK
    xs = xs.reshape(b, nc, l, SSD_GROUPS, SSD_REP, SSD_HEAD_DIM)
    bm = bm.reshape(b, nc, l, SSD_GROUPS, SSD_STATE).astype(jnp.float32)
    cm = cm.reshape(b, nc, l, SSD_GROUPS, SSD_STATE).astype(jnp.float32)
    dt = jax.nn.softplus(dt.astype(jnp.float32) + dt_bias.astype(jnp.float32))
    a = -jnp.exp(a_log.astype(jnp.float32))
    dt_c = dt.reshape(b, nc, l, SSD_GROUPS, SSD_REP)
    adt = (dt_c * a.reshape(SSD_GROUPS, SSD_REP)).transpose(0, 3, 4, 1, 2)
    xdt = xs.astype(jnp.float32) * dt_c[..., None]
    acum = jnp.cumsum(adt, axis=-1)
    causal = jnp.tril(jnp.ones((l, l), bool))
    seg = jnp.exp(jnp.where(causal, acum[..., :, None] - acum[..., None, :], -jnp.inf))
    cb = jnp.einsum('bclgn,bcsgn->bgcls', cm, bm)
    y_diag = jnp.einsum('bgcls,bgrcls,bcsgrp->bclgrp', cb, seg, xdt)
    decay_states = jnp.exp(acum[..., -1:] - acum)
    states = jnp.einsum('bclgn,bgrcl,bclgrp->cbgrpn', bm, decay_states, xdt)
    chunk_decay = jnp.moveaxis(jnp.exp(acum[..., -1]), -1, 0)

    def step(h, inp):
        st, dec = inp
        return h * dec[..., None, None] + st, h

    _, states_in = lax.scan(step, jnp.zeros(states.shape[1:], states.dtype), (states, chunk_decay))
    y_off = jnp.einsum('bclgn,cbgrpn,bgrcl->bclgrp', cm, states_in, jnp.exp(acum))
    y = y_diag + y_off + xs.astype(jnp.float32) * d_skip.astype(jnp.float32).reshape(SSD_GROUPS, SSD_REP, 1)
    y = y.reshape(b, s, SSD_WIDTH).astype(z.dtype)
    yg = (y * jax.nn.silu(z)).reshape(b, s, SSD_GROUPS, SSD_WIDTH // SSD_GROUPS)
    return rmsnorm(yg, norm_g.reshape(SSD_GROUPS, -1)).reshape(b, s, SSD_WIDTH)


def peer(x, w_q, sub_keys, w_u, w_v):
    b, s, dm = x.shape
    xt = x.reshape(b * s, dm)
    q = (xt @ w_q).reshape(-1, PEER_HEADS, 2, PEER_HALF)
    s1 = jnp.einsum('thd,hkd->thk', q[:, :, 0], sub_keys[:, 0])
    s2 = jnp.einsum('thd,hkd->thk', q[:, :, 1], sub_keys[:, 1])
    v1, i1 = lax.top_k(s1, PEER_TOPK)
    v2, i2 = lax.top_k(s2, PEER_TOPK)
    cand = (v1[..., :, None] + v2[..., None, :]).reshape(-1, PEER_HEADS, PEER_TOPK * PEER_TOPK)
    vals, ci = lax.top_k(cand, PEER_TOPK)
    e1 = jnp.take_along_axis(i1, ci // PEER_TOPK, axis=-1)
    e2 = jnp.take_along_axis(i2, ci % PEER_TOPK, axis=-1)
    idx = e1 * PEER_N_KEYS + e2
    gate = jax.nn.softmax(vals.astype(jnp.float32), axis=-1).astype(x.dtype)
    nb = xt.shape[0] // PEER_TOK_BLOCK

    def block(args):
        xb, ib, gb = args
        h = jax.nn.gelu(jnp.einsum('td,thkd->thk', xb, w_u[ib]))
        return jnp.einsum('thk,thkd->td', gb * h, w_v[ib])

    out = lax.map(block, (xt.reshape(nb, PEER_TOK_BLOCK, dm),
                          idx.reshape(nb, PEER_TOK_BLOCK, PEER_HEADS, PEER_TOPK),
                          gate.reshape(nb, PEER_TOK_BLOCK, PEER_HEADS, PEER_TOPK)))
    return out.reshape(b, s, dm)


def setup_inputs(seed: int = 0) -> dict:
    key = jax.random.key(seed)
    ks = jax.random.split(key, 32)
    n = jax.random.normal
    L = DEPTH
    dt0 = jnp.exp(jax.random.uniform(ks[14], (L, SSD_HEADS)) * (jnp.log(0.1) - jnp.log(0.001)) + jnp.log(0.001))
    return {
        "x": n(ks[0], (BATCH, SEQ, D_MODEL), jnp.float32),
        "attn_norm_g": 1.0 + 0.02 * n(ks[1], (L, D_MODEL)),
        "w_in": n(ks[2], (L, D_MODEL, IN_PROJ)) * D_MODEL ** -0.5,
        "cmp_pos_k": 0.02 * n(ks[3], (L, CMP_BLOCK, HEAD_DIM)),
        "cmp_w1_k": n(ks[4], (L, CMP_BLOCK * HEAD_DIM, CMP_HIDDEN)) * (CMP_BLOCK * HEAD_DIM) ** -0.5,
        "cmp_b1_k": 0.01 * n(ks[5], (L, CMP_HIDDEN)),
        "cmp_w2_k": n(ks[6], (L, CMP_HIDDEN, HEAD_DIM)) * CMP_HIDDEN ** -0.5,
        "cmp_pos_v": 0.02 * n(ks[7], (L, CMP_BLOCK, HEAD_DIM)),
        "cmp_w1_v": n(ks[8], (L, CMP_BLOCK * HEAD_DIM, CMP_HIDDEN)) * (CMP_BLOCK * HEAD_DIM) ** -0.5,
        "cmp_b1_v": 0.01 * n(ks[9], (L, CMP_HIDDEN)),
        "cmp_w2_v": n(ks[10], (L, CMP_HIDDEN, HEAD_DIM)) * CMP_HIDDEN ** -0.5,
        "conv_w": n(ks[11], (L, SSD_CONV, SSD_CONV_DIM)) * SSD_CONV ** -0.5,
        "conv_b": 0.01 * n(ks[12], (L, SSD_CONV_DIM)),
        "dt_bias": dt0 + jnp.log(-jnp.expm1(-dt0)),
        "a_log": jnp.log(jax.random.uniform(ks[13], (L, SSD_HEADS), minval=1.0, maxval=16.0)),
        "d_skip": 1.0 + 0.02 * n(ks[15], (L, SSD_HEADS)),
        "ssd_norm_g": 1.0 + 0.02 * n(ks[16], (L, SSD_WIDTH)),
        "nsa_norm_g": 1.0 + 0.02 * n(ks[17], (L, NSA_WIDTH)),
        "w_out": n(ks[18], (L, MIX_WIDTH, D_MODEL)) * MIX_WIDTH ** -0.5,
        "ffn_norm_g": 1.0 + 0.02 * n(ks[19], (L, D_MODEL)),
        "peer_w_q": n(ks[20], (L, D_MODEL, PEER_HEADS * PEER_KEY_DIM)) * D_MODEL ** -0.5,
        "peer_keys": n(ks[21], (L, PEER_HEADS, 2, PEER_N_KEYS, PEER_HALF)) * PEER_HALF ** -0.5,
        "peer_u": n(ks[22], (L, PEER_N_EXPERTS, D_MODEL)) * D_MODEL ** -0.5,
        "peer_v": n(ks[23], (L, PEER_N_EXPERTS, D_MODEL)) * PEER_HEADS ** -0.5,
        "final_norm_g": 1.0 + 0.02 * n(ks[24], (D_MODEL,)),
    }


def reference(x, attn_norm_g, w_in, cmp_pos_k, cmp_w1_k, cmp_b1_k, cmp_w2_k,
              cmp_pos_v, cmp_w1_v, cmp_b1_v, cmp_w2_v, conv_w, conv_b, dt_bias, a_log,
              d_skip, ssd_norm_g, nsa_norm_g, w_out, ffn_norm_g, peer_w_q, peer_keys,
              peer_u, peer_v, final_norm_g):
    b, s, _ = x.shape
    pos = jnp.arange(s)
    splits = in_proj_splits()
    for layer in range(DEPTH):
        h = rmsnorm(x, attn_norm_g[layer])
        proj = h @ w_in[layer]
        q, kc, vc, ksl, vsl, kw, vw, gl, z, xbc, dt = jnp.split(proj, splits, axis=-1)
        qh = partial_rope(q.reshape(b, s, NSA_KV_GROUPS, NSA_REP, HEAD_DIM).transpose(0, 2, 3, 1, 4), pos)

        def kv_heads(t_, rope):
            t_ = t_.reshape(b, s, NSA_KV_GROUPS, HEAD_DIM).transpose(0, 2, 1, 3)
            return partial_rope(t_, pos) if rope else t_

        gates = jax.nn.sigmoid(gl.astype(jnp.float32)).astype(x.dtype).reshape(b, s, NSA_HEADS, 3)
        o_nsa = nsa_mixer(qh, kv_heads(kc, True), kv_heads(vc, False), kv_heads(ksl, True),
                          kv_heads(vsl, False), kv_heads(kw, True), kv_heads(vw, False), gates,
                          (cmp_pos_k[layer], cmp_w1_k[layer], cmp_b1_k[layer], cmp_w2_k[layer]),
                          (cmp_pos_v[layer], cmp_w1_v[layer], cmp_b1_v[layer], cmp_w2_v[layer]))
        o_nsa = rmsnorm(o_nsa, nsa_norm_g[layer])
        o_ssd = ssd_mixer(z, xbc, dt, conv_w[layer], conv_b[layer], dt_bias[layer], a_log[layer],
                          d_skip[layer], ssd_norm_g[layer])
        x = x + jnp.concatenate([o_nsa, o_ssd.astype(o_nsa.dtype)], axis=-1) @ w_out[layer]
        x = x + peer(rmsnorm(x, ffn_norm_g[layer]), peer_w_q[layer], peer_keys[layer],
                     peer_u[layer], peer_v[layer])
    return rmsnorm(x, final_norm_g)
```

```python
import functools

import numpy as np
import jax
import jax.numpy as jnp
from jax import lax
from jax.experimental import pallas as pl
from jax.experimental.pallas import tpu as pltpu

F32 = jnp.float32
BF16 = jnp.bfloat16

EPS = 1e-6
D_MODEL = 1024
NSA_HEADS = 8
NSA_KV_GROUPS = 2
NSA_REP = NSA_HEADS // NSA_KV_GROUPS
HEAD_DIM = 64
NSA_WIDTH = NSA_HEADS * HEAD_DIM
KV_WIDTH = NSA_KV_GROUPS * HEAD_DIM
CMP_STRIDE = 16
CMP_BLOCK = 32
CMP_HIDDEN = 256
SEL_BLOCK = 64
SEL_TOP_N = 16
WINDOW = 512
ROPE_DIM = HEAD_DIM // 4
ROPE_THETA = 500000.0
SSD_HEADS = 8
SSD_HEAD_DIM = 64
SSD_WIDTH = SSD_HEADS * SSD_HEAD_DIM
SSD_GROUPS = 2
SSD_REP = SSD_HEADS // SSD_GROUPS
SSD_STATE = 128
SSD_CONV = 4
SSD_CHUNK = 128
SSD_CONV_DIM = SSD_WIDTH + 2 * SSD_GROUPS * SSD_STATE
PEER_HEADS = 8
PEER_N_KEYS = 128
PEER_N_EXPERTS = PEER_N_KEYS * PEER_N_KEYS
PEER_KEY_DIM = 256
PEER_HALF = PEER_KEY_DIM // 2
PEER_TOPK = 16

LANES = 128
SUBLANES = 8
VMEM_LIMIT = 48 * 1024 * 1024
NEG_BIG = -1e30
SEL_NEG = -1e9

TOK_TILE = 512
Q_TILE = SEL_BLOCK
K_TILE = 2 * SEL_BLOCK
PEER_TOK = 512
PEER_EXP = 1024
PEER_FRONT_TOK = 256


def _cparams(sem):
    return pltpu.CompilerParams(dimension_semantics=sem, vmem_limit_bytes=VMEM_LIMIT)


def _dot(a, b):
    return jnp.dot(a, b, preferred_element_type=F32)


def _dot_nt(a, b):
    return lax.dot_general(a, b, (((1,), (1,)), ((), ())), preferred_element_type=F32)


def _split3(a):
    a1 = a.astype(BF16)
    r1 = a - a1.astype(F32)
    a2 = r1.astype(BF16)
    a3 = (r1 - a2.astype(F32)).astype(BF16)
    return a1, a2, a3


def _dot_exact_lhs(a, b01):
    a1, a2, a3 = _split3(a)
    return _dot(a1, b01) + _dot(a2, b01) + _dot(a3, b01)


def _dot_nt_exact_lhs(a, b01):
    a1, a2, a3 = _split3(a)
    return _dot_nt(a1, b01) + _dot_nt(a2, b01) + _dot_nt(a3, b01)


def _dot_exact_rhs(a01, b):
    b1, b2, b3 = _split3(b)
    return _dot(a01, b1) + _dot(a01, b2) + _dot(a01, b3)


def _dot_nt_exact_rhs(a01, b):
    b1, b2, b3 = _split3(b)
    return _dot_nt(a01, b1) + _dot_nt(a01, b2) + _dot_nt(a01, b3)


def _softplus(x):
    return jnp.maximum(x, 0.0) + jnp.log(1.0 + jnp.exp(-jnp.abs(x)))


def _sigmoid(x):
    return 1.0 / (1.0 + jnp.exp(-x))


def _gelu_tanh(x):
    return 0.5 * x * (1.0 + jnp.tanh(0.7978845608028654 * (x + 0.044715 * (x * x * x))))


def _rope128(p, cos, sa, sb):
    return p * cos + pltpu.roll(p, LANES - ROPE_DIM // 2, 1) * sa + pltpu.roll(p, ROPE_DIM // 2, 1) * sb


def _in_proj_kernel(seq_tiles, x_ref, g_ref, w_ref, wdt_ref, cos_ref, sa_ref, sb_ref, gdb_ref, dtb_ref,
                    q_ref, kc_ref, vc_ref, ks_ref, vs_ref, kw_ref, vw_ref, gd_ref, dtt_ref, z_ref, xbc_ref):
    tm = x_ref.shape[0]
    x = x_ref[...]
    h = x * lax.rsqrt(jnp.mean(x * x, axis=-1, keepdims=True) + EPS) * g_ref[...]
    hb = h.astype(BF16)
    pos0 = pl.multiple_of((pl.program_id(0) % seq_tiles) * tm, tm)
    cos = cos_ref[pl.ds(pos0, tm), :]
    sa = sa_ref[pl.ds(pos0, tm), :]
    sb = sb_ref[pl.ds(pos0, tm), :]
    lane = lax.broadcasted_iota(jnp.int32, (tm, LANES), 1)
    low = lane < HEAD_DIM

    def proj(c0, width):
        return _dot(hb, w_ref[:, c0:c0 + width])

    pq = proj(0, NSA_WIDTH)
    for s in range(NSA_WIDTH // LANES):
        slab = _rope128(pq[:, s * LANES:(s + 1) * LANES], cos, sa, sb) * (HEAD_DIM ** -0.5)
        swapped = pltpu.roll(slab, HEAD_DIM, 1)
        for half in range(2):
            head = 2 * s + half
            grp = head // NSA_REP
            src = slab if half == grp else swapped
            keep = low if grp == 0 else jnp.logical_not(low)
            q_ref[:, head * LANES:(head + 1) * LANES] = jnp.where(keep, src, 0.0).astype(BF16)
    c0 = NSA_WIDTH
    for ref, rope in ((kc_ref, True), (vc_ref, False), (ks_ref, True), (vs_ref, False),
                      (kw_ref, True), (vw_ref, False)):
        p = proj(c0, LANES)
        if rope:
            p = _rope128(p, cos, sa, sb)
        ref[...] = p.astype(BF16)
        c0 += LANES
    for s in range(2):
        p = proj(c0, LANES) + gdb_ref[:, s * LANES:(s + 1) * LANES]
        act = jnp.where(lane < 3 * NSA_REP, _sigmoid(p),
                        jnp.where((lane >= 16) & (lane < 16 + SSD_REP), _softplus(p), 0.0))
        gd_ref[:, s * LANES:(s + 1) * LANES] = act
        c0 += LANES
    z_ref[...] = proj(c0, SSD_WIDTH)
    c0 += SSD_WIDTH
    xbc_ref[...] = proj(c0, SSD_CONV_DIM)
    dtt_ref[...] = _softplus(_dot_nt(wdt_ref[...], hb) + dtb_ref[:, 0:1])


def _rope_tables(s):
    half = ROPE_DIM // 2
    inv = jnp.power(ROPE_THETA, -jnp.arange(half, dtype=F32) * 2.0 / ROPE_DIM)
    ang = jnp.arange(s).astype(F32)[:, None] * inv[None, :]
    cos, sin = jnp.cos(ang), jnp.sin(ang)
    zeros = jnp.zeros((s, HEAD_DIM - ROPE_DIM), F32)
    cos64 = jnp.concatenate([cos, cos, jnp.ones((s, HEAD_DIM - ROPE_DIM), F32)], axis=1)
    sa64 = jnp.concatenate([-sin, jnp.zeros_like(sin), zeros], axis=1)
    sb64 = jnp.concatenate([jnp.zeros_like(sin), sin, zeros], axis=1)
    return tuple(jnp.concatenate([t, t], axis=1) for t in (cos64, sa64, sb64))


def _in_proj(xt, attn_g, w_in, dt_bias, s):
    t = xt.shape[0]
    tm = min(TOK_TILE, s)
    o_gl = NSA_WIDTH + 6 * KV_WIDTH
    o_z = o_gl + 3 * NSA_HEADS
    o_xbc = o_z + SSD_WIDTH
    o_dt = o_xbc + SSD_CONV_DIM
    gd_cols, gd_bias = [], []
    for g in range(NSA_KV_GROUPS):
        gates = w_in[:, o_gl + 3 * NSA_REP * g:o_gl + 3 * NSA_REP * (g + 1)]
        dts = w_in[:, o_dt + SSD_REP * g:o_dt + SSD_REP * (g + 1)]
        gd_cols += [gates, jnp.zeros((D_MODEL, 16 - 3 * NSA_REP), F32), dts,
                    jnp.zeros((D_MODEL, LANES - 16 - SSD_REP), F32)]
        gd_bias += [jnp.zeros((16,), F32), dt_bias[SSD_REP * g:SSD_REP * (g + 1)],
                    jnp.zeros((LANES - 16 - SSD_REP,), F32)]
    w_main = jnp.concatenate([w_in[:, :o_gl]] + gd_cols + [w_in[:, o_z:o_dt]], axis=1).astype(BF16)
    gdb = jnp.concatenate(gd_bias)[None, :]
    wdt_rows, dtb_rows = [], []
    for g in range(SSD_GROUPS):
        wdt_rows += [w_in[:, o_dt + SSD_REP * g:o_dt + SSD_REP * (g + 1)].T,
                     jnp.zeros((SUBLANES - SSD_REP, D_MODEL), F32)]
        dtb_rows += [dt_bias[SSD_REP * g:SSD_REP * (g + 1)], jnp.zeros((SUBLANES - SSD_REP,), F32)]
    wdt = jnp.concatenate(wdt_rows, axis=0).astype(BF16)
    dtb = jnp.broadcast_to(jnp.concatenate(dtb_rows)[:, None], (2 * SUBLANES, LANES))
    cos, sa, sb = _rope_tables(s)
    n_main = w_main.shape[1]

    def full(shape):
        return pl.BlockSpec(shape, lambda i: (0, 0))

    def tok(width):
        return pl.BlockSpec((tm, width), lambda i: (i, 0))

    out_shapes = ([jax.ShapeDtypeStruct((t, NSA_HEADS * LANES), BF16)]
                  + [jax.ShapeDtypeStruct((t, LANES), BF16)] * 6
                  + [jax.ShapeDtypeStruct((t, 2 * LANES), F32),
                     jax.ShapeDtypeStruct((2 * SUBLANES, t), F32),
                     jax.ShapeDtypeStruct((t, SSD_WIDTH), F32),
                     jax.ShapeDtypeStruct((t, SSD_CONV_DIM), F32)])
    out_specs = ([tok(NSA_HEADS * LANES)] + [tok(LANES)] * 6
                 + [tok(2 * LANES), pl.BlockSpec((2 * SUBLANES, tm), lambda i: (0, i)),
                    tok(SSD_WIDTH), tok(SSD_CONV_DIM)])
    return pl.pallas_call(
        functools.partial(_in_proj_kernel, s // tm),
        grid=(t // tm,),
        in_specs=[tok(D_MODEL), full((1, D_MODEL)), full((D_MODEL, n_main)), full((2 * SUBLANES, D_MODEL)),
                  full((s, LANES)), full((s, LANES)), full((s, LANES)), full((1, 2 * LANES)),
                  full((2 * SUBLANES, LANES))],
        out_specs=out_specs,
        out_shape=out_shapes,
        compiler_params=_cparams(("parallel",)),
    )(xt, attn_g[None, :], w_main, wdt, cos, sa, sb, gdb, dtb)


def _compress_kernel(kv_ref, w1a_ref, w1b_ref, bias_ref, w2_ref, out_ref):
    kv = kv_ref[0]
    acc = jnp.zeros(out_ref.shape[1:], F32)
    for g in range(NSA_KV_GROUPS):
        first = _dot(kv, w1a_ref[g])
        second = _dot(kv, w1b_ref[g])
        nxt = pltpu.roll(second, second.shape[0] - 1, 0)
        hid = _gelu_tanh(first + nxt + bias_ref[...])
        acc = acc + _dot(hid.astype(BF16), w2_ref[g])
    out_ref[0] = acc.astype(BF16)


def _compress(kv, pos_emb, w1, b1, w2, b, s):
    nch = s // CMP_STRIDE
    kvf = kv.reshape(b, nch, CMP_STRIDE * LANES)
    w1r = w1.reshape(CMP_BLOCK, HEAD_DIM, CMP_HIDDEN)
    zeros = jnp.zeros((CMP_STRIDE, HEAD_DIM, CMP_HIDDEN), F32)

    def expand(w_half, g):
        parts = [w_half, zeros] if g == 0 else [zeros, w_half]
        return jnp.concatenate(parts, axis=1).reshape(CMP_STRIDE * LANES, CMP_HIDDEN)

    w1a = jnp.stack([expand(w1r[:CMP_STRIDE], g) for g in range(NSA_KV_GROUPS)]).astype(BF16)
    w1b = jnp.stack([expand(w1r[CMP_STRIDE:], g) for g in range(NSA_KV_GROUPS)]).astype(BF16)
    z2 = jnp.zeros((CMP_HIDDEN, HEAD_DIM), F32)
    w2e = jnp.stack([jnp.concatenate([w2, z2], axis=1), jnp.concatenate([z2, w2], axis=1)]).astype(BF16)
    bias = (jnp.dot(pos_emb.reshape(1, -1).astype(BF16), w1.astype(BF16), preferred_element_type=F32) + b1[None, :])
    return pl.pallas_call(
        _compress_kernel,
        grid=(b,),
        in_specs=[pl.BlockSpec((1, nch, CMP_STRIDE * LANES), lambda i: (i, 0, 0)),
                  pl.BlockSpec((2, CMP_STRIDE * LANES, CMP_HIDDEN), lambda i: (0, 0, 0)),
                  pl.BlockSpec((2, CMP_STRIDE * LANES, CMP_HIDDEN), lambda i: (0, 0, 0)),
                  pl.BlockSpec((1, CMP_HIDDEN), lambda i: (0, 0)),
                  pl.BlockSpec((2, CMP_HIDDEN, LANES), lambda i: (0, 0, 0))],
        out_specs=pl.BlockSpec((1, nch, LANES), lambda i: (i, 0, 0)),
        out_shape=jax.ShapeDtypeStruct((b, nch, LANES), BF16),
        compiler_params=_cparams(("parallel",)),
    )(kvf, w1a, w1b, bias, w2e)


def _online_update(carry, s, v):
    m, l, acc = carry
    m_new = jnp.maximum(m, jnp.max(s, axis=-1, keepdims=True))
    alpha = jnp.exp(m - m_new)
    p = jnp.exp(s - m_new)
    l = alpha * l + jnp.sum(p, axis=-1, keepdims=True)
    acc = alpha * acc + _dot(p.astype(BF16), v)
    return m_new, l, acc


def _nsa_kernel(q_ref, kc_ref, vc_ref, ks_ref, vs_ref, kw_ref, vw_ref, gd_ref, ovl_ref, o_ref):
    g = pl.program_id(1)
    i = pl.program_id(2)
    tq = Q_TILE
    rows = NSA_REP * tq
    ncmp = kc_ref.shape[1]
    nblk = ovl_ref.shape[0]
    q = jnp.concatenate([q_ref[:, r * LANES:(r + 1) * LANES] for r in range(NSA_REP)], axis=0)
    t_row = i * tq + lax.broadcasted_iota(jnp.int32, (rows, 1), 0) % tq

    s_c = _dot_nt(q, kc_ref[0])
    cmp_end = lax.broadcasted_iota(jnp.int32, (rows, ncmp), 1) * CMP_STRIDE + (CMP_BLOCK - 1)
    mask_c = cmp_end <= t_row
    s_c = jnp.where(mask_c, s_c, NEG_BIG)
    e_c = jnp.exp(s_c - jnp.max(s_c, axis=-1, keepdims=True))
    p_c = jnp.where(mask_c, e_c / jnp.sum(e_c, axis=-1, keepdims=True), 0.0)
    o_cmp = _dot(p_c.astype(BF16), vc_ref[0])

    p_sum = p_c[0:tq]
    for r in range(1, NSA_REP):
        p_sum = p_sum + p_c[r * tq:(r + 1) * tq]
    imp = _dot_nt_exact_rhs(ovl_ref[...], p_sum)
    blk = lax.broadcasted_iota(jnp.int32, (nblk, tq), 0)
    forced = (blk == 0) | (blk == i) | (blk == i - 1)
    valid = blk <= i
    imp = jnp.where(forced, 1e9, jnp.where(valid, imp, -1.0))

    cnt = jnp.zeros((nblk, tq), F32)
    for j in range(nblk):
        row = imp[j:j + 1, :]
        ahead = (row > imp) | ((row == imp) & (blk > j))
        cnt = cnt + jnp.where(ahead, 1.0, 0.0)
    sel_t = jnp.where((cnt < float(min(SEL_TOP_N, nblk))) & valid, 1.0, 0.0)
    eye = jnp.where(lax.broadcasted_iota(jnp.int32, (tq, tq), 0) == lax.broadcasted_iota(jnp.int32, (tq, tq), 1),
                    1.0, 0.0).astype(BF16)
    sel = _dot_nt(eye, sel_t.astype(BF16))
    negb = ((sel - 1.0) * (-SEL_NEG)).astype(BF16)
    negb = jnp.concatenate([negb] * NSA_REP, axis=0)

    kpos_l = lax.broadcasted_iota(jnp.int32, (rows, K_TILE), 1)
    blk_r = lax.broadcasted_iota(jnp.int32, (nblk, K_TILE), 0)
    blk_l = lax.broadcasted_iota(jnp.int32, (nblk, K_TILE), 1) // SEL_BLOCK

    def sel_scores(jj):
        k0 = pl.multiple_of(jj * K_TILE, K_TILE)
        s = _dot_nt(q, ks_ref[pl.ds(k0, K_TILE), :])
        expand = jnp.where(blk_r == 2 * jj + blk_l, 1.0, 0.0).astype(BF16)
        return s + _dot(negb, expand), vs_ref[pl.ds(k0, K_TILE), :]

    init = (jnp.full((rows, 1), NEG_BIG, F32), jnp.zeros((rows, 1), F32), jnp.zeros((rows, LANES), F32))
    jd = i // 2
    s, v = sel_scores(jd)
    s = jnp.where(jd * K_TILE + kpos_l <= t_row, s, NEG_BIG)
    carry = _online_update(init, s, v)

    def sel_body(jj, c):
        s, v = sel_scores(jj)
        return _online_update(c, s, v)

    m_s, l_s, acc_s = lax.fori_loop(0, jd, sel_body, carry)
    o_sel = acc_s / l_s

    carry = init
    n_win = WINDOW // K_TILE + 1
    for w in range(n_win - 1, -1, -1):
        jj = jd - (n_win - 1) + w
        jc = jnp.maximum(jj, 0)
        k0 = pl.multiple_of(jc * K_TILE, K_TILE)
        s = _dot_nt(q, kw_ref[pl.ds(k0, K_TILE), :])
        diff = t_row - (jj * K_TILE + kpos_l)
        s = jnp.where((diff >= 0) & (diff < WINDOW) & (jj >= 0), s, NEG_BIG)
        carry = _online_update(carry, s, vw_ref[pl.ds(k0, K_TILE), :])
    m_w, l_w, acc_w = carry
    o_win = acc_w / l_w

    gd = gd_ref[...]
    lane = lax.broadcasted_iota(jnp.int32, (tq, LANES), 1)
    low = lane < HEAD_DIM
    is_g0 = g == 0
    heads = []
    for r in range(NSA_REP):
        rs = slice(r * tq, (r + 1) * tq)
        heads.append(gd[:, 3 * r:3 * r + 1] * o_cmp[rs] + gd[:, 3 * r + 1:3 * r + 2] * o_sel[rs]
                     + gd[:, 3 * r + 2:3 * r + 3] * o_win[rs])
    for sidx in range(NSA_REP // 2):
        a, b = heads[2 * sidx], heads[2 * sidx + 1]
        a_low = jnp.where(is_g0, a, pltpu.roll(a, HEAD_DIM, 1))
        b_high = jnp.where(is_g0, pltpu.roll(b, HEAD_DIM, 1), b)
        o_ref[:, sidx * LANES:(sidx + 1) * LANES] = jnp.where(low, a_low, b_high)


def _nsa(q, kcmp, vcmp, ks, vs, kw, vw, gd, b, s):
    t = q.shape[0]
    nq = s // Q_TILE
    ncmp = s // CMP_STRIDE
    nblk = s // SEL_BLOCK
    cs = np.arange(ncmp) * CMP_STRIDE
    ss = np.arange(nblk) * SEL_BLOCK
    overlap = (cs[:, None] < ss[None, :] + SEL_BLOCK) & (cs[:, None] + CMP_BLOCK > ss[None, :])
    overlap[ncmp - 1, :] = False
    ovl_t = jnp.asarray(overlap.T, BF16)

    def seq_spec():
        return pl.BlockSpec((s, LANES), lambda bi, g, i: (bi, 0))

    return pl.pallas_call(
        _nsa_kernel,
        grid=(b, NSA_KV_GROUPS, nq),
        in_specs=[pl.BlockSpec((Q_TILE, NSA_REP * LANES), lambda bi, g, i: (bi * nq + i, g)),
                  pl.BlockSpec((1, ncmp, LANES), lambda bi, g, i: (bi, 0, 0)),
                  pl.BlockSpec((1, ncmp, LANES), lambda bi, g, i: (bi, 0, 0)),
                  seq_spec(), seq_spec(), seq_spec(), seq_spec(),
                  pl.BlockSpec((Q_TILE, LANES), lambda bi, g, i: (bi * nq + i, g)),
                  pl.BlockSpec((nblk, ncmp), lambda bi, g, i: (0, 0))],
        out_specs=pl.BlockSpec((Q_TILE, NSA_REP * HEAD_DIM), lambda bi, g, i: (bi * nq + i, g)),
        out_shape=jax.ShapeDtypeStruct((t, NSA_WIDTH), F32),
        compiler_params=_cparams(("parallel", "parallel", "arbitrary")),
    )(q, kcmp, vcmp, ks, vs, kw, vw, gd, ovl_t)


def _conv_silu(cur_ref, ext_ref, tail_ref, w_ref, b_ref):
    l = cur_ref.shape[0]
    cur = cur_ref[...]
    ext_ref[0:SUBLANES, :] = tail_ref[...]
    ext_ref[SUBLANES:, :] = cur
    tail_ref[...] = cur[l - SUBLANES:, :]
    acc = jnp.zeros(cur.shape, F32) + b_ref[...]
    for k in range(SSD_CONV):
        off = SUBLANES - (SSD_CONV - 1) + k
        acc = acc + ext_ref[off:off + l, :] * w_ref[k:k + 1, :]
    return acc * _sigmoid(acc)


def _ssd_kernel(xs_ref, bm_ref, cm_ref, wx_ref, wb_ref, wc_ref, bx_ref, bb_ref, bc_ref, z_ref, gd_ref, dtt_ref,
                alane_ref, asub_ref, dskip_ref, ng_ref, tril_ref, edt_ref, eseg_ref, o_ref,
                tx_ref, tb_ref, tc_ref, ex_ref, eb_ref, ec_ref, st_ref):
    c = pl.program_id(2)
    l = xs_ref.shape[0]

    @pl.when(c == 0)
    def _():
        tx_ref[...] = jnp.zeros(tx_ref.shape, F32)
        tb_ref[...] = jnp.zeros(tb_ref.shape, F32)
        tc_ref[...] = jnp.zeros(tc_ref.shape, F32)
        st_ref[...] = jnp.zeros(st_ref.shape, F32)

    xs = _conv_silu(xs_ref, ex_ref, tx_ref, wx_ref, bx_ref)
    bm = _conv_silu(bm_ref, eb_ref, tb_ref, wb_ref, bb_ref)
    cm = _conv_silu(cm_ref, ec_ref, tc_ref, wc_ref, bc_ref)
    gd = gd_ref[...]
    tril = tril_ref[...]
    adt_c = gd * (-jnp.exp(alane_ref[0]))
    acum_c = _dot_exact_rhs(tril, adt_c)
    dt_full = _dot_exact_lhs(gd, edt_ref[...])
    ac_full = _dot_exact_lhs(acum_c, edt_ref[...])
    ac_seg = _dot_exact_lhs(acum_c, eseg_ref[...])
    adt_t = dtt_ref[...] * (-jnp.exp(asub_ref[0]))
    acum_t = _dot_nt_exact_lhs(adt_t, tril)

    xdt = xs * dt_full
    cmb = cm.astype(BF16)
    cb = _dot_nt(cmb, bm.astype(BF16))
    causal = lax.broadcasted_iota(jnp.int32, (l, l), 0) >= lax.broadcasted_iota(jnp.int32, (l, l), 1)
    head_of_lane = lax.broadcasted_iota(jnp.int32, xs.shape, 1) // SSD_HEAD_DIM
    y = jnp.zeros(xs.shape, F32)
    for r in range(SSD_REP):
        seg = jnp.exp(jnp.where(causal, ac_seg[:, r * l:(r + 1) * l] - acum_t[r:r + 1, :], NEG_BIG))
        x_r = jnp.where(head_of_lane == r, xdt, 0.0).astype(BF16)
        y = y + _dot((cb * seg).astype(BF16), x_r)
    ac_last = ac_full[l - 1:l, :]
    state = st_ref[...]
    y = y + _dot(cmb, state.astype(BF16)) * jnp.exp(ac_full)
    decayed = (xdt * jnp.exp(ac_last - ac_full)).astype(BF16)
    st_ref[...] = state * jnp.exp(ac_last) + _dot(bm.T.astype(BF16), decayed)
    y = y + xs * dskip_ref[...]
    z = z_ref[...]
    yz = y * (z * _sigmoid(z))
    o_ref[...] = yz * lax.rsqrt(jnp.mean(yz * yz, axis=-1, keepdims=True) + EPS) * ng_ref[...]


def _ssd(xbc, z, gd, dtt, conv_w, conv_b, a_log, d_skip, norm_g, b, s):
    t = xbc.shape[0]
    l = SSD_CHUNK
    nc = s // l
    gw = SSD_WIDTH // SSD_GROUPS
    xcols = gw // LANES
    b_blk0 = SSD_WIDTH // LANES
    c_blk0 = b_blk0 + SSD_GROUPS * SSD_STATE // LANES
    conv_b2 = conv_b[None, :]
    a_grp = a_log.reshape(SSD_GROUPS, SSD_REP)
    alane = jnp.zeros((SSD_GROUPS, 1, LANES), F32).at[:, 0, 16:16 + SSD_REP].set(a_grp)
    asub = jnp.zeros((SSD_GROUPS, SUBLANES, LANES), F32).at[:, :SSD_REP, :].set(
        jnp.broadcast_to(a_grp[:, :, None], (SSD_GROUPS, SSD_REP, LANES)))
    dskip = jnp.repeat(d_skip, SSD_HEAD_DIM)[None, :]
    ng = norm_g[None, :]
    ii = np.arange(l)
    tril = jnp.asarray(ii[:, None] >= ii[None, :], BF16)
    edt = np.zeros((LANES, gw), np.float32)
    eseg = np.zeros((LANES, SSD_REP * l), np.float32)
    for r in range(SSD_REP):
        edt[16 + r, r * SSD_HEAD_DIM:(r + 1) * SSD_HEAD_DIM] = 1.0
        eseg[16 + r, r * l:(r + 1) * l] = 1.0
    edt, eseg = jnp.asarray(edt, BF16), jnp.asarray(eseg, BF16)

    def rowblk(width, col):
        return pl.BlockSpec((l, width), lambda bi, g, c: (bi * nc + c, col(g)))

    def const(shape, col=None):
        if col is None:
            return pl.BlockSpec(shape, lambda bi, g, c: (0,) * len(shape))
        return pl.BlockSpec(shape, lambda bi, g, c: (0, col(g)))

    return pl.pallas_call(
        _ssd_kernel,
        grid=(b, SSD_GROUPS, nc),
        in_specs=[rowblk(gw, lambda g: g), rowblk(LANES, lambda g: b_blk0 + g), rowblk(LANES, lambda g: c_blk0 + g),
                  const((SSD_CONV, gw), lambda g: g), const((SSD_CONV, LANES), lambda g: b_blk0 + g),
                  const((SSD_CONV, LANES), lambda g: c_blk0 + g),
                  const((1, gw), lambda g: g), const((1, LANES), lambda g: b_blk0 + g),
                  const((1, LANES), lambda g: c_blk0 + g),
                  rowblk(gw, lambda g: g), rowblk(LANES, lambda g: g),
                  pl.BlockSpec((SUBLANES, l), lambda bi, g, c: (g, bi * nc + c)),
                  pl.BlockSpec((1, 1, LANES), lambda bi, g, c: (g, 0, 0)),
                  pl.BlockSpec((1, SUBLANES, LANES), lambda bi, g, c: (g, 0, 0)),
                  const((1, gw), lambda g: g), const((1, gw), lambda g: g),
                  const((l, l)), const((LANES, gw)), const((LANES, SSD_REP * l))],
        out_specs=rowblk(gw, lambda g: g),
        out_shape=jax.ShapeDtypeStruct((t, SSD_WIDTH), F32),
        scratch_shapes=[pltpu.VMEM((SUBLANES, gw), F32), pltpu.VMEM((SUBLANES, LANES), F32),
                        pltpu.VMEM((SUBLANES, LANES), F32),
                        pltpu.VMEM((l + SUBLANES, gw), F32), pltpu.VMEM((l + SUBLANES, LANES), F32),
                        pltpu.VMEM((l + SUBLANES, LANES), F32),
                        pltpu.VMEM((SSD_STATE, gw), F32)],
        compiler_params=_cparams(("parallel", "parallel", "arbitrary")),
    )(xbc, xbc, xbc, conv_w, conv_w, conv_w, conv_b2, conv_b2, conv_b2, z, gd, dtt,
      alane, asub, dskip, ng, tril, edt, eseg)


def _rms(v, g):
    return v * lax.rsqrt(jnp.mean(v * v, axis=-1, keepdims=True) + EPS) * g


def _out_proj_kernel(x_ref, on_ref, os_ref, ng_ref, w1_ref, w2_ref, fg_ref, x1_ref, xnt_ref):
    onn = _rms(on_ref[...], ng_ref[...]).astype(BF16)
    x1 = x_ref[...] + _dot(onn, w1_ref[...]) + _dot(os_ref[...].astype(BF16), w2_ref[...])
    x1_ref[...] = x1
    xnt_ref[...] = _rms(x1, fg_ref[...]).T.astype(BF16)


def _out_proj(xt, o_nsa, o_ssd, nsa_g, w_out, ffn_g):
    t = xt.shape[0]
    tm = min(TOK_TILE, t)

    def tok(width):
        return pl.BlockSpec((tm, width), lambda i: (i, 0))

    def full(shape):
        return pl.BlockSpec(shape, lambda i: (0, 0))

    wb = w_out.astype(BF16)
    return pl.pallas_call(
        _out_proj_kernel,
        grid=(t // tm,),
        in_specs=[tok(D_MODEL), tok(NSA_WIDTH), tok(SSD_WIDTH), full((1, NSA_WIDTH)),
                  full((NSA_WIDTH, D_MODEL)), full((SSD_WIDTH, D_MODEL)), full((1, D_MODEL))],
        out_specs=[tok(D_MODEL), pl.BlockSpec((D_MODEL, tm), lambda i: (0, i))],
        out_shape=[jax.ShapeDtypeStruct((t, D_MODEL), F32), jax.ShapeDtypeStruct((D_MODEL, t), BF16)],
        compiler_params=_cparams(("parallel",)),
    )(xt, o_nsa, o_ssd, nsa_g[None, :], wb[:NSA_WIDTH], wb[NSA_WIDTH:], ffn_g[None, :])


def _top_sorted(e, k):
    rows = lax.broadcasted_iota(jnp.int32, (k, e.shape[1]), 0)
    out = jnp.zeros((k, e.shape[1]), F32)
    cur = e
    for j in range(k):
        mk = jnp.max(cur, axis=0, keepdims=True)
        out = jnp.where(rows == j, mk, out)
        cur = jnp.where(cur == mk, -1.0, cur)
    return out


def _pair_candidates(a16, b16):
    row8 = lax.broadcasted_iota(jnp.int32, (SUBLANES, a16.shape[1]), 0)
    groups = [a16[0:1] * b16[0:SUBLANES], a16[0:1] * b16[SUBLANES:], a16[1:2] * b16[0:SUBLANES]]
    for a in range(2, SUBLANES):
        groups.append(jnp.where(row8 < PEER_TOPK // (a + 1), a16[a:a + 1] * b16[0:SUBLANES], -1.0))
    groups.append(a16[SUBLANES:] * b16[0:1])
    return jnp.concatenate(groups, axis=0)


def _peer_front_kernel(xnt_ref, wq_ref, keys_ref, a_ref, b_ref, thr_ref):
    xt = xnt_ref[...]
    tm = xt.shape[1]
    row8 = lax.broadcasted_iota(jnp.int32, (PEER_HEADS, tm), 0)
    thr_all = jnp.zeros((PEER_HEADS, tm), F32)
    for h in range(PEER_HEADS):
        qh = _dot(wq_ref[h * PEER_KEY_DIM:(h + 1) * PEER_KEY_DIM, :], xt)
        s1 = _dot(keys_ref[h, 0], qh[:PEER_HALF].astype(BF16))
        s2 = _dot(keys_ref[h, 1], qh[PEER_HALF:].astype(BF16))
        e1 = jnp.exp(s1 - jnp.max(s1, axis=0, keepdims=True))
        e2 = jnp.exp(s2 - jnp.max(s2, axis=0, keepdims=True))
        a16 = _top_sorted(e1, PEER_TOPK)
        b16 = _top_sorted(e2, PEER_TOPK)
        cand = _pair_candidates(a16, b16)
        cur = cand
        zsum = jnp.zeros((1, tm), F32)
        for _ in range(PEER_TOPK):
            tau = jnp.max(cur, axis=0, keepdims=True)
            zsum = zsum + tau
            cur = jnp.where(cur == tau, -1.0, cur)
        rinv = 1.0 / zsum
        cand_scaled = _pair_candidates(a16, b16 * rinv)
        thr = jnp.min(jnp.where(cand >= tau, cand_scaled, 3e38), axis=0, keepdims=True)
        a_ref[h * PEER_N_KEYS:(h + 1) * PEER_N_KEYS, :] = e1
        b_ref[h * PEER_N_KEYS:(h + 1) * PEER_N_KEYS, :] = e2 * rinv
        thr_all = jnp.where(row8 == h, thr, thr_all)
    thr_ref[...] = thr_all


def _peer_front(xnt, w_q, keys):
    t = xnt.shape[1]
    tm = min(PEER_FRONT_TOK, t)
    nk = PEER_HEADS * PEER_N_KEYS
    return pl.pallas_call(
        _peer_front_kernel,
        grid=(t // tm,),
        in_specs=[pl.BlockSpec((D_MODEL, tm), lambda i: (0, i)),
                  pl.BlockSpec((PEER_HEADS * PEER_KEY_DIM, D_MODEL), lambda i: (0, 0)),
                  pl.BlockSpec((PEER_HEADS, 2, PEER_N_KEYS, PEER_HALF), lambda i: (0, 0, 0, 0))],
        out_specs=[pl.BlockSpec((nk, tm), lambda i: (0, i)), pl.BlockSpec((nk, tm), lambda i: (0, i)),
                   pl.BlockSpec((PEER_HEADS, tm), lambda i: (0, i))],
        out_shape=[jax.ShapeDtypeStruct((nk, t), F32), jax.ShapeDtypeStruct((nk, t), F32),
                   jax.ShapeDtypeStruct((PEER_HEADS, t), F32)],
        compiler_params=_cparams(("parallel",)),
    )(xnt, w_q.T.astype(BF16), keys.astype(BF16))


def _peer_dense_kernel(xnt_ref, u_ref, vt_ref, a_ref, b_ref, thr_ref, o_ref, h_ref, act_ref):
    j = pl.program_id(1)
    n_sub = u_ref.shape[0] // PEER_N_KEYS

    @pl.when(j == 0)
    def _():
        o_ref[...] = jnp.zeros(o_ref.shape, F32)

    h_ref[...] = _dot(u_ref[...], xnt_ref[...])

    def sub(cl, carry):
        r0 = pl.multiple_of(cl * PEER_N_KEYS, PEER_N_KEYS)
        e1 = j * n_sub + cl
        gate = jnp.zeros((PEER_N_KEYS, h_ref.shape[1]), F32)
        for h in range(PEER_HEADS):
            a_row = a_ref[pl.ds(h * PEER_N_KEYS + e1, 1), :]
            p = b_ref[h * PEER_N_KEYS:(h + 1) * PEER_N_KEYS, :] * a_row
            gate = gate + jnp.where(p >= thr_ref[h:h + 1, :], p, 0.0)
        act_ref[pl.ds(r0, PEER_N_KEYS), :] = (_gelu_tanh(h_ref[pl.ds(r0, PEER_N_KEYS), :]) * gate).astype(BF16)
        return carry

    lax.fori_loop(0, n_sub, sub, 0)
    o_ref[...] += _dot(vt_ref[...], act_ref[...])


def _peer_dense(xnt, u, vt, a, b, thr):
    t = xnt.shape[1]
    tm = min(PEER_TOK, t)
    ne = u.shape[0]
    te = PEER_EXP
    nk = PEER_HEADS * PEER_N_KEYS
    return pl.pallas_call(
        _peer_dense_kernel,
        grid=(t // tm, ne // te),
        in_specs=[pl.BlockSpec((D_MODEL, tm), lambda i, j: (0, i)),
                  pl.BlockSpec((te, D_MODEL), lambda i, j: (j, 0)),
                  pl.BlockSpec((D_MODEL, te), lambda i, j: (0, j)),
                  pl.BlockSpec((nk, tm), lambda i, j: (0, i)),
                  pl.BlockSpec((nk, tm), lambda i, j: (0, i)),
                  pl.BlockSpec((PEER_HEADS, tm), lambda i, j: (0, i))],
        out_specs=pl.BlockSpec((D_MODEL, tm), lambda i, j: (0, i)),
        out_shape=jax.ShapeDtypeStruct((D_MODEL, t), F32),
        scratch_shapes=[pltpu.VMEM((te, tm), F32), pltpu.VMEM((te, tm), BF16)],
        compiler_params=_cparams(("parallel", "arbitrary")),
    )(xnt, u, vt, a, b, thr)


def _final_kernel(x1_ref, pt_ref, g_ref, o_ref):
    o_ref[...] = _rms(x1_ref[...] + pt_ref[...].T, g_ref[...])


def _residual_norm(x1, peer_t, g):
    t = x1.shape[0]
    tm = min(TOK_TILE, t)
    return pl.pallas_call(
        _final_kernel,
        grid=(t // tm,),
        in_specs=[pl.BlockSpec((tm, D_MODEL), lambda i: (i, 0)), pl.BlockSpec((D_MODEL, tm), lambda i: (0, i)),
                  pl.BlockSpec((1, D_MODEL), lambda i: (0, 0))],
        out_specs=pl.BlockSpec((tm, D_MODEL), lambda i: (i, 0)),
        out_shape=jax.ShapeDtypeStruct((t, D_MODEL), F32),
        compiler_params=_cparams(("parallel",)),
    )(x1, peer_t, g[None, :])


def kernel(x, attn_norm_g, w_in, cmp_pos_k, cmp_w1_k, cmp_b1_k, cmp_w2_k, cmp_pos_v, cmp_w1_v, cmp_b1_v, cmp_w2_v,
           conv_w, conv_b, dt_bias, a_log, d_skip, ssd_norm_g, nsa_norm_g, w_out, ffn_norm_g, peer_w_q, peer_keys,
           peer_u, peer_v, final_norm_g):
    b, s, d = x.shape
    xt = x.reshape(b * s, d)
    assert attn_norm_g.shape[0] == 1, "single-layer block"
    for layer in range(1):
        q, kc, vc, ks, vs, kw, vw, gd, dtt, z, xbc = _in_proj(xt, attn_norm_g[layer], w_in[layer], dt_bias[layer], s)
        kcmp = _compress(kc, cmp_pos_k[layer], cmp_w1_k[layer], cmp_b1_k[layer], cmp_w2_k[layer], b, s)
        vcmp = _compress(vc, cmp_pos_v[layer], cmp_w1_v[layer], cmp_b1_v[layer], cmp_w2_v[layer], b, s)
        o_nsa = _nsa(q, kcmp, vcmp, ks, vs, kw, vw, gd, b, s)
        o_ssd = _ssd(xbc, z, gd, dtt, conv_w[layer], conv_b[layer], a_log[layer], d_skip[layer],
                     ssd_norm_g[layer], b, s)
        x1, xnt = _out_proj(xt, o_nsa, o_ssd, nsa_norm_g[layer], w_out[layer], ffn_norm_g[layer])
        a, bb, thr = _peer_front(xnt, peer_w_q[layer], peer_keys[layer])
        peer_t = _peer_dense(xnt, peer_u[layer].astype(BF16), peer_v[layer].T.astype(BF16), a, bb, thr)
    return _residual_norm(x1, peer_t, final_norm_g).reshape(b, s, d)
```

```python
import functools

import numpy as np
import jax
import jax.numpy as jnp
from jax import lax
from jax.experimental import pallas as pl
from jax.experimental.pallas import tpu as pltpu

F32 = jnp.float32
BF16 = jnp.bfloat16

EPS = 1e-6
D_MODEL = 1024
NSA_HEADS = 8
NSA_KV_GROUPS = 2
NSA_REP = NSA_HEADS // NSA_KV_GROUPS
HEAD_DIM = 64
NSA_WIDTH = NSA_HEADS * HEAD_DIM
KV_WIDTH = NSA_KV_GROUPS * HEAD_DIM
CMP_STRIDE = 16
CMP_BLOCK = 32
CMP_HIDDEN = 256
SEL_BLOCK = 64
SEL_TOP_N = 16
WINDOW = 512
ROPE_DIM = HEAD_DIM // 4
ROPE_THETA = 500000.0
SSD_HEADS = 8
SSD_HEAD_DIM = 64
SSD_WIDTH = SSD_HEADS * SSD_HEAD_DIM
SSD_GROUPS = 2
SSD_REP = SSD_HEADS // SSD_GROUPS
SSD_STATE = 128
SSD_CONV = 4
SSD_CHUNK = 128
SSD_CONV_DIM = SSD_WIDTH + 2 * SSD_GROUPS * SSD_STATE
PEER_HEADS = 8
PEER_N_KEYS = 128
PEER_N_EXPERTS = PEER_N_KEYS * PEER_N_KEYS
PEER_KEY_DIM = 256
PEER_HALF = PEER_KEY_DIM // 2
PEER_TOPK = 16

LANES = 128
SUBLANES = 8
VMEM_LIMIT = 48 * 1024 * 1024
NEG_BIG = -1e30
SEL_NEG = -1e9

TOK_TILE = 512
Q_TILE = SEL_BLOCK
K_TILE = 2 * SEL_BLOCK
S_CHUNK = 4 * K_TILE
PEER_TOK = 512
PEER_EXP = 1024
PEER_FRONT_TOK = 256


def _cparams(sem, flags=None):
    return pltpu.CompilerParams(dimension_semantics=sem, vmem_limit_bytes=VMEM_LIMIT, flags=flags)


def _dot(a, b):
    return jnp.dot(a, b, preferred_element_type=F32)


def _dot_nt(a, b):
    return lax.dot_general(a, b, (((1,), (1,)), ((), ())), preferred_element_type=F32)


def _split3(a):
    a1 = a.astype(BF16)
    r1 = a - a1.astype(F32)
    a2 = r1.astype(BF16)
    a3 = (r1 - a2.astype(F32)).astype(BF16)
    return a1, a2, a3


def _dot_exact_lhs(a, b01):
    a1, a2, a3 = _split3(a)
    return _dot(a1, b01) + _dot(a2, b01) + _dot(a3, b01)


def _dot_nt_exact_lhs(a, b01):
    a1, a2, a3 = _split3(a)
    return _dot_nt(a1, b01) + _dot_nt(a2, b01) + _dot_nt(a3, b01)


def _dot_exact_rhs(a01, b):
    b1, b2, b3 = _split3(b)
    return _dot(a01, b1) + _dot(a01, b2) + _dot(a01, b3)


def _dot_nt_exact_rhs(a01, b):
    b1, b2, b3 = _split3(b)
    return _dot_nt(a01, b1) + _dot_nt(a01, b2) + _dot_nt(a01, b3)


def _softplus(x):
    return jnp.maximum(x, 0.0) + jnp.log(1.0 + jnp.exp(-jnp.abs(x)))


def _sigmoid(x):
    return 1.0 / (1.0 + jnp.exp(-x))


def _gelu_tanh(x):
    return 0.5 * x * (1.0 + jnp.tanh(0.7978845608028654 * (x + 0.044715 * (x * x * x))))


def _gelu_sigmoid(x):
    c1 = -2.0 * 0.7978845608028654 * 1.4426950408889634
    u = x * (c1 + (c1 * 0.044715) * (x * x))
    return x / (1.0 + jnp.exp2(u))


def _rope128(p, cos, sa, sb):
    return p * cos + pltpu.roll(p, LANES - ROPE_DIM // 2, 1) * sa + pltpu.roll(p, ROPE_DIM // 2, 1) * sb


def _in_proj_kernel(seq_tiles, x_ref, g_ref, w_ref, wdt_ref, cos_ref, sa_ref, sb_ref, gdb_ref, dtb_ref,
                    q_ref, kc_ref, vc_ref, ks_ref, vs_ref, kw_ref, vw_ref, gd_ref, dtt_ref, z_ref, xbc_ref):
    tm = x_ref.shape[0]
    x = x_ref[...]
    h = x * lax.rsqrt(jnp.mean(x * x, axis=-1, keepdims=True) + EPS) * g_ref[...]
    hb = h.astype(BF16)
    pos0 = pl.multiple_of((pl.program_id(0) % seq_tiles) * tm, tm)
    cos = cos_ref[pl.ds(pos0, tm), :]
    sa = sa_ref[pl.ds(pos0, tm), :]
    sb = sb_ref[pl.ds(pos0, tm), :]
    lane = lax.broadcasted_iota(jnp.int32, (tm, LANES), 1)
    low = lane < HEAD_DIM

    def proj(c0, width):
        return _dot(hb, w_ref[:, c0:c0 + width])

    def halves(p):
        return p, pltpu.roll(p, HEAD_DIM, 1)

    pq = proj(0, NSA_WIDTH)
    for s in range(NSA_WIDTH // LANES):
        srcs = halves(_rope128(pq[:, s * LANES:(s + 1) * LANES], cos, sa, sb) * (HEAD_DIM ** -0.5))
        for half in range(2):
            head = 2 * s + half
            q_ref[:, head * LANES:(head + 1) * LANES] = jnp.where(low, srcs[half], 0.0).astype(BF16)
    c0 = NSA_WIDTH
    kc_ref[...] = _rope128(proj(c0, LANES), cos, sa, sb).astype(BF16)
    vc_ref[...] = proj(c0 + LANES, LANES).astype(BF16)
    c0 += 2 * LANES
    blk_of_row = (pos0 + lax.broadcasted_iota(jnp.int32, (tm, LANES), 0)) // SEL_BLOCK
    onehot = jnp.where(lane - HEAD_DIM == blk_of_row, 1.0, 0.0)
    for ref, rope, fill in ((ks_ref, True, onehot), (vs_ref, False, 1.0), (kw_ref, True, 0.0), (vw_ref, False, 1.0)):
        p = proj(c0, LANES)
        if rope:
            p = _rope128(p, cos, sa, sb)
        for g, src in enumerate(halves(p)):
            ref[:, g * LANES:(g + 1) * LANES] = jnp.where(low, src, fill).astype(BF16)
        c0 += LANES
    for s in range(2):
        p = proj(c0, LANES) + gdb_ref[:, s * LANES:(s + 1) * LANES]
        act = jnp.where(lane < 3 * NSA_REP, _sigmoid(p),
                        jnp.where((lane >= 16) & (lane < 16 + SSD_REP), _softplus(p), 0.0))
        gd_ref[:, s * LANES:(s + 1) * LANES] = act
        c0 += LANES
    z_ref[...] = proj(c0, SSD_WIDTH)
    c0 += SSD_WIDTH
    xbc_ref[...] = proj(c0, SSD_CONV_DIM)
    dtt_ref[...] = _softplus(_dot_nt(wdt_ref[...], hb) + dtb_ref[:, 0:1])


def _rope_tables(s):
    half = ROPE_DIM // 2
    inv = jnp.power(ROPE_THETA, -jnp.arange(half, dtype=F32) * 2.0 / ROPE_DIM)
    ang = jnp.arange(s).astype(F32)[:, None] * inv[None, :]
    cos, sin = jnp.cos(ang), jnp.sin(ang)
    zeros = jnp.zeros((s, HEAD_DIM - ROPE_DIM), F32)
    cos64 = jnp.concatenate([cos, cos, jnp.ones((s, HEAD_DIM - ROPE_DIM), F32)], axis=1)
    sa64 = jnp.concatenate([-sin, jnp.zeros_like(sin), zeros], axis=1)
    sb64 = jnp.concatenate([jnp.zeros_like(sin), sin, zeros], axis=1)
    return tuple(jnp.concatenate([t, t], axis=1) for t in (cos64, sa64, sb64))


def _in_proj(xt, attn_g, w_in, dt_bias, s):
    t = xt.shape[0]
    tm = min(TOK_TILE, s)
    o_gl = NSA_WIDTH + 6 * KV_WIDTH
    o_z = o_gl + 3 * NSA_HEADS
    o_xbc = o_z + SSD_WIDTH
    o_dt = o_xbc + SSD_CONV_DIM
    gd_cols, gd_bias = [], []
    for g in range(NSA_KV_GROUPS):
        gates = w_in[:, o_gl + 3 * NSA_REP * g:o_gl + 3 * NSA_REP * (g + 1)]
        dts = w_in[:, o_dt + SSD_REP * g:o_dt + SSD_REP * (g + 1)]
        gd_cols += [gates, jnp.zeros((D_MODEL, 16 - 3 * NSA_REP), F32), dts,
                    jnp.zeros((D_MODEL, LANES - 16 - SSD_REP), F32)]
        gd_bias += [jnp.zeros((16,), F32), dt_bias[SSD_REP * g:SSD_REP * (g + 1)],
                    jnp.zeros((LANES - 16 - SSD_REP,), F32)]
    w_main = jnp.concatenate([w_in[:, :o_gl]] + gd_cols + [w_in[:, o_z:o_dt]], axis=1).astype(BF16)
    gdb = jnp.concatenate(gd_bias)[None, :]
    wdt_rows, dtb_rows = [], []
    for g in range(SSD_GROUPS):
        wdt_rows += [w_in[:, o_dt + SSD_REP * g:o_dt + SSD_REP * (g + 1)].T,
                     jnp.zeros((SUBLANES - SSD_REP, D_MODEL), F32)]
        dtb_rows += [dt_bias[SSD_REP * g:SSD_REP * (g + 1)], jnp.zeros((SUBLANES - SSD_REP,), F32)]
    wdt = jnp.concatenate(wdt_rows, axis=0).astype(BF16)
    dtb = jnp.broadcast_to(jnp.concatenate(dtb_rows)[:, None], (2 * SUBLANES, LANES))
    cos, sa, sb = _rope_tables(s)
    n_main = w_main.shape[1]

    def full(shape):
        return pl.BlockSpec(shape, lambda i: (0, 0))

    def tok(width):
        return pl.BlockSpec((tm, width), lambda i: (i, 0))

    kv_widths = [LANES, LANES] + [NSA_KV_GROUPS * LANES] * 4
    out_shapes = ([jax.ShapeDtypeStruct((t, NSA_HEADS * LANES), BF16)]
                  + [jax.ShapeDtypeStruct((t, w), BF16) for w in kv_widths]
                  + [jax.ShapeDtypeStruct((t, 2 * LANES), F32),
                     jax.ShapeDtypeStruct((2 * SUBLANES, t), F32),
                     jax.ShapeDtypeStruct((t, SSD_WIDTH), F32),
                     jax.ShapeDtypeStruct((t, SSD_CONV_DIM), F32)])
    out_specs = ([tok(NSA_HEADS * LANES)] + [tok(w) for w in kv_widths]
                 + [tok(2 * LANES), pl.BlockSpec((2 * SUBLANES, tm), lambda i: (0, i)),
                    tok(SSD_WIDTH), tok(SSD_CONV_DIM)])
    return pl.pallas_call(
        functools.partial(_in_proj_kernel, s // tm),
        name="in_proj",
        grid=(t // tm,),
        in_specs=[tok(D_MODEL), full((1, D_MODEL)), full((D_MODEL, n_main)), full((2 * SUBLANES, D_MODEL)),
                  full((s, LANES)), full((s, LANES)), full((s, LANES)), full((1, 2 * LANES)),
                  full((2 * SUBLANES, LANES))],
        out_specs=out_specs,
        out_shape=out_shapes,
        compiler_params=_cparams(("parallel",)),
    )(xt, attn_g[None, :], w_main, wdt, cos, sa, sb, gdb, dtb)


def _compress_kernel(kv_ref, w1a_ref, w1b_ref, pos_ref, w1_ref, b1_ref, w2_ref, out_ref):
    kv = kv_ref[0]
    bias = _dot(pos_ref[...], w1_ref[...])[0:1, :] + b1_ref[...]
    for g in range(NSA_KV_GROUPS):
        first = _dot(kv, w1a_ref[g])
        second = _dot(kv, w1b_ref[g])
        nxt = pltpu.roll(second, second.shape[0] - 1, 0)
        hid = _gelu_tanh(first + nxt + bias)
        out_ref[0, :, g * LANES:(g + 1) * LANES] = _dot(hid.astype(BF16), w2_ref[...]).astype(BF16)


def _compress(kv, pos_emb, w1, b1, w2, b, s):
    nch = s // CMP_STRIDE
    kvf = kv.reshape(b, nch, CMP_STRIDE * LANES)
    w1r = w1.reshape(CMP_BLOCK, HEAD_DIM, CMP_HIDDEN)
    zeros = jnp.zeros((CMP_STRIDE, HEAD_DIM, CMP_HIDDEN), F32)

    def expand(w_half, g):
        parts = [w_half, zeros] if g == 0 else [zeros, w_half]
        return jnp.concatenate(parts, axis=1).reshape(CMP_STRIDE * LANES, CMP_HIDDEN)

    w1a = jnp.stack([expand(w1r[:CMP_STRIDE], g) for g in range(NSA_KV_GROUPS)]).astype(BF16)
    w1b = jnp.stack([expand(w1r[CMP_STRIDE:], g) for g in range(NSA_KV_GROUPS)]).astype(BF16)
    w2e = jnp.concatenate([w2, jnp.zeros((CMP_HIDDEN, LANES - HEAD_DIM), F32)], axis=1).astype(BF16)
    pos = jnp.zeros((SUBLANES, CMP_BLOCK * HEAD_DIM), F32).at[0].set(pos_emb.reshape(-1)).astype(BF16)
    return pl.pallas_call(
        _compress_kernel,
        name="compress",
        grid=(b,),
        in_specs=[pl.BlockSpec((1, nch, CMP_STRIDE * LANES), lambda i: (i, 0, 0)),
                  pl.BlockSpec((2, CMP_STRIDE * LANES, CMP_HIDDEN), lambda i: (0, 0, 0)),
                  pl.BlockSpec((2, CMP_STRIDE * LANES, CMP_HIDDEN), lambda i: (0, 0, 0)),
                  pl.BlockSpec((SUBLANES, CMP_BLOCK * HEAD_DIM), lambda i: (0, 0)),
                  pl.BlockSpec((CMP_BLOCK * HEAD_DIM, CMP_HIDDEN), lambda i: (0, 0)),
                  pl.BlockSpec((1, CMP_HIDDEN), lambda i: (0, 0)),
                  pl.BlockSpec((CMP_HIDDEN, LANES), lambda i: (0, 0))],
        out_specs=pl.BlockSpec((1, nch, NSA_KV_GROUPS * LANES), lambda i: (i, 0, 0)),
        out_shape=jax.ShapeDtypeStruct((b, nch, NSA_KV_GROUPS * LANES), BF16),
        compiler_params=_cparams(("parallel",)),
    )(kvf, w1a, w1b, pos, w1.astype(BF16), b1[None, :], w2e)


def _lane_tiles(a):
    return [a[:, k * LANES:(k + 1) * LANES] for k in range(a.shape[1] // LANES)]


def _max_tiles(macc, s):
    for tile in _lane_tiles(s):
        macc = jnp.maximum(macc, tile)
    return macc


def _exp_tiles(s, m_b):
    return jnp.concatenate([jnp.exp(tile - m_b) for tile in _lane_tiles(s)], axis=1).astype(BF16)


def _normalize(acc):
    return acc / pltpu.roll(acc, HEAD_DIM, 1)


def _block_rank(imp):
    nblk, tq = imp.shape
    sub = lax.broadcasted_iota(jnp.int32, (SUBLANES, tq), 0)
    groups = [imp[SUBLANES * v:SUBLANES * (v + 1)] for v in range(nblk // SUBLANES)]
    cnt = [jnp.zeros((SUBLANES, tq), F32) for _ in groups]
    for j in range(nblk):
        row = imp[j:j + 1, :]
        for v, grp in enumerate(groups):
            if SUBLANES * v > j:
                ahead = row >= grp
            elif SUBLANES * (v + 1) <= j:
                ahead = row > grp
            else:
                ahead = (row > grp) | ((row == grp) & (sub > j - SUBLANES * v))
            cnt[v] = cnt[v] + jnp.where(ahead, 1.0, 0.0)
    return jnp.concatenate(cnt, axis=0)


def _nsa_kernel(q_ref, kc_ref, vc_ref, ks_ref, vs_ref, kw_ref, vw_ref, gd_ref, ovl_ref, o_ref, s_ref):
    i = pl.program_id(2)
    tq = Q_TILE
    rows = NSA_REP * tq
    ncmp = kc_ref.shape[1]
    nblk = ovl_ref.shape[0]
    q = jnp.concatenate([q_ref[:, r * LANES:(r + 1) * LANES] for r in range(NSA_REP)], axis=0)
    t_row = i * tq + lax.broadcasted_iota(jnp.int32, (rows, 1), 0) % tq

    s_c = _dot_nt(q, kc_ref[0])
    cmp_end = lax.broadcasted_iota(jnp.int32, (rows, ncmp), 1) * CMP_STRIDE + (CMP_BLOCK - 1)
    mask_c = cmp_end <= t_row
    s_c = jnp.where(mask_c, s_c, NEG_BIG)
    e_c = jnp.exp(s_c - jnp.max(s_c, axis=-1, keepdims=True))
    p_c = jnp.where(mask_c, e_c / jnp.sum(e_c, axis=-1, keepdims=True), 0.0)
    o_cmp = _dot(p_c.astype(BF16), vc_ref[0])

    p_sum = p_c[0:tq]
    for r in range(1, NSA_REP):
        p_sum = p_sum + p_c[r * tq:(r + 1) * tq]
    imp = _dot_nt_exact_rhs(ovl_ref[...], p_sum)
    blk = lax.broadcasted_iota(jnp.int32, (nblk, tq), 0)
    forced = (blk == 0) | (blk == i) | (blk == i - 1)
    valid = blk <= i
    imp = jnp.where(forced, 1e9, jnp.where(valid, imp, -1.0))

    sel_t = jnp.where((_block_rank(imp) < float(min(SEL_TOP_N, nblk))) & valid, 1.0, 0.0)
    ones_lo = jnp.ones((HEAD_DIM, tq), F32)
    sel_pad = jnp.concatenate([ones_lo, sel_t] + ([jnp.ones((HEAD_DIM - nblk, tq), F32)] if nblk < HEAD_DIM else []),
                              axis=0).astype(BF16)
    eye = jnp.where(lax.broadcasted_iota(jnp.int32, (tq, tq), 0) == lax.broadcasted_iota(jnp.int32, (tq, tq), 1),
                    1.0, 0.0).astype(BF16)
    negb = ((_dot_nt(eye, sel_pad) - 1.0) * (-SEL_NEG)).astype(BF16)
    q2 = q + jnp.concatenate([negb] * NSA_REP, axis=0)

    jd = i // 2
    nfull = jd // (S_CHUNK // K_TILE)
    lane_c = lax.broadcasted_iota(jnp.int32, (rows, S_CHUNK), 1)

    def scores(c):
        k0 = pl.multiple_of(c * S_CHUNK, S_CHUNK)
        return _dot_nt(q2, ks_ref[pl.ds(k0, S_CHUNK), :])

    def pass_max(c, macc):
        s = scores(c)
        s_ref[c] = s
        return _max_tiles(macc, s)

    macc = lax.fori_loop(0, nfull, pass_max, jnp.full((rows, LANES), NEG_BIG, F32))
    s_last = jnp.where(nfull * S_CHUNK + lane_c <= t_row, scores(nfull), NEG_BIG)
    macc = _max_tiles(macc, s_last)
    m_b = jnp.broadcast_to(jnp.max(macc, axis=-1, keepdims=True), (rows, LANES))

    def pass_sum(c, acc):
        k0 = pl.multiple_of(c * S_CHUNK, S_CHUNK)
        return acc + _dot(_exp_tiles(s_ref[c], m_b), vs_ref[pl.ds(k0, S_CHUNK), :])

    acc = lax.fori_loop(0, nfull, pass_sum, jnp.zeros((rows, LANES), F32))
    k_last = pl.multiple_of(nfull * S_CHUNK, S_CHUNK)
    o_sel = _normalize(acc + _dot(_exp_tiles(s_last, m_b), vs_ref[pl.ds(k_last, S_CHUNK), :]))

    n_wk = WINDOW + K_TILE
    k0 = pl.multiple_of(jnp.maximum(jd - WINDOW // K_TILE, 0) * K_TILE, K_TILE)
    diff = t_row - (k0 + lax.broadcasted_iota(jnp.int32, (rows, n_wk), 1))
    s_w = jnp.where((diff >= 0) & (diff < WINDOW), _dot_nt(q2, kw_ref[pl.ds(k0, n_wk), :]), NEG_BIG)
    m_w = jnp.max(_max_tiles(jnp.full((rows, LANES), NEG_BIG, F32), s_w), axis=-1, keepdims=True)
    p_w = _exp_tiles(s_w, jnp.broadcast_to(m_w, (rows, LANES)))
    o_win = _normalize(_dot(p_w, vw_ref[pl.ds(k0, n_wk), :]))

    gd = gd_ref[...]
    low = lax.broadcasted_iota(jnp.int32, (tq, LANES), 1) < HEAD_DIM
    heads = []
    for r in range(NSA_REP):
        rs = slice(r * tq, (r + 1) * tq)
        heads.append(gd[:, 3 * r:3 * r + 1] * o_cmp[rs] + gd[:, 3 * r + 1:3 * r + 2] * o_sel[rs]
                     + gd[:, 3 * r + 2:3 * r + 3] * o_win[rs])
    for sidx in range(NSA_REP // 2):
        o_ref[:, sidx * LANES:(sidx + 1) * LANES] = jnp.where(
            low, heads[2 * sidx], pltpu.roll(heads[2 * sidx + 1], HEAD_DIM, 1))


def _nsa(q, kcmp, vcmp, ks, vs, kw, vw, gd, b, s):
    t = q.shape[0]
    nq = s // Q_TILE
    ncmp = s // CMP_STRIDE
    nblk = s // SEL_BLOCK
    cs = np.arange(ncmp) * CMP_STRIDE
    ss = np.arange(nblk) * SEL_BLOCK
    overlap = (cs[:, None] < ss[None, :] + SEL_BLOCK) & (cs[:, None] + CMP_BLOCK > ss[None, :])
    overlap[ncmp - 1, :] = False
    ovl_t = jnp.asarray(overlap.T, BF16)

    assert nblk <= HEAD_DIM and s % S_CHUNK == 0 and s >= WINDOW + K_TILE

    def seq_spec():
        return pl.BlockSpec((s, LANES), lambda bi, g, i: (bi, g))

    return pl.pallas_call(
        _nsa_kernel,
        name="nsa",
        grid=(b, NSA_KV_GROUPS, nq),
        in_specs=[pl.BlockSpec((Q_TILE, NSA_REP * LANES), lambda bi, g, i: (bi * nq + i, g)),
                  pl.BlockSpec((1, ncmp, LANES), lambda bi, g, i: (bi, 0, g)),
                  pl.BlockSpec((1, ncmp, LANES), lambda bi, g, i: (bi, 0, g)),
                  seq_spec(), seq_spec(), seq_spec(), seq_spec(),
                  pl.BlockSpec((Q_TILE, LANES), lambda bi, g, i: (bi * nq + i, g)),
                  pl.BlockSpec((nblk, ncmp), lambda bi, g, i: (0, 0))],
        out_specs=pl.BlockSpec((Q_TILE, NSA_REP * HEAD_DIM), lambda bi, g, i: (bi * nq + i, g)),
        out_shape=jax.ShapeDtypeStruct((t, NSA_WIDTH), F32),
        scratch_shapes=[pltpu.VMEM((s // S_CHUNK, NSA_REP * Q_TILE, S_CHUNK), F32)],
        compiler_params=_cparams(("parallel", "parallel", "arbitrary")),
    )(q, kcmp, vcmp, ks, vs, kw, vw, gd, ovl_t)


def _conv_silu(cur_ref, ext_ref, tail_ref, w_ref, b_ref):
    l = cur_ref.shape[0]
    cur = cur_ref[...]
    ext_ref[0:SUBLANES, :] = tail_ref[...]
    ext_ref[SUBLANES:, :] = cur
    tail_ref[...] = cur[l - SUBLANES:, :]
    acc = jnp.zeros(cur.shape, F32) + b_ref[...]
    for k in range(SSD_CONV):
        off = SUBLANES - (SSD_CONV - 1) + k
        acc = acc + ext_ref[off:off + l, :] * w_ref[k:k + 1, :]
    return acc * _sigmoid(acc)


def _ssd_kernel(xs_ref, bm_ref, cm_ref, wx_ref, wb_ref, wc_ref, bx_ref, bb_ref, bc_ref, z_ref, gd_ref, dtt_ref,
                alane_ref, asub_ref, dskip_ref, ng_ref, tril_ref, edt_ref, eseg_ref, o_ref,
                tx_ref, tb_ref, tc_ref, ex_ref, eb_ref, ec_ref, st_ref):
    c = pl.program_id(2)
    l = xs_ref.shape[0]

    @pl.when(c == 0)
    def _():
        tx_ref[...] = jnp.zeros(tx_ref.shape, F32)
        tb_ref[...] = jnp.zeros(tb_ref.shape, F32)
        tc_ref[...] = jnp.zeros(tc_ref.shape, F32)
        st_ref[...] = jnp.zeros(st_ref.shape, F32)

    xs = _conv_silu(xs_ref, ex_ref, tx_ref, wx_ref, bx_ref)
    bm = _conv_silu(bm_ref, eb_ref, tb_ref, wb_ref, bb_ref)
    cm = _conv_silu(cm_ref, ec_ref, tc_ref, wc_ref, bc_ref)
    gd = gd_ref[...]
    tril = tril_ref[...]
    adt_c = gd * (-jnp.exp(alane_ref[0]))
    acum_c = _dot_exact_rhs(tril, adt_c)
    dt_full = _dot_exact_lhs(gd, edt_ref[...])
    ac_full = _dot_exact_lhs(acum_c, edt_ref[...])
    ac_seg = _dot_exact_lhs(acum_c, eseg_ref[...])
    adt_t = dtt_ref[...] * (-jnp.exp(asub_ref[0]))
    acum_t = _dot_nt_exact_lhs(adt_t, tril)

    xdt = xs * dt_full
    cmb = cm.astype(BF16)
    cb = _dot_nt(cmb, bm.astype(BF16))
    causal = lax.broadcasted_iota(jnp.int32, (l, l), 0) >= lax.broadcasted_iota(jnp.int32, (l, l), 1)
    head_of_lane = lax.broadcasted_iota(jnp.int32, xs.shape, 1) // SSD_HEAD_DIM
    y = jnp.zeros(xs.shape, F32)
    for r in range(SSD_REP):
        seg = jnp.exp(jnp.where(causal, ac_seg[:, r * l:(r + 1) * l] - acum_t[r:r + 1, :], NEG_BIG))
        x_r = jnp.where(head_of_lane == r, xdt, 0.0).astype(BF16)
        y = y + _dot((cb * seg).astype(BF16), x_r)
    ac_last = ac_full[l - 1:l, :]
    state = st_ref[...]
    y = y + _dot(cmb, state.astype(BF16)) * jnp.exp(ac_full)
    decayed = (xdt * jnp.exp(ac_last - ac_full)).astype(BF16)
    st_ref[...] = state * jnp.exp(ac_last) + _dot(bm.T.astype(BF16), decayed)
    y = y + xs * dskip_ref[...]
    z = z_ref[...]
    yz = y * (z * _sigmoid(z))
    o_ref[...] = yz * lax.rsqrt(jnp.mean(yz * yz, axis=-1, keepdims=True) + EPS) * ng_ref[...]


def _ssd(xbc, z, gd, dtt, conv_w, conv_b, a_log, d_skip, norm_g, b, s):
    t = xbc.shape[0]
    l = SSD_CHUNK
    nc = s // l
    gw = SSD_WIDTH // SSD_GROUPS
    xcols = gw // LANES
    b_blk0 = SSD_WIDTH // LANES
    c_blk0 = b_blk0 + SSD_GROUPS * SSD_STATE // LANES
    conv_b2 = conv_b[None, :]
    a_grp = a_log.reshape(SSD_GROUPS, SSD_REP)
    alane = jnp.zeros((SSD_GROUPS, 1, LANES), F32).at[:, 0, 16:16 + SSD_REP].set(a_grp)
    asub = jnp.zeros((SSD_GROUPS, SUBLANES, LANES), F32).at[:, :SSD_REP, :].set(
        jnp.broadcast_to(a_grp[:, :, None], (SSD_GROUPS, SSD_REP, LANES)))
    dskip = jnp.repeat(d_skip, SSD_HEAD_DIM)[None, :]
    ng = norm_g[None, :]
    ii = np.arange(l)
    tril = jnp.asarray(ii[:, None] >= ii[None, :], BF16)
    edt = np.zeros((LANES, gw), np.float32)
    eseg = np.zeros((LANES, SSD_REP * l), np.float32)
    for r in range(SSD_REP):
        edt[16 + r, r * SSD_HEAD_DIM:(r + 1) * SSD_HEAD_DIM] = 1.0
        eseg[16 + r, r * l:(r + 1) * l] = 1.0
    edt, eseg = jnp.asarray(edt, BF16), jnp.asarray(eseg, BF16)

    def rowblk(width, col):
        return pl.BlockSpec((l, width), lambda bi, g, c: (bi * nc + c, col(g)))

    def const(shape, col=None):
        if col is None:
            return pl.BlockSpec(shape, lambda bi, g, c: (0,) * len(shape))
        return pl.BlockSpec(shape, lambda bi, g, c: (0, col(g)))

    return pl.pallas_call(
        _ssd_kernel,
        name="ssd",
        grid=(b, SSD_GROUPS, nc),
        in_specs=[rowblk(gw, lambda g: g), rowblk(LANES, lambda g: b_blk0 + g), rowblk(LANES, lambda g: c_blk0 + g),
                  const((SSD_CONV, gw), lambda g: g), const((SSD_CONV, LANES), lambda g: b_blk0 + g),
                  const((SSD_CONV, LANES), lambda g: c_blk0 + g),
                  const((1, gw), lambda g: g), const((1, LANES), lambda g: b_blk0 + g),
                  const((1, LANES), lambda g: c_blk0 + g),
                  rowblk(gw, lambda g: g), rowblk(LANES, lambda g: g),
                  pl.BlockSpec((SUBLANES, l), lambda bi, g, c: (g, bi * nc + c)),
                  pl.BlockSpec((1, 1, LANES), lambda bi, g, c: (g, 0, 0)),
                  pl.BlockSpec((1, SUBLANES, LANES), lambda bi, g, c: (g, 0, 0)),
                  const((1, gw), lambda g: g), const((1, gw), lambda g: g),
                  const((l, l)), const((LANES, gw)), const((LANES, SSD_REP * l))],
        out_specs=rowblk(gw, lambda g: g),
        out_shape=jax.ShapeDtypeStruct((t, SSD_WIDTH), F32),
        scratch_shapes=[pltpu.VMEM((SUBLANES, gw), F32), pltpu.VMEM((SUBLANES, LANES), F32),
                        pltpu.VMEM((SUBLANES, LANES), F32),
                        pltpu.VMEM((l + SUBLANES, gw), F32), pltpu.VMEM((l + SUBLANES, LANES), F32),
                        pltpu.VMEM((l + SUBLANES, LANES), F32),
                        pltpu.VMEM((SSD_STATE, gw), F32)],
        compiler_params=_cparams(("parallel", "parallel", "arbitrary")),
    )(xbc, xbc, xbc, conv_w, conv_w, conv_w, conv_b2, conv_b2, conv_b2, z, gd, dtt,
      alane, asub, dskip, ng, tril, edt, eseg)


def _rms(v, g):
    return v * lax.rsqrt(jnp.mean(v * v, axis=-1, keepdims=True) + EPS) * g


def _out_proj_kernel(x_ref, on_ref, os_ref, ng_ref, w1_ref, w2_ref, fg_ref, x1_ref, xnt_ref):
    onn = _rms(on_ref[...], ng_ref[...]).astype(BF16)
    x1 = x_ref[...] + _dot(onn, w1_ref[...]) + _dot(os_ref[...].astype(BF16), w2_ref[...])
    x1_ref[...] = x1
    xnt_ref[...] = _rms(x1, fg_ref[...]).T.astype(BF16)


def _out_proj(xt, o_nsa, o_ssd, nsa_g, w_out, ffn_g):
    t = xt.shape[0]
    tm = min(TOK_TILE, t)

    def tok(width):
        return pl.BlockSpec((tm, width), lambda i: (i, 0))

    def full(shape):
        return pl.BlockSpec(shape, lambda i: (0, 0))

    wb = w_out.astype(BF16)
    return pl.pallas_call(
        _out_proj_kernel,
        name="out_proj",
        grid=(t // tm,),
        in_specs=[tok(D_MODEL), tok(NSA_WIDTH), tok(SSD_WIDTH), full((1, NSA_WIDTH)),
                  full((NSA_WIDTH, D_MODEL)), full((SSD_WIDTH, D_MODEL)), full((1, D_MODEL))],
        out_specs=[tok(D_MODEL), pl.BlockSpec((D_MODEL, tm), lambda i: (0, i))],
        out_shape=[jax.ShapeDtypeStruct((t, D_MODEL), F32), jax.ShapeDtypeStruct((D_MODEL, t), BF16)],
        compiler_params=_cparams(("parallel",)),
    )(xt, o_nsa, o_ssd, nsa_g[None, :], wb[:NSA_WIDTH], wb[NSA_WIDTH:], ffn_g[None, :])


def _top_sorted(e, k):
    rows = lax.broadcasted_iota(jnp.int32, (k, e.shape[1]), 0)
    out = jnp.zeros((k, e.shape[1]), F32)
    cur = e
    for j in range(k):
        mk = jnp.max(cur, axis=0, keepdims=True)
        out = jnp.where(rows == j, mk, out)
        cur = jnp.where(cur == mk, -1.0, cur)
    return out


def _pair_candidates(a16, b16):
    row8 = lax.broadcasted_iota(jnp.int32, (SUBLANES, a16.shape[1]), 0)
    groups = [a16[0:1] * b16[0:SUBLANES], a16[0:1] * b16[SUBLANES:], a16[1:2] * b16[0:SUBLANES]]
    for a in range(2, SUBLANES):
        groups.append(jnp.where(row8 < PEER_TOPK // (a + 1), a16[a:a + 1] * b16[0:SUBLANES], -1.0))
    groups.append(a16[SUBLANES:] * b16[0:1])
    return jnp.concatenate(groups, axis=0)


def _peer_front_kernel(xnt_ref, wq_ref, keys_ref, a_ref, b_ref, thr_ref):
    xt = xnt_ref[...]
    tm = xt.shape[1]
    row8 = lax.broadcasted_iota(jnp.int32, (PEER_HEADS, tm), 0)
    thr_all = jnp.zeros((PEER_HEADS, tm), F32)
    for h in range(PEER_HEADS):
        qh = _dot(wq_ref[h * PEER_KEY_DIM:(h + 1) * PEER_KEY_DIM, :], xt)
        s1 = _dot(keys_ref[h, 0], qh[:PEER_HALF].astype(BF16))
        s2 = _dot(keys_ref[h, 1], qh[PEER_HALF:].astype(BF16))
        e1 = jnp.exp(s1 - jnp.max(s1, axis=0, keepdims=True))
        e2 = jnp.exp(s2 - jnp.max(s2, axis=0, keepdims=True))
        a16 = _top_sorted(e1, PEER_TOPK)
        b16 = _top_sorted(e2, PEER_TOPK)
        cand = _pair_candidates(a16, b16)
        cur = cand
        zsum = jnp.zeros((1, tm), F32)
        for _ in range(PEER_TOPK):
            tau = jnp.max(cur, axis=0, keepdims=True)
            zsum = zsum + tau
            cur = jnp.where(cur == tau, -1.0, cur)
        rinv = 1.0 / zsum
        cand_scaled = _pair_candidates(a16, b16 * rinv)
        thr = jnp.min(jnp.where(cand >= tau, cand_scaled, 3e38), axis=0, keepdims=True)
        a_ref[h * PEER_N_KEYS:(h + 1) * PEER_N_KEYS, :] = e1
        b_ref[h * PEER_N_KEYS:(h + 1) * PEER_N_KEYS, :] = e2 * rinv
        thr_all = jnp.where(row8 == h, thr, thr_all)
    thr_ref[...] = thr_all


def _peer_front(xnt, w_q, keys):
    t = xnt.shape[1]
    tm = min(PEER_FRONT_TOK, t)
    nk = PEER_HEADS * PEER_N_KEYS
    return pl.pallas_call(
        _peer_front_kernel,
        name="peer_front",
        grid=(t // tm,),
        in_specs=[pl.BlockSpec((D_MODEL, tm), lambda i: (0, i)),
                  pl.BlockSpec((PEER_HEADS * PEER_KEY_DIM, D_MODEL), lambda i: (0, 0)),
                  pl.BlockSpec((PEER_HEADS, 2, PEER_N_KEYS, PEER_HALF), lambda i: (0, 0, 0, 0))],
        out_specs=[pl.BlockSpec((nk, tm), lambda i: (0, i)), pl.BlockSpec((nk, tm), lambda i: (0, i)),
                   pl.BlockSpec((PEER_HEADS, tm), lambda i: (0, i))],
        out_shape=[jax.ShapeDtypeStruct((nk, t), F32), jax.ShapeDtypeStruct((nk, t), F32),
                   jax.ShapeDtypeStruct((PEER_HEADS, t), F32)],
        compiler_params=_cparams(("parallel",)),
    )(xnt, w_q.T.astype(BF16), keys.astype(BF16))


def _peer_dense_kernel(xnt_ref, u_ref, vt_ref, a_ref, b_ref, thr_ref, o_ref, hid0_ref, hid1_ref, opnd_ref):
    j = pl.program_id(1)

    @pl.when(j == 0)
    def _():
        o_ref[...] = jnp.zeros(o_ref.shape, F32)
        hid1_ref[...] = jnp.zeros(hid1_ref.shape, F32)

    @pl.when(j % 2 == 0)
    def _():
        _peer_dense_step(j, xnt_ref, u_ref, vt_ref, a_ref, b_ref, thr_ref, o_ref, hid0_ref, hid1_ref, opnd_ref)

    @pl.when(j % 2 == 1)
    def _():
        _peer_dense_step(j, xnt_ref, u_ref, vt_ref, a_ref, b_ref, thr_ref, o_ref, hid1_ref, hid0_ref, opnd_ref)


def _peer_dense_step(j, xnt_ref, u_ref, vt_ref, a_ref, b_ref, thr_ref, o_ref, hid_ref, hid_prev_ref, opnd_ref):
    n_sub = u_ref.shape[0] // PEER_N_KEYS
    n_tok = xnt_ref.shape[1]
    hid_ref[...] = _gelu_sigmoid(_dot(u_ref[...], xnt_ref[...]))
    prev = jnp.maximum(j - 1, 0)
    piece_rows = 2 * PEER_N_KEYS
    half_keys = PEER_N_KEYS // 2
    acc = None
    for piece in range(u_ref.shape[0] // piece_rows):
        for cl in range(2 * piece, 2 * piece + 2):
            e1 = prev * n_sub + cl
            for lt in range(n_tok // LANES):
                cols = slice(lt * LANES, (lt + 1) * LANES)
                for half in range(2):
                    gate = jnp.zeros((half_keys, LANES), F32)
                    for h in range(PEER_HEADS):
                        a_row = a_ref[pl.ds(h * PEER_N_KEYS + e1, 1), :][:, cols]
                        b0 = h * PEER_N_KEYS + half * half_keys
                        p = b_ref[b0:b0 + half_keys, cols] * a_row
                        gate = gate + jnp.where(p >= thr_ref[h:h + 1, cols], p, 0.0)
                    rows = slice(cl * PEER_N_KEYS + half * half_keys, cl * PEER_N_KEYS + (half + 1) * half_keys)
                    opnd_ref[rows, cols] = (hid_prev_ref[rows, cols] * gate).astype(BF16)
        ks = slice(piece * piece_rows, (piece + 1) * piece_rows)
        o_ref[...] += _dot(vt_ref[:, ks], opnd_ref[ks, :])


def _peer_dense(xnt, u, vt, a, b, thr):
    t = xnt.shape[1]
    tm = min(PEER_TOK, t)
    ne = u.shape[0]
    te = PEER_EXP
    nk = PEER_HEADS * PEER_N_KEYS
    return pl.pallas_call(
        _peer_dense_kernel,
        name="peer_dense",
        grid=(t // tm, ne // te + 1),
        in_specs=[pl.BlockSpec((D_MODEL, tm), lambda i, j: (0, i)),
                  pl.BlockSpec((te, D_MODEL), lambda i, j: (jnp.minimum(j, ne // te - 1), 0)),
                  pl.BlockSpec((D_MODEL, te), lambda i, j: (0, jnp.maximum(j - 1, 0))),
                  pl.BlockSpec((nk, tm), lambda i, j: (0, i)),
                  pl.BlockSpec((nk, tm), lambda i, j: (0, i)),
                  pl.BlockSpec((PEER_HEADS, tm), lambda i, j: (0, i))],
        out_specs=pl.BlockSpec((D_MODEL, tm), lambda i, j: (0, i)),
        out_shape=jax.ShapeDtypeStruct((D_MODEL, t), F32),
        scratch_shapes=[pltpu.VMEM((te, tm), F32), pltpu.VMEM((te, tm), F32), pltpu.VMEM((te, tm), BF16)],
        compiler_params=_cparams(("parallel", "arbitrary")),
    )(xnt, u, vt, a, b, thr)


def _final_kernel(x1_ref, pt_ref, g_ref, o_ref):
    o_ref[...] = _rms(x1_ref[...] + pt_ref[...].T, g_ref[...])


def _residual_norm(x1, peer_t, g):
    t = x1.shape[0]
    tm = min(TOK_TILE, t)
    return pl.pallas_call(
        _final_kernel,
        name="residual_norm",
        grid=(t // tm,),
        in_specs=[pl.BlockSpec((tm, D_MODEL), lambda i: (i, 0)), pl.BlockSpec((D_MODEL, tm), lambda i: (0, i)),
                  pl.BlockSpec((1, D_MODEL), lambda i: (0, 0))],
        out_specs=pl.BlockSpec((tm, D_MODEL), lambda i: (i, 0)),
        out_shape=jax.ShapeDtypeStruct((t, D_MODEL), F32),
        compiler_params=_cparams(("parallel",)),
    )(x1, peer_t, g[None, :])


def kernel(x, attn_norm_g, w_in, cmp_pos_k, cmp_w1_k, cmp_b1_k, cmp_w2_k, cmp_pos_v, cmp_w1_v, cmp_b1_v, cmp_w2_v,
           conv_w, conv_b, dt_bias, a_log, d_skip, ssd_norm_g, nsa_norm_g, w_out, ffn_norm_g, peer_w_q, peer_keys,
           peer_u, peer_v, final_norm_g):
    b, s, d = x.shape
    xt = x.reshape(b * s, d)
    assert attn_norm_g.shape[0] == 1, "single-layer block"
    for layer in range(1):
        q, kc, vc, ks, vs, kw, vw, gd, dtt, z, xbc = _in_proj(xt, attn_norm_g[layer], w_in[layer], dt_bias[layer], s)
        kcmp = _compress(kc, cmp_pos_k[layer], cmp_w1_k[layer], cmp_b1_k[layer], cmp_w2_k[layer], b, s)
        vcmp = _compress(vc, cmp_pos_v[layer], cmp_w1_v[layer], cmp_b1_v[layer], cmp_w2_v[layer], b, s)
        o_nsa = _nsa(q, kcmp, vcmp, ks, vs, kw, vw, gd, b, s)
        o_ssd = _ssd(xbc, z, gd, dtt, conv_w[layer], conv_b[layer], a_log[layer], d_skip[layer],
                     ssd_norm_g[layer], b, s)
        x1, xnt = _out_proj(xt, o_nsa, o_ssd, nsa_norm_g[layer], w_out[layer], ffn_norm_g[layer])
        a, bb, thr = _peer_front(xnt, peer_w_q[layer], peer_keys[layer])
        peer_t = _peer_dense(xnt, peer_u[layer].astype(BF16), peer_v[layer].T.astype(BF16), a, bb, thr)
    return _residual_norm(x1, peer_t, final_norm_g).reshape(b, s, d)
```

```python
import functools

import numpy as np
import jax
import jax.numpy as jnp
from jax import lax
from jax.experimental import pallas as pl
from jax.experimental.pallas import tpu as pltpu

F32 = jnp.float32
BF16 = jnp.bfloat16

EPS = 1e-6
D_MODEL = 1024
NSA_HEADS = 8
NSA_KV_GROUPS = 2
NSA_REP = NSA_HEADS // NSA_KV_GROUPS
HEAD_DIM = 64
NSA_WIDTH = NSA_HEADS * HEAD_DIM
KV_WIDTH = NSA_KV_GROUPS * HEAD_DIM
CMP_STRIDE = 16
CMP_BLOCK = 32
CMP_HIDDEN = 256
SEL_BLOCK = 64
SEL_TOP_N = 16
WINDOW = 512
ROPE_DIM = HEAD_DIM // 4
ROPE_THETA = 500000.0
SSD_HEADS = 8
SSD_HEAD_DIM = 64
SSD_WIDTH = SSD_HEADS * SSD_HEAD_DIM
SSD_GROUPS = 2
SSD_REP = SSD_HEADS // SSD_GROUPS
SSD_STATE = 128
SSD_CONV = 4
SSD_CHUNK = 128
SSD_CONV_DIM = SSD_WIDTH + 2 * SSD_GROUPS * SSD_STATE
PEER_HEADS = 8
PEER_N_KEYS = 128
PEER_N_EXPERTS = PEER_N_KEYS * PEER_N_KEYS
PEER_KEY_DIM = 256
PEER_HALF = PEER_KEY_DIM // 2
PEER_TOPK = 16

LANES = 128
SUBLANES = 8
VMEM_LIMIT = 48 * 1024 * 1024
NEG_BIG = -1e30
SEL_NEG = -1e9

TOK_TILE = 512
Q_TILE = 2 * SEL_BLOCK
K_TILE = 2 * SEL_BLOCK
S_CHUNK = 4 * K_TILE
PEER_TOK = 512
PEER_EXP = 1024
PEER_FRONT_TOK = 256


def _cparams(sem, flags=None):
    return pltpu.CompilerParams(dimension_semantics=sem, vmem_limit_bytes=VMEM_LIMIT, flags=flags)


def _dot(a, b):
    return jnp.dot(a, b, preferred_element_type=F32)


def _dot_nt(a, b):
    return lax.dot_general(a, b, (((1,), (1,)), ((), ())), preferred_element_type=F32)


def _split3(a):
    a1 = a.astype(BF16)
    r1 = a - a1.astype(F32)
    a2 = r1.astype(BF16)
    a3 = (r1 - a2.astype(F32)).astype(BF16)
    return a1, a2, a3


def _dot_exact_lhs(a, b01):
    a1, a2, a3 = _split3(a)
    return _dot(a1, b01) + _dot(a2, b01) + _dot(a3, b01)


def _dot_nt_exact_lhs(a, b01):
    a1, a2, a3 = _split3(a)
    return _dot_nt(a1, b01) + _dot_nt(a2, b01) + _dot_nt(a3, b01)


def _dot_exact_rhs(a01, b):
    b1, b2, b3 = _split3(b)
    return _dot(a01, b1) + _dot(a01, b2) + _dot(a01, b3)


def _dot_nt_exact_rhs(a01, b):
    b1, b2, b3 = _split3(b)
    return _dot_nt(a01, b1) + _dot_nt(a01, b2) + _dot_nt(a01, b3)


def _softplus(x):
    return jnp.maximum(x, 0.0) + jnp.log(1.0 + jnp.exp(-jnp.abs(x)))


def _sigmoid(x):
    return 1.0 / (1.0 + jnp.exp(-x))


def _gelu_tanh(x):
    return 0.5 * x * (1.0 + jnp.tanh(0.7978845608028654 * (x + 0.044715 * (x * x * x))))


def _gelu_sigmoid(x):
    c1 = -2.0 * 0.7978845608028654 * 1.4426950408889634
    u = x * (c1 + (c1 * 0.044715) * (x * x))
    return x / (1.0 + jnp.exp2(u))


def _rope128(p, cos, sa, sb):
    return p * cos + pltpu.roll(p, LANES - ROPE_DIM // 2, 1) * sa + pltpu.roll(p, ROPE_DIM // 2, 1) * sb


def _in_proj_kernel(seq_tiles, x_ref, g_ref, w_ref, wdt_ref, cos_ref, sa_ref, sb_ref, gdb_ref, dtb_ref,
                    q_ref, kc_ref, vc_ref, ks_ref, vs_ref, kw_ref, vw_ref, gd_ref, dtt_ref, z_ref, xbc_ref):
    tm = x_ref.shape[0]
    x = x_ref[...]
    h = x * lax.rsqrt(jnp.mean(x * x, axis=-1, keepdims=True) + EPS) * g_ref[...]
    hb = h.astype(BF16)
    pos0 = pl.multiple_of((pl.program_id(0) % seq_tiles) * tm, tm)
    cos = cos_ref[pl.ds(pos0, tm), :]
    sa = sa_ref[pl.ds(pos0, tm), :]
    sb = sb_ref[pl.ds(pos0, tm), :]
    lane = lax.broadcasted_iota(jnp.int32, (tm, LANES), 1)
    low = lane < HEAD_DIM

    def proj(c0, width):
        return _dot(hb, w_ref[:, c0:c0 + width])

    def halves(p):
        return p, pltpu.roll(p, HEAD_DIM, 1)

    pq = proj(0, NSA_WIDTH)
    for s in range(NSA_WIDTH // LANES):
        srcs = halves(_rope128(pq[:, s * LANES:(s + 1) * LANES], cos, sa, sb) * (HEAD_DIM ** -0.5))
        for half in range(2):
            head = 2 * s + half
            q_ref[:, head * LANES:(head + 1) * LANES] = jnp.where(low, srcs[half], 0.0).astype(BF16)
    c0 = NSA_WIDTH
    kc_ref[...] = _rope128(proj(c0, LANES), cos, sa, sb).astype(BF16)
    vc_ref[...] = proj(c0 + LANES, LANES).astype(BF16)
    c0 += 2 * LANES
    blk_of_row = (pos0 + lax.broadcasted_iota(jnp.int32, (tm, LANES), 0)) // SEL_BLOCK
    onehot = jnp.where(lane - HEAD_DIM == blk_of_row, 1.0, 0.0)
    for ref, rope, fill in ((ks_ref, True, onehot), (vs_ref, False, 1.0), (kw_ref, True, 0.0), (vw_ref, False, 1.0)):
        p = proj(c0, LANES)
        if rope:
            p = _rope128(p, cos, sa, sb)
        for g, src in enumerate(halves(p)):
            ref[:, g * LANES:(g + 1) * LANES] = jnp.where(low, src, fill).astype(BF16)
        c0 += LANES
    for s in range(2):
        p = proj(c0, LANES) + gdb_ref[:, s * LANES:(s + 1) * LANES]
        act = jnp.where(lane < 3 * NSA_REP, _sigmoid(p),
                        jnp.where((lane >= 16) & (lane < 16 + SSD_REP), _softplus(p), 0.0))
        gd_ref[:, s * LANES:(s + 1) * LANES] = act
        c0 += LANES
    z_ref[...] = proj(c0, SSD_WIDTH)
    c0 += SSD_WIDTH
    xbc_ref[...] = proj(c0, SSD_CONV_DIM)
    dtt_ref[...] = _softplus(_dot_nt(wdt_ref[...], hb) + dtb_ref[:, 0:1])


def _rope_tables(s):
    half = ROPE_DIM // 2
    inv = jnp.power(ROPE_THETA, -jnp.arange(half, dtype=F32) * 2.0 / ROPE_DIM)
    ang = jnp.arange(s).astype(F32)[:, None] * inv[None, :]
    cos, sin = jnp.cos(ang), jnp.sin(ang)
    zeros = jnp.zeros((s, HEAD_DIM - ROPE_DIM), F32)
    cos64 = jnp.concatenate([cos, cos, jnp.ones((s, HEAD_DIM - ROPE_DIM), F32)], axis=1)
    sa64 = jnp.concatenate([-sin, jnp.zeros_like(sin), zeros], axis=1)
    sb64 = jnp.concatenate([jnp.zeros_like(sin), sin, zeros], axis=1)
    return tuple(jnp.concatenate([t, t], axis=1) for t in (cos64, sa64, sb64))


def _in_proj(xt, attn_g, w_in, dt_bias, s):
    t = xt.shape[0]
    tm = min(TOK_TILE, s)
    o_gl = NSA_WIDTH + 6 * KV_WIDTH
    o_z = o_gl + 3 * NSA_HEADS
    o_xbc = o_z + SSD_WIDTH
    o_dt = o_xbc + SSD_CONV_DIM
    gd_cols, gd_bias = [], []
    for g in range(NSA_KV_GROUPS):
        gates = w_in[:, o_gl + 3 * NSA_REP * g:o_gl + 3 * NSA_REP * (g + 1)]
        dts = w_in[:, o_dt + SSD_REP * g:o_dt + SSD_REP * (g + 1)]
        gd_cols += [gates, jnp.zeros((D_MODEL, 16 - 3 * NSA_REP), F32), dts,
                    jnp.zeros((D_MODEL, LANES - 16 - SSD_REP), F32)]
        gd_bias += [jnp.zeros((16,), F32), dt_bias[SSD_REP * g:SSD_REP * (g + 1)],
                    jnp.zeros((LANES - 16 - SSD_REP,), F32)]
    w_main = jnp.concatenate([w_in[:, :o_gl]] + gd_cols + [w_in[:, o_z:o_dt]], axis=1).astype(BF16)
    gdb = jnp.concatenate(gd_bias)[None, :]
    wdt_rows, dtb_rows = [], []
    for g in range(SSD_GROUPS):
        wdt_rows += [w_in[:, o_dt + SSD_REP * g:o_dt + SSD_REP * (g + 1)].T,
                     jnp.zeros((SUBLANES - SSD_REP, D_MODEL), F32)]
        dtb_rows += [dt_bias[SSD_REP * g:SSD_REP * (g + 1)], jnp.zeros((SUBLANES - SSD_REP,), F32)]
    wdt = jnp.concatenate(wdt_rows, axis=0).astype(BF16)
    dtb = jnp.broadcast_to(jnp.concatenate(dtb_rows)[:, None], (2 * SUBLANES, LANES))
    cos, sa, sb = _rope_tables(s)
    n_main = w_main.shape[1]

    def full(shape):
        return pl.BlockSpec(shape, lambda i: (0, 0))

    def tok(width):
        return pl.BlockSpec((tm, width), lambda i: (i, 0))

    kv_widths = [LANES, LANES] + [NSA_KV_GROUPS * LANES] * 4
    out_shapes = ([jax.ShapeDtypeStruct((t, NSA_HEADS * LANES), BF16)]
                  + [jax.ShapeDtypeStruct((t, w), BF16) for w in kv_widths]
                  + [jax.ShapeDtypeStruct((t, 2 * LANES), F32),
                     jax.ShapeDtypeStruct((2 * SUBLANES, t), F32),
                     jax.ShapeDtypeStruct((t, SSD_WIDTH), F32),
                     jax.ShapeDtypeStruct((t, SSD_CONV_DIM), F32)])
    out_specs = ([tok(NSA_HEADS * LANES)] + [tok(w) for w in kv_widths]
                 + [tok(2 * LANES), pl.BlockSpec((2 * SUBLANES, tm), lambda i: (0, i)),
                    tok(SSD_WIDTH), tok(SSD_CONV_DIM)])
    return pl.pallas_call(
        functools.partial(_in_proj_kernel, s // tm),
        name="in_proj",
        grid=(t // tm,),
        in_specs=[tok(D_MODEL), full((1, D_MODEL)), full((D_MODEL, n_main)), full((2 * SUBLANES, D_MODEL)),
                  full((s, LANES)), full((s, LANES)), full((s, LANES)), full((1, 2 * LANES)),
                  full((2 * SUBLANES, LANES))],
        out_specs=out_specs,
        out_shape=out_shapes,
        compiler_params=_cparams(("parallel",)),
    )(xt, attn_g[None, :], w_main, wdt, cos, sa, sb, gdb, dtb)


def _compress_kernel(kv_ref, w1a_ref, w1b_ref, pos_ref, w1_ref, b1_ref, w2_ref, out_ref):
    kv = kv_ref[0]
    bias = _dot(pos_ref[...], w1_ref[...])[0:1, :] + b1_ref[...]
    for g in range(NSA_KV_GROUPS):
        first = _dot(kv, w1a_ref[g])
        second = _dot(kv, w1b_ref[g])
        nxt = pltpu.roll(second, second.shape[0] - 1, 0)
        hid = _gelu_tanh(first + nxt + bias)
        out_ref[0, :, g * LANES:(g + 1) * LANES] = _dot(hid.astype(BF16), w2_ref[...]).astype(BF16)


def _compress(kv, pos_emb, w1, b1, w2, b, s):
    nch = s // CMP_STRIDE
    kvf = kv.reshape(b, nch, CMP_STRIDE * LANES)
    w1r = w1.reshape(CMP_BLOCK, HEAD_DIM, CMP_HIDDEN)
    zeros = jnp.zeros((CMP_STRIDE, HEAD_DIM, CMP_HIDDEN), F32)

    def expand(w_half, g):
        parts = [w_half, zeros] if g == 0 else [zeros, w_half]
        return jnp.concatenate(parts, axis=1).reshape(CMP_STRIDE * LANES, CMP_HIDDEN)

    w1a = jnp.stack([expand(w1r[:CMP_STRIDE], g) for g in range(NSA_KV_GROUPS)]).astype(BF16)
    w1b = jnp.stack([expand(w1r[CMP_STRIDE:], g) for g in range(NSA_KV_GROUPS)]).astype(BF16)
    w2e = jnp.concatenate([w2, jnp.zeros((CMP_HIDDEN, LANES - HEAD_DIM), F32)], axis=1).astype(BF16)
    pos = jnp.zeros((SUBLANES, CMP_BLOCK * HEAD_DIM), F32).at[0].set(pos_emb.reshape(-1)).astype(BF16)
    return pl.pallas_call(
        _compress_kernel,
        name="compress",
        grid=(b,),
        in_specs=[pl.BlockSpec((1, nch, CMP_STRIDE * LANES), lambda i: (i, 0, 0)),
                  pl.BlockSpec((2, CMP_STRIDE * LANES, CMP_HIDDEN), lambda i: (0, 0, 0)),
                  pl.BlockSpec((2, CMP_STRIDE * LANES, CMP_HIDDEN), lambda i: (0, 0, 0)),
                  pl.BlockSpec((SUBLANES, CMP_BLOCK * HEAD_DIM), lambda i: (0, 0)),
                  pl.BlockSpec((CMP_BLOCK * HEAD_DIM, CMP_HIDDEN), lambda i: (0, 0)),
                  pl.BlockSpec((1, CMP_HIDDEN), lambda i: (0, 0)),
                  pl.BlockSpec((CMP_HIDDEN, LANES), lambda i: (0, 0))],
        out_specs=pl.BlockSpec((1, nch, NSA_KV_GROUPS * LANES), lambda i: (i, 0, 0)),
        out_shape=jax.ShapeDtypeStruct((b, nch, NSA_KV_GROUPS * LANES), BF16),
        compiler_params=_cparams(("parallel",)),
    )(kvf, w1a, w1b, pos, w1.astype(BF16), b1[None, :], w2e)


def _lane_tiles(a):
    return [a[:, k * LANES:(k + 1) * LANES] for k in range(a.shape[1] // LANES)]


def _max_tiles(macc, s):
    for tile in _lane_tiles(s):
        macc = jnp.maximum(macc, tile)
    return macc


def _exp_tiles(s, m_b):
    return jnp.concatenate([jnp.exp(tile - m_b) for tile in _lane_tiles(s)], axis=1).astype(BF16)


def _normalize(acc):
    return acc / pltpu.roll(acc, HEAD_DIM, 1)


def _block_rank(imp):
    nblk, tq = imp.shape
    sub = lax.broadcasted_iota(jnp.int32, (SUBLANES, tq), 0)
    groups = [imp[SUBLANES * v:SUBLANES * (v + 1)] for v in range(nblk // SUBLANES)]
    cnt = [jnp.zeros((SUBLANES, tq), F32) for _ in groups]
    for j in range(nblk):
        row = imp[j:j + 1, :]
        for v, grp in enumerate(groups):
            if SUBLANES * v > j:
                ahead = row >= grp
            elif SUBLANES * (v + 1) <= j:
                ahead = row > grp
            else:
                ahead = (row > grp) | ((row == grp) & (sub > j - SUBLANES * v))
            cnt[v] = cnt[v] + jnp.where(ahead, 1.0, 0.0)
    return jnp.concatenate(cnt, axis=0)


def _nsa_kernel(q_ref, kc_ref, vc_ref, ks_ref, vs_ref, kw_ref, vw_ref, gd_ref, ovl_ref, o_ref, s_ref):
    i = pl.program_id(2)
    tq = Q_TILE
    rows = NSA_REP * tq
    ncmp = kc_ref.shape[1]
    nblk = ovl_ref.shape[0]
    q = jnp.concatenate([q_ref[:, r * LANES:(r + 1) * LANES] for r in range(NSA_REP)], axis=0)
    t_row = i * tq + lax.broadcasted_iota(jnp.int32, (rows, 1), 0) % tq

    s_c = _dot_nt(q, kc_ref[0])
    cmp_end = lax.broadcasted_iota(jnp.int32, (rows, ncmp), 1) * CMP_STRIDE + (CMP_BLOCK - 1)
    mask_c = cmp_end <= t_row
    s_c = jnp.where(mask_c, s_c, NEG_BIG)
    e_c = jnp.exp(s_c - jnp.max(s_c, axis=-1, keepdims=True))
    p_c = jnp.where(mask_c, e_c / jnp.sum(e_c, axis=-1, keepdims=True), 0.0)
    o_cmp = _dot(p_c.astype(BF16), vc_ref[0])

    p_sum = p_c[0:tq]
    for r in range(1, NSA_REP):
        p_sum = p_sum + p_c[r * tq:(r + 1) * tq]
    imp = _dot_nt_exact_rhs(ovl_ref[...], p_sum)
    blk = lax.broadcasted_iota(jnp.int32, (nblk, tq), 0)
    cur = (i * tq + lax.broadcasted_iota(jnp.int32, (nblk, tq), 1)) // SEL_BLOCK
    forced = (blk == 0) | (blk == cur) | (blk == cur - 1)
    valid = blk <= cur
    imp = jnp.where(forced, 1e9, jnp.where(valid, imp, -1.0))

    sel_t = jnp.where((_block_rank(imp) < float(min(SEL_TOP_N, nblk))) & valid, 1.0, 0.0)
    ones_lo = jnp.ones((HEAD_DIM, tq), F32)
    sel_pad = jnp.concatenate([ones_lo, sel_t] + ([jnp.ones((HEAD_DIM - nblk, tq), F32)] if nblk < HEAD_DIM else []),
                              axis=0).astype(BF16)
    eye = jnp.where(lax.broadcasted_iota(jnp.int32, (tq, tq), 0) == lax.broadcasted_iota(jnp.int32, (tq, tq), 1),
                    1.0, 0.0).astype(BF16)
    negb = ((_dot_nt(eye, sel_pad) - 1.0) * (-SEL_NEG)).astype(BF16)
    q2 = q + jnp.concatenate([negb] * NSA_REP, axis=0)

    jd = (i * tq) // K_TILE
    nfull = jd // (S_CHUNK // K_TILE)
    lane_c = lax.broadcasted_iota(jnp.int32, (rows, S_CHUNK), 1)

    def scores(c):
        k0 = pl.multiple_of(c * S_CHUNK, S_CHUNK)
        return _dot_nt(q2, ks_ref[pl.ds(k0, S_CHUNK), :])

    def pass_max(c, macc):
        s = scores(c)
        s_ref[c] = s
        return _max_tiles(macc, s)

    macc = lax.fori_loop(0, nfull, pass_max, jnp.full((rows, LANES), NEG_BIG, F32))
    s_last = jnp.where(nfull * S_CHUNK + lane_c <= t_row, scores(nfull), NEG_BIG)
    macc = _max_tiles(macc, s_last)
    m_b = jnp.broadcast_to(jnp.max(macc, axis=-1, keepdims=True), (rows, LANES))

    def pass_sum(c, acc):
        k0 = pl.multiple_of(c * S_CHUNK, S_CHUNK)
        return acc + _dot(_exp_tiles(s_ref[c], m_b), vs_ref[pl.ds(k0, S_CHUNK), :])

    acc = lax.fori_loop(0, nfull, pass_sum, jnp.zeros((rows, LANES), F32))
    k_last = pl.multiple_of(nfull * S_CHUNK, S_CHUNK)
    o_sel = _normalize(acc + _dot(_exp_tiles(s_last, m_b), vs_ref[pl.ds(k_last, S_CHUNK), :]))

    n_wk = WINDOW + K_TILE
    k0 = pl.multiple_of(jnp.maximum(jd - WINDOW // K_TILE, 0) * K_TILE, K_TILE)
    diff = t_row - (k0 + lax.broadcasted_iota(jnp.int32, (rows, n_wk), 1))
    s_w = jnp.where((diff >= 0) & (diff < WINDOW), _dot_nt(q2, kw_ref[pl.ds(k0, n_wk), :]), NEG_BIG)
    m_w = jnp.max(_max_tiles(jnp.full((rows, LANES), NEG_BIG, F32), s_w), axis=-1, keepdims=True)
    p_w = _exp_tiles(s_w, jnp.broadcast_to(m_w, (rows, LANES)))
    o_win = _normalize(_dot(p_w, vw_ref[pl.ds(k0, n_wk), :]))

    gd = gd_ref[...]
    low = lax.broadcasted_iota(jnp.int32, (tq, LANES), 1) < HEAD_DIM
    heads = []
    for r in range(NSA_REP):
        rs = slice(r * tq, (r + 1) * tq)
        heads.append(gd[:, 3 * r:3 * r + 1] * o_cmp[rs] + gd[:, 3 * r + 1:3 * r + 2] * o_sel[rs]
                     + gd[:, 3 * r + 2:3 * r + 3] * o_win[rs])
    for sidx in range(NSA_REP // 2):
        o_ref[:, sidx * LANES:(sidx + 1) * LANES] = jnp.where(
            low, heads[2 * sidx], pltpu.roll(heads[2 * sidx + 1], HEAD_DIM, 1))


def _nsa(q, kcmp, vcmp, ks, vs, kw, vw, gd, b, s):
    t = q.shape[0]
    nq = s // Q_TILE
    ncmp = s // CMP_STRIDE
    nblk = s // SEL_BLOCK
    cs = np.arange(ncmp) * CMP_STRIDE
    ss = np.arange(nblk) * SEL_BLOCK
    overlap = (cs[:, None] < ss[None, :] + SEL_BLOCK) & (cs[:, None] + CMP_BLOCK > ss[None, :])
    overlap[ncmp - 1, :] = False
    ovl_t = jnp.asarray(overlap.T, BF16)

    assert nblk <= HEAD_DIM and s % S_CHUNK == 0 and s >= WINDOW + K_TILE

    def seq_spec():
        return pl.BlockSpec((s, LANES), lambda bi, g, i: (bi, g))

    return pl.pallas_call(
        _nsa_kernel,
        name="nsa",
        grid=(b, NSA_KV_GROUPS, nq),
        in_specs=[pl.BlockSpec((Q_TILE, NSA_REP * LANES), lambda bi, g, i: (bi * nq + i, g)),
                  pl.BlockSpec((1, ncmp, LANES), lambda bi, g, i: (bi, 0, g)),
                  pl.BlockSpec((1, ncmp, LANES), lambda bi, g, i: (bi, 0, g)),
                  seq_spec(), seq_spec(), seq_spec(), seq_spec(),
                  pl.BlockSpec((Q_TILE, LANES), lambda bi, g, i: (bi * nq + i, g)),
                  pl.BlockSpec((nblk, ncmp), lambda bi, g, i: (0, 0))],
        out_specs=pl.BlockSpec((Q_TILE, NSA_REP * HEAD_DIM), lambda bi, g, i: (bi * nq + i, g)),
        out_shape=jax.ShapeDtypeStruct((t, NSA_WIDTH), F32),
        scratch_shapes=[pltpu.VMEM((s // S_CHUNK, NSA_REP * Q_TILE, S_CHUNK), F32)],
        compiler_params=_cparams(("parallel", "parallel", "arbitrary")),
    )(q, kcmp, vcmp, ks, vs, kw, vw, gd, ovl_t)


def _conv_silu(cur_ref, ext_ref, tail_ref, w_ref, b_ref):
    l = cur_ref.shape[0]
    cur = cur_ref[...]
    ext_ref[0:SUBLANES, :] = tail_ref[...]
    ext_ref[SUBLANES:, :] = cur
    tail_ref[...] = cur[l - SUBLANES:, :]
    acc = jnp.zeros(cur.shape, F32) + b_ref[...]
    for k in range(SSD_CONV):
        off = SUBLANES - (SSD_CONV - 1) + k
        acc = acc + ext_ref[off:off + l, :] * w_ref[k:k + 1, :]
    return acc * _sigmoid(acc)


def _ssd_kernel(xs_ref, bm_ref, cm_ref, wx_ref, wb_ref, wc_ref, bx_ref, bb_ref, bc_ref, z_ref, gd_ref, dtt_ref,
                alane_ref, asub_ref, dskip_ref, ng_ref, tril_ref, edt_ref, eseg_ref, o_ref,
                tx_ref, tb_ref, tc_ref, ex_ref, eb_ref, ec_ref, st_ref):
    c = pl.program_id(2)
    l = xs_ref.shape[0]

    @pl.when(c == 0)
    def _():
        tx_ref[...] = jnp.zeros(tx_ref.shape, F32)
        tb_ref[...] = jnp.zeros(tb_ref.shape, F32)
        tc_ref[...] = jnp.zeros(tc_ref.shape, F32)
        st_ref[...] = jnp.zeros(st_ref.shape, F32)

    xs = _conv_silu(xs_ref, ex_ref, tx_ref, wx_ref, bx_ref)
    bm = _conv_silu(bm_ref, eb_ref, tb_ref, wb_ref, bb_ref)
    cm = _conv_silu(cm_ref, ec_ref, tc_ref, wc_ref, bc_ref)
    gd = gd_ref[...]
    tril = tril_ref[...]
    adt_c = gd * (-jnp.exp(alane_ref[0]))
    acum_c = _dot_exact_rhs(tril, adt_c)
    dt_full = _dot_exact_lhs(gd, edt_ref[...])
    ac_full = _dot_exact_lhs(acum_c, edt_ref[...])
    ac_seg = _dot_exact_lhs(acum_c, eseg_ref[...])
    adt_t = dtt_ref[...] * (-jnp.exp(asub_ref[0]))
    acum_t = _dot_nt_exact_lhs(adt_t, tril)

    xdt = xs * dt_full
    cmb = cm.astype(BF16)
    cb = _dot_nt(cmb, bm.astype(BF16))
    causal = lax.broadcasted_iota(jnp.int32, (l, l), 0) >= lax.broadcasted_iota(jnp.int32, (l, l), 1)
    head_of_lane = lax.broadcasted_iota(jnp.int32, xs.shape, 1) // SSD_HEAD_DIM
    y = jnp.zeros(xs.shape, F32)
    for r in range(SSD_REP):
        seg = jnp.exp(jnp.where(causal, ac_seg[:, r * l:(r + 1) * l] - acum_t[r:r + 1, :], NEG_BIG))
        x_r = jnp.where(head_of_lane == r, xdt, 0.0).astype(BF16)
        y = y + _dot((cb * seg).astype(BF16), x_r)
    ac_last = ac_full[l - 1:l, :]
    state = st_ref[...]
    y = y + _dot(cmb, state.astype(BF16)) * jnp.exp(ac_full)
    decayed = (xdt * jnp.exp(ac_last - ac_full)).astype(BF16)
    st_ref[...] = state * jnp.exp(ac_last) + _dot(bm.T.astype(BF16), decayed)
    y = y + xs * dskip_ref[...]
    z = z_ref[...]
    yz = y * (z * _sigmoid(z))
    o_ref[...] = yz * lax.rsqrt(jnp.mean(yz * yz, axis=-1, keepdims=True) + EPS) * ng_ref[...]


def _ssd(xbc, z, gd, dtt, conv_w, conv_b, a_log, d_skip, norm_g, b, s):
    t = xbc.shape[0]
    l = SSD_CHUNK
    nc = s // l
    gw = SSD_WIDTH // SSD_GROUPS
    xcols = gw // LANES
    b_blk0 = SSD_WIDTH // LANES
    c_blk0 = b_blk0 + SSD_GROUPS * SSD_STATE // LANES
    conv_b2 = conv_b[None, :]
    a_grp = a_log.reshape(SSD_GROUPS, SSD_REP)
    alane = jnp.zeros((SSD_GROUPS, 1, LANES), F32).at[:, 0, 16:16 + SSD_REP].set(a_grp)
    asub = jnp.zeros((SSD_GROUPS, SUBLANES, LANES), F32).at[:, :SSD_REP, :].set(
        jnp.broadcast_to(a_grp[:, :, None], (SSD_GROUPS, SSD_REP, LANES)))
    dskip = jnp.repeat(d_skip, SSD_HEAD_DIM)[None, :]
    ng = norm_g[None, :]
    ii = np.arange(l)
    tril = jnp.asarray(ii[:, None] >= ii[None, :], BF16)
    edt = np.zeros((LANES, gw), np.float32)
    eseg = np.zeros((LANES, SSD_REP * l), np.float32)
    for r in range(SSD_REP):
        edt[16 + r, r * SSD_HEAD_DIM:(r + 1) * SSD_HEAD_DIM] = 1.0
        eseg[16 + r, r * l:(r + 1) * l] = 1.0
    edt, eseg = jnp.asarray(edt, BF16), jnp.asarray(eseg, BF16)

    def rowblk(width, col):
        return pl.BlockSpec((l, width), lambda bi, g, c: (bi * nc + c, col(g)))

    def const(shape, col=None):
        if col is None:
            return pl.BlockSpec(shape, lambda bi, g, c: (0,) * len(shape))
        return pl.BlockSpec(shape, lambda bi, g, c: (0, col(g)))

    return pl.pallas_call(
        _ssd_kernel,
        name="ssd",
        grid=(b, SSD_GROUPS, nc),
        in_specs=[rowblk(gw, lambda g: g), rowblk(LANES, lambda g: b_blk0 + g), rowblk(LANES, lambda g: c_blk0 + g),
                  const((SSD_CONV, gw), lambda g: g), const((SSD_CONV, LANES), lambda g: b_blk0 + g),
                  const((SSD_CONV, LANES), lambda g: c_blk0 + g),
                  const((1, gw), lambda g: g), const((1, LANES), lambda g: b_blk0 + g),
                  const((1, LANES), lambda g: c_blk0 + g),
                  rowblk(gw, lambda g: g), rowblk(LANES, lambda g: g),
                  pl.BlockSpec((SUBLANES, l), lambda bi, g, c: (g, bi * nc + c)),
                  pl.BlockSpec((1, 1, LANES), lambda bi, g, c: (g, 0, 0)),
                  pl.BlockSpec((1, SUBLANES, LANES), lambda bi, g, c: (g, 0, 0)),
                  const((1, gw), lambda g: g), const((1, gw), lambda g: g),
                  const((l, l)), const((LANES, gw)), const((LANES, SSD_REP * l))],
        out_specs=rowblk(gw, lambda g: g),
        out_shape=jax.ShapeDtypeStruct((t, SSD_WIDTH), F32),
        scratch_shapes=[pltpu.VMEM((SUBLANES, gw), F32), pltpu.VMEM((SUBLANES, LANES), F32),
                        pltpu.VMEM((SUBLANES, LANES), F32),
                        pltpu.VMEM((l + SUBLANES, gw), F32), pltpu.VMEM((l + SUBLANES, LANES), F32),
                        pltpu.VMEM((l + SUBLANES, LANES), F32),
                        pltpu.VMEM((SSD_STATE, gw), F32)],
        compiler_params=_cparams(("parallel", "parallel", "arbitrary")),
    )(xbc, xbc, xbc, conv_w, conv_w, conv_w, conv_b2, conv_b2, conv_b2, z, gd, dtt,
      alane, asub, dskip, ng, tril, edt, eseg)


def _rms(v, g):
    return v * lax.rsqrt(jnp.mean(v * v, axis=-1, keepdims=True) + EPS) * g


def _out_proj_kernel(x_ref, on_ref, os_ref, ng_ref, w1_ref, w2_ref, fg_ref, x1_ref, xnt_ref):
    onn = _rms(on_ref[...], ng_ref[...]).astype(BF16)
    x1 = x_ref[...] + _dot(onn, w1_ref[...]) + _dot(os_ref[...].astype(BF16), w2_ref[...])
    x1_ref[...] = x1
    xnt_ref[...] = _rms(x1, fg_ref[...]).T.astype(BF16)


def _out_proj(xt, o_nsa, o_ssd, nsa_g, w_out, ffn_g):
    t = xt.shape[0]
    tm = min(TOK_TILE, t)

    def tok(width):
        return pl.BlockSpec((tm, width), lambda i: (i, 0))

    def full(shape):
        return pl.BlockSpec(shape, lambda i: (0, 0))

    wb = w_out.astype(BF16)
    return pl.pallas_call(
        _out_proj_kernel,
        name="out_proj",
        grid=(t // tm,),
        in_specs=[tok(D_MODEL), tok(NSA_WIDTH), tok(SSD_WIDTH), full((1, NSA_WIDTH)),
                  full((NSA_WIDTH, D_MODEL)), full((SSD_WIDTH, D_MODEL)), full((1, D_MODEL))],
        out_specs=[tok(D_MODEL), pl.BlockSpec((D_MODEL, tm), lambda i: (0, i))],
        out_shape=[jax.ShapeDtypeStruct((t, D_MODEL), F32), jax.ShapeDtypeStruct((D_MODEL, t), BF16)],
        compiler_params=_cparams(("parallel",)),
    )(xt, o_nsa, o_ssd, nsa_g[None, :], wb[:NSA_WIDTH], wb[NSA_WIDTH:], ffn_g[None, :])


def _top_sorted(e, k):
    rows = lax.broadcasted_iota(jnp.int32, (k, e.shape[1]), 0)
    out = jnp.zeros((k, e.shape[1]), F32)
    cur = e
    for j in range(k):
        mk = jnp.max(cur, axis=0, keepdims=True)
        out = jnp.where(rows == j, mk, out)
        cur = jnp.where(cur == mk, -1.0, cur)
    return out


def _pair_candidates(a16, b16):
    row8 = lax.broadcasted_iota(jnp.int32, (SUBLANES, a16.shape[1]), 0)
    groups = [a16[0:1] * b16[0:SUBLANES], a16[0:1] * b16[SUBLANES:], a16[1:2] * b16[0:SUBLANES]]
    for a in range(2, SUBLANES):
        groups.append(jnp.where(row8 < PEER_TOPK // (a + 1), a16[a:a + 1] * b16[0:SUBLANES], -1.0))
    groups.append(a16[SUBLANES:] * b16[0:1])
    return jnp.concatenate(groups, axis=0)


def _peer_front_kernel(xnt_ref, wq_ref, keys_ref, a_ref, b_ref, thr_ref):
    xt = xnt_ref[...]
    tm = xt.shape[1]
    row8 = lax.broadcasted_iota(jnp.int32, (PEER_HEADS, tm), 0)
    thr_all = jnp.zeros((PEER_HEADS, tm), F32)
    for h in range(PEER_HEADS):
        qh = _dot(wq_ref[h * PEER_KEY_DIM:(h + 1) * PEER_KEY_DIM, :], xt)
        s1 = _dot(keys_ref[h, 0], qh[:PEER_HALF].astype(BF16))
        s2 = _dot(keys_ref[h, 1], qh[PEER_HALF:].astype(BF16))
        e1 = jnp.exp(s1 - jnp.max(s1, axis=0, keepdims=True))
        e2 = jnp.exp(s2 - jnp.max(s2, axis=0, keepdims=True))
        a16 = _top_sorted(e1, PEER_TOPK)
        b16 = _top_sorted(e2, PEER_TOPK)
        cand = _pair_candidates(a16, b16)
        cur = cand
        zsum = jnp.zeros((1, tm), F32)
        for _ in range(PEER_TOPK):
            tau = jnp.max(cur, axis=0, keepdims=True)
            zsum = zsum + tau
            cur = jnp.where(cur == tau, -1.0, cur)
        rinv = 1.0 / zsum
        cand_scaled = _pair_candidates(a16, b16 * rinv)
        thr = jnp.min(jnp.where(cand >= tau, cand_scaled, 3e38), axis=0, keepdims=True)
        a_ref[h * PEER_N_KEYS:(h + 1) * PEER_N_KEYS, :] = e1
        b_ref[h * PEER_N_KEYS:(h + 1) * PEER_N_KEYS, :] = e2 * rinv
        thr_all = jnp.where(row8 == h, thr, thr_all)
    thr_ref[...] = thr_all


def _peer_front(xnt, w_q, keys):
    t = xnt.shape[1]
    tm = min(PEER_FRONT_TOK, t)
    nk = PEER_HEADS * PEER_N_KEYS
    return pl.pallas_call(
        _peer_front_kernel,
        name="peer_front",
        grid=(t // tm,),
        in_specs=[pl.BlockSpec((D_MODEL, tm), lambda i: (0, i)),
                  pl.BlockSpec((PEER_HEADS * PEER_KEY_DIM, D_MODEL), lambda i: (0, 0)),
                  pl.BlockSpec((PEER_HEADS, 2, PEER_N_KEYS, PEER_HALF), lambda i: (0, 0, 0, 0))],
        out_specs=[pl.BlockSpec((nk, tm), lambda i: (0, i)), pl.BlockSpec((nk, tm), lambda i: (0, i)),
                   pl.BlockSpec((PEER_HEADS, tm), lambda i: (0, i))],
        out_shape=[jax.ShapeDtypeStruct((nk, t), F32), jax.ShapeDtypeStruct((nk, t), F32),
                   jax.ShapeDtypeStruct((PEER_HEADS, t), F32)],
        compiler_params=_cparams(("parallel",)),
    )(xnt, w_q.T.astype(BF16), keys.astype(BF16))


def _peer_dense_kernel(xnt_ref, u_ref, vt_ref, a_ref, b_ref, thr_ref, o_ref, hid_ref, act_ref):
    j = pl.program_id(1)
    n_sub = u_ref.shape[0] // PEER_N_KEYS
    n_tok = xnt_ref.shape[1]
    half_keys = PEER_N_KEYS // 2

    @pl.when(j == 0)
    def _():
        o_ref[...] = jnp.zeros(o_ref.shape, F32)

    hid_ref[...] = _dot(u_ref[...], xnt_ref[...])

    def sub(cl, carry):
        e1 = j * n_sub + cl
        r0 = pl.multiple_of(cl * PEER_N_KEYS, PEER_N_KEYS)
        for lt in range(n_tok // LANES):
            cols = slice(lt * LANES, (lt + 1) * LANES)
            for half in range(2):
                gate = jnp.zeros((half_keys, LANES), F32)
                for h in range(PEER_HEADS):
                    a_row = a_ref[pl.ds(h * PEER_N_KEYS + e1, 1), :][:, cols]
                    b0 = h * PEER_N_KEYS + half * half_keys
                    p = b_ref[b0:b0 + half_keys, cols] * a_row
                    gate = gate + jnp.where(p >= thr_ref[h:h + 1, cols], p, 0.0)
                rr = pl.ds(r0 + half * half_keys, half_keys)
                act_ref[rr, cols] = (_gelu_sigmoid(hid_ref[rr, cols]) * gate).astype(BF16)
        return carry

    lax.fori_loop(0, n_sub, sub, 0)
    o_ref[...] += _dot(vt_ref[...], act_ref[...])


def _peer_dense(xnt, u, vt, a, b, thr):
    t = xnt.shape[1]
    tm = min(PEER_TOK, t)
    ne = u.shape[0]
    te = PEER_EXP
    nk = PEER_HEADS * PEER_N_KEYS
    return pl.pallas_call(
        _peer_dense_kernel,
        name="peer_dense",
        grid=(t // tm, ne // te),
        in_specs=[pl.BlockSpec((D_MODEL, tm), lambda i, j: (0, i)),
                  pl.BlockSpec((te, D_MODEL), lambda i, j: (j, 0)),
                  pl.BlockSpec((D_MODEL, te), lambda i, j: (0, j)),
                  pl.BlockSpec((nk, tm), lambda i, j: (0, i)),
                  pl.BlockSpec((nk, tm), lambda i, j: (0, i)),
                  pl.BlockSpec((PEER_HEADS, tm), lambda i, j: (0, i))],
        out_specs=pl.BlockSpec((D_MODEL, tm), lambda i, j: (0, i)),
        out_shape=jax.ShapeDtypeStruct((D_MODEL, t), F32),
        scratch_shapes=[pltpu.VMEM((te, tm), F32), pltpu.VMEM((te, tm), BF16)],
        compiler_params=_cparams(("parallel", "arbitrary")),
    )(xnt, u, vt, a, b, thr)


def _final_kernel(x1_ref, pt_ref, g_ref, o_ref):
    o_ref[...] = _rms(x1_ref[...] + pt_ref[...].T, g_ref[...])


def _residual_norm(x1, peer_t, g):
    t = x1.shape[0]
    tm = min(TOK_TILE, t)
    return pl.pallas_call(
        _final_kernel,
        name="residual_norm",
        grid=(t // tm,),
        in_specs=[pl.BlockSpec((tm, D_MODEL), lambda i: (i, 0)), pl.BlockSpec((D_MODEL, tm), lambda i: (0, i)),
                  pl.BlockSpec((1, D_MODEL), lambda i: (0, 0))],
        out_specs=pl.BlockSpec((tm, D_MODEL), lambda i: (i, 0)),
        out_shape=jax.ShapeDtypeStruct((t, D_MODEL), F32),
        compiler_params=_cparams(("parallel",)),
    )(x1, peer_t, g[None, :])


def kernel(x, attn_norm_g, w_in, cmp_pos_k, cmp_w1_k, cmp_b1_k, cmp_w2_k, cmp_pos_v, cmp_w1_v, cmp_b1_v, cmp_w2_v,
           conv_w, conv_b, dt_bias, a_log, d_skip, ssd_norm_g, nsa_norm_g, w_out, ffn_norm_g, peer_w_q, peer_keys,
           peer_u, peer_v, final_norm_g):
    b, s, d = x.shape
    xt = x.reshape(b * s, d)
    assert attn_norm_g.shape[0] == 1, "single-layer block"
    for layer in range(1):
        q, kc, vc, ks, vs, kw, vw, gd, dtt, z, xbc = _in_proj(xt, attn_norm_g[layer], w_in[layer], dt_bias[layer], s)
        kcmp = _compress(kc, cmp_pos_k[layer], cmp_w1_k[layer], cmp_b1_k[layer], cmp_w2_k[layer], b, s)
        vcmp = _compress(vc, cmp_pos_v[layer], cmp_w1_v[layer], cmp_b1_v[layer], cmp_w2_v[layer], b, s)
        o_nsa = _nsa(q, kcmp, vcmp, ks, vs, kw, vw, gd, b, s)
        o_ssd = _ssd(xbc, z, gd, dtt, conv_w[layer], conv_b[layer], a_log[layer], d_skip[layer],
                     ssd_norm_g[layer], b, s)
        x1, xnt = _out_proj(xt, o_nsa, o_ssd, nsa_norm_g[layer], w_out[layer], ffn_norm_g[layer])
        a, bb, thr = _peer_front(xnt, peer_w_q[layer], peer_keys[layer])
        peer_t = _peer_dense(xnt, peer_u[layer].astype(BF16), peer_v[layer].T.astype(BF16), a, bb, thr)
    return _residual_norm(x1, peer_t, final_norm_g).reshape(b, s, d)
```

```python
import functools

import numpy as np
import jax
import jax.numpy as jnp
from jax import lax
from jax.experimental import pallas as pl
from jax.experimental.pallas import tpu as pltpu

F32 = jnp.float32
BF16 = jnp.bfloat16

EPS = 1e-6
D_MODEL = 1024
NSA_HEADS = 8
NSA_KV_GROUPS = 2
NSA_REP = NSA_HEADS // NSA_KV_GROUPS
HEAD_DIM = 64
NSA_WIDTH = NSA_HEADS * HEAD_DIM
KV_WIDTH = NSA_KV_GROUPS * HEAD_DIM
CMP_STRIDE = 16
CMP_BLOCK = 32
CMP_HIDDEN = 256
SEL_BLOCK = 64
SEL_TOP_N = 16
WINDOW = 512
ROPE_DIM = HEAD_DIM // 4
ROPE_THETA = 500000.0
SSD_HEADS = 8
SSD_HEAD_DIM = 64
SSD_WIDTH = SSD_HEADS * SSD_HEAD_DIM
SSD_GROUPS = 2
SSD_REP = SSD_HEADS // SSD_GROUPS
SSD_STATE = 128
SSD_CONV = 4
SSD_CHUNK = 128
SSD_CONV_DIM = SSD_WIDTH + 2 * SSD_GROUPS * SSD_STATE
PEER_HEADS = 8
PEER_N_KEYS = 128
PEER_N_EXPERTS = PEER_N_KEYS * PEER_N_KEYS
PEER_KEY_DIM = 256
PEER_HALF = PEER_KEY_DIM // 2
PEER_TOPK = 16

LANES = 128
SUBLANES = 8
VMEM_LIMIT = 48 * 1024 * 1024
NEG_BIG = -1e30
SEL_NEG = -1e9

TOK_TILE = 512
Q_TILE = 2 * SEL_BLOCK
K_TILE = 2 * SEL_BLOCK
S_CHUNK = 4 * K_TILE
PEER_TOK = 512
PEER_EXP = 1024
PEER_FRONT_TOK = 256


def _cparams(sem, flags=None):
    return pltpu.CompilerParams(dimension_semantics=sem, vmem_limit_bytes=VMEM_LIMIT, flags=flags)


def _dot(a, b):
    return jnp.dot(a, b, preferred_element_type=F32)


def _dot_nt(a, b):
    return lax.dot_general(a, b, (((1,), (1,)), ((), ())), preferred_element_type=F32)


def _split3(a):
    a1 = a.astype(BF16)
    r1 = a - a1.astype(F32)
    a2 = r1.astype(BF16)
    a3 = (r1 - a2.astype(F32)).astype(BF16)
    return a1, a2, a3


def _dot_exact_lhs(a, b01):
    a1, a2, a3 = _split3(a)
    return _dot(a1, b01) + _dot(a2, b01) + _dot(a3, b01)


def _dot_nt_exact_lhs(a, b01):
    a1, a2, a3 = _split3(a)
    return _dot_nt(a1, b01) + _dot_nt(a2, b01) + _dot_nt(a3, b01)


def _dot_exact_rhs(a01, b):
    b1, b2, b3 = _split3(b)
    return _dot(a01, b1) + _dot(a01, b2) + _dot(a01, b3)


def _dot_nt_exact_rhs(a01, b):
    b1, b2, b3 = _split3(b)
    return _dot_nt(a01, b1) + _dot_nt(a01, b2) + _dot_nt(a01, b3)


def _softplus(x):
    return jnp.maximum(x, 0.0) + jnp.log(1.0 + jnp.exp(-jnp.abs(x)))


def _sigmoid(x):
    return 1.0 / (1.0 + jnp.exp(-x))


def _gelu_tanh(x):
    return 0.5 * x * (1.0 + jnp.tanh(0.7978845608028654 * (x + 0.044715 * (x * x * x))))


def _gelu_sigmoid(x):
    c1 = -2.0 * 0.7978845608028654 * 1.4426950408889634
    u = x * (c1 + (c1 * 0.044715) * (x * x))
    return x / (1.0 + jnp.exp2(u))


def _rope128(p, cos, sa, sb):
    return p * cos + pltpu.roll(p, LANES - ROPE_DIM // 2, 1) * sa + pltpu.roll(p, ROPE_DIM // 2, 1) * sb


def _in_proj_kernel(seq_tiles, x_ref, g_ref, w_ref, wdt_ref, cos_ref, sa_ref, sb_ref, gdb_ref, dtb_ref,
                    q_ref, kc_ref, vc_ref, ks_ref, vs_ref, kw_ref, vw_ref, gd_ref, dtt_ref, z_ref, xbc_ref):
    tm = x_ref.shape[0]
    x = x_ref[...]
    h = x * lax.rsqrt(jnp.mean(x * x, axis=-1, keepdims=True) + EPS) * g_ref[...]
    hb = h.astype(BF16)
    pos0 = pl.multiple_of((pl.program_id(0) % seq_tiles) * tm, tm)
    cos = cos_ref[pl.ds(pos0, tm), :]
    sa = sa_ref[pl.ds(pos0, tm), :]
    sb = sb_ref[pl.ds(pos0, tm), :]
    lane = lax.broadcasted_iota(jnp.int32, (tm, LANES), 1)
    low = lane < HEAD_DIM

    def proj(c0, width):
        return _dot(hb, w_ref[:, c0:c0 + width])

    def halves(p):
        return p, pltpu.roll(p, HEAD_DIM, 1)

    pq = proj(0, NSA_WIDTH)
    for s in range(NSA_WIDTH // LANES):
        srcs = halves(_rope128(pq[:, s * LANES:(s + 1) * LANES], cos, sa, sb) * (HEAD_DIM ** -0.5))
        for half in range(2):
            head = 2 * s + half
            q_ref[:, head * LANES:(head + 1) * LANES] = jnp.where(low, srcs[half], 0.0).astype(BF16)
    c0 = NSA_WIDTH
    kc_ref[...] = _rope128(proj(c0, LANES), cos, sa, sb).astype(BF16)
    vc_ref[...] = proj(c0 + LANES, LANES).astype(BF16)
    c0 += 2 * LANES
    blk_of_row = (pos0 + lax.broadcasted_iota(jnp.int32, (tm, LANES), 0)) // SEL_BLOCK
    onehot = jnp.where(lane - HEAD_DIM == blk_of_row, 1.0, 0.0)
    for ref, rope, fill in ((ks_ref, True, onehot), (vs_ref, False, 1.0), (kw_ref, True, 0.0), (vw_ref, False, 1.0)):
        p = proj(c0, LANES)
        if rope:
            p = _rope128(p, cos, sa, sb)
        for g, src in enumerate(halves(p)):
            ref[:, g * LANES:(g + 1) * LANES] = jnp.where(low, src, fill).astype(BF16)
        c0 += LANES
    for s in range(2):
        p = proj(c0, LANES) + gdb_ref[:, s * LANES:(s + 1) * LANES]
        act = jnp.where(lane < 3 * NSA_REP, _sigmoid(p),
                        jnp.where((lane >= 16) & (lane < 16 + SSD_REP), _softplus(p), 0.0))
        gd_ref[:, s * LANES:(s + 1) * LANES] = act
        c0 += LANES
    z_ref[...] = proj(c0, SSD_WIDTH)
    c0 += SSD_WIDTH
    xbc_ref[...] = proj(c0, SSD_CONV_DIM)
    dtt_ref[...] = _softplus(_dot_nt(wdt_ref[...], hb) + dtb_ref[:, 0:1])


def _rope_tables(s):
    half = ROPE_DIM // 2
    inv = jnp.power(ROPE_THETA, -jnp.arange(half, dtype=F32) * 2.0 / ROPE_DIM)
    ang = jnp.arange(s).astype(F32)[:, None] * inv[None, :]
    cos, sin = jnp.cos(ang), jnp.sin(ang)
    zeros = jnp.zeros((s, HEAD_DIM - ROPE_DIM), F32)
    cos64 = jnp.concatenate([cos, cos, jnp.ones((s, HEAD_DIM - ROPE_DIM), F32)], axis=1)
    sa64 = jnp.concatenate([-sin, jnp.zeros_like(sin), zeros], axis=1)
    sb64 = jnp.concatenate([jnp.zeros_like(sin), sin, zeros], axis=1)
    return tuple(jnp.concatenate([t, t], axis=1) for t in (cos64, sa64, sb64))


def _in_proj(xt, attn_g, w_in, dt_bias, s):
    t = xt.shape[0]
    tm = min(TOK_TILE, s)
    o_gl = NSA_WIDTH + 6 * KV_WIDTH
    o_z = o_gl + 3 * NSA_HEADS
    o_xbc = o_z + SSD_WIDTH
    o_dt = o_xbc + SSD_CONV_DIM
    gd_cols, gd_bias = [], []
    for g in range(NSA_KV_GROUPS):
        gates = w_in[:, o_gl + 3 * NSA_REP * g:o_gl + 3 * NSA_REP * (g + 1)]
        dts = w_in[:, o_dt + SSD_REP * g:o_dt + SSD_REP * (g + 1)]
        gd_cols += [gates, jnp.zeros((D_MODEL, 16 - 3 * NSA_REP), F32), dts,
                    jnp.zeros((D_MODEL, LANES - 16 - SSD_REP), F32)]
        gd_bias += [jnp.zeros((16,), F32), dt_bias[SSD_REP * g:SSD_REP * (g + 1)],
                    jnp.zeros((LANES - 16 - SSD_REP,), F32)]
    w_main = jnp.concatenate([w_in[:, :o_gl]] + gd_cols + [w_in[:, o_z:o_dt]], axis=1).astype(BF16)
    gdb = jnp.concatenate(gd_bias)[None, :]
    wdt_rows, dtb_rows = [], []
    for g in range(SSD_GROUPS):
        wdt_rows += [w_in[:, o_dt + SSD_REP * g:o_dt + SSD_REP * (g + 1)].T,
                     jnp.zeros((SUBLANES - SSD_REP, D_MODEL), F32)]
        dtb_rows += [dt_bias[SSD_REP * g:SSD_REP * (g + 1)], jnp.zeros((SUBLANES - SSD_REP,), F32)]
    wdt = jnp.concatenate(wdt_rows, axis=0).astype(BF16)
    dtb = jnp.broadcast_to(jnp.concatenate(dtb_rows)[:, None], (2 * SUBLANES, LANES))
    cos, sa, sb = _rope_tables(s)
    n_main = w_main.shape[1]

    def full(shape):
        return pl.BlockSpec(shape, lambda i: (0, 0))

    def tok(width):
        return pl.BlockSpec((tm, width), lambda i: (i, 0))

    kv_widths = [LANES, LANES] + [NSA_KV_GROUPS * LANES] * 4
    out_shapes = ([jax.ShapeDtypeStruct((t, NSA_HEADS * LANES), BF16)]
                  + [jax.ShapeDtypeStruct((t, w), BF16) for w in kv_widths]
                  + [jax.ShapeDtypeStruct((t, 2 * LANES), F32),
                     jax.ShapeDtypeStruct((2 * SUBLANES, t), F32),
                     jax.ShapeDtypeStruct((t, SSD_WIDTH), F32),
                     jax.ShapeDtypeStruct((t, SSD_CONV_DIM), F32)])
    out_specs = ([tok(NSA_HEADS * LANES)] + [tok(w) for w in kv_widths]
                 + [tok(2 * LANES), pl.BlockSpec((2 * SUBLANES, tm), lambda i: (0, i)),
                    tok(SSD_WIDTH), tok(SSD_CONV_DIM)])
    return pl.pallas_call(
        functools.partial(_in_proj_kernel, s // tm),
        name="in_proj",
        grid=(t // tm,),
        in_specs=[tok(D_MODEL), full((1, D_MODEL)), full((D_MODEL, n_main)), full((2 * SUBLANES, D_MODEL)),
                  full((s, LANES)), full((s, LANES)), full((s, LANES)), full((1, 2 * LANES)),
                  full((2 * SUBLANES, LANES))],
        out_specs=out_specs,
        out_shape=out_shapes,
        compiler_params=_cparams(("parallel",)),
    )(xt, attn_g[None, :], w_main, wdt, cos, sa, sb, gdb, dtb)


def _compress_kernel(kv_ref, w1a_ref, w1b_ref, pos_ref, w1_ref, b1_ref, w2_ref, out_ref):
    kv = kv_ref[0]
    bias = _dot(pos_ref[...], w1_ref[...])[0:1, :] + b1_ref[...]
    for g in range(NSA_KV_GROUPS):
        first = _dot(kv, w1a_ref[g])
        second = _dot(kv, w1b_ref[g])
        nxt = pltpu.roll(second, second.shape[0] - 1, 0)
        hid = _gelu_tanh(first + nxt + bias)
        out_ref[0, :, g * LANES:(g + 1) * LANES] = _dot(hid.astype(BF16), w2_ref[...]).astype(BF16)


def _compress(kv, pos_emb, w1, b1, w2, b, s):
    nch = s // CMP_STRIDE
    kvf = kv.reshape(b, nch, CMP_STRIDE * LANES)
    w1r = w1.reshape(CMP_BLOCK, HEAD_DIM, CMP_HIDDEN)
    zeros = jnp.zeros((CMP_STRIDE, HEAD_DIM, CMP_HIDDEN), F32)

    def expand(w_half, g):
        parts = [w_half, zeros] if g == 0 else [zeros, w_half]
        return jnp.concatenate(parts, axis=1).reshape(CMP_STRIDE * LANES, CMP_HIDDEN)

    w1a = jnp.stack([expand(w1r[:CMP_STRIDE], g) for g in range(NSA_KV_GROUPS)]).astype(BF16)
    w1b = jnp.stack([expand(w1r[CMP_STRIDE:], g) for g in range(NSA_KV_GROUPS)]).astype(BF16)
    w2e = jnp.concatenate([w2, jnp.zeros((CMP_HIDDEN, LANES - HEAD_DIM), F32)], axis=1).astype(BF16)
    pos = jnp.zeros((SUBLANES, CMP_BLOCK * HEAD_DIM), F32).at[0].set(pos_emb.reshape(-1)).astype(BF16)
    return pl.pallas_call(
        _compress_kernel,
        name="compress",
        grid=(b,),
        in_specs=[pl.BlockSpec((1, nch, CMP_STRIDE * LANES), lambda i: (i, 0, 0)),
                  pl.BlockSpec((2, CMP_STRIDE * LANES, CMP_HIDDEN), lambda i: (0, 0, 0)),
                  pl.BlockSpec((2, CMP_STRIDE * LANES, CMP_HIDDEN), lambda i: (0, 0, 0)),
                  pl.BlockSpec((SUBLANES, CMP_BLOCK * HEAD_DIM), lambda i: (0, 0)),
                  pl.BlockSpec((CMP_BLOCK * HEAD_DIM, CMP_HIDDEN), lambda i: (0, 0)),
                  pl.BlockSpec((1, CMP_HIDDEN), lambda i: (0, 0)),
                  pl.BlockSpec((CMP_HIDDEN, LANES), lambda i: (0, 0))],
        out_specs=pl.BlockSpec((1, nch, NSA_KV_GROUPS * LANES), lambda i: (i, 0, 0)),
        out_shape=jax.ShapeDtypeStruct((b, nch, NSA_KV_GROUPS * LANES), BF16),
        compiler_params=_cparams(("parallel",)),
    )(kvf, w1a, w1b, pos, w1.astype(BF16), b1[None, :], w2e)


def _lane_tiles(a):
    return [a[:, k * LANES:(k + 1) * LANES] for k in range(a.shape[1] // LANES)]


def _max_tiles(macc, s):
    for tile in _lane_tiles(s):
        macc = jnp.maximum(macc, tile)
    return macc


def _exp_tiles(s, m_b):
    return jnp.concatenate([jnp.exp(tile - m_b) for tile in _lane_tiles(s)], axis=1).astype(BF16)


def _normalize(acc):
    return acc / pltpu.roll(acc, HEAD_DIM, 1)


def _block_rank(imp):
    nblk, tq = imp.shape
    sub = lax.broadcasted_iota(jnp.int32, (SUBLANES, tq), 0)
    groups = [imp[SUBLANES * v:SUBLANES * (v + 1)] for v in range(nblk // SUBLANES)]
    cnt = [jnp.zeros((SUBLANES, tq), F32) for _ in groups]
    for j in range(nblk):
        row = imp[j:j + 1, :]
        for v, grp in enumerate(groups):
            if SUBLANES * v > j:
                ahead = row >= grp
            elif SUBLANES * (v + 1) <= j:
                ahead = row > grp
            else:
                ahead = (row > grp) | ((row == grp) & (sub > j - SUBLANES * v))
            cnt[v] = cnt[v] + jnp.where(ahead, 1.0, 0.0)
    return jnp.concatenate(cnt, axis=0)


def _nsa_kernel(q_ref, kc_ref, vc_ref, ks_ref, vs_ref, kw_ref, vw_ref, gd_ref, ovl_ref, o_ref, s_ref):
    i = pl.program_id(2)
    tq = Q_TILE
    rows = NSA_REP * tq
    ncmp = kc_ref.shape[1]
    nblk = ovl_ref.shape[0]
    q = jnp.concatenate([q_ref[:, r * LANES:(r + 1) * LANES] for r in range(NSA_REP)], axis=0)
    t_row = i * tq + lax.broadcasted_iota(jnp.int32, (rows, 1), 0) % tq

    s_c = _dot_nt(q, kc_ref[0])
    cmp_end = lax.broadcasted_iota(jnp.int32, (rows, ncmp), 1) * CMP_STRIDE + (CMP_BLOCK - 1)
    mask_c = cmp_end <= t_row
    s_c = jnp.where(mask_c, s_c, NEG_BIG)
    e_c = jnp.exp(s_c - jnp.max(s_c, axis=-1, keepdims=True))
    p_c = jnp.where(mask_c, e_c / jnp.sum(e_c, axis=-1, keepdims=True), 0.0)
    o_cmp = _dot(p_c.astype(BF16), vc_ref[0])

    p_sum = p_c[0:tq]
    for r in range(1, NSA_REP):
        p_sum = p_sum + p_c[r * tq:(r + 1) * tq]
    imp = _dot_nt_exact_rhs(ovl_ref[...], p_sum)
    blk = lax.broadcasted_iota(jnp.int32, (nblk, tq), 0)
    cur = (i * tq + lax.broadcasted_iota(jnp.int32, (nblk, tq), 1)) // SEL_BLOCK
    forced = (blk == 0) | (blk == cur) | (blk == cur - 1)
    valid = blk <= cur
    imp = jnp.where(forced, 1e9, jnp.where(valid, imp, -1.0))

    sel_t = jnp.where((_block_rank(imp) < float(min(SEL_TOP_N, nblk))) & valid, 1.0, 0.0)
    ones_lo = jnp.ones((HEAD_DIM, tq), F32)
    sel_pad = jnp.concatenate([ones_lo, sel_t] + ([jnp.ones((HEAD_DIM - nblk, tq), F32)] if nblk < HEAD_DIM else []),
                              axis=0).astype(BF16)
    eye = jnp.where(lax.broadcasted_iota(jnp.int32, (tq, tq), 0) == lax.broadcasted_iota(jnp.int32, (tq, tq), 1),
                    1.0, 0.0).astype(BF16)
    negb = ((_dot_nt(eye, sel_pad) - 1.0) * (-SEL_NEG)).astype(BF16)
    q2 = q + jnp.concatenate([negb] * NSA_REP, axis=0)

    jd = (i * tq) // K_TILE
    nfull = jd // (S_CHUNK // K_TILE)
    lane_c = lax.broadcasted_iota(jnp.int32, (rows, S_CHUNK), 1)

    def scores(c):
        k0 = pl.multiple_of(c * S_CHUNK, S_CHUNK)
        return _dot_nt(q2, ks_ref[pl.ds(k0, S_CHUNK), :])

    def pass_max(c, macc):
        s = scores(c)
        s_ref[c] = s
        return _max_tiles(macc, s)

    macc = lax.fori_loop(0, nfull, pass_max, jnp.full((rows, LANES), NEG_BIG, F32))
    s_last = jnp.where(nfull * S_CHUNK + lane_c <= t_row, scores(nfull), NEG_BIG)
    macc = _max_tiles(macc, s_last)
    m_b = jnp.broadcast_to(jnp.max(macc, axis=-1, keepdims=True), (rows, LANES))

    def pass_sum(c, acc):
        k0 = pl.multiple_of(c * S_CHUNK, S_CHUNK)
        return acc + _dot(_exp_tiles(s_ref[c], m_b), vs_ref[pl.ds(k0, S_CHUNK), :])

    acc = lax.fori_loop(0, nfull, pass_sum, jnp.zeros((rows, LANES), F32))
    k_last = pl.multiple_of(nfull * S_CHUNK, S_CHUNK)
    o_sel = _normalize(acc + _dot(_exp_tiles(s_last, m_b), vs_ref[pl.ds(k_last, S_CHUNK), :]))

    n_wk = WINDOW + K_TILE
    k0 = pl.multiple_of(jnp.maximum(jd - WINDOW // K_TILE, 0) * K_TILE, K_TILE)
    diff = t_row - (k0 + lax.broadcasted_iota(jnp.int32, (rows, n_wk), 1))
    s_w = jnp.where((diff >= 0) & (diff < WINDOW), _dot_nt(q2, kw_ref[pl.ds(k0, n_wk), :]), NEG_BIG)
    m_w = jnp.max(_max_tiles(jnp.full((rows, LANES), NEG_BIG, F32), s_w), axis=-1, keepdims=True)
    p_w = _exp_tiles(s_w, jnp.broadcast_to(m_w, (rows, LANES)))
    o_win = _normalize(_dot(p_w, vw_ref[pl.ds(k0, n_wk), :]))

    gd = gd_ref[...]
    low = lax.broadcasted_iota(jnp.int32, (tq, LANES), 1) < HEAD_DIM
    heads = []
    for r in range(NSA_REP):
        rs = slice(r * tq, (r + 1) * tq)
        heads.append(gd[:, 3 * r:3 * r + 1] * o_cmp[rs] + gd[:, 3 * r + 1:3 * r + 2] * o_sel[rs]
                     + gd[:, 3 * r + 2:3 * r + 3] * o_win[rs])
    for sidx in range(NSA_REP // 2):
        o_ref[:, sidx * LANES:(sidx + 1) * LANES] = jnp.where(
            low, heads[2 * sidx], pltpu.roll(heads[2 * sidx + 1], HEAD_DIM, 1))


def _nsa(q, kcmp, vcmp, ks, vs, kw, vw, gd, b, s):
    t = q.shape[0]
    nq = s // Q_TILE
    ncmp = s // CMP_STRIDE
    nblk = s // SEL_BLOCK
    cs = np.arange(ncmp) * CMP_STRIDE
    ss = np.arange(nblk) * SEL_BLOCK
    overlap = (cs[:, None] < ss[None, :] + SEL_BLOCK) & (cs[:, None] + CMP_BLOCK > ss[None, :])
    overlap[ncmp - 1, :] = False
    ovl_t = jnp.asarray(overlap.T, BF16)

    assert nblk <= HEAD_DIM and s % S_CHUNK == 0 and s >= WINDOW + K_TILE

    def seq_spec():
        return pl.BlockSpec((s, LANES), lambda bi, g, i: (bi, g))

    return pl.pallas_call(
        _nsa_kernel,
        name="nsa",
        grid=(b, NSA_KV_GROUPS, nq),
        in_specs=[pl.BlockSpec((Q_TILE, NSA_REP * LANES), lambda bi, g, i: (bi * nq + i, g)),
                  pl.BlockSpec((1, ncmp, LANES), lambda bi, g, i: (bi, 0, g)),
                  pl.BlockSpec((1, ncmp, LANES), lambda bi, g, i: (bi, 0, g)),
                  seq_spec(), seq_spec(), seq_spec(), seq_spec(),
                  pl.BlockSpec((Q_TILE, LANES), lambda bi, g, i: (bi * nq + i, g)),
                  pl.BlockSpec((nblk, ncmp), lambda bi, g, i: (0, 0))],
        out_specs=pl.BlockSpec((Q_TILE, NSA_REP * HEAD_DIM), lambda bi, g, i: (bi * nq + i, g)),
        out_shape=jax.ShapeDtypeStruct((t, NSA_WIDTH), F32),
        scratch_shapes=[pltpu.VMEM((s // S_CHUNK, NSA_REP * Q_TILE, S_CHUNK), F32)],
        compiler_params=_cparams(("parallel", "parallel", "arbitrary")),
    )(q, kcmp, vcmp, ks, vs, kw, vw, gd, ovl_t)


def _conv_silu(cur_ref, ext_ref, tail_ref, w_ref, b_ref):
    l = cur_ref.shape[0]
    cur = cur_ref[...]
    ext_ref[0:SUBLANES, :] = tail_ref[...]
    ext_ref[SUBLANES:, :] = cur
    tail_ref[...] = cur[l - SUBLANES:, :]
    acc = jnp.zeros(cur.shape, F32) + b_ref[...]
    for k in range(SSD_CONV):
        off = SUBLANES - (SSD_CONV - 1) + k
        acc = acc + ext_ref[off:off + l, :] * w_ref[k:k + 1, :]
    return acc * _sigmoid(acc)


def _ssd_kernel(xs_ref, bm_ref, cm_ref, wx_ref, wb_ref, wc_ref, bx_ref, bb_ref, bc_ref, z_ref, gd_ref, dtt_ref,
                alane_ref, asub_ref, dskip_ref, ng_ref, tril_ref, edt_ref, eseg_ref, o_ref,
                tx_ref, tb_ref, tc_ref, ex_ref, eb_ref, ec_ref, st_ref):
    c = pl.program_id(2)
    l = xs_ref.shape[0]

    @pl.when(c == 0)
    def _():
        tx_ref[...] = jnp.zeros(tx_ref.shape, F32)
        tb_ref[...] = jnp.zeros(tb_ref.shape, F32)
        tc_ref[...] = jnp.zeros(tc_ref.shape, F32)
        st_ref[...] = jnp.zeros(st_ref.shape, F32)

    xs = _conv_silu(xs_ref, ex_ref, tx_ref, wx_ref, bx_ref)
    bm = _conv_silu(bm_ref, eb_ref, tb_ref, wb_ref, bb_ref)
    cm = _conv_silu(cm_ref, ec_ref, tc_ref, wc_ref, bc_ref)
    gd = gd_ref[...]
    tril = tril_ref[...]
    adt_c = gd * (-jnp.exp(alane_ref[0]))
    acum_c = _dot_exact_rhs(tril, adt_c)
    dt_full = _dot_exact_lhs(gd, edt_ref[...])
    ac_full = _dot_exact_lhs(acum_c, edt_ref[...])
    ac_seg = _dot_exact_lhs(acum_c, eseg_ref[...])
    adt_t = dtt_ref[...] * (-jnp.exp(asub_ref[0]))
    acum_t = _dot_nt_exact_lhs(adt_t, tril)

    xdt = xs * dt_full
    cmb = cm.astype(BF16)
    cb = _dot_nt(cmb, bm.astype(BF16))
    causal = lax.broadcasted_iota(jnp.int32, (l, l), 0) >= lax.broadcasted_iota(jnp.int32, (l, l), 1)
    head_of_lane = lax.broadcasted_iota(jnp.int32, xs.shape, 1) // SSD_HEAD_DIM
    y = jnp.zeros(xs.shape, F32)
    for r in range(SSD_REP):
        seg = jnp.exp(jnp.where(causal, ac_seg[:, r * l:(r + 1) * l] - acum_t[r:r + 1, :], NEG_BIG))
        x_r = jnp.where(head_of_lane == r, xdt, 0.0).astype(BF16)
        y = y + _dot((cb * seg).astype(BF16), x_r)
    ac_last = ac_full[l - 1:l, :]
    state = st_ref[...]
    y = y + _dot(cmb, state.astype(BF16)) * jnp.exp(ac_full)
    decayed = (xdt * jnp.exp(ac_last - ac_full)).astype(BF16)
    st_ref[...] = state * jnp.exp(ac_last) + _dot(bm.T.astype(BF16), decayed)
    y = y + xs * dskip_ref[...]
    z = z_ref[...]
    yz = y * (z * _sigmoid(z))
    o_ref[...] = yz * lax.rsqrt(jnp.mean(yz * yz, axis=-1, keepdims=True) + EPS) * ng_ref[...]


def _ssd(xbc, z, gd, dtt, conv_w, conv_b, a_log, d_skip, norm_g, b, s):
    t = xbc.shape[0]
    l = SSD_CHUNK
    nc = s // l
    gw = SSD_WIDTH // SSD_GROUPS
    xcols = gw // LANES
    b_blk0 = SSD_WIDTH // LANES
    c_blk0 = b_blk0 + SSD_GROUPS * SSD_STATE // LANES
    conv_b2 = conv_b[None, :]
    a_grp = a_log.reshape(SSD_GROUPS, SSD_REP)
    alane = jnp.zeros((SSD_GROUPS, 1, LANES), F32).at[:, 0, 16:16 + SSD_REP].set(a_grp)
    asub = jnp.zeros((SSD_GROUPS, SUBLANES, LANES), F32).at[:, :SSD_REP, :].set(
        jnp.broadcast_to(a_grp[:, :, None], (SSD_GROUPS, SSD_REP, LANES)))
    dskip = jnp.repeat(d_skip, SSD_HEAD_DIM)[None, :]
    ng = norm_g[None, :]
    ii = np.arange(l)
    tril = jnp.asarray(ii[:, None] >= ii[None, :], BF16)
    edt = np.zeros((LANES, gw), np.float32)
    eseg = np.zeros((LANES, SSD_REP * l), np.float32)
    for r in range(SSD_REP):
        edt[16 + r, r * SSD_HEAD_DIM:(r + 1) * SSD_HEAD_DIM] = 1.0
        eseg[16 + r, r * l:(r + 1) * l] = 1.0
    edt, eseg = jnp.asarray(edt, BF16), jnp.asarray(eseg, BF16)

    def rowblk(width, col):
        return pl.BlockSpec((l, width), lambda bi, g, c: (bi * nc + c, col(g)))

    def const(shape, col=None):
        if col is None:
            return pl.BlockSpec(shape, lambda bi, g, c: (0,) * len(shape))
        return pl.BlockSpec(shape, lambda bi, g, c: (0, col(g)))

    return pl.pallas_call(
        _ssd_kernel,
        name="ssd",
        grid=(b, SSD_GROUPS, nc),
        in_specs=[rowblk(gw, lambda g: g), rowblk(LANES, lambda g: b_blk0 + g), rowblk(LANES, lambda g: c_blk0 + g),
                  const((SSD_CONV, gw), lambda g: g), const((SSD_CONV, LANES), lambda g: b_blk0 + g),
                  const((SSD_CONV, LANES), lambda g: c_blk0 + g),
                  const((1, gw), lambda g: g), const((1, LANES), lambda g: b_blk0 + g),
                  const((1, LANES), lambda g: c_blk0 + g),
                  rowblk(gw, lambda g: g), rowblk(LANES, lambda g: g),
                  pl.BlockSpec((SUBLANES, l), lambda bi, g, c: (g, bi * nc + c)),
                  pl.BlockSpec((1, 1, LANES), lambda bi, g, c: (g, 0, 0)),
                  pl.BlockSpec((1, SUBLANES, LANES), lambda bi, g, c: (g, 0, 0)),
                  const((1, gw), lambda g: g), const((1, gw), lambda g: g),
                  const((l, l)), const((LANES, gw)), const((LANES, SSD_REP * l))],
        out_specs=rowblk(gw, lambda g: g),
        out_shape=jax.ShapeDtypeStruct((t, SSD_WIDTH), F32),
        scratch_shapes=[pltpu.VMEM((SUBLANES, gw), F32), pltpu.VMEM((SUBLANES, LANES), F32),
                        pltpu.VMEM((SUBLANES, LANES), F32),
                        pltpu.VMEM((l + SUBLANES, gw), F32), pltpu.VMEM((l + SUBLANES, LANES), F32),
                        pltpu.VMEM((l + SUBLANES, LANES), F32),
                        pltpu.VMEM((SSD_STATE, gw), F32)],
        compiler_params=_cparams(("parallel", "parallel", "arbitrary")),
    )(xbc, xbc, xbc, conv_w, conv_w, conv_w, conv_b2, conv_b2, conv_b2, z, gd, dtt,
      alane, asub, dskip, ng, tril, edt, eseg)


def _rms(v, g):
    return v * lax.rsqrt(jnp.mean(v * v, axis=-1, keepdims=True) + EPS) * g


def _out_proj_kernel(x_ref, on_ref, os_ref, ng_ref, w1_ref, w2_ref, fg_ref, x1_ref, xnt_ref):
    onn = _rms(on_ref[...], ng_ref[...]).astype(BF16)
    x1 = x_ref[...] + _dot(onn, w1_ref[...]) + _dot(os_ref[...].astype(BF16), w2_ref[...])
    x1_ref[...] = x1
    xnt_ref[...] = _rms(x1, fg_ref[...]).T.astype(BF16)


def _out_proj(xt, o_nsa, o_ssd, nsa_g, w_out, ffn_g):
    t = xt.shape[0]
    tm = min(TOK_TILE, t)

    def tok(width):
        return pl.BlockSpec((tm, width), lambda i: (i, 0))

    def full(shape):
        return pl.BlockSpec(shape, lambda i: (0, 0))

    wb = w_out.astype(BF16)
    return pl.pallas_call(
        _out_proj_kernel,
        name="out_proj",
        grid=(t // tm,),
        in_specs=[tok(D_MODEL), tok(NSA_WIDTH), tok(SSD_WIDTH), full((1, NSA_WIDTH)),
                  full((NSA_WIDTH, D_MODEL)), full((SSD_WIDTH, D_MODEL)), full((1, D_MODEL))],
        out_specs=[tok(D_MODEL), pl.BlockSpec((D_MODEL, tm), lambda i: (0, i))],
        out_shape=[jax.ShapeDtypeStruct((t, D_MODEL), F32), jax.ShapeDtypeStruct((D_MODEL, t), BF16)],
        compiler_params=_cparams(("parallel",)),
    )(xt, o_nsa, o_ssd, nsa_g[None, :], wb[:NSA_WIDTH], wb[NSA_WIDTH:], ffn_g[None, :])


def _top_sorted(e, k):
    rows = lax.broadcasted_iota(jnp.int32, (k, e.shape[1]), 0)
    out = jnp.zeros((k, e.shape[1]), F32)
    cur = e
    for j in range(k):
        mk = jnp.max(cur, axis=0, keepdims=True)
        out = jnp.where(rows == j, jnp.maximum(mk, 0.0), out)
        cur = jnp.where((cur == mk) & (mk >= 0.0), -float(j + 1), cur)
    rank = jnp.where(cur < 0.0, -1.0 - cur, float(k))
    return out, rank


def _pair_candidates(a16, b16):
    row8 = lax.broadcasted_iota(jnp.int32, (SUBLANES, a16.shape[1]), 0)
    groups = [a16[0:1] * b16[0:SUBLANES], a16[0:1] * b16[SUBLANES:], a16[1:2] * b16[0:SUBLANES]]
    for a in range(2, SUBLANES):
        groups.append(jnp.where(row8 < PEER_TOPK // (a + 1), a16[a:a + 1] * b16[0:SUBLANES], -1.0))
    groups.append(a16[SUBLANES:] * b16[0:1])
    return jnp.concatenate(groups, axis=0)


def _peer_front_kernel(xnt_ref, wq_ref, keys_ref, av_ref, lc_ref, bv_ref, rb_ref):
    xt = xnt_ref[...]
    tm = xt.shape[1]
    row8 = lax.broadcasted_iota(jnp.int32, (SUBLANES, tm), 0)
    for h in range(PEER_HEADS):
        qh = _dot(wq_ref[h * PEER_KEY_DIM:(h + 1) * PEER_KEY_DIM, :], xt)
        s1 = _dot(keys_ref[h, 0], qh[:PEER_HALF].astype(BF16))
        s2 = _dot(keys_ref[h, 1], qh[PEER_HALF:].astype(BF16))
        e1 = jnp.exp(s1 - jnp.max(s1, axis=0, keepdims=True))
        e2 = jnp.exp(s2 - jnp.max(s2, axis=0, keepdims=True))
        a16, ra = _top_sorted(e1, PEER_TOPK)
        b16, rb = _top_sorted(e2, PEER_TOPK)
        cand = _pair_candidates(a16, b16)
        cur = cand
        zsum = jnp.zeros((1, tm), F32)
        for _ in range(PEER_TOPK):
            tau = jnp.max(cur, axis=0, keepdims=True)
            zsum = zsum + tau
            cur = jnp.where(cur == tau, -1.0, cur)
        picked = jnp.where(cand >= tau, 1.0, 0.0)

        def count(g):
            return jnp.sum(picked[g * SUBLANES:(g + 1) * SUBLANES], axis=0, keepdims=True)

        lens = [count(0) + count(1)] + [count(a + 1) for a in range(1, SUBLANES)]
        len_lo = jnp.zeros((SUBLANES, tm), F32)
        for a, row in enumerate(lens):
            len_lo = jnp.where(row8 == a, row, len_lo)
        len16 = jnp.concatenate([len_lo, picked[(SUBLANES + 1) * SUBLANES:]], axis=0)
        pack = 2 * SUBLANES
        ra_b = ra.astype(BF16)
        lc = jnp.zeros(ra.shape, BF16)
        for a in range(PEER_TOPK):
            len_a = jnp.tile(jnp.broadcast_to(len16[a:a + 1], (pack, tm)).astype(BF16), (PEER_N_KEYS // pack, 1))
            lc = jnp.where(ra_b == float(a), len_a, lc)
        rows = slice(h * PEER_N_KEYS, (h + 1) * PEER_N_KEYS)
        av_ref[rows, :] = e1
        lc_ref[rows, :] = lc.astype(F32)
        bv_ref[rows, :] = (e2 * (1.0 / zsum)).astype(BF16)
        rb_ref[rows, :] = rb.astype(BF16)


def _peer_front(xnt, w_q, keys):
    t = xnt.shape[1]
    tm = min(PEER_FRONT_TOK, t)
    nk = PEER_HEADS * PEER_N_KEYS
    return pl.pallas_call(
        _peer_front_kernel,
        name="peer_front",
        grid=(t // tm,),
        in_specs=[pl.BlockSpec((D_MODEL, tm), lambda i: (0, i)),
                  pl.BlockSpec((PEER_HEADS * PEER_KEY_DIM, D_MODEL), lambda i: (0, 0)),
                  pl.BlockSpec((PEER_HEADS, 2, PEER_N_KEYS, PEER_HALF), lambda i: (0, 0, 0, 0))],
        out_specs=[pl.BlockSpec((nk, tm), lambda i: (0, i))] * 4,
        out_shape=[jax.ShapeDtypeStruct((nk, t), F32), jax.ShapeDtypeStruct((nk, t), F32),
                   jax.ShapeDtypeStruct((nk, t), BF16), jax.ShapeDtypeStruct((nk, t), BF16)],
        compiler_params=_cparams(("parallel",)),
    )(xnt, w_q.T.astype(BF16), keys.astype(BF16))


def _row_bf16(ref, row, n_rows):
    pack = 2 * SUBLANES
    tile = jnp.broadcast_to(ref[pl.ds(row, 1), :], (pack, ref.shape[1])).astype(BF16)
    return jnp.tile(tile, (n_rows // pack, 1))


def _peer_dense_kernel(xnt_ref, u_ref, vt_ref, av_ref, lc_ref, bv_ref, rb_ref, o_ref, hid_ref, act_ref):
    j = pl.program_id(1)
    n_sub = u_ref.shape[0] // PEER_N_KEYS

    @pl.when(j == 0)
    def _():
        o_ref[...] = jnp.zeros(o_ref.shape, F32)

    hid_ref[...] = _dot(u_ref[...], xnt_ref[...])

    def sub(cl, carry):
        e1 = j * n_sub + cl
        gate = jnp.zeros((PEER_N_KEYS, xnt_ref.shape[1]), BF16)
        for h in range(PEER_HEADS):
            rows = slice(h * PEER_N_KEYS, (h + 1) * PEER_N_KEYS)
            lc_row = _row_bf16(lc_ref, h * PEER_N_KEYS + e1, PEER_N_KEYS)
            av_row = _row_bf16(av_ref, h * PEER_N_KEYS + e1, PEER_N_KEYS)
            bv = bv_ref[rows, :]
            gate = gate + jnp.where(rb_ref[rows, :] < lc_row, bv, jnp.zeros_like(bv)) * av_row
        rr = pl.ds(pl.multiple_of(cl * PEER_N_KEYS, PEER_N_KEYS), PEER_N_KEYS)
        act_ref[rr, :] = _gelu_sigmoid(hid_ref[rr, :]).astype(BF16) * gate
        return carry

    lax.fori_loop(0, n_sub, sub, 0)
    o_ref[...] += _dot(vt_ref[...], act_ref[...])


def _peer_dense(xnt, u, vt, av, lc, bv, rb):
    t = xnt.shape[1]
    tm = min(PEER_TOK, t)
    ne = u.shape[0]
    te = PEER_EXP
    tok = pl.BlockSpec((PEER_HEADS * PEER_N_KEYS, tm), lambda i, j: (0, i))
    return pl.pallas_call(
        _peer_dense_kernel,
        name="peer_dense",
        grid=(t // tm, ne // te),
        in_specs=[pl.BlockSpec((D_MODEL, tm), lambda i, j: (0, i)),
                  pl.BlockSpec((te, D_MODEL), lambda i, j: (j, 0)),
                  pl.BlockSpec((D_MODEL, te), lambda i, j: (0, j)),
                  tok, tok, tok, tok],
        out_specs=pl.BlockSpec((D_MODEL, tm), lambda i, j: (0, i)),
        out_shape=jax.ShapeDtypeStruct((D_MODEL, t), F32),
        scratch_shapes=[pltpu.VMEM((te, tm), F32), pltpu.VMEM((te, tm), BF16)],
        compiler_params=_cparams(("parallel", "arbitrary")),
    )(xnt, u, vt, av, lc, bv, rb)


def _final_kernel(x1_ref, pt_ref, g_ref, o_ref):
    o_ref[...] = _rms(x1_ref[...] + pt_ref[...].T, g_ref[...])


def _residual_norm(x1, peer_t, g):
    t = x1.shape[0]
    tm = min(TOK_TILE, t)
    return pl.pallas_call(
        _final_kernel,
        name="residual_norm",
        grid=(t // tm,),
        in_specs=[pl.BlockSpec((tm, D_MODEL), lambda i: (i, 0)), pl.BlockSpec((D_MODEL, tm), lambda i: (0, i)),
                  pl.BlockSpec((1, D_MODEL), lambda i: (0, 0))],
        out_specs=pl.BlockSpec((tm, D_MODEL), lambda i: (i, 0)),
        out_shape=jax.ShapeDtypeStruct((t, D_MODEL), F32),
        compiler_params=_cparams(("parallel",)),
    )(x1, peer_t, g[None, :])


def kernel(x, attn_norm_g, w_in, cmp_pos_k, cmp_w1_k, cmp_b1_k, cmp_w2_k, cmp_pos_v, cmp_w1_v, cmp_b1_v, cmp_w2_v,
           conv_w, conv_b, dt_bias, a_log, d_skip, ssd_norm_g, nsa_norm_g, w_out, ffn_norm_g, peer_w_q, peer_keys,
           peer_u, peer_v, final_norm_g):
    b, s, d = x.shape
    xt = x.reshape(b * s, d)
    assert attn_norm_g.shape[0] == 1, "single-layer block"
    for layer in range(1):
        q, kc, vc, ks, vs, kw, vw, gd, dtt, z, xbc = _in_proj(xt, attn_norm_g[layer], w_in[layer], dt_bias[layer], s)
        kcmp = _compress(kc, cmp_pos_k[layer], cmp_w1_k[layer], cmp_b1_k[layer], cmp_w2_k[layer], b, s)
        vcmp = _compress(vc, cmp_pos_v[layer], cmp_w1_v[layer], cmp_b1_v[layer], cmp_w2_v[layer], b, s)
        o_nsa = _nsa(q, kcmp, vcmp, ks, vs, kw, vw, gd, b, s)
        o_ssd = _ssd(xbc, z, gd, dtt, conv_w[layer], conv_b[layer], a_log[layer], d_skip[layer],
                     ssd_norm_g[layer], b, s)
        x1, xnt = _out_proj(xt, o_nsa, o_ssd, nsa_norm_g[layer], w_out[layer], ffn_norm_g[layer])
        av, lc, bv, rb = _peer_front(xnt, peer_w_q[layer], peer_keys[layer])
        peer_t = _peer_dense(xnt, peer_u[layer].astype(BF16), peer_v[layer].T.astype(BF16), av, lc, bv, rb)
    return _residual_norm(x1, peer_t, final_norm_g).reshape(b, s, d)
```

```python
import functools

import numpy as np
import jax
import jax.numpy as jnp
from jax import lax
from jax.experimental import pallas as pl
from jax.experimental.pallas import tpu as pltpu

F32 = jnp.float32
BF16 = jnp.bfloat16

EPS = 1e-6
D_MODEL = 1024
NSA_HEADS = 8
NSA_KV_GROUPS = 2
NSA_REP = NSA_HEADS // NSA_KV_GROUPS
HEAD_DIM = 64
NSA_WIDTH = NSA_HEADS * HEAD_DIM
KV_WIDTH = NSA_KV_GROUPS * HEAD_DIM
CMP_STRIDE = 16
CMP_BLOCK = 32
CMP_HIDDEN = 256
SEL_BLOCK = 64
SEL_TOP_N = 16
WINDOW = 512
ROPE_DIM = HEAD_DIM // 4
ROPE_THETA = 500000.0
SSD_HEADS = 8
SSD_HEAD_DIM = 64
SSD_WIDTH = SSD_HEADS * SSD_HEAD_DIM
SSD_GROUPS = 2
SSD_REP = SSD_HEADS // SSD_GROUPS
SSD_STATE = 128
SSD_CONV = 4
SSD_CHUNK = 128
SSD_CONV_DIM = SSD_WIDTH + 2 * SSD_GROUPS * SSD_STATE
PEER_HEADS = 8
PEER_N_KEYS = 128
PEER_N_EXPERTS = PEER_N_KEYS * PEER_N_KEYS
PEER_KEY_DIM = 256
PEER_HALF = PEER_KEY_DIM // 2
PEER_TOPK = 16

LANES = 128
SUBLANES = 8
VMEM_LIMIT = 48 * 1024 * 1024
NEG_BIG = -1e30
SEL_NEG = -1e9

TOK_TILE = 512
Q_TILE = 2 * SEL_BLOCK
K_TILE = 2 * SEL_BLOCK
S_CHUNK = 4 * K_TILE
PEER_TOK = 512
PEER_EXP = 1024
PEER_FRONT_TOK = 256


def _cparams(sem, flags=None):
    return pltpu.CompilerParams(dimension_semantics=sem, vmem_limit_bytes=VMEM_LIMIT, flags=flags)


def _dot(a, b):
    return jnp.dot(a, b, preferred_element_type=F32)


def _dot_nt(a, b):
    return lax.dot_general(a, b, (((1,), (1,)), ((), ())), preferred_element_type=F32)


def _split3(a):
    a1 = a.astype(BF16)
    r1 = a - a1.astype(F32)
    a2 = r1.astype(BF16)
    a3 = (r1 - a2.astype(F32)).astype(BF16)
    return a1, a2, a3


def _dot_exact_lhs(a, b01):
    a1, a2, a3 = _split3(a)
    return _dot(a1, b01) + _dot(a2, b01) + _dot(a3, b01)


def _dot_nt_exact_lhs(a, b01):
    a1, a2, a3 = _split3(a)
    return _dot_nt(a1, b01) + _dot_nt(a2, b01) + _dot_nt(a3, b01)


def _dot_exact_rhs(a01, b):
    b1, b2, b3 = _split3(b)
    return _dot(a01, b1) + _dot(a01, b2) + _dot(a01, b3)


def _dot_nt_exact_rhs(a01, b):
    b1, b2, b3 = _split3(b)
    return _dot_nt(a01, b1) + _dot_nt(a01, b2) + _dot_nt(a01, b3)


def _softplus(x):
    return jnp.maximum(x, 0.0) + jnp.log(1.0 + jnp.exp(-jnp.abs(x)))


def _sigmoid(x):
    return 1.0 / (1.0 + jnp.exp(-x))


def _gelu_tanh(x):
    return 0.5 * x * (1.0 + jnp.tanh(0.7978845608028654 * (x + 0.044715 * (x * x * x))))


def _gelu_sigmoid(x):
    c1 = -2.0 * 0.7978845608028654 * 1.4426950408889634
    u = x * (c1 + (c1 * 0.044715) * (x * x))
    return x / (1.0 + jnp.exp2(u))


def _rope128(p, cos, sa, sb):
    return p * cos + pltpu.roll(p, LANES - ROPE_DIM // 2, 1) * sa + pltpu.roll(p, ROPE_DIM // 2, 1) * sb


def _in_proj_kernel(seq_tiles, x_ref, g_ref, w_ref, wdt_ref, cos_ref, sa_ref, sb_ref, gdb_ref, dtb_ref,
                    q_ref, kc_ref, vc_ref, ks_ref, vs_ref, kw_ref, vw_ref, gd_ref, dtt_ref, z_ref, xbc_ref):
    tm = x_ref.shape[0]
    x = x_ref[...]
    h = x * lax.rsqrt(jnp.mean(x * x, axis=-1, keepdims=True) + EPS) * g_ref[...]
    hb = h.astype(BF16)
    pos0 = pl.multiple_of((pl.program_id(0) % seq_tiles) * tm, tm)
    cos = cos_ref[pl.ds(pos0, tm), :]
    sa = sa_ref[pl.ds(pos0, tm), :]
    sb = sb_ref[pl.ds(pos0, tm), :]
    lane = lax.broadcasted_iota(jnp.int32, (tm, LANES), 1)
    low = lane < HEAD_DIM

    def proj(c0, width):
        return _dot(hb, w_ref[:, c0:c0 + width])

    def halves(p):
        return p, pltpu.roll(p, HEAD_DIM, 1)

    pq = proj(0, NSA_WIDTH)
    for s in range(NSA_WIDTH // LANES):
        srcs = halves(_rope128(pq[:, s * LANES:(s + 1) * LANES], cos, sa, sb) * (HEAD_DIM ** -0.5))
        for half in range(2):
            head = 2 * s + half
            q_ref[:, head * LANES:(head + 1) * LANES] = jnp.where(low, srcs[half], 0.0).astype(BF16)
    c0 = NSA_WIDTH
    kc_ref[...] = _rope128(proj(c0, LANES), cos, sa, sb).astype(BF16)
    vc_ref[...] = proj(c0 + LANES, LANES).astype(BF16)
    c0 += 2 * LANES
    blk_of_row = (pos0 + lax.broadcasted_iota(jnp.int32, (tm, LANES), 0)) // SEL_BLOCK
    onehot = jnp.where(lane - HEAD_DIM == blk_of_row, 1.0, 0.0)
    for ref, rope, fill in ((ks_ref, True, onehot), (vs_ref, False, 1.0), (kw_ref, True, 0.0), (vw_ref, False, 1.0)):
        p = proj(c0, LANES)
        if rope:
            p = _rope128(p, cos, sa, sb)
        for g, src in enumerate(halves(p)):
            ref[:, g * LANES:(g + 1) * LANES] = jnp.where(low, src, fill).astype(BF16)
        c0 += LANES
    for s in range(2):
        p = proj(c0, LANES) + gdb_ref[:, s * LANES:(s + 1) * LANES]
        act = jnp.where(lane < 3 * NSA_REP, _sigmoid(p),
                        jnp.where((lane >= 16) & (lane < 16 + SSD_REP), _softplus(p), 0.0))
        gd_ref[:, s * LANES:(s + 1) * LANES] = act
        c0 += LANES
    z_ref[...] = proj(c0, SSD_WIDTH)
    c0 += SSD_WIDTH
    xbc_ref[...] = proj(c0, SSD_CONV_DIM)
    dtt_ref[...] = _softplus(_dot_nt(wdt_ref[...], hb) + dtb_ref[:, 0:1])


def _rope_tables(s):
    half = ROPE_DIM // 2
    inv = jnp.power(ROPE_THETA, -jnp.arange(half, dtype=F32) * 2.0 / ROPE_DIM)
    ang = jnp.arange(s).astype(F32)[:, None] * inv[None, :]
    cos, sin = jnp.cos(ang), jnp.sin(ang)
    zeros = jnp.zeros((s, HEAD_DIM - ROPE_DIM), F32)
    cos64 = jnp.concatenate([cos, cos, jnp.ones((s, HEAD_DIM - ROPE_DIM), F32)], axis=1)
    sa64 = jnp.concatenate([-sin, jnp.zeros_like(sin), zeros], axis=1)
    sb64 = jnp.concatenate([jnp.zeros_like(sin), sin, zeros], axis=1)
    return tuple(jnp.concatenate([t, t], axis=1) for t in (cos64, sa64, sb64))


def _in_proj(xt, attn_g, w_in, dt_bias, s):
    t = xt.shape[0]
    tm = min(TOK_TILE, s)
    o_gl = NSA_WIDTH + 6 * KV_WIDTH
    o_z = o_gl + 3 * NSA_HEADS
    o_xbc = o_z + SSD_WIDTH
    o_dt = o_xbc + SSD_CONV_DIM
    gd_cols, gd_bias = [], []
    for g in range(NSA_KV_GROUPS):
        gates = w_in[:, o_gl + 3 * NSA_REP * g:o_gl + 3 * NSA_REP * (g + 1)]
        dts = w_in[:, o_dt + SSD_REP * g:o_dt + SSD_REP * (g + 1)]
        gd_cols += [gates, jnp.zeros((D_MODEL, 16 - 3 * NSA_REP), F32), dts,
                    jnp.zeros((D_MODEL, LANES - 16 - SSD_REP), F32)]
        gd_bias += [jnp.zeros((16,), F32), dt_bias[SSD_REP * g:SSD_REP * (g + 1)],
                    jnp.zeros((LANES - 16 - SSD_REP,), F32)]
    w_main = jnp.concatenate([w_in[:, :o_gl]] + gd_cols + [w_in[:, o_z:o_dt]], axis=1).astype(BF16)
    gdb = jnp.concatenate(gd_bias)[None, :]
    wdt_rows, dtb_rows = [], []
    for g in range(SSD_GROUPS):
        wdt_rows += [w_in[:, o_dt + SSD_REP * g:o_dt + SSD_REP * (g + 1)].T,
                     jnp.zeros((SUBLANES - SSD_REP, D_MODEL), F32)]
        dtb_rows += [dt_bias[SSD_REP * g:SSD_REP * (g + 1)], jnp.zeros((SUBLANES - SSD_REP,), F32)]
    wdt = jnp.concatenate(wdt_rows, axis=0).astype(BF16)
    dtb = jnp.broadcast_to(jnp.concatenate(dtb_rows)[:, None], (2 * SUBLANES, LANES))
    cos, sa, sb = _rope_tables(s)
    n_main = w_main.shape[1]

    def full(shape):
        return pl.BlockSpec(shape, lambda i: (0, 0))

    def tok(width):
        return pl.BlockSpec((tm, width), lambda i: (i, 0))

    kv_widths = [LANES, LANES] + [NSA_KV_GROUPS * LANES] * 4
    out_shapes = ([jax.ShapeDtypeStruct((t, NSA_HEADS * LANES), BF16)]
                  + [jax.ShapeDtypeStruct((t, w), BF16) for w in kv_widths]
                  + [jax.ShapeDtypeStruct((t, 2 * LANES), F32),
                     jax.ShapeDtypeStruct((2 * SUBLANES, t), F32),
                     jax.ShapeDtypeStruct((t, SSD_WIDTH), F32),
                     jax.ShapeDtypeStruct((t, SSD_CONV_DIM), F32)])
    out_specs = ([tok(NSA_HEADS * LANES)] + [tok(w) for w in kv_widths]
                 + [tok(2 * LANES), pl.BlockSpec((2 * SUBLANES, tm), lambda i: (0, i)),
                    tok(SSD_WIDTH), tok(SSD_CONV_DIM)])
    return pl.pallas_call(
        functools.partial(_in_proj_kernel, s // tm),
        name="in_proj",
        grid=(t // tm,),
        in_specs=[tok(D_MODEL), full((1, D_MODEL)), full((D_MODEL, n_main)), full((2 * SUBLANES, D_MODEL)),
                  full((s, LANES)), full((s, LANES)), full((s, LANES)), full((1, 2 * LANES)),
                  full((2 * SUBLANES, LANES))],
        out_specs=out_specs,
        out_shape=out_shapes,
        compiler_params=_cparams(("parallel",)),
    )(xt, attn_g[None, :], w_main, wdt, cos, sa, sb, gdb, dtb)


def _compress_kernel(kv_ref, w1a_ref, w1b_ref, pos_ref, w1_ref, b1_ref, w2_ref, out_ref):
    kv = kv_ref[0]
    bias = _dot(pos_ref[...], w1_ref[...])[0:1, :] + b1_ref[...]
    for g in range(NSA_KV_GROUPS):
        first = _dot(kv, w1a_ref[g])
        second = _dot(kv, w1b_ref[g])
        nxt = pltpu.roll(second, second.shape[0] - 1, 0)
        hid = _gelu_tanh(first + nxt + bias)
        out_ref[0, :, g * LANES:(g + 1) * LANES] = _dot(hid.astype(BF16), w2_ref[...]).astype(BF16)


def _compress(kv, pos_emb, w1, b1, w2, b, s):
    nch = s // CMP_STRIDE
    kvf = kv.reshape(b, nch, CMP_STRIDE * LANES)
    w1r = w1.reshape(CMP_BLOCK, HEAD_DIM, CMP_HIDDEN)
    zeros = jnp.zeros((CMP_STRIDE, HEAD_DIM, CMP_HIDDEN), F32)

    def expand(w_half, g):
        parts = [w_half, zeros] if g == 0 else [zeros, w_half]
        return jnp.concatenate(parts, axis=1).reshape(CMP_STRIDE * LANES, CMP_HIDDEN)

    w1a = jnp.stack([expand(w1r[:CMP_STRIDE], g) for g in range(NSA_KV_GROUPS)]).astype(BF16)
    w1b = jnp.stack([expand(w1r[CMP_STRIDE:], g) for g in range(NSA_KV_GROUPS)]).astype(BF16)
    w2e = jnp.concatenate([w2, jnp.zeros((CMP_HIDDEN, LANES - HEAD_DIM), F32)], axis=1).astype(BF16)
    pos = jnp.zeros((SUBLANES, CMP_BLOCK * HEAD_DIM), F32).at[0].set(pos_emb.reshape(-1)).astype(BF16)
    return pl.pallas_call(
        _compress_kernel,
        name="compress",
        grid=(b,),
        in_specs=[pl.BlockSpec((1, nch, CMP_STRIDE * LANES), lambda i: (i, 0, 0)),
                  pl.BlockSpec((2, CMP_STRIDE * LANES, CMP_HIDDEN), lambda i: (0, 0, 0)),
                  pl.BlockSpec((2, CMP_STRIDE * LANES, CMP_HIDDEN), lambda i: (0, 0, 0)),
                  pl.BlockSpec((SUBLANES, CMP_BLOCK * HEAD_DIM), lambda i: (0, 0)),
                  pl.BlockSpec((CMP_BLOCK * HEAD_DIM, CMP_HIDDEN), lambda i: (0, 0)),
                  pl.BlockSpec((1, CMP_HIDDEN), lambda i: (0, 0)),
                  pl.BlockSpec((CMP_HIDDEN, LANES), lambda i: (0, 0))],
        out_specs=pl.BlockSpec((1, nch, NSA_KV_GROUPS * LANES), lambda i: (i, 0, 0)),
        out_shape=jax.ShapeDtypeStruct((b, nch, NSA_KV_GROUPS * LANES), BF16),
        compiler_params=_cparams(("parallel",)),
    )(kvf, w1a, w1b, pos, w1.astype(BF16), b1[None, :], w2e)


def _lane_tiles(a):
    return [a[:, k * LANES:(k + 1) * LANES] for k in range(a.shape[1] // LANES)]


def _max_tiles(macc, s):
    for tile in _lane_tiles(s):
        macc = jnp.maximum(macc, tile)
    return macc


def _exp_tiles(s, m_b):
    return jnp.concatenate([jnp.exp(tile - m_b) for tile in _lane_tiles(s)], axis=1).astype(BF16)


def _normalize(acc):
    return acc / pltpu.roll(acc, HEAD_DIM, 1)


def _block_rank(imp):
    nblk, tq = imp.shape
    sub = lax.broadcasted_iota(jnp.int32, (SUBLANES, tq), 0)
    groups = [imp[SUBLANES * v:SUBLANES * (v + 1)] for v in range(nblk // SUBLANES)]
    cnt = [jnp.zeros((SUBLANES, tq), F32) for _ in groups]
    for j in range(nblk):
        row = imp[j:j + 1, :]
        for v, grp in enumerate(groups):
            if SUBLANES * v > j:
                ahead = row >= grp
            elif SUBLANES * (v + 1) <= j:
                ahead = row > grp
            else:
                ahead = (row > grp) | ((row == grp) & (sub > j - SUBLANES * v))
            cnt[v] = cnt[v] + jnp.where(ahead, 1.0, 0.0)
    return jnp.concatenate(cnt, axis=0)


def _nsa_kernel(q_ref, kc_ref, vc_ref, ks_ref, vs_ref, kw_ref, vw_ref, gd_ref, ovl_ref, o_ref, s_ref):
    i = pl.program_id(2)
    tq = Q_TILE
    rows = NSA_REP * tq
    ncmp = kc_ref.shape[1]
    nblk = ovl_ref.shape[0]
    q = jnp.concatenate([q_ref[:, r * LANES:(r + 1) * LANES] for r in range(NSA_REP)], axis=0)
    t_row = i * tq + lax.broadcasted_iota(jnp.int32, (rows, 1), 0) % tq

    s_c = _dot_nt(q, kc_ref[0])
    cmp_end = lax.broadcasted_iota(jnp.int32, (rows, ncmp), 1) * CMP_STRIDE + (CMP_BLOCK - 1)
    mask_c = cmp_end <= t_row
    s_c = jnp.where(mask_c, s_c, NEG_BIG)
    e_c = jnp.exp(s_c - jnp.max(s_c, axis=-1, keepdims=True))
    p_c = jnp.where(mask_c, e_c / jnp.sum(e_c, axis=-1, keepdims=True), 0.0)
    o_cmp = _dot(p_c.astype(BF16), vc_ref[0])

    p_sum = p_c[0:tq]
    for r in range(1, NSA_REP):
        p_sum = p_sum + p_c[r * tq:(r + 1) * tq]
    imp = _dot_nt_exact_rhs(ovl_ref[...], p_sum)
    blk = lax.broadcasted_iota(jnp.int32, (nblk, tq), 0)
    cur = (i * tq + lax.broadcasted_iota(jnp.int32, (nblk, tq), 1)) // SEL_BLOCK
    forced = (blk == 0) | (blk == cur) | (blk == cur - 1)
    valid = blk <= cur
    imp = jnp.where(forced, 1e9, jnp.where(valid, imp, -1.0))

    sel_t = jnp.where((_block_rank(imp) < float(min(SEL_TOP_N, nblk))) & valid, 1.0, 0.0)
    ones_lo = jnp.ones((HEAD_DIM, tq), F32)
    sel_pad = jnp.concatenate([ones_lo, sel_t] + ([jnp.ones((HEAD_DIM - nblk, tq), F32)] if nblk < HEAD_DIM else []),
                              axis=0).astype(BF16)
    eye = jnp.where(lax.broadcasted_iota(jnp.int32, (tq, tq), 0) == lax.broadcasted_iota(jnp.int32, (tq, tq), 1),
                    1.0, 0.0).astype(BF16)
    negb = ((_dot_nt(eye, sel_pad) - 1.0) * (-SEL_NEG)).astype(BF16)
    q2 = q + jnp.concatenate([negb] * NSA_REP, axis=0)

    jd = (i * tq) // K_TILE
    nfull = jd // (S_CHUNK // K_TILE)
    lane_c = lax.broadcasted_iota(jnp.int32, (rows, S_CHUNK), 1)

    def scores(c):
        k0 = pl.multiple_of(c * S_CHUNK, S_CHUNK)
        return _dot_nt(q2, ks_ref[pl.ds(k0, S_CHUNK), :])

    def pass_max(c, macc):
        s = scores(c)
        s_ref[c] = s
        return _max_tiles(macc, s)

    macc = lax.fori_loop(0, nfull, pass_max, jnp.full((rows, LANES), NEG_BIG, F32))
    s_last = jnp.where(nfull * S_CHUNK + lane_c <= t_row, scores(nfull), NEG_BIG)
    macc = _max_tiles(macc, s_last)
    m_b = jnp.broadcast_to(jnp.max(macc, axis=-1, keepdims=True), (rows, LANES))

    def pass_sum(c, acc):
        k0 = pl.multiple_of(c * S_CHUNK, S_CHUNK)
        return acc + _dot(_exp_tiles(s_ref[c], m_b), vs_ref[pl.ds(k0, S_CHUNK), :])

    acc = lax.fori_loop(0, nfull, pass_sum, jnp.zeros((rows, LANES), F32))
    k_last = pl.multiple_of(nfull * S_CHUNK, S_CHUNK)
    o_sel = _normalize(acc + _dot(_exp_tiles(s_last, m_b), vs_ref[pl.ds(k_last, S_CHUNK), :]))

    n_wk = WINDOW + K_TILE
    k0 = pl.multiple_of(jnp.maximum(jd - WINDOW // K_TILE, 0) * K_TILE, K_TILE)
    diff = t_row - (k0 + lax.broadcasted_iota(jnp.int32, (rows, n_wk), 1))
    s_w = jnp.where((diff >= 0) & (diff < WINDOW), _dot_nt(q2, kw_ref[pl.ds(k0, n_wk), :]), NEG_BIG)
    m_w = jnp.max(_max_tiles(jnp.full((rows, LANES), NEG_BIG, F32), s_w), axis=-1, keepdims=True)
    p_w = _exp_tiles(s_w, jnp.broadcast_to(m_w, (rows, LANES)))
    o_win = _normalize(_dot(p_w, vw_ref[pl.ds(k0, n_wk), :]))

    gd = gd_ref[...]
    low = lax.broadcasted_iota(jnp.int32, (tq, LANES), 1) < HEAD_DIM
    heads = []
    for r in range(NSA_REP):
        rs = slice(r * tq, (r + 1) * tq)
        heads.append(gd[:, 3 * r:3 * r + 1] * o_cmp[rs] + gd[:, 3 * r + 1:3 * r + 2] * o_sel[rs]
                     + gd[:, 3 * r + 2:3 * r + 3] * o_win[rs])
    for sidx in range(NSA_REP // 2):
        o_ref[:, sidx * LANES:(sidx + 1) * LANES] = jnp.where(
            low, heads[2 * sidx], pltpu.roll(heads[2 * sidx + 1], HEAD_DIM, 1))


def _nsa(q, kcmp, vcmp, ks, vs, kw, vw, gd, b, s):
    t = q.shape[0]
    nq = s // Q_TILE
    ncmp = s // CMP_STRIDE
    nblk = s // SEL_BLOCK
    cs = np.arange(ncmp) * CMP_STRIDE
    ss = np.arange(nblk) * SEL_BLOCK
    overlap = (cs[:, None] < ss[None, :] + SEL_BLOCK) & (cs[:, None] + CMP_BLOCK > ss[None, :])
    overlap[ncmp - 1, :] = False
    ovl_t = jnp.asarray(overlap.T, BF16)

    assert nblk <= HEAD_DIM and s % S_CHUNK == 0 and s >= WINDOW + K_TILE

    def seq_spec():
        return pl.BlockSpec((s, LANES), lambda bi, g, i: (bi, g))

    return pl.pallas_call(
        _nsa_kernel,
        name="nsa",
        grid=(b, NSA_KV_GROUPS, nq),
        in_specs=[pl.BlockSpec((Q_TILE, NSA_REP * LANES), lambda bi, g, i: (bi * nq + i, g)),
                  pl.BlockSpec((1, ncmp, LANES), lambda bi, g, i: (bi, 0, g)),
                  pl.BlockSpec((1, ncmp, LANES), lambda bi, g, i: (bi, 0, g)),
                  seq_spec(), seq_spec(), seq_spec(), seq_spec(),
                  pl.BlockSpec((Q_TILE, LANES), lambda bi, g, i: (bi * nq + i, g)),
                  pl.BlockSpec((nblk, ncmp), lambda bi, g, i: (0, 0))],
        out_specs=pl.BlockSpec((Q_TILE, NSA_REP * HEAD_DIM), lambda bi, g, i: (bi * nq + i, g)),
        out_shape=jax.ShapeDtypeStruct((t, NSA_WIDTH), F32),
        scratch_shapes=[pltpu.VMEM((s // S_CHUNK, NSA_REP * Q_TILE, S_CHUNK), F32)],
        compiler_params=_cparams(("parallel", "parallel", "arbitrary")),
    )(q, kcmp, vcmp, ks, vs, kw, vw, gd, ovl_t)


def _conv_silu(cur_ref, ext_ref, tail_ref, w_ref, b_ref):
    l = cur_ref.shape[0]
    cur = cur_ref[...]
    ext_ref[0:SUBLANES, :] = tail_ref[...]
    ext_ref[SUBLANES:, :] = cur
    tail_ref[...] = cur[l - SUBLANES:, :]
    acc = jnp.zeros(cur.shape, F32) + b_ref[...]
    for k in range(SSD_CONV):
        off = SUBLANES - (SSD_CONV - 1) + k
        acc = acc + ext_ref[off:off + l, :] * w_ref[k:k + 1, :]
    return acc * _sigmoid(acc)


def _ssd_kernel(xs_ref, bm_ref, cm_ref, wx_ref, wb_ref, wc_ref, bx_ref, bb_ref, bc_ref, z_ref, gd_ref, dtt_ref,
                alane_ref, asub_ref, dskip_ref, ng_ref, tril_ref, edt_ref, eseg_ref, o_ref,
                tx_ref, tb_ref, tc_ref, ex_ref, eb_ref, ec_ref, st_ref):
    c = pl.program_id(2)
    l = xs_ref.shape[0]

    @pl.when(c == 0)
    def _():
        tx_ref[...] = jnp.zeros(tx_ref.shape, F32)
        tb_ref[...] = jnp.zeros(tb_ref.shape, F32)
        tc_ref[...] = jnp.zeros(tc_ref.shape, F32)
        st_ref[...] = jnp.zeros(st_ref.shape, F32)

    xs = _conv_silu(xs_ref, ex_ref, tx_ref, wx_ref, bx_ref)
    bm = _conv_silu(bm_ref, eb_ref, tb_ref, wb_ref, bb_ref)
    cm = _conv_silu(cm_ref, ec_ref, tc_ref, wc_ref, bc_ref)
    gd = gd_ref[...]
    tril = tril_ref[...]
    adt_c = gd * (-jnp.exp(alane_ref[0]))
    acum_c = _dot_exact_rhs(tril, adt_c)
    dt_full = _dot_exact_lhs(gd, edt_ref[...])
    ac_full = _dot_exact_lhs(acum_c, edt_ref[...])
    ac_seg = _dot_exact_lhs(acum_c, eseg_ref[...])
    adt_t = dtt_ref[...] * (-jnp.exp(asub_ref[0]))
    acum_t = _dot_nt_exact_lhs(adt_t, tril)

    xdt = xs * dt_full
    cmb = cm.astype(BF16)
    cb = _dot_nt(cmb, bm.astype(BF16))
    causal = lax.broadcasted_iota(jnp.int32, (l, l), 0) >= lax.broadcasted_iota(jnp.int32, (l, l), 1)
    head_of_lane = lax.broadcasted_iota(jnp.int32, xs.shape, 1) // SSD_HEAD_DIM
    y = jnp.zeros(xs.shape, F32)
    for r in range(SSD_REP):
        seg = jnp.exp(jnp.where(causal, ac_seg[:, r * l:(r + 1) * l] - acum_t[r:r + 1, :], NEG_BIG))
        x_r = jnp.where(head_of_lane == r, xdt, 0.0).astype(BF16)
        y = y + _dot((cb * seg).astype(BF16), x_r)
    ac_last = ac_full[l - 1:l, :]
    state = st_ref[...]
    y = y + _dot(cmb, state.astype(BF16)) * jnp.exp(ac_full)
    decayed = (xdt * jnp.exp(ac_last - ac_full)).astype(BF16)
    st_ref[...] = state * jnp.exp(ac_last) + _dot(bm.T.astype(BF16), decayed)
    y = y + xs * dskip_ref[...]
    z = z_ref[...]
    yz = y * (z * _sigmoid(z))
    o_ref[...] = yz * lax.rsqrt(jnp.mean(yz * yz, axis=-1, keepdims=True) + EPS) * ng_ref[...]


def _ssd(xbc, z, gd, dtt, conv_w, conv_b, a_log, d_skip, norm_g, b, s):
    t = xbc.shape[0]
    l = SSD_CHUNK
    nc = s // l
    gw = SSD_WIDTH // SSD_GROUPS
    xcols = gw // LANES
    b_blk0 = SSD_WIDTH // LANES
    c_blk0 = b_blk0 + SSD_GROUPS * SSD_STATE // LANES
    conv_b2 = conv_b[None, :]
    a_grp = a_log.reshape(SSD_GROUPS, SSD_REP)
    alane = jnp.zeros((SSD_GROUPS, 1, LANES), F32).at[:, 0, 16:16 + SSD_REP].set(a_grp)
    asub = jnp.zeros((SSD_GROUPS, SUBLANES, LANES), F32).at[:, :SSD_REP, :].set(
        jnp.broadcast_to(a_grp[:, :, None], (SSD_GROUPS, SSD_REP, LANES)))
    dskip = jnp.repeat(d_skip, SSD_HEAD_DIM)[None, :]
    ng = norm_g[None, :]
    ii = np.arange(l)
    tril = jnp.asarray(ii[:, None] >= ii[None, :], BF16)
    edt = np.zeros((LANES, gw), np.float32)
    eseg = np.zeros((LANES, SSD_REP * l), np.float32)
    for r in range(SSD_REP):
        edt[16 + r, r * SSD_HEAD_DIM:(r + 1) * SSD_HEAD_DIM] = 1.0
        eseg[16 + r, r * l:(r + 1) * l] = 1.0
    edt, eseg = jnp.asarray(edt, BF16), jnp.asarray(eseg, BF16)

    def rowblk(width, col):
        return pl.BlockSpec((l, width), lambda bi, g, c: (bi * nc + c, col(g)))

    def const(shape, col=None):
        if col is None:
            return pl.BlockSpec(shape, lambda bi, g, c: (0,) * len(shape))
        return pl.BlockSpec(shape, lambda bi, g, c: (0, col(g)))

    return pl.pallas_call(
        _ssd_kernel,
        name="ssd",
        grid=(b, SSD_GROUPS, nc),
        in_specs=[rowblk(gw, lambda g: g), rowblk(LANES, lambda g: b_blk0 + g), rowblk(LANES, lambda g: c_blk0 + g),
                  const((SSD_CONV, gw), lambda g: g), const((SSD_CONV, LANES), lambda g: b_blk0 + g),
                  const((SSD_CONV, LANES), lambda g: c_blk0 + g),
                  const((1, gw), lambda g: g), const((1, LANES), lambda g: b_blk0 + g),
                  const((1, LANES), lambda g: c_blk0 + g),
                  rowblk(gw, lambda g: g), rowblk(LANES, lambda g: g),
                  pl.BlockSpec((SUBLANES, l), lambda bi, g, c: (g, bi * nc + c)),
                  pl.BlockSpec((1, 1, LANES), lambda bi, g, c: (g, 0, 0)),
                  pl.BlockSpec((1, SUBLANES, LANES), lambda bi, g, c: (g, 0, 0)),
                  const((1, gw), lambda g: g), const((1, gw), lambda g: g),
                  const((l, l)), const((LANES, gw)), const((LANES, SSD_REP * l))],
        out_specs=rowblk(gw, lambda g: g),
        out_shape=jax.ShapeDtypeStruct((t, SSD_WIDTH), F32),
        scratch_shapes=[pltpu.VMEM((SUBLANES, gw), F32), pltpu.VMEM((SUBLANES, LANES), F32),
                        pltpu.VMEM((SUBLANES, LANES), F32),
                        pltpu.VMEM((l + SUBLANES, gw), F32), pltpu.VMEM((l + SUBLANES, LANES), F32),
                        pltpu.VMEM((l + SUBLANES, LANES), F32),
                        pltpu.VMEM((SSD_STATE, gw), F32)],
        compiler_params=_cparams(("parallel", "parallel", "arbitrary")),
    )(xbc, xbc, xbc, conv_w, conv_w, conv_w, conv_b2, conv_b2, conv_b2, z, gd, dtt,
      alane, asub, dskip, ng, tril, edt, eseg)


def _rms(v, g):
    return v * lax.rsqrt(jnp.mean(v * v, axis=-1, keepdims=True) + EPS) * g


def _out_proj_kernel(x_ref, on_ref, os_ref, ng_ref, w1_ref, w2_ref, fg_ref, x1_ref, xnt_ref):
    onn = _rms(on_ref[...], ng_ref[...]).astype(BF16)
    x1 = x_ref[...] + _dot(onn, w1_ref[...]) + _dot(os_ref[...].astype(BF16), w2_ref[...])
    x1_ref[...] = x1
    xnt_ref[...] = _rms(x1, fg_ref[...]).T.astype(BF16)


def _out_proj(xt, o_nsa, o_ssd, nsa_g, w_out, ffn_g):
    t = xt.shape[0]
    tm = min(TOK_TILE, t)

    def tok(width):
        return pl.BlockSpec((tm, width), lambda i: (i, 0))

    def full(shape):
        return pl.BlockSpec(shape, lambda i: (0, 0))

    wb = w_out.astype(BF16)
    return pl.pallas_call(
        _out_proj_kernel,
        name="out_proj",
        grid=(t // tm,),
        in_specs=[tok(D_MODEL), tok(NSA_WIDTH), tok(SSD_WIDTH), full((1, NSA_WIDTH)),
                  full((NSA_WIDTH, D_MODEL)), full((SSD_WIDTH, D_MODEL)), full((1, D_MODEL))],
        out_specs=[tok(D_MODEL), pl.BlockSpec((D_MODEL, tm), lambda i: (0, i))],
        out_shape=[jax.ShapeDtypeStruct((t, D_MODEL), F32), jax.ShapeDtypeStruct((D_MODEL, t), BF16)],
        compiler_params=_cparams(("parallel",)),
    )(xt, o_nsa, o_ssd, nsa_g[None, :], wb[:NSA_WIDTH], wb[NSA_WIDTH:], ffn_g[None, :])


def _top_sorted(e, k):
    rows = lax.broadcasted_iota(jnp.int32, (k, e.shape[1]), 0)
    out = jnp.zeros((k, e.shape[1]), F32)
    cur = e
    for j in range(k):
        mk = jnp.max(cur, axis=0, keepdims=True)
        out = jnp.where(rows == j, jnp.maximum(mk, 0.0), out)
        cur = jnp.where((cur == mk) & (mk >= 0.0), -float(j + 1), cur)
    rank = jnp.where(cur < 0.0, -1.0 - cur, float(k))
    return out, rank


def _pair_candidates(a16, b16):
    row8 = lax.broadcasted_iota(jnp.int32, (SUBLANES, a16.shape[1]), 0)
    groups = [a16[0:1] * b16[0:SUBLANES], a16[0:1] * b16[SUBLANES:], a16[1:2] * b16[0:SUBLANES]]
    for a in range(2, SUBLANES):
        groups.append(jnp.where(row8 < PEER_TOPK // (a + 1), a16[a:a + 1] * b16[0:SUBLANES], -1.0))
    groups.append(a16[SUBLANES:] * b16[0:1])
    return jnp.concatenate(groups, axis=0)


def _peer_front_kernel(xnt_ref, wq_ref, keys_ref, av_ref, lc_ref, bv_ref, rb_ref):
    xt = xnt_ref[...]
    tm = xt.shape[1]
    row8 = lax.broadcasted_iota(jnp.int32, (SUBLANES, tm), 0)
    for h in range(PEER_HEADS):
        qh = _dot(wq_ref[h * PEER_KEY_DIM:(h + 1) * PEER_KEY_DIM, :], xt)
        s1 = _dot(keys_ref[h, 0], qh[:PEER_HALF].astype(BF16))
        s2 = _dot(keys_ref[h, 1], qh[PEER_HALF:].astype(BF16))
        e1 = jnp.exp(s1 - jnp.max(s1, axis=0, keepdims=True))
        e2 = jnp.exp(s2 - jnp.max(s2, axis=0, keepdims=True))
        a16, ra = _top_sorted(e1, PEER_TOPK)
        b16, rb = _top_sorted(e2, PEER_TOPK)
        cand = _pair_candidates(a16, b16)
        cur = cand
        zsum = jnp.zeros((1, tm), F32)
        for _ in range(PEER_TOPK):
            tau = jnp.max(cur, axis=0, keepdims=True)
            zsum = zsum + tau
            cur = jnp.where(cur == tau, -1.0, cur)
        picked = jnp.where(cand >= tau, 1.0, 0.0)

        def count(g):
            return jnp.sum(picked[g * SUBLANES:(g + 1) * SUBLANES], axis=0, keepdims=True)

        lens = [count(0) + count(1)] + [count(a + 1) for a in range(1, SUBLANES)]
        len_lo = jnp.zeros((SUBLANES, tm), F32)
        for a, row in enumerate(lens):
            len_lo = jnp.where(row8 == a, row, len_lo)
        len16 = jnp.concatenate([len_lo, picked[(SUBLANES + 1) * SUBLANES:]], axis=0)
        pack = 2 * SUBLANES
        ra_b = ra.astype(BF16)
        lc = jnp.zeros(ra.shape, BF16)
        for a in range(PEER_TOPK):
            len_a = jnp.tile(jnp.broadcast_to(len16[a:a + 1], (pack, tm)).astype(BF16), (PEER_N_KEYS // pack, 1))
            lc = jnp.where(ra_b == float(a), len_a, lc)
        rows = slice(h * PEER_N_KEYS, (h + 1) * PEER_N_KEYS)
        av_ref[rows, :] = e1
        lc_ref[rows, :] = lc.astype(F32)
        bv_ref[rows, :] = (e2 * (1.0 / zsum)).astype(BF16)
        rb_ref[rows, :] = rb.astype(BF16)


def _peer_front(xnt, w_q, keys):
    t = xnt.shape[1]
    tm = min(PEER_FRONT_TOK, t)
    nk = PEER_HEADS * PEER_N_KEYS
    return pl.pallas_call(
        _peer_front_kernel,
        name="peer_front",
        grid=(t // tm,),
        in_specs=[pl.BlockSpec((D_MODEL, tm), lambda i: (0, i)),
                  pl.BlockSpec((PEER_HEADS * PEER_KEY_DIM, D_MODEL), lambda i: (0, 0)),
                  pl.BlockSpec((PEER_HEADS, 2, PEER_N_KEYS, PEER_HALF), lambda i: (0, 0, 0, 0))],
        out_specs=[pl.BlockSpec((nk, tm), lambda i: (0, i))] * 4,
        out_shape=[jax.ShapeDtypeStruct((nk, t), F32), jax.ShapeDtypeStruct((nk, t), F32),
                   jax.ShapeDtypeStruct((nk, t), BF16), jax.ShapeDtypeStruct((nk, t), BF16)],
        compiler_params=_cparams(("parallel",)),
    )(xnt, w_q.T.astype(BF16), keys.astype(BF16))


def _row_bf16(ref, row, n_rows):
    pack = 2 * SUBLANES
    tile = jnp.broadcast_to(ref[pl.ds(row, 1), :], (pack, ref.shape[1])).astype(BF16)
    return jnp.tile(tile, (n_rows // pack, 1))


def _peer_dense_kernel(xnt_ref, u_ref, vt_ref, av_ref, lc_ref, bv_ref, rb_ref, o_ref, hid_ref, act_ref):
    j = pl.program_id(1)
    n_sub = u_ref.shape[0] // PEER_N_KEYS

    @pl.when(j == 0)
    def _():
        o_ref[...] = jnp.zeros(o_ref.shape, F32)

    hid_ref[...] = _dot(u_ref[...], xnt_ref[...])

    def sub(cl, carry):
        e1 = j * n_sub + cl
        gate = jnp.zeros((PEER_N_KEYS, xnt_ref.shape[1]), BF16)
        for h in range(PEER_HEADS):
            rows = slice(h * PEER_N_KEYS, (h + 1) * PEER_N_KEYS)
            lc_row = _row_bf16(lc_ref, h * PEER_N_KEYS + e1, PEER_N_KEYS)
            av_row = _row_bf16(av_ref, h * PEER_N_KEYS + e1, PEER_N_KEYS)
            bv = bv_ref[rows, :]
            gate = gate + jnp.where(rb_ref[rows, :] < lc_row, bv, jnp.zeros_like(bv)) * av_row
        rr = slice(cl * PEER_N_KEYS, (cl + 1) * PEER_N_KEYS)
        act_ref[rr, :] = _gelu_sigmoid(hid_ref[rr, :]).astype(BF16) * gate
        return carry

    for cl in range(n_sub):
        sub(cl, 0)
    o_ref[...] += _dot(vt_ref[...], act_ref[...])


def _peer_dense(xnt, u, vt, av, lc, bv, rb):
    t = xnt.shape[1]
    tm = min(PEER_TOK, t)
    ne = u.shape[0]
    te = PEER_EXP
    tok = pl.BlockSpec((PEER_HEADS * PEER_N_KEYS, tm), lambda i, j: (0, i))
    return pl.pallas_call(
        _peer_dense_kernel,
        name="peer_dense",
        grid=(t // tm, ne // te),
        in_specs=[pl.BlockSpec((D_MODEL, tm), lambda i, j: (0, i)),
                  pl.BlockSpec((te, D_MODEL), lambda i, j: (j, 0)),
                  pl.BlockSpec((D_MODEL, te), lambda i, j: (0, j)),
                  tok, tok, tok, tok],
        out_specs=pl.BlockSpec((D_MODEL, tm), lambda i, j: (0, i)),
        out_shape=jax.ShapeDtypeStruct((D_MODEL, t), F32),
        scratch_shapes=[pltpu.VMEM((te, tm), F32), pltpu.VMEM((te, tm), BF16)],
        compiler_params=_cparams(("parallel", "arbitrary")),
    )(xnt, u, vt, av, lc, bv, rb)


def _final_kernel(x1_ref, pt_ref, g_ref, o_ref):
    o_ref[...] = _rms(x1_ref[...] + pt_ref[...].T, g_ref[...])


def _residual_norm(x1, peer_t, g):
    t = x1.shape[0]
    tm = min(TOK_TILE, t)
    return pl.pallas_call(
        _final_kernel,
        name="residual_norm",
        grid=(t // tm,),
        in_specs=[pl.BlockSpec((tm, D_MODEL), lambda i: (i, 0)), pl.BlockSpec((D_MODEL, tm), lambda i: (0, i)),
                  pl.BlockSpec((1, D_MODEL), lambda i: (0, 0))],
        out_specs=pl.BlockSpec((tm, D_MODEL), lambda i: (i, 0)),
        out_shape=jax.ShapeDtypeStruct((t, D_MODEL), F32),
        compiler_params=_cparams(("parallel",)),
    )(x1, peer_t, g[None, :])


def kernel(x, attn_norm_g, w_in, cmp_pos_k, cmp_w1_k, cmp_b1_k, cmp_w2_k, cmp_pos_v, cmp_w1_v, cmp_b1_v, cmp_w2_v,
           conv_w, conv_b, dt_bias, a_log, d_skip, ssd_norm_g, nsa_norm_g, w_out, ffn_norm_g, peer_w_q, peer_keys,
           peer_u, peer_v, final_norm_g):
    b, s, d = x.shape
    xt = x.reshape(b * s, d)
    assert attn_norm_g.shape[0] == 1, "single-layer block"
    for layer in range(1):
        q, kc, vc, ks, vs, kw, vw, gd, dtt, z, xbc = _in_proj(xt, attn_norm_g[layer], w_in[layer], dt_bias[layer], s)
        kcmp = _compress(kc, cmp_pos_k[layer], cmp_w1_k[layer], cmp_b1_k[layer], cmp_w2_k[layer], b, s)
        vcmp = _compress(vc, cmp_pos_v[layer], cmp_w1_v[layer], cmp_b1_v[layer], cmp_w2_v[layer], b, s)
        o_nsa = _nsa(q, kcmp, vcmp, ks, vs, kw, vw, gd, b, s)
        o_ssd = _ssd(xbc, z, gd, dtt, conv_w[layer], conv_b[layer], a_log[layer], d_skip[layer],
                     ssd_norm_g[layer], b, s)
        x1, xnt = _out_proj(xt, o_nsa, o_ssd, nsa_norm_g[layer], w_out[layer], ffn_norm_g[layer])
        av, lc, bv, rb = _peer_front(xnt, peer_w_q[layer], peer_keys[layer])
        peer_t = _peer_dense(xnt, peer_u[layer].astype(BF16), peer_v[layer].T.astype(BF16), av, lc, bv, rb)
    return _residual_norm(x1, peer_t, final_norm_g).reshape(b, s, d)
```

```python
import functools

import numpy as np
import jax
import jax.numpy as jnp
from jax import lax
from jax.experimental import pallas as pl
from jax.experimental.pallas import tpu as pltpu

F32 = jnp.float32
BF16 = jnp.bfloat16

EPS = 1e-6
D_MODEL = 1024
NSA_HEADS = 8
NSA_KV_GROUPS = 2
NSA_REP = NSA_HEADS // NSA_KV_GROUPS
HEAD_DIM = 64
NSA_WIDTH = NSA_HEADS * HEAD_DIM
KV_WIDTH = NSA_KV_GROUPS * HEAD_DIM
CMP_STRIDE = 16
CMP_BLOCK = 32
CMP_HIDDEN = 256
SEL_BLOCK = 64
SEL_TOP_N = 16
WINDOW = 512
ROPE_DIM = HEAD_DIM // 4
ROPE_THETA = 500000.0
SSD_HEADS = 8
SSD_HEAD_DIM = 64
SSD_WIDTH = SSD_HEADS * SSD_HEAD_DIM
SSD_GROUPS = 2
SSD_REP = SSD_HEADS // SSD_GROUPS
SSD_STATE = 128
SSD_CONV = 4
SSD_CHUNK = 128
SSD_CONV_DIM = SSD_WIDTH + 2 * SSD_GROUPS * SSD_STATE
PEER_HEADS = 8
PEER_N_KEYS = 128
PEER_N_EXPERTS = PEER_N_KEYS * PEER_N_KEYS
PEER_KEY_DIM = 256
PEER_HALF = PEER_KEY_DIM // 2
PEER_TOPK = 16

LANES = 128
SUBLANES = 8
VMEM_LIMIT = 48 * 1024 * 1024
NEG_BIG = -1e30
SEL_NEG = -1e9

TOK_TILE = 512
Q_TILE = 2 * SEL_BLOCK
K_TILE = 2 * SEL_BLOCK
S_CHUNK = 4 * K_TILE
PEER_TOK = 512
PEER_EXP = 1024
PEER_FRONT_TOK = 256


def _cparams(sem, flags=None):
    return pltpu.CompilerParams(dimension_semantics=sem, vmem_limit_bytes=VMEM_LIMIT, flags=flags)


def _dot(a, b):
    return jnp.dot(a, b, preferred_element_type=F32)


def _dot_nt(a, b):
    return lax.dot_general(a, b, (((1,), (1,)), ((), ())), preferred_element_type=F32)


def _split3(a):
    a1 = a.astype(BF16)
    r1 = a - a1.astype(F32)
    a2 = r1.astype(BF16)
    a3 = (r1 - a2.astype(F32)).astype(BF16)
    return a1, a2, a3


def _dot_exact_lhs(a, b01):
    a1, a2, a3 = _split3(a)
    return _dot(a1, b01) + _dot(a2, b01) + _dot(a3, b01)


def _dot_nt_exact_lhs(a, b01):
    a1, a2, a3 = _split3(a)
    return _dot_nt(a1, b01) + _dot_nt(a2, b01) + _dot_nt(a3, b01)


def _dot_exact_rhs(a01, b):
    b1, b2, b3 = _split3(b)
    return _dot(a01, b1) + _dot(a01, b2) + _dot(a01, b3)


def _dot_nt_exact_rhs(a01, b):
    b1, b2, b3 = _split3(b)
    return _dot_nt(a01, b1) + _dot_nt(a01, b2) + _dot_nt(a01, b3)


def _softplus(x):
    return jnp.maximum(x, 0.0) + jnp.log(1.0 + jnp.exp(-jnp.abs(x)))


def _sigmoid(x):
    return 1.0 / (1.0 + jnp.exp(-x))


def _gelu_tanh(x):
    return 0.5 * x * (1.0 + jnp.tanh(0.7978845608028654 * (x + 0.044715 * (x * x * x))))


def _gelu_sigmoid(x):
    c1 = -2.0 * 0.7978845608028654 * 1.4426950408889634
    u = x * (c1 + (c1 * 0.044715) * (x * x))
    return x / (1.0 + jnp.exp2(u))


def _rope128(p, cos, sa, sb):
    return p * cos + pltpu.roll(p, LANES - ROPE_DIM // 2, 1) * sa + pltpu.roll(p, ROPE_DIM // 2, 1) * sb


def _in_proj_kernel(seq_tiles, x_ref, g_ref, w_ref, wdt_ref, cos_ref, sa_ref, sb_ref, gdb_ref, dtb_ref,
                    q_ref, kc_ref, vc_ref, ks_ref, vs_ref, kw_ref, vw_ref, gd_ref, dtt_ref, z_ref, xbc_ref):
    tm = x_ref.shape[0]
    x = x_ref[...]
    h = x * lax.rsqrt(jnp.mean(x * x, axis=-1, keepdims=True) + EPS) * g_ref[...]
    hb = h.astype(BF16)
    pos0 = pl.multiple_of((pl.program_id(0) % seq_tiles) * tm, tm)
    cos = cos_ref[pl.ds(pos0, tm), :]
    sa = sa_ref[pl.ds(pos0, tm), :]
    sb = sb_ref[pl.ds(pos0, tm), :]
    lane = lax.broadcasted_iota(jnp.int32, (tm, LANES), 1)
    low = lane < HEAD_DIM

    def proj(c0, width):
        return _dot(hb, w_ref[:, c0:c0 + width])

    def halves(p):
        return p, pltpu.roll(p, HEAD_DIM, 1)

    pq = proj(0, NSA_WIDTH)
    for s in range(NSA_WIDTH // LANES):
        srcs = halves(_rope128(pq[:, s * LANES:(s + 1) * LANES], cos, sa, sb) * (HEAD_DIM ** -0.5))
        for half in range(2):
            head = 2 * s + half
            q_ref[:, head * LANES:(head + 1) * LANES] = jnp.where(low, srcs[half], 0.0).astype(BF16)
    c0 = NSA_WIDTH
    kc_ref[...] = _rope128(proj(c0, LANES), cos, sa, sb).astype(BF16)
    vc_ref[...] = proj(c0 + LANES, LANES).astype(BF16)
    c0 += 2 * LANES
    blk_of_row = (pos0 + lax.broadcasted_iota(jnp.int32, (tm, LANES), 0)) // SEL_BLOCK
    onehot = jnp.where(lane - HEAD_DIM == blk_of_row, 1.0, 0.0)
    for ref, rope, fill in ((ks_ref, True, onehot), (vs_ref, False, 1.0), (kw_ref, True, 0.0), (vw_ref, False, 1.0)):
        p = proj(c0, LANES)
        if rope:
            p = _rope128(p, cos, sa, sb)
        for g, src in enumerate(halves(p)):
            ref[:, g * LANES:(g + 1) * LANES] = jnp.where(low, src, fill).astype(BF16)
        c0 += LANES
    for s in range(2):
        p = proj(c0, LANES) + gdb_ref[:, s * LANES:(s + 1) * LANES]
        act = jnp.where(lane < 3 * NSA_REP, _sigmoid(p),
                        jnp.where((lane >= 16) & (lane < 16 + SSD_REP), _softplus(p), 0.0))
        gd_ref[:, s * LANES:(s + 1) * LANES] = act
        c0 += LANES
    z_ref[...] = proj(c0, SSD_WIDTH)
    c0 += SSD_WIDTH
    xbc_ref[...] = proj(c0, SSD_CONV_DIM)
    dtt_ref[...] = _softplus(_dot_nt(wdt_ref[...], hb) + dtb_ref[:, 0:1])


def _rope_tables(s):
    half = ROPE_DIM // 2
    inv = jnp.power(ROPE_THETA, -jnp.arange(half, dtype=F32) * 2.0 / ROPE_DIM)
    ang = jnp.arange(s).astype(F32)[:, None] * inv[None, :]
    cos, sin = jnp.cos(ang), jnp.sin(ang)
    zeros = jnp.zeros((s, HEAD_DIM - ROPE_DIM), F32)
    cos64 = jnp.concatenate([cos, cos, jnp.ones((s, HEAD_DIM - ROPE_DIM), F32)], axis=1)
    sa64 = jnp.concatenate([-sin, jnp.zeros_like(sin), zeros], axis=1)
    sb64 = jnp.concatenate([jnp.zeros_like(sin), sin, zeros], axis=1)
    return tuple(jnp.concatenate([t, t], axis=1) for t in (cos64, sa64, sb64))


def _in_proj(xt, attn_g, w_in, dt_bias, s):
    t = xt.shape[0]
    tm = min(TOK_TILE, s)
    o_gl = NSA_WIDTH + 6 * KV_WIDTH
    o_z = o_gl + 3 * NSA_HEADS
    o_xbc = o_z + SSD_WIDTH
    o_dt = o_xbc + SSD_CONV_DIM
    gd_cols, gd_bias = [], []
    for g in range(NSA_KV_GROUPS):
        gates = w_in[:, o_gl + 3 * NSA_REP * g:o_gl + 3 * NSA_REP * (g + 1)]
        dts = w_in[:, o_dt + SSD_REP * g:o_dt + SSD_REP * (g + 1)]
        gd_cols += [gates, jnp.zeros((D_MODEL, 16 - 3 * NSA_REP), F32), dts,
                    jnp.zeros((D_MODEL, LANES - 16 - SSD_REP), F32)]
        gd_bias += [jnp.zeros((16,), F32), dt_bias[SSD_REP * g:SSD_REP * (g + 1)],
                    jnp.zeros((LANES - 16 - SSD_REP,), F32)]
    w_main = jnp.concatenate([w_in[:, :o_gl]] + gd_cols + [w_in[:, o_z:o_dt]], axis=1).astype(BF16)
    gdb = jnp.concatenate(gd_bias)[None, :]
    wdt_rows, dtb_rows = [], []
    for g in range(SSD_GROUPS):
        wdt_rows += [w_in[:, o_dt + SSD_REP * g:o_dt + SSD_REP * (g + 1)].T,
                     jnp.zeros((SUBLANES - SSD_REP, D_MODEL), F32)]
        dtb_rows += [dt_bias[SSD_REP * g:SSD_REP * (g + 1)], jnp.zeros((SUBLANES - SSD_REP,), F32)]
    wdt = jnp.concatenate(wdt_rows, axis=0).astype(BF16)
    dtb = jnp.broadcast_to(jnp.concatenate(dtb_rows)[:, None], (2 * SUBLANES, LANES))
    cos, sa, sb = _rope_tables(s)
    n_main = w_main.shape[1]

    def full(shape):
        return pl.BlockSpec(shape, lambda i: (0, 0))

    def tok(width):
        return pl.BlockSpec((tm, width), lambda i: (i, 0))

    kv_widths = [LANES, LANES] + [NSA_KV_GROUPS * LANES] * 4
    out_shapes = ([jax.ShapeDtypeStruct((t, NSA_HEADS * LANES), BF16)]
                  + [jax.ShapeDtypeStruct((t, w), BF16) for w in kv_widths]
                  + [jax.ShapeDtypeStruct((t, 2 * LANES), F32),
                     jax.ShapeDtypeStruct((2 * SUBLANES, t), F32),
                     jax.ShapeDtypeStruct((t, SSD_WIDTH), F32),
                     jax.ShapeDtypeStruct((t, SSD_CONV_DIM), F32)])
    out_specs = ([tok(NSA_HEADS * LANES)] + [tok(w) for w in kv_widths]
                 + [tok(2 * LANES), pl.BlockSpec((2 * SUBLANES, tm), lambda i: (0, i)),
                    tok(SSD_WIDTH), tok(SSD_CONV_DIM)])
    return pl.pallas_call(
        functools.partial(_in_proj_kernel, s // tm),
        name="in_proj",
        grid=(t // tm,),
        in_specs=[tok(D_MODEL), full((1, D_MODEL)), full((D_MODEL, n_main)), full((2 * SUBLANES, D_MODEL)),
                  full((s, LANES)), full((s, LANES)), full((s, LANES)), full((1, 2 * LANES)),
                  full((2 * SUBLANES, LANES))],
        out_specs=out_specs,
        out_shape=out_shapes,
        compiler_params=_cparams(("parallel",)),
    )(xt, attn_g[None, :], w_main, wdt, cos, sa, sb, gdb, dtb)


def _compress_kernel(kv_ref, w1a_ref, w1b_ref, pos_ref, w1_ref, b1_ref, w2_ref, out_ref):
    kv = kv_ref[0]
    bias = _dot(pos_ref[...], w1_ref[...])[0:1, :] + b1_ref[...]
    for g in range(NSA_KV_GROUPS):
        first = _dot(kv, w1a_ref[g])
        second = _dot(kv, w1b_ref[g])
        nxt = pltpu.roll(second, second.shape[0] - 1, 0)
        hid = _gelu_tanh(first + nxt + bias)
        out_ref[0, :, g * LANES:(g + 1) * LANES] = _dot(hid.astype(BF16), w2_ref[...]).astype(BF16)


def _compress(kv, pos_emb, w1, b1, w2, b, s):
    nch = s // CMP_STRIDE
    kvf = kv.reshape(b, nch, CMP_STRIDE * LANES)
    w1r = w1.reshape(CMP_BLOCK, HEAD_DIM, CMP_HIDDEN)
    zeros = jnp.zeros((CMP_STRIDE, HEAD_DIM, CMP_HIDDEN), F32)

    def expand(w_half, g):
        parts = [w_half, zeros] if g == 0 else [zeros, w_half]
        return jnp.concatenate(parts, axis=1).reshape(CMP_STRIDE * LANES, CMP_HIDDEN)

    w1a = jnp.stack([expand(w1r[:CMP_STRIDE], g) for g in range(NSA_KV_GROUPS)]).astype(BF16)
    w1b = jnp.stack([expand(w1r[CMP_STRIDE:], g) for g in range(NSA_KV_GROUPS)]).astype(BF16)
    w2e = jnp.concatenate([w2, jnp.zeros((CMP_HIDDEN, LANES - HEAD_DIM), F32)], axis=1).astype(BF16)
    pos = jnp.zeros((SUBLANES, CMP_BLOCK * HEAD_DIM), F32).at[0].set(pos_emb.reshape(-1)).astype(BF16)
    return pl.pallas_call(
        _compress_kernel,
        name="compress",
        grid=(b,),
        in_specs=[pl.BlockSpec((1, nch, CMP_STRIDE * LANES), lambda i: (i, 0, 0)),
                  pl.BlockSpec((2, CMP_STRIDE * LANES, CMP_HIDDEN), lambda i: (0, 0, 0)),
                  pl.BlockSpec((2, CMP_STRIDE * LANES, CMP_HIDDEN), lambda i: (0, 0, 0)),
                  pl.BlockSpec((SUBLANES, CMP_BLOCK * HEAD_DIM), lambda i: (0, 0)),
                  pl.BlockSpec((CMP_BLOCK * HEAD_DIM, CMP_HIDDEN), lambda i: (0, 0)),
                  pl.BlockSpec((1, CMP_HIDDEN), lambda i: (0, 0)),
                  pl.BlockSpec((CMP_HIDDEN, LANES), lambda i: (0, 0))],
        out_specs=pl.BlockSpec((1, nch, NSA_KV_GROUPS * LANES), lambda i: (i, 0, 0)),
        out_shape=jax.ShapeDtypeStruct((b, nch, NSA_KV_GROUPS * LANES), BF16),
        compiler_params=_cparams(("parallel",)),
    )(kvf, w1a, w1b, pos, w1.astype(BF16), b1[None, :], w2e)


def _lane_tiles(a):
    return [a[:, k * LANES:(k + 1) * LANES] for k in range(a.shape[1] // LANES)]


def _max_tiles(macc, s):
    for tile in _lane_tiles(s):
        macc = jnp.maximum(macc, tile)
    return macc


def _exp_tiles(s, m_b):
    return jnp.concatenate([jnp.exp(tile - m_b) for tile in _lane_tiles(s)], axis=1).astype(BF16)


def _normalize(acc):
    return acc / pltpu.roll(acc, HEAD_DIM, 1)


def _block_rank(imp):
    nblk, tq = imp.shape
    sub = lax.broadcasted_iota(jnp.int32, (SUBLANES, tq), 0)
    groups = [imp[SUBLANES * v:SUBLANES * (v + 1)] for v in range(nblk // SUBLANES)]
    cnt = [jnp.zeros((SUBLANES, tq), F32) for _ in groups]
    for j in range(nblk):
        row = imp[j:j + 1, :]
        for v, grp in enumerate(groups):
            if SUBLANES * v > j:
                ahead = row >= grp
            elif SUBLANES * (v + 1) <= j:
                ahead = row > grp
            else:
                ahead = (row > grp) | ((row == grp) & (sub > j - SUBLANES * v))
            cnt[v] = cnt[v] + jnp.where(ahead, 1.0, 0.0)
    return jnp.concatenate(cnt, axis=0)


def _nsa_kernel(q_ref, kc_ref, vc_ref, ks_ref, vs_ref, kw_ref, vw_ref, gd_ref, ovl_ref, o_ref, s_ref):
    i = pl.program_id(2)
    tq = Q_TILE
    rows = NSA_REP * tq
    ncmp = kc_ref.shape[1]
    nblk = ovl_ref.shape[0]
    q = jnp.concatenate([q_ref[:, r * LANES:(r + 1) * LANES] for r in range(NSA_REP)], axis=0)
    t_row = i * tq + lax.broadcasted_iota(jnp.int32, (rows, 1), 0) % tq

    s_c = _dot_nt(q, kc_ref[0])
    cmp_end = lax.broadcasted_iota(jnp.int32, (rows, ncmp), 1) * CMP_STRIDE + (CMP_BLOCK - 1)
    mask_c = cmp_end <= t_row
    s_c = jnp.where(mask_c, s_c, NEG_BIG)
    e_c = jnp.exp(s_c - jnp.max(s_c, axis=-1, keepdims=True))
    p_c = jnp.where(mask_c, e_c / jnp.sum(e_c, axis=-1, keepdims=True), 0.0)
    o_cmp = _dot(p_c.astype(BF16), vc_ref[0])

    p_sum = p_c[0:tq]
    for r in range(1, NSA_REP):
        p_sum = p_sum + p_c[r * tq:(r + 1) * tq]
    imp = _dot_nt_exact_rhs(ovl_ref[...], p_sum)
    blk = lax.broadcasted_iota(jnp.int32, (nblk, tq), 0)
    cur = (i * tq + lax.broadcasted_iota(jnp.int32, (nblk, tq), 1)) // SEL_BLOCK
    forced = (blk == 0) | (blk == cur) | (blk == cur - 1)
    valid = blk <= cur
    imp = jnp.where(forced, 1e9, jnp.where(valid, imp, -1.0))

    sel_t = jnp.where((_block_rank(imp) < float(min(SEL_TOP_N, nblk))) & valid, 1.0, 0.0)
    ones_lo = jnp.ones((HEAD_DIM, tq), F32)
    sel_pad = jnp.concatenate([ones_lo, sel_t] + ([jnp.ones((HEAD_DIM - nblk, tq), F32)] if nblk < HEAD_DIM else []),
                              axis=0).astype(BF16)
    eye = jnp.where(lax.broadcasted_iota(jnp.int32, (tq, tq), 0) == lax.broadcasted_iota(jnp.int32, (tq, tq), 1),
                    1.0, 0.0).astype(BF16)
    negb = ((_dot_nt(eye, sel_pad) - 1.0) * (-SEL_NEG)).astype(BF16)
    q2 = q + jnp.concatenate([negb] * NSA_REP, axis=0)

    jd = (i * tq) // K_TILE
    nfull = jd // (S_CHUNK // K_TILE)
    lane_c = lax.broadcasted_iota(jnp.int32, (rows, S_CHUNK), 1)

    def scores(c):
        k0 = pl.multiple_of(c * S_CHUNK, S_CHUNK)
        return _dot_nt(q2, ks_ref[pl.ds(k0, S_CHUNK), :])

    def pass_max(c, macc):
        s = scores(c)
        s_ref[c] = s
        return _max_tiles(macc, s)

    macc = lax.fori_loop(0, nfull, pass_max, jnp.full((rows, LANES), NEG_BIG, F32))
    s_last = jnp.where(nfull * S_CHUNK + lane_c <= t_row, scores(nfull), NEG_BIG)
    macc = _max_tiles(macc, s_last)
    m_b = jnp.broadcast_to(jnp.max(macc, axis=-1, keepdims=True), (rows, LANES))

    def pass_sum(c, acc):
        k0 = pl.multiple_of(c * S_CHUNK, S_CHUNK)
        return acc + _dot(_exp_tiles(s_ref[c], m_b), vs_ref[pl.ds(k0, S_CHUNK), :])

    acc = lax.fori_loop(0, nfull, pass_sum, jnp.zeros((rows, LANES), F32))
    k_last = pl.multiple_of(nfull * S_CHUNK, S_CHUNK)
    o_sel = _normalize(acc + _dot(_exp_tiles(s_last, m_b), vs_ref[pl.ds(k_last, S_CHUNK), :]))

    n_wk = WINDOW + K_TILE
    k0 = pl.multiple_of(jnp.maximum(jd - WINDOW // K_TILE, 0) * K_TILE, K_TILE)
    diff = t_row - (k0 + lax.broadcasted_iota(jnp.int32, (rows, n_wk), 1))
    s_w = jnp.where((diff >= 0) & (diff < WINDOW), _dot_nt(q2, kw_ref[pl.ds(k0, n_wk), :]), NEG_BIG)
    m_w = jnp.max(_max_tiles(jnp.full((rows, LANES), NEG_BIG, F32), s_w), axis=-1, keepdims=True)
    p_w = _exp_tiles(s_w, jnp.broadcast_to(m_w, (rows, LANES)))
    o_win = _normalize(_dot(p_w, vw_ref[pl.ds(k0, n_wk), :]))

    gd = gd_ref[...]
    low = lax.broadcasted_iota(jnp.int32, (tq, LANES), 1) < HEAD_DIM
    heads = []
    for r in range(NSA_REP):
        rs = slice(r * tq, (r + 1) * tq)
        heads.append(gd[:, 3 * r:3 * r + 1] * o_cmp[rs] + gd[:, 3 * r + 1:3 * r + 2] * o_sel[rs]
                     + gd[:, 3 * r + 2:3 * r + 3] * o_win[rs])
    for sidx in range(NSA_REP // 2):
        o_ref[:, sidx * LANES:(sidx + 1) * LANES] = jnp.where(
            low, heads[2 * sidx], pltpu.roll(heads[2 * sidx + 1], HEAD_DIM, 1))


def _nsa(q, kcmp, vcmp, ks, vs, kw, vw, gd, b, s):
    t = q.shape[0]
    nq = s // Q_TILE
    ncmp = s // CMP_STRIDE
    nblk = s // SEL_BLOCK
    cs = np.arange(ncmp) * CMP_STRIDE
    ss = np.arange(nblk) * SEL_BLOCK
    overlap = (cs[:, None] < ss[None, :] + SEL_BLOCK) & (cs[:, None] + CMP_BLOCK > ss[None, :])
    overlap[ncmp - 1, :] = False
    ovl_t = jnp.asarray(overlap.T, BF16)

    assert nblk <= HEAD_DIM and s % S_CHUNK == 0 and s >= WINDOW + K_TILE

    def seq_spec():
        return pl.BlockSpec((s, LANES), lambda bi, g, i: (bi, g))

    return pl.pallas_call(
        _nsa_kernel,
        name="nsa",
        grid=(b, NSA_KV_GROUPS, nq),
        in_specs=[pl.BlockSpec((Q_TILE, NSA_REP * LANES), lambda bi, g, i: (bi * nq + i, g)),
                  pl.BlockSpec((1, ncmp, LANES), lambda bi, g, i: (bi, 0, g)),
                  pl.BlockSpec((1, ncmp, LANES), lambda bi, g, i: (bi, 0, g)),
                  seq_spec(), seq_spec(), seq_spec(), seq_spec(),
                  pl.BlockSpec((Q_TILE, LANES), lambda bi, g, i: (bi * nq + i, g)),
                  pl.BlockSpec((nblk, ncmp), lambda bi, g, i: (0, 0))],
        out_specs=pl.BlockSpec((Q_TILE, NSA_REP * HEAD_DIM), lambda bi, g, i: (bi * nq + i, g)),
        out_shape=jax.ShapeDtypeStruct((t, NSA_WIDTH), F32),
        scratch_shapes=[pltpu.VMEM((s // S_CHUNK, NSA_REP * Q_TILE, S_CHUNK), F32)],
        compiler_params=_cparams(("parallel", "parallel", "arbitrary")),
    )(q, kcmp, vcmp, ks, vs, kw, vw, gd, ovl_t)


def _conv_silu(cur_ref, ext_ref, tail_ref, w_ref, b_ref):
    l = cur_ref.shape[0]
    cur = cur_ref[...]
    ext_ref[0:SUBLANES, :] = tail_ref[...]
    ext_ref[SUBLANES:, :] = cur
    tail_ref[...] = cur[l - SUBLANES:, :]
    acc = jnp.zeros(cur.shape, F32) + b_ref[...]
    for k in range(SSD_CONV):
        off = SUBLANES - (SSD_CONV - 1) + k
        acc = acc + ext_ref[off:off + l, :] * w_ref[k:k + 1, :]
    return acc * _sigmoid(acc)


def _ssd_kernel(xs_ref, bm_ref, cm_ref, wx_ref, wb_ref, wc_ref, bx_ref, bb_ref, bc_ref, z_ref, gd_ref, dtt_ref,
                alane_ref, asub_ref, dskip_ref, ng_ref, tril_ref, edt_ref, eseg_ref, o_ref,
                tx_ref, tb_ref, tc_ref, ex_ref, eb_ref, ec_ref, st_ref):
    c = pl.program_id(2)
    l = xs_ref.shape[0]

    @pl.when(c == 0)
    def _():
        tx_ref[...] = jnp.zeros(tx_ref.shape, F32)
        tb_ref[...] = jnp.zeros(tb_ref.shape, F32)
        tc_ref[...] = jnp.zeros(tc_ref.shape, F32)
        st_ref[...] = jnp.zeros(st_ref.shape, F32)

    xs = _conv_silu(xs_ref, ex_ref, tx_ref, wx_ref, bx_ref)
    bm = _conv_silu(bm_ref, eb_ref, tb_ref, wb_ref, bb_ref)
    cm = _conv_silu(cm_ref, ec_ref, tc_ref, wc_ref, bc_ref)
    gd = gd_ref[...]
    tril = tril_ref[...]
    adt_c = gd * (-jnp.exp(alane_ref[0]))
    acum_c = _dot_exact_rhs(tril, adt_c)
    dt_full = _dot_exact_lhs(gd, edt_ref[...])
    ac_full = _dot_exact_lhs(acum_c, edt_ref[...])
    ac_seg = _dot_exact_lhs(acum_c, eseg_ref[...])
    adt_t = dtt_ref[...] * (-jnp.exp(asub_ref[0]))
    acum_t = _dot_nt_exact_lhs(adt_t, tril)

    xdt = xs * dt_full
    cmb = cm.astype(BF16)
    cb = _dot_nt(cmb, bm.astype(BF16))
    causal = lax.broadcasted_iota(jnp.int32, (l, l), 0) >= lax.broadcasted_iota(jnp.int32, (l, l), 1)
    head_of_lane = lax.broadcasted_iota(jnp.int32, xs.shape, 1) // SSD_HEAD_DIM
    y = jnp.zeros(xs.shape, F32)
    for r in range(SSD_REP):
        seg = jnp.exp(jnp.where(causal, ac_seg[:, r * l:(r + 1) * l] - acum_t[r:r + 1, :], NEG_BIG))
        x_r = jnp.where(head_of_lane == r, xdt, 0.0).astype(BF16)
        y = y + _dot((cb * seg).astype(BF16), x_r)
    ac_last = ac_full[l - 1:l, :]
    state = st_ref[...]
    y = y + _dot(cmb, state.astype(BF16)) * jnp.exp(ac_full)
    decayed = (xdt * jnp.exp(ac_last - ac_full)).astype(BF16)
    st_ref[...] = state * jnp.exp(ac_last) + _dot(bm.T.astype(BF16), decayed)
    y = y + xs * dskip_ref[...]
    z = z_ref[...]
    yz = y * (z * _sigmoid(z))
    o_ref[...] = yz * lax.rsqrt(jnp.mean(yz * yz, axis=-1, keepdims=True) + EPS) * ng_ref[...]


def _ssd(xbc, z, gd, dtt, conv_w, conv_b, a_log, d_skip, norm_g, b, s):
    t = xbc.shape[0]
    l = SSD_CHUNK
    nc = s // l
    gw = SSD_WIDTH // SSD_GROUPS
    xcols = gw // LANES
    b_blk0 = SSD_WIDTH // LANES
    c_blk0 = b_blk0 + SSD_GROUPS * SSD_STATE // LANES
    conv_b2 = conv_b[None, :]
    a_grp = a_log.reshape(SSD_GROUPS, SSD_REP)
    alane = jnp.zeros((SSD_GROUPS, 1, LANES), F32).at[:, 0, 16:16 + SSD_REP].set(a_grp)
    asub = jnp.zeros((SSD_GROUPS, SUBLANES, LANES), F32).at[:, :SSD_REP, :].set(
        jnp.broadcast_to(a_grp[:, :, None], (SSD_GROUPS, SSD_REP, LANES)))
    dskip = jnp.repeat(d_skip, SSD_HEAD_DIM)[None, :]
    ng = norm_g[None, :]
    ii = np.arange(l)
    tril = jnp.asarray(ii[:, None] >= ii[None, :], BF16)
    edt = np.zeros((LANES, gw), np.float32)
    eseg = np.zeros((LANES, SSD_REP * l), np.float32)
    for r in range(SSD_REP):
        edt[16 + r, r * SSD_HEAD_DIM:(r + 1) * SSD_HEAD_DIM] = 1.0
        eseg[16 + r, r * l:(r + 1) * l] = 1.0
    edt, eseg = jnp.asarray(edt, BF16), jnp.asarray(eseg, BF16)

    def rowblk(width, col):
        return pl.BlockSpec((l, width), lambda bi, g, c: (bi * nc + c, col(g)))

    def const(shape, col=None):
        if col is None:
            return pl.BlockSpec(shape, lambda bi, g, c: (0,) * len(shape))
        return pl.BlockSpec(shape, lambda bi, g, c: (0, col(g)))

    return pl.pallas_call(
        _ssd_kernel,
        name="ssd",
        grid=(b, SSD_GROUPS, nc),
        in_specs=[rowblk(gw, lambda g: g), rowblk(LANES, lambda g: b_blk0 + g), rowblk(LANES, lambda g: c_blk0 + g),
                  const((SSD_CONV, gw), lambda g: g), const((SSD_CONV, LANES), lambda g: b_blk0 + g),
                  const((SSD_CONV, LANES), lambda g: c_blk0 + g),
                  const((1, gw), lambda g: g), const((1, LANES), lambda g: b_blk0 + g),
                  const((1, LANES), lambda g: c_blk0 + g),
                  rowblk(gw, lambda g: g), rowblk(LANES, lambda g: g),
                  pl.BlockSpec((SUBLANES, l), lambda bi, g, c: (g, bi * nc + c)),
                  pl.BlockSpec((1, 1, LANES), lambda bi, g, c: (g, 0, 0)),
                  pl.BlockSpec((1, SUBLANES, LANES), lambda bi, g, c: (g, 0, 0)),
                  const((1, gw), lambda g: g), const((1, gw), lambda g: g),
                  const((l, l)), const((LANES, gw)), const((LANES, SSD_REP * l))],
        out_specs=rowblk(gw, lambda g: g),
        out_shape=jax.ShapeDtypeStruct((t, SSD_WIDTH), F32),
        scratch_shapes=[pltpu.VMEM((SUBLANES, gw), F32), pltpu.VMEM((SUBLANES, LANES), F32),
                        pltpu.VMEM((SUBLANES, LANES), F32),
                        pltpu.VMEM((l + SUBLANES, gw), F32), pltpu.VMEM((l + SUBLANES, LANES), F32),
                        pltpu.VMEM((l + SUBLANES, LANES), F32),
                        pltpu.VMEM((SSD_STATE, gw), F32)],
        compiler_params=_cparams(("parallel", "parallel", "arbitrary")),
    )(xbc, xbc, xbc, conv_w, conv_w, conv_w, conv_b2, conv_b2, conv_b2, z, gd, dtt,
      alane, asub, dskip, ng, tril, edt, eseg)


def _rms(v, g):
    return v * lax.rsqrt(jnp.mean(v * v, axis=-1, keepdims=True) + EPS) * g


def _out_proj_kernel(x_ref, on_ref, os_ref, ng_ref, w1_ref, w2_ref, fg_ref, x1_ref, xnt_ref):
    onn = _rms(on_ref[...], ng_ref[...]).astype(BF16)
    x1 = x_ref[...] + _dot(onn, w1_ref[...]) + _dot(os_ref[...].astype(BF16), w2_ref[...])
    x1_ref[...] = x1
    xnt_ref[...] = _rms(x1, fg_ref[...]).T.astype(BF16)


def _out_proj(xt, o_nsa, o_ssd, nsa_g, w_out, ffn_g):
    t = xt.shape[0]
    tm = min(TOK_TILE, t)

    def tok(width):
        return pl.BlockSpec((tm, width), lambda i: (i, 0))

    def full(shape):
        return pl.BlockSpec(shape, lambda i: (0, 0))

    wb = w_out.astype(BF16)
    return pl.pallas_call(
        _out_proj_kernel,
        name="out_proj",
        grid=(t // tm,),
        in_specs=[tok(D_MODEL), tok(NSA_WIDTH), tok(SSD_WIDTH), full((1, NSA_WIDTH)),
                  full((NSA_WIDTH, D_MODEL)), full((SSD_WIDTH, D_MODEL)), full((1, D_MODEL))],
        out_specs=[tok(D_MODEL), pl.BlockSpec((D_MODEL, tm), lambda i: (0, i))],
        out_shape=[jax.ShapeDtypeStruct((t, D_MODEL), F32), jax.ShapeDtypeStruct((D_MODEL, t), BF16)],
        compiler_params=_cparams(("parallel",)),
    )(xt, o_nsa, o_ssd, nsa_g[None, :], wb[:NSA_WIDTH], wb[NSA_WIDTH:], ffn_g[None, :])


def _top_sorted(e, k):
    rows = lax.broadcasted_iota(jnp.int32, (k, e.shape[1]), 0)
    out = jnp.zeros((k, e.shape[1]), F32)
    cur = e
    for j in range(k):
        mk = jnp.max(cur, axis=0, keepdims=True)
        out = jnp.where(rows == j, jnp.maximum(mk, 0.0), out)
        target = jnp.where(mk >= 0.0, mk, 2.0)
        cur = jnp.where(cur == target, -float(j + 1), cur)
    rank = jnp.where(cur < 0.0, -1.0 - cur, float(k))
    return out, rank


def _pair_candidates(a16, b16):
    row8 = lax.broadcasted_iota(jnp.int32, (SUBLANES, a16.shape[1]), 0)
    groups = [a16[0:1] * b16[0:SUBLANES], a16[0:1] * b16[SUBLANES:], a16[1:2] * b16[0:SUBLANES]]
    for a in range(2, SUBLANES):
        groups.append(jnp.where(row8 < PEER_TOPK // (a + 1), a16[a:a + 1] * b16[0:SUBLANES], -1.0))
    groups.append(a16[SUBLANES:] * b16[0:1])
    return jnp.concatenate(groups, axis=0)


def _peer_front_kernel(xnt_ref, wq_ref, keys_ref, av_ref, lc_ref, bv_ref, rb_ref):
    xt = xnt_ref[...]
    tm = xt.shape[1]
    row8 = lax.broadcasted_iota(jnp.int32, (SUBLANES, tm), 0)
    for h in range(PEER_HEADS):
        qh = _dot(wq_ref[h * PEER_KEY_DIM:(h + 1) * PEER_KEY_DIM, :], xt)
        s1 = _dot(keys_ref[h, 0], qh[:PEER_HALF].astype(BF16))
        s2 = _dot(keys_ref[h, 1], qh[PEER_HALF:].astype(BF16))
        e1 = jnp.exp(s1 - jnp.max(s1, axis=0, keepdims=True))
        e2 = jnp.exp(s2 - jnp.max(s2, axis=0, keepdims=True))
        a16, ra = _top_sorted(e1, PEER_TOPK)
        b16, rb = _top_sorted(e2, PEER_TOPK)
        cand = _pair_candidates(a16, b16)
        cur = cand
        zsum = jnp.zeros((1, tm), F32)
        for _ in range(PEER_TOPK):
            tau = jnp.max(cur, axis=0, keepdims=True)
            zsum = zsum + tau
            cur = jnp.where(cur == tau, -1.0, cur)
        picked = jnp.where(cand >= tau, 1.0, 0.0)

        def count(g):
            return jnp.sum(picked[g * SUBLANES:(g + 1) * SUBLANES], axis=0, keepdims=True)

        lens = [count(0) + count(1)] + [count(a + 1) for a in range(1, SUBLANES)]
        len_lo = jnp.zeros((SUBLANES, tm), F32)
        for a, row in enumerate(lens):
            len_lo = jnp.where(row8 == a, row, len_lo)
        len16 = jnp.concatenate([len_lo, picked[(SUBLANES + 1) * SUBLANES:]], axis=0)
        pack = 2 * SUBLANES
        ra_b = ra.astype(BF16)
        lc = jnp.zeros(ra.shape, BF16)
        for a in range(PEER_TOPK):
            len_a = jnp.tile(jnp.broadcast_to(len16[a:a + 1], (pack, tm)).astype(BF16), (PEER_N_KEYS // pack, 1))
            lc = jnp.where(ra_b == float(a), len_a, lc)
        rows = slice(h * PEER_N_KEYS, (h + 1) * PEER_N_KEYS)
        av_ref[rows, :] = e1
        lc_ref[rows, :] = lc.astype(F32)
        bv_ref[rows, :] = (e2 * (1.0 / zsum)).astype(BF16)
        rb_ref[rows, :] = rb.astype(BF16)


def _peer_front(xnt, w_q, keys):
    t = xnt.shape[1]
    tm = min(PEER_FRONT_TOK, t)
    nk = PEER_HEADS * PEER_N_KEYS
    return pl.pallas_call(
        _peer_front_kernel,
        name="peer_front",
        grid=(t // tm,),
        in_specs=[pl.BlockSpec((D_MODEL, tm), lambda i: (0, i)),
                  pl.BlockSpec((PEER_HEADS * PEER_KEY_DIM, D_MODEL), lambda i: (0, 0)),
                  pl.BlockSpec((PEER_HEADS, 2, PEER_N_KEYS, PEER_HALF), lambda i: (0, 0, 0, 0))],
        out_specs=[pl.BlockSpec((nk, tm), lambda i: (0, i))] * 4,
        out_shape=[jax.ShapeDtypeStruct((nk, t), F32), jax.ShapeDtypeStruct((nk, t), F32),
                   jax.ShapeDtypeStruct((nk, t), BF16), jax.ShapeDtypeStruct((nk, t), BF16)],
        compiler_params=_cparams(("parallel",)),
    )(xnt, w_q.T.astype(BF16), keys.astype(BF16))


def _row_bf16(ref, row, n_rows):
    pack = 2 * SUBLANES
    tile = jnp.broadcast_to(ref[pl.ds(row, 1), :], (pack, ref.shape[1])).astype(BF16)
    return jnp.tile(tile, (n_rows // pack, 1))


def _peer_dense_kernel(xnt_ref, u_ref, vt_ref, av_ref, lc_ref, bv_ref, rb_ref, o_ref, hid_ref, act_ref):
    j = pl.program_id(1)
    n_sub = u_ref.shape[0] // PEER_N_KEYS

    @pl.when(j == 0)
    def _():
        o_ref[...] = jnp.zeros(o_ref.shape, F32)

    def gate_of(cl):
        e1 = j * n_sub + cl
        gate = jnp.zeros((PEER_N_KEYS, xnt_ref.shape[1]), BF16)
        for h in range(PEER_HEADS):
            rows = slice(h * PEER_N_KEYS, (h + 1) * PEER_N_KEYS)
            lc_row = _row_bf16(lc_ref, h * PEER_N_KEYS + e1, PEER_N_KEYS)
            av_row = _row_bf16(av_ref, h * PEER_N_KEYS + e1, PEER_N_KEYS)
            bv = bv_ref[rows, :]
            gate = gate + jnp.where(rb_ref[rows, :] < lc_row, bv, jnp.zeros_like(bv)) * av_row
        return gate

    piece = 2 * PEER_N_KEYS
    pack = 2 * SUBLANES
    for p in range(u_ref.shape[0] // piece):
        g0, g1 = gate_of(2 * p), gate_of(2 * p + 1)
        zero = jnp.tile(g0[0:pack, 0:LANES] * 0.0, (piece // pack, u_ref.shape[1] // LANES))
        hid = _dot(u_ref[p * piece:(p + 1) * piece, :] + zero, xnt_ref[...])
        act_ref[p * piece:p * piece + PEER_N_KEYS, :] = _gelu_sigmoid(hid[:PEER_N_KEYS]).astype(BF16) * g0
        act_ref[p * piece + PEER_N_KEYS:(p + 1) * piece, :] = _gelu_sigmoid(hid[PEER_N_KEYS:]).astype(BF16) * g1
    o_ref[...] += _dot(vt_ref[...], act_ref[...])


def _peer_dense(xnt, u, vt, av, lc, bv, rb):
    t = xnt.shape[1]
    tm = min(PEER_TOK, t)
    ne = u.shape[0]
    te = PEER_EXP
    tok = pl.BlockSpec((PEER_HEADS * PEER_N_KEYS, tm), lambda i, j: (0, i))
    return pl.pallas_call(
        _peer_dense_kernel,
        name="peer_dense",
        grid=(t // tm, ne // te),
        in_specs=[pl.BlockSpec((D_MODEL, tm), lambda i, j: (0, i)),
                  pl.BlockSpec((te, D_MODEL), lambda i, j: (j, 0)),
                  pl.BlockSpec((D_MODEL, te), lambda i, j: (0, j)),
                  tok, tok, tok, tok],
        out_specs=pl.BlockSpec((D_MODEL, tm), lambda i, j: (0, i)),
        out_shape=jax.ShapeDtypeStruct((D_MODEL, t), F32),
        scratch_shapes=[pltpu.VMEM((te, tm), F32), pltpu.VMEM((te, tm), BF16)],
        compiler_params=_cparams(("parallel", "arbitrary")),
    )(xnt, u, vt, av, lc, bv, rb)


def _final_kernel(x1_ref, pt_ref, g_ref, o_ref):
    o_ref[...] = _rms(x1_ref[...] + pt_ref[...].T, g_ref[...])


def _residual_norm(x1, peer_t, g):
    t = x1.shape[0]
    tm = min(TOK_TILE, t)
    return pl.pallas_call(
        _final_kernel,
        name="residual_norm",
        grid=(t // tm,),
        in_specs=[pl.BlockSpec((tm, D_MODEL), lambda i: (i, 0)), pl.BlockSpec((D_MODEL, tm), lambda i: (0, i)),
                  pl.BlockSpec((1, D_MODEL), lambda i: (0, 0))],
        out_specs=pl.BlockSpec((tm, D_MODEL), lambda i: (i, 0)),
        out_shape=jax.ShapeDtypeStruct((t, D_MODEL), F32),
        compiler_params=_cparams(("parallel",)),
    )(x1, peer_t, g[None, :])


def kernel(x, attn_norm_g, w_in, cmp_pos_k, cmp_w1_k, cmp_b1_k, cmp_w2_k, cmp_pos_v, cmp_w1_v, cmp_b1_v, cmp_w2_v,
           conv_w, conv_b, dt_bias, a_log, d_skip, ssd_norm_g, nsa_norm_g, w_out, ffn_norm_g, peer_w_q, peer_keys,
           peer_u, peer_v, final_norm_g):
    b, s, d = x.shape
    xt = x.reshape(b * s, d)
    assert attn_norm_g.shape[0] == 1, "single-layer block"
    for layer in range(1):
        q, kc, vc, ks, vs, kw, vw, gd, dtt, z, xbc = _in_proj(xt, attn_norm_g[layer], w_in[layer], dt_bias[layer], s)
        kcmp = _compress(kc, cmp_pos_k[layer], cmp_w1_k[layer], cmp_b1_k[layer], cmp_w2_k[layer], b, s)
        vcmp = _compress(vc, cmp_pos_v[layer], cmp_w1_v[layer], cmp_b1_v[layer], cmp_w2_v[layer], b, s)
        o_nsa = _nsa(q, kcmp, vcmp, ks, vs, kw, vw, gd, b, s)
        o_ssd = _ssd(xbc, z, gd, dtt, conv_w[layer], conv_b[layer], a_log[layer], d_skip[layer],
                     ssd_norm_g[layer], b, s)
        x1, xnt = _out_proj(xt, o_nsa, o_ssd, nsa_norm_g[layer], w_out[layer], ffn_norm_g[layer])
        av, lc, bv, rb = _peer_front(xnt, peer_w_q[layer], peer_keys[layer])
        peer_t = _peer_dense(xnt, peer_u[layer].astype(BF16), peer_v[layer].T.astype(BF16), av, lc, bv, rb)
    return _residual_norm(x1, peer_t, final_norm_g).reshape(b, s, d)
```

```python
import functools

import numpy as np
import jax
import jax.numpy as jnp
from jax import lax
from jax.experimental import pallas as pl
from jax.experimental.pallas import tpu as pltpu

F32 = jnp.float32
BF16 = jnp.bfloat16

EPS = 1e-6
D_MODEL = 1024
NSA_HEADS = 8
NSA_KV_GROUPS = 2
NSA_REP = NSA_HEADS // NSA_KV_GROUPS
HEAD_DIM = 64
NSA_WIDTH = NSA_HEADS * HEAD_DIM
KV_WIDTH = NSA_KV_GROUPS * HEAD_DIM
CMP_STRIDE = 16
CMP_BLOCK = 32
CMP_HIDDEN = 256
SEL_BLOCK = 64
SEL_TOP_N = 16
WINDOW = 512
ROPE_DIM = HEAD_DIM // 4
ROPE_THETA = 500000.0
SSD_HEADS = 8
SSD_HEAD_DIM = 64
SSD_WIDTH = SSD_HEADS * SSD_HEAD_DIM
SSD_GROUPS = 2
SSD_REP = SSD_HEADS // SSD_GROUPS
SSD_STATE = 128
SSD_CONV = 4
SSD_CHUNK = 128
SSD_CONV_DIM = SSD_WIDTH + 2 * SSD_GROUPS * SSD_STATE
PEER_HEADS = 8
PEER_N_KEYS = 128
PEER_N_EXPERTS = PEER_N_KEYS * PEER_N_KEYS
PEER_KEY_DIM = 256
PEER_HALF = PEER_KEY_DIM // 2
PEER_TOPK = 16

LANES = 128
SUBLANES = 8
VMEM_LIMIT = 48 * 1024 * 1024
NEG_BIG = -1e30
SEL_NEG = -1e9

TOK_TILE = 512
Q_TILE = 2 * SEL_BLOCK
K_TILE = 2 * SEL_BLOCK
S_CHUNK = 4 * K_TILE
PEER_TOK = 512
PEER_EXP = 1024
PEER_FRONT_TOK = 256


def _cparams(sem, flags=None):
    return pltpu.CompilerParams(dimension_semantics=sem, vmem_limit_bytes=VMEM_LIMIT, flags=flags)


def _dot(a, b):
    return jnp.dot(a, b, preferred_element_type=F32)


def _dot_nt(a, b):
    return lax.dot_general(a, b, (((1,), (1,)), ((), ())), preferred_element_type=F32)


def _split3(a):
    a1 = a.astype(BF16)
    r1 = a - a1.astype(F32)
    a2 = r1.astype(BF16)
    a3 = (r1 - a2.astype(F32)).astype(BF16)
    return a1, a2, a3


def _dot_exact_lhs(a, b01):
    a1, a2, a3 = _split3(a)
    return _dot(a1, b01) + _dot(a2, b01) + _dot(a3, b01)


def _dot_nt_exact_lhs(a, b01):
    a1, a2, a3 = _split3(a)
    return _dot_nt(a1, b01) + _dot_nt(a2, b01) + _dot_nt(a3, b01)


def _dot_exact_rhs(a01, b):
    b1, b2, b3 = _split3(b)
    return _dot(a01, b1) + _dot(a01, b2) + _dot(a01, b3)


def _dot_nt_exact_rhs(a01, b):
    b1, b2, b3 = _split3(b)
    return _dot_nt(a01, b1) + _dot_nt(a01, b2) + _dot_nt(a01, b3)


def _softplus(x):
    return jnp.maximum(x, 0.0) + jnp.log(1.0 + jnp.exp(-jnp.abs(x)))


def _sigmoid(x):
    return 1.0 / (1.0 + jnp.exp(-x))


def _gelu_tanh(x):
    return 0.5 * x * (1.0 + jnp.tanh(0.7978845608028654 * (x + 0.044715 * (x * x * x))))


def _gelu_sigmoid(x):
    c1 = -2.0 * 0.7978845608028654 * 1.4426950408889634
    u = x * (c1 + (c1 * 0.044715) * (x * x))
    return x / (1.0 + jnp.exp2(u))


def _rope128(p, cos, sa, sb):
    return p * cos + pltpu.roll(p, LANES - ROPE_DIM // 2, 1) * sa + pltpu.roll(p, ROPE_DIM // 2, 1) * sb


def _in_proj_kernel(seq_tiles, x_ref, g_ref, w_ref, wdt_ref, cos_ref, sa_ref, sb_ref, gdb_ref, dtb_ref,
                    q_ref, kc_ref, vc_ref, ks_ref, vs_ref, kw_ref, vw_ref, gd_ref, dtt_ref, z_ref, xbc_ref):
    tm = x_ref.shape[0]
    x = x_ref[...]
    h = x * lax.rsqrt(jnp.mean(x * x, axis=-1, keepdims=True) + EPS) * g_ref[...]
    hb = h.astype(BF16)
    pos0 = pl.multiple_of((pl.program_id(0) % seq_tiles) * tm, tm)
    cos = cos_ref[pl.ds(pos0, tm), :]
    sa = sa_ref[pl.ds(pos0, tm), :]
    sb = sb_ref[pl.ds(pos0, tm), :]
    lane = lax.broadcasted_iota(jnp.int32, (tm, LANES), 1)
    low = lane < HEAD_DIM

    def proj(c0, width):
        return _dot(hb, w_ref[:, c0:c0 + width])

    def halves(p):
        return p, pltpu.roll(p, HEAD_DIM, 1)

    pq = proj(0, NSA_WIDTH)
    for s in range(NSA_WIDTH // LANES):
        srcs = halves(_rope128(pq[:, s * LANES:(s + 1) * LANES], cos, sa, sb) * (HEAD_DIM ** -0.5))
        for half in range(2):
            head = 2 * s + half
            q_ref[:, head * LANES:(head + 1) * LANES] = jnp.where(low, srcs[half], 0.0).astype(BF16)
    c0 = NSA_WIDTH
    kc_ref[...] = _rope128(proj(c0, LANES), cos, sa, sb).astype(BF16)
    vc_ref[...] = proj(c0 + LANES, LANES).astype(BF16)
    c0 += 2 * LANES
    blk_of_row = (pos0 + lax.broadcasted_iota(jnp.int32, (tm, LANES), 0)) // SEL_BLOCK
    onehot = jnp.where(lane - HEAD_DIM == blk_of_row, 1.0, 0.0)
    for ref, rope, fill in ((ks_ref, True, onehot), (vs_ref, False, 1.0), (kw_ref, True, 0.0), (vw_ref, False, 1.0)):
        p = proj(c0, LANES)
        if rope:
            p = _rope128(p, cos, sa, sb)
        for g, src in enumerate(halves(p)):
            ref[:, g * LANES:(g + 1) * LANES] = jnp.where(low, src, fill).astype(BF16)
        c0 += LANES
    for s in range(2):
        p = proj(c0, LANES) + gdb_ref[:, s * LANES:(s + 1) * LANES]
        act = jnp.where(lane < 3 * NSA_REP, _sigmoid(p),
                        jnp.where((lane >= 16) & (lane < 16 + SSD_REP), _softplus(p), 0.0))
        gd_ref[:, s * LANES:(s + 1) * LANES] = act
        c0 += LANES
    z_ref[...] = proj(c0, SSD_WIDTH)
    c0 += SSD_WIDTH
    xbc_ref[...] = proj(c0, SSD_CONV_DIM)
    dtt_ref[...] = _softplus(_dot_nt(wdt_ref[...], hb) + dtb_ref[:, 0:1])


def _rope_tables(s):
    half = ROPE_DIM // 2
    inv = jnp.power(ROPE_THETA, -jnp.arange(half, dtype=F32) * 2.0 / ROPE_DIM)
    ang = jnp.arange(s).astype(F32)[:, None] * inv[None, :]
    cos, sin = jnp.cos(ang), jnp.sin(ang)
    zeros = jnp.zeros((s, HEAD_DIM - ROPE_DIM), F32)
    cos64 = jnp.concatenate([cos, cos, jnp.ones((s, HEAD_DIM - ROPE_DIM), F32)], axis=1)
    sa64 = jnp.concatenate([-sin, jnp.zeros_like(sin), zeros], axis=1)
    sb64 = jnp.concatenate([jnp.zeros_like(sin), sin, zeros], axis=1)
    return tuple(jnp.concatenate([t, t], axis=1) for t in (cos64, sa64, sb64))


def _in_proj(xt, attn_g, w_in, dt_bias, s):
    t = xt.shape[0]
    tm = min(TOK_TILE, s)
    o_gl = NSA_WIDTH + 6 * KV_WIDTH
    o_z = o_gl + 3 * NSA_HEADS
    o_xbc = o_z + SSD_WIDTH
    o_dt = o_xbc + SSD_CONV_DIM
    gd_cols, gd_bias = [], []
    for g in range(NSA_KV_GROUPS):
        gates = w_in[:, o_gl + 3 * NSA_REP * g:o_gl + 3 * NSA_REP * (g + 1)]
        dts = w_in[:, o_dt + SSD_REP * g:o_dt + SSD_REP * (g + 1)]
        gd_cols += [gates, jnp.zeros((D_MODEL, 16 - 3 * NSA_REP), F32), dts,
                    jnp.zeros((D_MODEL, LANES - 16 - SSD_REP), F32)]
        gd_bias += [jnp.zeros((16,), F32), dt_bias[SSD_REP * g:SSD_REP * (g + 1)],
                    jnp.zeros((LANES - 16 - SSD_REP,), F32)]
    w_main = jnp.concatenate([w_in[:, :o_gl]] + gd_cols + [w_in[:, o_z:o_dt]], axis=1).astype(BF16)
    gdb = jnp.concatenate(gd_bias)[None, :]
    wdt_rows, dtb_rows = [], []
    for g in range(SSD_GROUPS):
        wdt_rows += [w_in[:, o_dt + SSD_REP * g:o_dt + SSD_REP * (g + 1)].T,
                     jnp.zeros((SUBLANES - SSD_REP, D_MODEL), F32)]
        dtb_rows += [dt_bias[SSD_REP * g:SSD_REP * (g + 1)], jnp.zeros((SUBLANES - SSD_REP,), F32)]
    wdt = jnp.concatenate(wdt_rows, axis=0).astype(BF16)
    dtb = jnp.broadcast_to(jnp.concatenate(dtb_rows)[:, None], (2 * SUBLANES, LANES))
    cos, sa, sb = _rope_tables(s)
    n_main = w_main.shape[1]

    def full(shape):
        return pl.BlockSpec(shape, lambda i: (0, 0))

    def tok(width):
        return pl.BlockSpec((tm, width), lambda i: (i, 0))

    kv_widths = [LANES, LANES] + [NSA_KV_GROUPS * LANES] * 4
    out_shapes = ([jax.ShapeDtypeStruct((t, NSA_HEADS * LANES), BF16)]
                  + [jax.ShapeDtypeStruct((t, w), BF16) for w in kv_widths]
                  + [jax.ShapeDtypeStruct((t, 2 * LANES), F32),
                     jax.ShapeDtypeStruct((2 * SUBLANES, t), F32),
                     jax.ShapeDtypeStruct((t, SSD_WIDTH), F32),
                     jax.ShapeDtypeStruct((t, SSD_CONV_DIM), F32)])
    out_specs = ([tok(NSA_HEADS * LANES)] + [tok(w) for w in kv_widths]
                 + [tok(2 * LANES), pl.BlockSpec((2 * SUBLANES, tm), lambda i: (0, i)),
                    tok(SSD_WIDTH), tok(SSD_CONV_DIM)])
    return pl.pallas_call(
        functools.partial(_in_proj_kernel, s // tm),
        name="in_proj",
        grid=(t // tm,),
        in_specs=[tok(D_MODEL), full((1, D_MODEL)), full((D_MODEL, n_main)), full((2 * SUBLANES, D_MODEL)),
                  full((s, LANES)), full((s, LANES)), full((s, LANES)), full((1, 2 * LANES)),
                  full((2 * SUBLANES, LANES))],
        out_specs=out_specs,
        out_shape=out_shapes,
        compiler_params=_cparams(("parallel",)),
    )(xt, attn_g[None, :], w_main, wdt, cos, sa, sb, gdb, dtb)


def _compress_kernel(kv_ref, w1a_ref, w1b_ref, pos_ref, w1_ref, b1_ref, w2_ref, out_ref):
    kv = kv_ref[0]
    bias = _dot(pos_ref[...], w1_ref[...])[0:1, :] + b1_ref[...]
    for g in range(NSA_KV_GROUPS):
        first = _dot(kv, w1a_ref[g])
        second = _dot(kv, w1b_ref[g])
        nxt = pltpu.roll(second, second.shape[0] - 1, 0)
        hid = _gelu_tanh(first + nxt + bias)
        out_ref[0, :, g * LANES:(g + 1) * LANES] = _dot(hid.astype(BF16), w2_ref[...]).astype(BF16)


def _compress(kv, pos_emb, w1, b1, w2, b, s):
    nch = s // CMP_STRIDE
    kvf = kv.reshape(b, nch, CMP_STRIDE * LANES)
    w1r = w1.reshape(CMP_BLOCK, HEAD_DIM, CMP_HIDDEN)
    zeros = jnp.zeros((CMP_STRIDE, HEAD_DIM, CMP_HIDDEN), F32)

    def expand(w_half, g):
        parts = [w_half, zeros] if g == 0 else [zeros, w_half]
        return jnp.concatenate(parts, axis=1).reshape(CMP_STRIDE * LANES, CMP_HIDDEN)

    w1a = jnp.stack([expand(w1r[:CMP_STRIDE], g) for g in range(NSA_KV_GROUPS)]).astype(BF16)
    w1b = jnp.stack([expand(w1r[CMP_STRIDE:], g) for g in range(NSA_KV_GROUPS)]).astype(BF16)
    w2e = jnp.concatenate([w2, jnp.zeros((CMP_HIDDEN, LANES - HEAD_DIM), F32)], axis=1).astype(BF16)
    pos = jnp.zeros((SUBLANES, CMP_BLOCK * HEAD_DIM), F32).at[0].set(pos_emb.reshape(-1)).astype(BF16)
    return pl.pallas_call(
        _compress_kernel,
        name="compress",
        grid=(b,),
        in_specs=[pl.BlockSpec((1, nch, CMP_STRIDE * LANES), lambda i: (i, 0, 0)),
                  pl.BlockSpec((2, CMP_STRIDE * LANES, CMP_HIDDEN), lambda i: (0, 0, 0)),
                  pl.BlockSpec((2, CMP_STRIDE * LANES, CMP_HIDDEN), lambda i: (0, 0, 0)),
                  pl.BlockSpec((SUBLANES, CMP_BLOCK * HEAD_DIM), lambda i: (0, 0)),
                  pl.BlockSpec((CMP_BLOCK * HEAD_DIM, CMP_HIDDEN), lambda i: (0, 0)),
                  pl.BlockSpec((1, CMP_HIDDEN), lambda i: (0, 0)),
                  pl.BlockSpec((CMP_HIDDEN, LANES), lambda i: (0, 0))],
        out_specs=pl.BlockSpec((1, nch, NSA_KV_GROUPS * LANES), lambda i: (i, 0, 0)),
        out_shape=jax.ShapeDtypeStruct((b, nch, NSA_KV_GROUPS * LANES), BF16),
        compiler_params=_cparams(("parallel",)),
    )(kvf, w1a, w1b, pos, w1.astype(BF16), b1[None, :], w2e)


def _lane_tiles(a):
    return [a[:, k * LANES:(k + 1) * LANES] for k in range(a.shape[1] // LANES)]


def _max_tiles(macc, s):
    for tile in _lane_tiles(s):
        macc = jnp.maximum(macc, tile)
    return macc


def _exp_tiles(s, m_b):
    return jnp.concatenate([jnp.exp(tile - m_b) for tile in _lane_tiles(s)], axis=1).astype(BF16)


def _normalize(acc):
    return acc / pltpu.roll(acc, HEAD_DIM, 1)


def _block_rank(imp):
    nblk, tq = imp.shape
    sub = lax.broadcasted_iota(jnp.int32, (SUBLANES, tq), 0)
    groups = [imp[SUBLANES * v:SUBLANES * (v + 1)] for v in range(nblk // SUBLANES)]
    cnt = [jnp.zeros((SUBLANES, tq), F32) for _ in groups]
    for j in range(nblk):
        row = imp[j:j + 1, :]
        for v, grp in enumerate(groups):
            if SUBLANES * v > j:
                ahead = row >= grp
            elif SUBLANES * (v + 1) <= j:
                ahead = row > grp
            else:
                ahead = (row > grp) | ((row == grp) & (sub > j - SUBLANES * v))
            cnt[v] = cnt[v] + jnp.where(ahead, 1.0, 0.0)
    return jnp.concatenate(cnt, axis=0)


def _nsa_pair_kernel(q_ref, kc_ref, vc_ref, ks_ref, vs_ref, kw_ref, vw_ref, gd_ref, ovl_ref, o_ref, s_ref):
    i = pl.program_id(1)
    tq = Q_TILE
    rows = NSA_REP * tq
    ncmp = kc_ref.shape[1]
    nblk = ovl_ref.shape[0]
    groups = range(NSA_KV_GROUPS)
    t_row = i * tq + lax.broadcasted_iota(jnp.int32, (rows, 1), 0) % tq

    def lanes(g):
        return slice(g * LANES, (g + 1) * LANES)

    qs = [jnp.concatenate([q_ref[:, (g * NSA_REP + r) * LANES:(g * NSA_REP + r + 1) * LANES]
                           for r in range(NSA_REP)], axis=0) for g in groups]

    cmp_end = lax.broadcasted_iota(jnp.int32, (rows, ncmp), 1) * CMP_STRIDE + (CMP_BLOCK - 1)
    mask_c = cmp_end <= t_row
    p_cs, o_cmps = [], []
    for g in groups:
        s_c = jnp.where(mask_c, _dot_nt(qs[g], kc_ref[0, :, lanes(g)]), NEG_BIG)
        e_c = jnp.exp(s_c - jnp.max(s_c, axis=-1, keepdims=True))
        p_c = jnp.where(mask_c, e_c / jnp.sum(e_c, axis=-1, keepdims=True), 0.0)
        p_cs.append(p_c)
        o_cmps.append(_dot(p_c.astype(BF16), vc_ref[0, :, lanes(g)]))

    blk = lax.broadcasted_iota(jnp.int32, (nblk, tq), 0)
    cur = (i * tq + lax.broadcasted_iota(jnp.int32, (nblk, tq), 1)) // SEL_BLOCK
    forced = (blk == 0) | (blk == cur) | (blk == cur - 1)
    valid = blk <= cur
    eye = jnp.where(lax.broadcasted_iota(jnp.int32, (tq, tq), 0) == lax.broadcasted_iota(jnp.int32, (tq, tq), 1),
                    1.0, 0.0).astype(BF16)
    ones_lo = jnp.ones((HEAD_DIM, tq), F32)
    q2s = []
    for g in groups:
        p_sum = p_cs[g][0:tq]
        for r in range(1, NSA_REP):
            p_sum = p_sum + p_cs[g][r * tq:(r + 1) * tq]
        imp = _dot_nt_exact_rhs(ovl_ref[...], p_sum)
        imp = jnp.where(forced, 1e9, jnp.where(valid, imp, -1.0))
        sel_t = jnp.where((_block_rank(imp) < float(min(SEL_TOP_N, nblk))) & valid, 1.0, 0.0)
        pad = [jnp.ones((HEAD_DIM - nblk, tq), F32)] if nblk < HEAD_DIM else []
        sel_pad = jnp.concatenate([ones_lo, sel_t] + pad, axis=0).astype(BF16)
        negb = ((_dot_nt(eye, sel_pad) - 1.0) * (-SEL_NEG)).astype(BF16)
        q2s.append(qs[g] + jnp.concatenate([negb] * NSA_REP, axis=0))

    jd = (i * tq) // K_TILE
    nfull = jd // (S_CHUNK // K_TILE)
    lane_c = lax.broadcasted_iota(jnp.int32, (rows, S_CHUNK), 1)

    def scores(g, c):
        k0 = pl.multiple_of(c * S_CHUNK, S_CHUNK)
        return _dot_nt(q2s[g], ks_ref[pl.ds(k0, S_CHUNK), lanes(g)])

    def pass_max(c, maccs):
        out = []
        for g in groups:
            s = scores(g, c)
            s_ref[g, c] = s
            out.append(_max_tiles(maccs[g], s))
        return tuple(out)

    neg = jnp.full((rows, LANES), NEG_BIG, F32)
    maccs = lax.fori_loop(0, nfull, pass_max, (neg,) * NSA_KV_GROUPS)
    causal = nfull * S_CHUNK + lane_c <= t_row
    s_lasts = [jnp.where(causal, scores(g, nfull), NEG_BIG) for g in groups]
    m_bs = [jnp.broadcast_to(jnp.max(_max_tiles(maccs[g], s_lasts[g]), axis=-1, keepdims=True), (rows, LANES))
            for g in groups]

    def pass_sum(c, accs):
        k0 = pl.multiple_of(c * S_CHUNK, S_CHUNK)
        return tuple(accs[g] + _dot(_exp_tiles(s_ref[g, c], m_bs[g]), vs_ref[pl.ds(k0, S_CHUNK), lanes(g)])
                     for g in groups)

    zero = jnp.zeros((rows, LANES), F32)
    accs = lax.fori_loop(0, nfull, pass_sum, (zero,) * NSA_KV_GROUPS)
    k_last = pl.multiple_of(nfull * S_CHUNK, S_CHUNK)
    o_sels = [_normalize(accs[g] + _dot(_exp_tiles(s_lasts[g], m_bs[g]), vs_ref[pl.ds(k_last, S_CHUNK), lanes(g)]))
              for g in groups]

    n_wk = WINDOW + K_TILE
    k0 = pl.multiple_of(jnp.maximum(jd - WINDOW // K_TILE, 0) * K_TILE, K_TILE)
    diff = t_row - (k0 + lax.broadcasted_iota(jnp.int32, (rows, n_wk), 1))
    in_win = (diff >= 0) & (diff < WINDOW)
    o_wins = []
    for g in groups:
        s_w = jnp.where(in_win, _dot_nt(q2s[g], kw_ref[pl.ds(k0, n_wk), lanes(g)]), NEG_BIG)
        m_w = jnp.max(_max_tiles(neg, s_w), axis=-1, keepdims=True)
        p_w = _exp_tiles(s_w, jnp.broadcast_to(m_w, (rows, LANES)))
        o_wins.append(_normalize(_dot(p_w, vw_ref[pl.ds(k0, n_wk), lanes(g)])))

    low = lax.broadcasted_iota(jnp.int32, (tq, LANES), 1) < HEAD_DIM
    for g in groups:
        gd = gd_ref[:, lanes(g)]
        heads = []
        for r in range(NSA_REP):
            rs = slice(r * tq, (r + 1) * tq)
            heads.append(gd[:, 3 * r:3 * r + 1] * o_cmps[g][rs] + gd[:, 3 * r + 1:3 * r + 2] * o_sels[g][rs]
                         + gd[:, 3 * r + 2:3 * r + 3] * o_wins[g][rs])
        for sidx in range(NSA_REP // 2):
            slab = g * (NSA_REP // 2) + sidx
            o_ref[:, slab * LANES:(slab + 1) * LANES] = jnp.where(
                low, heads[2 * sidx], pltpu.roll(heads[2 * sidx + 1], HEAD_DIM, 1))


def _nsa(q, kcmp, vcmp, ks, vs, kw, vw, gd, b, s):
    t = q.shape[0]
    nq = s // Q_TILE
    ncmp = s // CMP_STRIDE
    nblk = s // SEL_BLOCK
    cs = np.arange(ncmp) * CMP_STRIDE
    ss = np.arange(nblk) * SEL_BLOCK
    overlap = (cs[:, None] < ss[None, :] + SEL_BLOCK) & (cs[:, None] + CMP_BLOCK > ss[None, :])
    overlap[ncmp - 1, :] = False
    ovl_t = jnp.asarray(overlap.T, BF16)

    assert nblk <= HEAD_DIM and s % S_CHUNK == 0 and s >= WINDOW + K_TILE

    ng = NSA_KV_GROUPS

    def seq_spec():
        return pl.BlockSpec((s, ng * LANES), lambda bi, i: (bi, 0))

    return pl.pallas_call(
        _nsa_pair_kernel,
        name="nsa",
        grid=(b, nq),
        in_specs=[pl.BlockSpec((Q_TILE, NSA_HEADS * LANES), lambda bi, i: (bi * nq + i, 0)),
                  pl.BlockSpec((1, ncmp, ng * LANES), lambda bi, i: (bi, 0, 0)),
                  pl.BlockSpec((1, ncmp, ng * LANES), lambda bi, i: (bi, 0, 0)),
                  seq_spec(), seq_spec(), seq_spec(), seq_spec(),
                  pl.BlockSpec((Q_TILE, ng * LANES), lambda bi, i: (bi * nq + i, 0)),
                  pl.BlockSpec((nblk, ncmp), lambda bi, i: (0, 0))],
        out_specs=pl.BlockSpec((Q_TILE, NSA_WIDTH), lambda bi, i: (bi * nq + i, 0)),
        out_shape=jax.ShapeDtypeStruct((t, NSA_WIDTH), F32),
        scratch_shapes=[pltpu.VMEM((ng, s // S_CHUNK, NSA_REP * Q_TILE, S_CHUNK), F32)],
        compiler_params=_cparams(("parallel", "arbitrary")),
    )(q, kcmp, vcmp, ks, vs, kw, vw, gd, ovl_t)


def _conv_silu(cur_ref, ext_ref, tail_ref, w_ref, b_ref):
    l = cur_ref.shape[0]
    cur = cur_ref[...]
    ext_ref[0:SUBLANES, :] = tail_ref[...]
    ext_ref[SUBLANES:, :] = cur
    tail_ref[...] = cur[l - SUBLANES:, :]
    acc = jnp.zeros(cur.shape, F32) + b_ref[...]
    for k in range(SSD_CONV):
        off = SUBLANES - (SSD_CONV - 1) + k
        acc = acc + ext_ref[off:off + l, :] * w_ref[k:k + 1, :]
    return acc * _sigmoid(acc)


def _ssd_kernel(xbc_ref, w_ref, b_ref, z_ref, gd_ref, dtt_ref, alane_ref, asub_ref, dskip_ref, ng_ref,
                tril_ref, edt_ref, eseg_ref, o_ref, tail_ref, ext_ref, st_ref):
    c = pl.program_id(1)
    l = xbc_ref.shape[0]
    gw = SSD_WIDTH // SSD_GROUPS
    b_col0 = SSD_WIDTH
    c_col0 = SSD_WIDTH + SSD_GROUPS * SSD_STATE

    @pl.when(c == 0)
    def _():
        tail_ref[...] = jnp.zeros(tail_ref.shape, F32)
        st_ref[...] = jnp.zeros(st_ref.shape, F32)

    conv = _conv_silu(xbc_ref, ext_ref, tail_ref, w_ref, b_ref)
    tril = tril_ref[...]
    causal = lax.broadcasted_iota(jnp.int32, (l, l), 0) >= lax.broadcasted_iota(jnp.int32, (l, l), 1)
    head_of_lane = lax.broadcasted_iota(jnp.int32, (l, gw), 1) // SSD_HEAD_DIM
    for g in range(SSD_GROUPS):
        xs = conv[:, g * gw:(g + 1) * gw]
        bm = conv[:, b_col0 + g * SSD_STATE:b_col0 + (g + 1) * SSD_STATE]
        cm = conv[:, c_col0 + g * SSD_STATE:c_col0 + (g + 1) * SSD_STATE]
        gd = gd_ref[:, g * LANES:(g + 1) * LANES]
        adt_c = gd * (-jnp.exp(alane_ref[g]))
        acum_c = _dot_exact_rhs(tril, adt_c)
        dt_full = _dot_exact_lhs(gd, edt_ref[...])
        ac_full = _dot_exact_lhs(acum_c, edt_ref[...])
        ac_seg = _dot_exact_lhs(acum_c, eseg_ref[...])
        adt_t = dtt_ref[g * SUBLANES:(g + 1) * SUBLANES, :] * (-jnp.exp(asub_ref[g]))
        acum_t = _dot_nt_exact_lhs(adt_t, tril)

        xdt = xs * dt_full
        cmb = cm.astype(BF16)
        cb = _dot_nt(cmb, bm.astype(BF16))
        y = jnp.zeros(xs.shape, F32)
        for r in range(SSD_REP):
            seg = jnp.exp(jnp.where(causal, ac_seg[:, r * l:(r + 1) * l] - acum_t[r:r + 1, :], NEG_BIG))
            x_r = jnp.where(head_of_lane == r, xdt, 0.0).astype(BF16)
            y = y + _dot((cb * seg).astype(BF16), x_r)
        ac_last = ac_full[l - 1:l, :]
        state = st_ref[g]
        y = y + _dot(cmb, state.astype(BF16)) * jnp.exp(ac_full)
        decayed = (xdt * jnp.exp(ac_last - ac_full)).astype(BF16)
        st_ref[g] = state * jnp.exp(ac_last) + _dot(bm.T.astype(BF16), decayed)
        y = y + xs * dskip_ref[:, g * gw:(g + 1) * gw]
        z = z_ref[:, g * gw:(g + 1) * gw]
        yz = y * (z * _sigmoid(z))
        o_ref[:, g * gw:(g + 1) * gw] = (yz * lax.rsqrt(jnp.mean(yz * yz, axis=-1, keepdims=True) + EPS)
                                        * ng_ref[:, g * gw:(g + 1) * gw])


def _ssd(xbc, z, gd, dtt, conv_w, conv_b, a_log, d_skip, norm_g, b, s):
    t = xbc.shape[0]
    l = SSD_CHUNK
    nc = s // l
    gw = SSD_WIDTH // SSD_GROUPS
    xcols = gw // LANES
    b_blk0 = SSD_WIDTH // LANES
    c_blk0 = b_blk0 + SSD_GROUPS * SSD_STATE // LANES
    conv_b2 = conv_b[None, :]
    a_grp = a_log.reshape(SSD_GROUPS, SSD_REP)
    alane = jnp.zeros((SSD_GROUPS, 1, LANES), F32).at[:, 0, 16:16 + SSD_REP].set(a_grp)
    asub = jnp.zeros((SSD_GROUPS, SUBLANES, LANES), F32).at[:, :SSD_REP, :].set(
        jnp.broadcast_to(a_grp[:, :, None], (SSD_GROUPS, SSD_REP, LANES)))
    dskip = jnp.repeat(d_skip, SSD_HEAD_DIM)[None, :]
    ng = norm_g[None, :]
    ii = np.arange(l)
    tril = jnp.asarray(ii[:, None] >= ii[None, :], BF16)
    edt = np.zeros((LANES, gw), np.float32)
    eseg = np.zeros((LANES, SSD_REP * l), np.float32)
    for r in range(SSD_REP):
        edt[16 + r, r * SSD_HEAD_DIM:(r + 1) * SSD_HEAD_DIM] = 1.0
        eseg[16 + r, r * l:(r + 1) * l] = 1.0
    edt, eseg = jnp.asarray(edt, BF16), jnp.asarray(eseg, BF16)

    def rowblk(width):
        return pl.BlockSpec((l, width), lambda bi, c: (bi * nc + c, 0))

    def const(shape):
        return pl.BlockSpec(shape, lambda bi, c: (0,) * len(shape))

    return pl.pallas_call(
        _ssd_kernel,
        name="ssd",
        grid=(b, nc),
        in_specs=[rowblk(SSD_CONV_DIM), const((SSD_CONV, SSD_CONV_DIM)), const((1, SSD_CONV_DIM)),
                  rowblk(SSD_WIDTH), rowblk(SSD_GROUPS * LANES),
                  pl.BlockSpec((SSD_GROUPS * SUBLANES, l), lambda bi, c: (0, bi * nc + c)),
                  const((SSD_GROUPS, 1, LANES)), const((SSD_GROUPS, SUBLANES, LANES)),
                  const((1, SSD_WIDTH)), const((1, SSD_WIDTH)),
                  const((l, l)), const((LANES, gw)), const((LANES, SSD_REP * l))],
        out_specs=rowblk(SSD_WIDTH),
        out_shape=jax.ShapeDtypeStruct((t, SSD_WIDTH), F32),
        scratch_shapes=[pltpu.VMEM((SUBLANES, SSD_CONV_DIM), F32), pltpu.VMEM((l + SUBLANES, SSD_CONV_DIM), F32),
                        pltpu.VMEM((SSD_GROUPS, SSD_STATE, gw), F32)],
        compiler_params=_cparams(("parallel", "arbitrary")),
    )(xbc, conv_w, conv_b2, z, gd, dtt, alane, asub, dskip, ng, tril, edt, eseg)


def _rms(v, g):
    return v * lax.rsqrt(jnp.mean(v * v, axis=-1, keepdims=True) + EPS) * g


def _out_proj_kernel(x_ref, on_ref, os_ref, ng_ref, w1_ref, w2_ref, fg_ref, x1_ref, xnt_ref):
    onn = _rms(on_ref[...], ng_ref[...]).astype(BF16)
    x1 = x_ref[...] + _dot(onn, w1_ref[...]) + _dot(os_ref[...].astype(BF16), w2_ref[...])
    x1_ref[...] = x1
    xnt_ref[...] = _rms(x1, fg_ref[...]).T.astype(BF16)


def _out_proj(xt, o_nsa, o_ssd, nsa_g, w_out, ffn_g):
    t = xt.shape[0]
    tm = min(TOK_TILE, t)

    def tok(width):
        return pl.BlockSpec((tm, width), lambda i: (i, 0))

    def full(shape):
        return pl.BlockSpec(shape, lambda i: (0, 0))

    wb = w_out.astype(BF16)
    return pl.pallas_call(
        _out_proj_kernel,
        name="out_proj",
        grid=(t // tm,),
        in_specs=[tok(D_MODEL), tok(NSA_WIDTH), tok(SSD_WIDTH), full((1, NSA_WIDTH)),
                  full((NSA_WIDTH, D_MODEL)), full((SSD_WIDTH, D_MODEL)), full((1, D_MODEL))],
        out_specs=[tok(D_MODEL), pl.BlockSpec((D_MODEL, tm), lambda i: (0, i))],
        out_shape=[jax.ShapeDtypeStruct((t, D_MODEL), F32), jax.ShapeDtypeStruct((D_MODEL, t), BF16)],
        compiler_params=_cparams(("parallel",)),
    )(xt, o_nsa, o_ssd, nsa_g[None, :], wb[:NSA_WIDTH], wb[NSA_WIDTH:], ffn_g[None, :])


def _top_sorted(e, k):
    rows = lax.broadcasted_iota(jnp.int32, (k, e.shape[1]), 0)
    out = jnp.zeros((k, e.shape[1]), F32)
    cur = e
    for j in range(k):
        mk = jnp.max(cur, axis=0, keepdims=True)
        out = jnp.where(rows == j, jnp.maximum(mk, 0.0), out)
        target = jnp.where(mk >= 0.0, mk, 2.0)
        cur = jnp.where(cur == target, -float(j + 1), cur)
    rank = jnp.where(cur < 0.0, -1.0 - cur, float(k))
    return out, rank


def _pair_candidates(a16, b16):
    row8 = lax.broadcasted_iota(jnp.int32, (SUBLANES, a16.shape[1]), 0)
    groups = [a16[0:1] * b16[0:SUBLANES], a16[0:1] * b16[SUBLANES:], a16[1:2] * b16[0:SUBLANES]]
    for a in range(2, SUBLANES):
        groups.append(jnp.where(row8 < PEER_TOPK // (a + 1), a16[a:a + 1] * b16[0:SUBLANES], -1.0))
    groups.append(a16[SUBLANES:] * b16[0:1])
    return jnp.concatenate(groups, axis=0)


def _peer_front_kernel(xnt_ref, wq_ref, keys_ref, av_ref, lc_ref, bv_ref, rb_ref):
    xt = xnt_ref[...]
    tm = xt.shape[1]
    row8 = lax.broadcasted_iota(jnp.int32, (SUBLANES, tm), 0)
    for h in range(PEER_HEADS):
        qh = _dot(wq_ref[h * PEER_KEY_DIM:(h + 1) * PEER_KEY_DIM, :], xt)
        s1 = _dot(keys_ref[h, 0], qh[:PEER_HALF].astype(BF16))
        s2 = _dot(keys_ref[h, 1], qh[PEER_HALF:].astype(BF16))
        e1 = jnp.exp(s1 - jnp.max(s1, axis=0, keepdims=True))
        e2 = jnp.exp(s2 - jnp.max(s2, axis=0, keepdims=True))
        a16, ra = _top_sorted(e1, PEER_TOPK)
        b16, rb = _top_sorted(e2, PEER_TOPK)
        cand = _pair_candidates(a16, b16)
        cur = cand
        zsum = jnp.zeros((1, tm), F32)
        for _ in range(PEER_TOPK):
            tau = jnp.max(cur, axis=0, keepdims=True)
            zsum = zsum + tau
            cur = jnp.where(cur == tau, -1.0, cur)
        picked = jnp.where(cand >= tau, 1.0, 0.0)

        def count(g):
            return jnp.sum(picked[g * SUBLANES:(g + 1) * SUBLANES], axis=0, keepdims=True)

        lens = [count(0) + count(1)] + [count(a + 1) for a in range(1, SUBLANES)]
        len_lo = jnp.zeros((SUBLANES, tm), F32)
        for a, row in enumerate(lens):
            len_lo = jnp.where(row8 == a, row, len_lo)
        len16 = jnp.concatenate([len_lo, picked[(SUBLANES + 1) * SUBLANES:]], axis=0)
        pack = 2 * SUBLANES
        ra_b = ra.astype(BF16)
        lc = jnp.zeros(ra.shape, BF16)
        for a in range(PEER_TOPK):
            len_a = jnp.tile(jnp.broadcast_to(len16[a:a + 1], (pack, tm)).astype(BF16), (PEER_N_KEYS // pack, 1))
            lc = jnp.where(ra_b == float(a), len_a, lc)
        rows = slice(h * PEER_N_KEYS, (h + 1) * PEER_N_KEYS)
        av_ref[rows, :] = e1
        lc_ref[rows, :] = lc.astype(F32)
        bv_ref[rows, :] = (e2 * (1.0 / zsum)).astype(BF16)
        rb_ref[rows, :] = rb.astype(BF16)


def _peer_front(xnt, w_q, keys):
    t = xnt.shape[1]
    tm = min(PEER_FRONT_TOK, t)
    nk = PEER_HEADS * PEER_N_KEYS
    return pl.pallas_call(
        _peer_front_kernel,
        name="peer_front",
        grid=(t // tm,),
        in_specs=[pl.BlockSpec((D_MODEL, tm), lambda i: (0, i)),
                  pl.BlockSpec((PEER_HEADS * PEER_KEY_DIM, D_MODEL), lambda i: (0, 0)),
                  pl.BlockSpec((PEER_HEADS, 2, PEER_N_KEYS, PEER_HALF), lambda i: (0, 0, 0, 0))],
        out_specs=[pl.BlockSpec((nk, tm), lambda i: (0, i))] * 4,
        out_shape=[jax.ShapeDtypeStruct((nk, t), F32), jax.ShapeDtypeStruct((nk, t), F32),
                   jax.ShapeDtypeStruct((nk, t), BF16), jax.ShapeDtypeStruct((nk, t), BF16)],
        compiler_params=_cparams(("parallel",)),
    )(xnt, w_q.T.astype(BF16), keys.astype(BF16))


def _row_bf16(ref, row, n_rows):
    pack = 2 * SUBLANES
    tile = jnp.broadcast_to(ref[pl.ds(row, 1), :], (pack, ref.shape[1])).astype(BF16)
    return jnp.tile(tile, (n_rows // pack, 1))


def _peer_dense_kernel(xnt_ref, u_ref, vt_ref, av_ref, lc_ref, bv_ref, rb_ref, o_ref, hid_ref, act_ref):
    j = pl.program_id(1)
    n_sub = u_ref.shape[0] // PEER_N_KEYS

    @pl.when(j == 0)
    def _():
        o_ref[...] = jnp.zeros(o_ref.shape, F32)

    def gate_of(cl):
        e1 = j * n_sub + cl
        gate = jnp.zeros((PEER_N_KEYS, xnt_ref.shape[1]), BF16)
        for h in range(PEER_HEADS):
            rows = slice(h * PEER_N_KEYS, (h + 1) * PEER_N_KEYS)
            lc_row = _row_bf16(lc_ref, h * PEER_N_KEYS + e1, PEER_N_KEYS)
            av_row = _row_bf16(av_ref, h * PEER_N_KEYS + e1, PEER_N_KEYS)
            bv = bv_ref[rows, :]
            gate = gate + jnp.where(rb_ref[rows, :] < lc_row, bv, jnp.zeros_like(bv)) * av_row
        return gate

    piece = 2 * PEER_N_KEYS
    pack = 2 * SUBLANES
    for p in range(u_ref.shape[0] // piece):
        g0, g1 = gate_of(2 * p), gate_of(2 * p + 1)
        zero = jnp.tile(g0[0:pack, 0:LANES] * 0.0, (piece // pack, u_ref.shape[1] // LANES))
        hid = _dot(u_ref[p * piece:(p + 1) * piece, :] + zero, xnt_ref[...])
        act_ref[p * piece:p * piece + PEER_N_KEYS, :] = _gelu_sigmoid(hid[:PEER_N_KEYS]).astype(BF16) * g0
        act_ref[p * piece + PEER_N_KEYS:(p + 1) * piece, :] = _gelu_sigmoid(hid[PEER_N_KEYS:]).astype(BF16) * g1
    o_ref[...] += _dot(vt_ref[...], act_ref[...])


def _peer_dense(xnt, u, vt, av, lc, bv, rb):
    t = xnt.shape[1]
    tm = min(PEER_TOK, t)
    ne = u.shape[0]
    te = PEER_EXP
    tok = pl.BlockSpec((PEER_HEADS * PEER_N_KEYS, tm), lambda i, j: (0, i))
    return pl.pallas_call(
        _peer_dense_kernel,
        name="peer_dense",
        grid=(t // tm, ne // te),
        in_specs=[pl.BlockSpec((D_MODEL, tm), lambda i, j: (0, i)),
                  pl.BlockSpec((te, D_MODEL), lambda i, j: (j, 0)),
                  pl.BlockSpec((D_MODEL, te), lambda i, j: (0, j)),
                  tok, tok, tok, tok],
        out_specs=pl.BlockSpec((D_MODEL, tm), lambda i, j: (0, i)),
        out_shape=jax.ShapeDtypeStruct((D_MODEL, t), F32),
        scratch_shapes=[pltpu.VMEM((te, tm), F32), pltpu.VMEM((te, tm), BF16)],
        compiler_params=_cparams(("parallel", "arbitrary")),
    )(xnt, u, vt, av, lc, bv, rb)


def _final_kernel(x1_ref, pt_ref, g_ref, o_ref):
    o_ref[...] = _rms(x1_ref[...] + pt_ref[...].T, g_ref[...])


def _residual_norm(x1, peer_t, g):
    t = x1.shape[0]
    tm = min(TOK_TILE, t)
    return pl.pallas_call(
        _final_kernel,
        name="residual_norm",
        grid=(t // tm,),
        in_specs=[pl.BlockSpec((tm, D_MODEL), lambda i: (i, 0)), pl.BlockSpec((D_MODEL, tm), lambda i: (0, i)),
                  pl.BlockSpec((1, D_MODEL), lambda i: (0, 0))],
        out_specs=pl.BlockSpec((tm, D_MODEL), lambda i: (i, 0)),
        out_shape=jax.ShapeDtypeStruct((t, D_MODEL), F32),
        compiler_params=_cparams(("parallel",)),
    )(x1, peer_t, g[None, :])


def kernel(x, attn_norm_g, w_in, cmp_pos_k, cmp_w1_k, cmp_b1_k, cmp_w2_k, cmp_pos_v, cmp_w1_v, cmp_b1_v, cmp_w2_v,
           conv_w, conv_b, dt_bias, a_log, d_skip, ssd_norm_g, nsa_norm_g, w_out, ffn_norm_g, peer_w_q, peer_keys,
           peer_u, peer_v, final_norm_g):
    b, s, d = x.shape
    xt = x.reshape(b * s, d)
    assert attn_norm_g.shape[0] == 1, "single-layer block"
    for layer in range(1):
        q, kc, vc, ks, vs, kw, vw, gd, dtt, z, xbc = _in_proj(xt, attn_norm_g[layer], w_in[layer], dt_bias[layer], s)
        kcmp = _compress(kc, cmp_pos_k[layer], cmp_w1_k[layer], cmp_b1_k[layer], cmp_w2_k[layer], b, s)
        vcmp = _compress(vc, cmp_pos_v[layer], cmp_w1_v[layer], cmp_b1_v[layer], cmp_w2_v[layer], b, s)
        o_nsa = _nsa(q, kcmp, vcmp, ks, vs, kw, vw, gd, b, s)
        o_ssd = _ssd(xbc, z, gd, dtt, conv_w[layer], conv_b[layer], a_log[layer], d_skip[layer],
                     ssd_norm_g[layer], b, s)
        x1, xnt = _out_proj(xt, o_nsa, o_ssd, nsa_norm_g[layer], w_out[layer], ffn_norm_g[layer])
        av, lc, bv, rb = _peer_front(xnt, peer_w_q[layer], peer_keys[layer])
        peer_t = _peer_dense(xnt, peer_u[layer].astype(BF16), peer_v[layer].T.astype(BF16), av, lc, bv, rb)
    return _residual_norm(x1, peer_t, final_norm_g).reshape(b, s, d)
```

```python
import functools

import numpy as np
import jax
import jax.numpy as jnp
from jax import lax
from jax.experimental import pallas as pl
from jax.experimental.pallas import tpu as pltpu

F32 = jnp.float32
BF16 = jnp.bfloat16

EPS = 1e-6
D_MODEL = 1024
NSA_HEADS = 8
NSA_KV_GROUPS = 2
NSA_REP = NSA_HEADS // NSA_KV_GROUPS
HEAD_DIM = 64
NSA_WIDTH = NSA_HEADS * HEAD_DIM
KV_WIDTH = NSA_KV_GROUPS * HEAD_DIM
CMP_STRIDE = 16
CMP_BLOCK = 32
CMP_HIDDEN = 256
SEL_BLOCK = 64
SEL_TOP_N = 16
WINDOW = 512
ROPE_DIM = HEAD_DIM // 4
ROPE_THETA = 500000.0
SSD_HEADS = 8
SSD_HEAD_DIM = 64
SSD_WIDTH = SSD_HEADS * SSD_HEAD_DIM
SSD_GROUPS = 2
SSD_REP = SSD_HEADS // SSD_GROUPS
SSD_STATE = 128
SSD_CONV = 4
SSD_CHUNK = 128
SSD_CONV_DIM = SSD_WIDTH + 2 * SSD_GROUPS * SSD_STATE
PEER_HEADS = 8
PEER_N_KEYS = 128
PEER_N_EXPERTS = PEER_N_KEYS * PEER_N_KEYS
PEER_KEY_DIM = 256
PEER_HALF = PEER_KEY_DIM // 2
PEER_TOPK = 16

LANES = 128
SUBLANES = 8
VMEM_LIMIT = 48 * 1024 * 1024
NEG_BIG = -1e30
SEL_NEG = -1e9

TOK_TILE = 512
Q_TILE = 2 * SEL_BLOCK
K_TILE = 2 * SEL_BLOCK
S_CHUNK = 4 * K_TILE
PEER_TOK = 512
PEER_EXP = 2048
PEER_FRONT_TOK = 256


def _cparams(sem, flags=None):
    return pltpu.CompilerParams(dimension_semantics=sem, vmem_limit_bytes=VMEM_LIMIT, flags=flags)


def _dot(a, b):
    return jnp.dot(a, b, preferred_element_type=F32)


def _dot_nt(a, b):
    return lax.dot_general(a, b, (((1,), (1,)), ((), ())), preferred_element_type=F32)


def _split3(a):
    a1 = a.astype(BF16)
    r1 = a - a1.astype(F32)
    a2 = r1.astype(BF16)
    a3 = (r1 - a2.astype(F32)).astype(BF16)
    return a1, a2, a3


def _dot_exact_lhs(a, b01):
    a1, a2, a3 = _split3(a)
    return _dot(a1, b01) + _dot(a2, b01) + _dot(a3, b01)


def _dot_nt_exact_lhs(a, b01):
    a1, a2, a3 = _split3(a)
    return _dot_nt(a1, b01) + _dot_nt(a2, b01) + _dot_nt(a3, b01)


def _dot_exact_rhs(a01, b):
    b1, b2, b3 = _split3(b)
    return _dot(a01, b1) + _dot(a01, b2) + _dot(a01, b3)


def _dot_nt_exact_rhs(a01, b):
    b1, b2, b3 = _split3(b)
    return _dot_nt(a01, b1) + _dot_nt(a01, b2) + _dot_nt(a01, b3)


def _softplus(x):
    return jnp.maximum(x, 0.0) + jnp.log(1.0 + jnp.exp(-jnp.abs(x)))


def _sigmoid(x):
    return 1.0 / (1.0 + jnp.exp(-x))


def _gelu_tanh(x):
    return 0.5 * x * (1.0 + jnp.tanh(0.7978845608028654 * (x + 0.044715 * (x * x * x))))


def _gelu_sigmoid(x):
    c1 = -2.0 * 0.7978845608028654 * 1.4426950408889634
    u = x * (c1 + (c1 * 0.044715) * (x * x))
    return x / (1.0 + jnp.exp2(u))


def _rope128(p, cos, sa, sb):
    return p * cos + pltpu.roll(p, LANES - ROPE_DIM // 2, 1) * sa + pltpu.roll(p, ROPE_DIM // 2, 1) * sb


def _in_proj_kernel(seq_tiles, x_ref, g_ref, w_ref, wdt_ref, cos_ref, sa_ref, sb_ref, gdb_ref, dtb_ref,
                    q_ref, kc_ref, vc_ref, ks_ref, vs_ref, kw_ref, vw_ref, gd_ref, dtt_ref, z_ref, xbc_ref):
    tm = x_ref.shape[0]
    x = x_ref[...]
    h = x * lax.rsqrt(jnp.mean(x * x, axis=-1, keepdims=True) + EPS) * g_ref[...]
    hb = h.astype(BF16)
    pos0 = pl.multiple_of((pl.program_id(0) % seq_tiles) * tm, tm)
    cos = cos_ref[pl.ds(pos0, tm), :]
    sa = sa_ref[pl.ds(pos0, tm), :]
    sb = sb_ref[pl.ds(pos0, tm), :]
    lane = lax.broadcasted_iota(jnp.int32, (tm, LANES), 1)
    low = lane < HEAD_DIM

    def proj(c0, width):
        return _dot(hb, w_ref[:, c0:c0 + width])

    def halves(p):
        return p, pltpu.roll(p, HEAD_DIM, 1)

    pq = proj(0, NSA_WIDTH)
    for s in range(NSA_WIDTH // LANES):
        srcs = halves(_rope128(pq[:, s * LANES:(s + 1) * LANES], cos, sa, sb) * (HEAD_DIM ** -0.5))
        for half in range(2):
            head = 2 * s + half
            q_ref[:, head * LANES:(head + 1) * LANES] = jnp.where(low, srcs[half], 0.0).astype(BF16)
    c0 = NSA_WIDTH
    kc_ref[...] = _rope128(proj(c0, LANES), cos, sa, sb).astype(BF16)
    vc_ref[...] = proj(c0 + LANES, LANES).astype(BF16)
    c0 += 2 * LANES
    blk_of_row = (pos0 + lax.broadcasted_iota(jnp.int32, (tm, LANES), 0)) // SEL_BLOCK
    onehot = jnp.where(lane - HEAD_DIM == blk_of_row, 1.0, 0.0)
    for ref, rope, fill in ((ks_ref, True, onehot), (vs_ref, False, 1.0), (kw_ref, True, 0.0), (vw_ref, False, 1.0)):
        p = proj(c0, LANES)
        if rope:
            p = _rope128(p, cos, sa, sb)
        for g, src in enumerate(halves(p)):
            ref[:, g * LANES:(g + 1) * LANES] = jnp.where(low, src, fill).astype(BF16)
        c0 += LANES
    for s in range(2):
        p = proj(c0, LANES) + gdb_ref[:, s * LANES:(s + 1) * LANES]
        act = jnp.where(lane < 3 * NSA_REP, _sigmoid(p),
                        jnp.where((lane >= 16) & (lane < 16 + SSD_REP), _softplus(p), 0.0))
        gd_ref[:, s * LANES:(s + 1) * LANES] = act
        c0 += LANES
    z_ref[...] = proj(c0, SSD_WIDTH)
    c0 += SSD_WIDTH
    xbc_ref[...] = proj(c0, SSD_CONV_DIM)
    dtt_ref[...] = _softplus(_dot_nt(wdt_ref[...], hb) + dtb_ref[:, 0:1])


def _rope_tables(s):
    half = ROPE_DIM // 2
    inv = jnp.power(ROPE_THETA, -jnp.arange(half, dtype=F32) * 2.0 / ROPE_DIM)
    ang = jnp.arange(s).astype(F32)[:, None] * inv[None, :]
    cos, sin = jnp.cos(ang), jnp.sin(ang)
    zeros = jnp.zeros((s, HEAD_DIM - ROPE_DIM), F32)
    cos64 = jnp.concatenate([cos, cos, jnp.ones((s, HEAD_DIM - ROPE_DIM), F32)], axis=1)
    sa64 = jnp.concatenate([-sin, jnp.zeros_like(sin), zeros], axis=1)
    sb64 = jnp.concatenate([jnp.zeros_like(sin), sin, zeros], axis=1)
    return tuple(jnp.concatenate([t, t], axis=1) for t in (cos64, sa64, sb64))


def _in_proj(xt, attn_g, w_in, dt_bias, s):
    t = xt.shape[0]
    tm = min(TOK_TILE, s)
    o_gl = NSA_WIDTH + 6 * KV_WIDTH
    o_z = o_gl + 3 * NSA_HEADS
    o_xbc = o_z + SSD_WIDTH
    o_dt = o_xbc + SSD_CONV_DIM
    gd_cols, gd_bias = [], []
    for g in range(NSA_KV_GROUPS):
        gates = w_in[:, o_gl + 3 * NSA_REP * g:o_gl + 3 * NSA_REP * (g + 1)]
        dts = w_in[:, o_dt + SSD_REP * g:o_dt + SSD_REP * (g + 1)]
        gd_cols += [gates, jnp.zeros((D_MODEL, 16 - 3 * NSA_REP), F32), dts,
                    jnp.zeros((D_MODEL, LANES - 16 - SSD_REP), F32)]
        gd_bias += [jnp.zeros((16,), F32), dt_bias[SSD_REP * g:SSD_REP * (g + 1)],
                    jnp.zeros((LANES - 16 - SSD_REP,), F32)]
    w_main = jnp.concatenate([w_in[:, :o_gl]] + gd_cols + [w_in[:, o_z:o_dt]], axis=1).astype(BF16)
    gdb = jnp.concatenate(gd_bias)[None, :]
    wdt_rows, dtb_rows = [], []
    for g in range(SSD_GROUPS):
        wdt_rows += [w_in[:, o_dt + SSD_REP * g:o_dt + SSD_REP * (g + 1)].T,
                     jnp.zeros((SUBLANES - SSD_REP, D_MODEL), F32)]
        dtb_rows += [dt_bias[SSD_REP * g:SSD_REP * (g + 1)], jnp.zeros((SUBLANES - SSD_REP,), F32)]
    wdt = jnp.concatenate(wdt_rows, axis=0).astype(BF16)
    dtb = jnp.broadcast_to(jnp.concatenate(dtb_rows)[:, None], (2 * SUBLANES, LANES))
    cos, sa, sb = _rope_tables(s)
    n_main = w_main.shape[1]

    def full(shape):
        return pl.BlockSpec(shape, lambda i: (0, 0))

    def tok(width):
        return pl.BlockSpec((tm, width), lambda i: (i, 0))

    kv_widths = [LANES, LANES] + [NSA_KV_GROUPS * LANES] * 4
    out_shapes = ([jax.ShapeDtypeStruct((t, NSA_HEADS * LANES), BF16)]
                  + [jax.ShapeDtypeStruct((t, w), BF16) for w in kv_widths]
                  + [jax.ShapeDtypeStruct((t, 2 * LANES), F32),
                     jax.ShapeDtypeStruct((2 * SUBLANES, t), F32),
                     jax.ShapeDtypeStruct((t, SSD_WIDTH), F32),
                     jax.ShapeDtypeStruct((t, SSD_CONV_DIM), F32)])
    out_specs = ([tok(NSA_HEADS * LANES)] + [tok(w) for w in kv_widths]
                 + [tok(2 * LANES), pl.BlockSpec((2 * SUBLANES, tm), lambda i: (0, i)),
                    tok(SSD_WIDTH), tok(SSD_CONV_DIM)])
    return pl.pallas_call(
        functools.partial(_in_proj_kernel, s // tm),
        name="in_proj",
        grid=(t // tm,),
        in_specs=[tok(D_MODEL), full((1, D_MODEL)), full((D_MODEL, n_main)), full((2 * SUBLANES, D_MODEL)),
                  full((s, LANES)), full((s, LANES)), full((s, LANES)), full((1, 2 * LANES)),
                  full((2 * SUBLANES, LANES))],
        out_specs=out_specs,
        out_shape=out_shapes,
        compiler_params=_cparams(("parallel",)),
    )(xt, attn_g[None, :], w_main, wdt, cos, sa, sb, gdb, dtb)


def _compress_kernel(kv_ref, w1a_ref, w1b_ref, pos_ref, w1_ref, b1_ref, w2_ref, out_ref):
    kv = kv_ref[0]
    bias = _dot(pos_ref[...], w1_ref[...])[0:1, :] + b1_ref[...]
    for g in range(NSA_KV_GROUPS):
        first = _dot(kv, w1a_ref[g])
        second = _dot(kv, w1b_ref[g])
        nxt = pltpu.roll(second, second.shape[0] - 1, 0)
        hid = _gelu_tanh(first + nxt + bias)
        out_ref[0, :, g * LANES:(g + 1) * LANES] = _dot(hid.astype(BF16), w2_ref[...]).astype(BF16)


def _compress(kv, pos_emb, w1, b1, w2, b, s):
    nch = s // CMP_STRIDE
    kvf = kv.reshape(b, nch, CMP_STRIDE * LANES)
    w1r = w1.reshape(CMP_BLOCK, HEAD_DIM, CMP_HIDDEN)
    zeros = jnp.zeros((CMP_STRIDE, HEAD_DIM, CMP_HIDDEN), F32)

    def expand(w_half, g):
        parts = [w_half, zeros] if g == 0 else [zeros, w_half]
        return jnp.concatenate(parts, axis=1).reshape(CMP_STRIDE * LANES, CMP_HIDDEN)

    w1a = jnp.stack([expand(w1r[:CMP_STRIDE], g) for g in range(NSA_KV_GROUPS)]).astype(BF16)
    w1b = jnp.stack([expand(w1r[CMP_STRIDE:], g) for g in range(NSA_KV_GROUPS)]).astype(BF16)
    w2e = jnp.concatenate([w2, jnp.zeros((CMP_HIDDEN, LANES - HEAD_DIM), F32)], axis=1).astype(BF16)
    pos = jnp.zeros((SUBLANES, CMP_BLOCK * HEAD_DIM), F32).at[0].set(pos_emb.reshape(-1)).astype(BF16)
    return pl.pallas_call(
        _compress_kernel,
        name="compress",
        grid=(b,),
        in_specs=[pl.BlockSpec((1, nch, CMP_STRIDE * LANES), lambda i: (i, 0, 0)),
                  pl.BlockSpec((2, CMP_STRIDE * LANES, CMP_HIDDEN), lambda i: (0, 0, 0)),
                  pl.BlockSpec((2, CMP_STRIDE * LANES, CMP_HIDDEN), lambda i: (0, 0, 0)),
                  pl.BlockSpec((SUBLANES, CMP_BLOCK * HEAD_DIM), lambda i: (0, 0)),
                  pl.BlockSpec((CMP_BLOCK * HEAD_DIM, CMP_HIDDEN), lambda i: (0, 0)),
                  pl.BlockSpec((1, CMP_HIDDEN), lambda i: (0, 0)),
                  pl.BlockSpec((CMP_HIDDEN, LANES), lambda i: (0, 0))],
        out_specs=pl.BlockSpec((1, nch, NSA_KV_GROUPS * LANES), lambda i: (i, 0, 0)),
        out_shape=jax.ShapeDtypeStruct((b, nch, NSA_KV_GROUPS * LANES), BF16),
        compiler_params=_cparams(("parallel",)),
    )(kvf, w1a, w1b, pos, w1.astype(BF16), b1[None, :], w2e)


def _lane_tiles(a):
    return [a[:, k * LANES:(k + 1) * LANES] for k in range(a.shape[1] // LANES)]


def _max_tiles(macc, s):
    for tile in _lane_tiles(s):
        macc = jnp.maximum(macc, tile)
    return macc


def _exp_tiles(s, m_b):
    return jnp.concatenate([jnp.exp(tile - m_b) for tile in _lane_tiles(s)], axis=1).astype(BF16)


def _normalize(acc):
    return acc / pltpu.roll(acc, HEAD_DIM, 1)


def _block_rank(imp):
    nblk, tq = imp.shape
    sub = lax.broadcasted_iota(jnp.int32, (SUBLANES, tq), 0)
    groups = [imp[SUBLANES * v:SUBLANES * (v + 1)] for v in range(nblk // SUBLANES)]
    cnt = [jnp.zeros((SUBLANES, tq), F32) for _ in groups]
    for j in range(nblk):
        row = imp[j:j + 1, :]
        for v, grp in enumerate(groups):
            if SUBLANES * v > j:
                ahead = row >= grp
            elif SUBLANES * (v + 1) <= j:
                ahead = row > grp
            else:
                ahead = (row > grp) | ((row == grp) & (sub > j - SUBLANES * v))
            cnt[v] = cnt[v] + jnp.where(ahead, 1.0, 0.0)
    return jnp.concatenate(cnt, axis=0)


def _nsa_pair_kernel(q_ref, kc_ref, vc_ref, ks_ref, vs_ref, kw_ref, vw_ref, gd_ref, ovl_ref, o_ref, s_ref):
    i = pl.program_id(1)
    tq = Q_TILE
    rows = NSA_REP * tq
    ncmp = kc_ref.shape[1]
    nblk = ovl_ref.shape[0]
    groups = range(NSA_KV_GROUPS)
    t_row = i * tq + lax.broadcasted_iota(jnp.int32, (rows, 1), 0) % tq

    def lanes(g):
        return slice(g * LANES, (g + 1) * LANES)

    qs = [jnp.concatenate([q_ref[:, (g * NSA_REP + r) * LANES:(g * NSA_REP + r + 1) * LANES]
                           for r in range(NSA_REP)], axis=0) for g in groups]

    cmp_end = lax.broadcasted_iota(jnp.int32, (rows, ncmp), 1) * CMP_STRIDE + (CMP_BLOCK - 1)
    mask_c = cmp_end <= t_row
    p_cs, o_cmps = [], []
    for g in groups:
        s_c = jnp.where(mask_c, _dot_nt(qs[g], kc_ref[0, :, lanes(g)]), NEG_BIG)
        e_c = jnp.exp(s_c - jnp.max(s_c, axis=-1, keepdims=True))
        p_c = jnp.where(mask_c, e_c / jnp.sum(e_c, axis=-1, keepdims=True), 0.0)
        p_cs.append(p_c)
        o_cmps.append(_dot(p_c.astype(BF16), vc_ref[0, :, lanes(g)]))

    blk = lax.broadcasted_iota(jnp.int32, (nblk, tq), 0)
    cur = (i * tq + lax.broadcasted_iota(jnp.int32, (nblk, tq), 1)) // SEL_BLOCK
    forced = (blk == 0) | (blk == cur) | (blk == cur - 1)
    valid = blk <= cur
    eye = jnp.where(lax.broadcasted_iota(jnp.int32, (tq, tq), 0) == lax.broadcasted_iota(jnp.int32, (tq, tq), 1),
                    1.0, 0.0).astype(BF16)
    ones_lo = jnp.ones((HEAD_DIM, tq), F32)
    q2s = []
    for g in groups:
        p_sum = p_cs[g][0:tq]
        for r in range(1, NSA_REP):
            p_sum = p_sum + p_cs[g][r * tq:(r + 1) * tq]
        imp = _dot_nt_exact_rhs(ovl_ref[...], p_sum)
        imp = jnp.where(forced, 1e9, jnp.where(valid, imp, -1.0))
        sel_t = jnp.where((_block_rank(imp) < float(min(SEL_TOP_N, nblk))) & valid, 1.0, 0.0)
        pad = [jnp.ones((HEAD_DIM - nblk, tq), F32)] if nblk < HEAD_DIM else []
        sel_pad = jnp.concatenate([ones_lo, sel_t] + pad, axis=0).astype(BF16)
        negb = ((_dot_nt(eye, sel_pad) - 1.0) * (-SEL_NEG)).astype(BF16)
        q2s.append(qs[g] + jnp.concatenate([negb] * NSA_REP, axis=0))

    jd = (i * tq) // K_TILE
    nfull = jd // (S_CHUNK // K_TILE)
    lane_c = lax.broadcasted_iota(jnp.int32, (rows, S_CHUNK), 1)

    def scores(g, c):
        k0 = pl.multiple_of(c * S_CHUNK, S_CHUNK)
        return _dot_nt(q2s[g], ks_ref[pl.ds(k0, S_CHUNK), lanes(g)])

    def pass_max(c, maccs):
        out = []
        for g in groups:
            s = scores(g, c)
            s_ref[g, c] = s
            out.append(_max_tiles(maccs[g], s))
        return tuple(out)

    neg = jnp.full((rows, LANES), NEG_BIG, F32)
    maccs = lax.fori_loop(0, nfull, pass_max, (neg,) * NSA_KV_GROUPS)
    causal = nfull * S_CHUNK + lane_c <= t_row
    s_lasts = [jnp.where(causal, scores(g, nfull), NEG_BIG) for g in groups]
    m_bs = [jnp.broadcast_to(jnp.max(_max_tiles(maccs[g], s_lasts[g]), axis=-1, keepdims=True), (rows, LANES))
            for g in groups]

    def pass_sum(c, accs):
        k0 = pl.multiple_of(c * S_CHUNK, S_CHUNK)
        return tuple(accs[g] + _dot(_exp_tiles(s_ref[g, c], m_bs[g]), vs_ref[pl.ds(k0, S_CHUNK), lanes(g)])
                     for g in groups)

    zero = jnp.zeros((rows, LANES), F32)
    accs = lax.fori_loop(0, nfull, pass_sum, (zero,) * NSA_KV_GROUPS)
    k_last = pl.multiple_of(nfull * S_CHUNK, S_CHUNK)
    o_sels = [_normalize(accs[g] + _dot(_exp_tiles(s_lasts[g], m_bs[g]), vs_ref[pl.ds(k_last, S_CHUNK), lanes(g)]))
              for g in groups]

    n_wk = WINDOW + K_TILE
    k0 = pl.multiple_of(jnp.maximum(jd - WINDOW // K_TILE, 0) * K_TILE, K_TILE)
    diff = t_row - (k0 + lax.broadcasted_iota(jnp.int32, (rows, n_wk), 1))
    in_win = (diff >= 0) & (diff < WINDOW)
    o_wins = []
    for g in groups:
        s_w = jnp.where(in_win, _dot_nt(q2s[g], kw_ref[pl.ds(k0, n_wk), lanes(g)]), NEG_BIG)
        m_w = jnp.max(_max_tiles(neg, s_w), axis=-1, keepdims=True)
        p_w = _exp_tiles(s_w, jnp.broadcast_to(m_w, (rows, LANES)))
        o_wins.append(_normalize(_dot(p_w, vw_ref[pl.ds(k0, n_wk), lanes(g)])))

    low = lax.broadcasted_iota(jnp.int32, (tq, LANES), 1) < HEAD_DIM
    for g in groups:
        gd = gd_ref[:, lanes(g)]
        heads = []
        for r in range(NSA_REP):
            rs = slice(r * tq, (r + 1) * tq)
            heads.append(gd[:, 3 * r:3 * r + 1] * o_cmps[g][rs] + gd[:, 3 * r + 1:3 * r + 2] * o_sels[g][rs]
                         + gd[:, 3 * r + 2:3 * r + 3] * o_wins[g][rs])
        for sidx in range(NSA_REP // 2):
            slab = g * (NSA_REP // 2) + sidx
            o_ref[:, slab * LANES:(slab + 1) * LANES] = jnp.where(
                low, heads[2 * sidx], pltpu.roll(heads[2 * sidx + 1], HEAD_DIM, 1))


def _nsa(q, kcmp, vcmp, ks, vs, kw, vw, gd, b, s):
    t = q.shape[0]
    nq = s // Q_TILE
    ncmp = s // CMP_STRIDE
    nblk = s // SEL_BLOCK
    cs = np.arange(ncmp) * CMP_STRIDE
    ss = np.arange(nblk) * SEL_BLOCK
    overlap = (cs[:, None] < ss[None, :] + SEL_BLOCK) & (cs[:, None] + CMP_BLOCK > ss[None, :])
    overlap[ncmp - 1, :] = False
    ovl_t = jnp.asarray(overlap.T, BF16)

    assert nblk <= HEAD_DIM and s % S_CHUNK == 0 and s >= WINDOW + K_TILE

    ng = NSA_KV_GROUPS

    def seq_spec():
        return pl.BlockSpec((s, ng * LANES), lambda bi, i: (bi, 0))

    return pl.pallas_call(
        _nsa_pair_kernel,
        name="nsa",
        grid=(b, nq),
        in_specs=[pl.BlockSpec((Q_TILE, NSA_HEADS * LANES), lambda bi, i: (bi * nq + i, 0)),
                  pl.BlockSpec((1, ncmp, ng * LANES), lambda bi, i: (bi, 0, 0)),
                  pl.BlockSpec((1, ncmp, ng * LANES), lambda bi, i: (bi, 0, 0)),
                  seq_spec(), seq_spec(), seq_spec(), seq_spec(),
                  pl.BlockSpec((Q_TILE, ng * LANES), lambda bi, i: (bi * nq + i, 0)),
                  pl.BlockSpec((nblk, ncmp), lambda bi, i: (0, 0))],
        out_specs=pl.BlockSpec((Q_TILE, NSA_WIDTH), lambda bi, i: (bi * nq + i, 0)),
        out_shape=jax.ShapeDtypeStruct((t, NSA_WIDTH), F32),
        scratch_shapes=[pltpu.VMEM((ng, s // S_CHUNK, NSA_REP * Q_TILE, S_CHUNK), F32)],
        compiler_params=_cparams(("parallel", "arbitrary")),
    )(q, kcmp, vcmp, ks, vs, kw, vw, gd, ovl_t)


def _conv_silu(cur_ref, ext_ref, tail_ref, w_ref, b_ref):
    l = cur_ref.shape[0]
    cur = cur_ref[...]
    ext_ref[0:SUBLANES, :] = tail_ref[...]
    ext_ref[SUBLANES:, :] = cur
    tail_ref[...] = cur[l - SUBLANES:, :]
    acc = jnp.zeros(cur.shape, F32) + b_ref[...]
    for k in range(SSD_CONV):
        off = SUBLANES - (SSD_CONV - 1) + k
        acc = acc + ext_ref[off:off + l, :] * w_ref[k:k + 1, :]
    return acc * _sigmoid(acc)


def _ssd_kernel(xbc_ref, w_ref, b_ref, z_ref, gd_ref, dtt_ref, alane_ref, asub_ref, dskip_ref, ng_ref,
                tril_ref, edt_ref, eseg_ref, o_ref, tail_ref, ext_ref, st_ref):
    c = pl.program_id(1)
    l = xbc_ref.shape[0]
    gw = SSD_WIDTH // SSD_GROUPS
    b_col0 = SSD_WIDTH
    c_col0 = SSD_WIDTH + SSD_GROUPS * SSD_STATE

    @pl.when(c == 0)
    def _():
        tail_ref[...] = jnp.zeros(tail_ref.shape, F32)
        st_ref[...] = jnp.zeros(st_ref.shape, F32)

    conv = _conv_silu(xbc_ref, ext_ref, tail_ref, w_ref, b_ref)
    tril = tril_ref[...]
    causal = lax.broadcasted_iota(jnp.int32, (l, l), 0) >= lax.broadcasted_iota(jnp.int32, (l, l), 1)
    head_of_lane = lax.broadcasted_iota(jnp.int32, (l, gw), 1) // SSD_HEAD_DIM
    for g in range(SSD_GROUPS):
        xs = conv[:, g * gw:(g + 1) * gw]
        bm = conv[:, b_col0 + g * SSD_STATE:b_col0 + (g + 1) * SSD_STATE]
        cm = conv[:, c_col0 + g * SSD_STATE:c_col0 + (g + 1) * SSD_STATE]
        gd = gd_ref[:, g * LANES:(g + 1) * LANES]
        adt_c = gd * (-jnp.exp(alane_ref[g]))
        acum_c = _dot_exact_rhs(tril, adt_c)
        dt_full = _dot_exact_lhs(gd, edt_ref[...])
        ac_full = _dot_exact_lhs(acum_c, edt_ref[...])
        ac_seg = _dot_exact_lhs(acum_c, eseg_ref[...])
        adt_t = dtt_ref[g * SUBLANES:(g + 1) * SUBLANES, :] * (-jnp.exp(asub_ref[g]))
        acum_t = _dot_nt_exact_lhs(adt_t, tril)

        xdt = xs * dt_full
        cmb = cm.astype(BF16)
        cb = _dot_nt(cmb, bm.astype(BF16))
        y = jnp.zeros(xs.shape, F32)
        for r in range(SSD_REP):
            seg = jnp.exp(jnp.where(causal, ac_seg[:, r * l:(r + 1) * l] - acum_t[r:r + 1, :], NEG_BIG))
            x_r = jnp.where(head_of_lane == r, xdt, 0.0).astype(BF16)
            y = y + _dot((cb * seg).astype(BF16), x_r)
        ac_last = ac_full[l - 1:l, :]
        state = st_ref[g]
        y = y + _dot(cmb, state.astype(BF16)) * jnp.exp(ac_full)
        decayed = (xdt * jnp.exp(ac_last - ac_full)).astype(BF16)
        st_ref[g] = state * jnp.exp(ac_last) + _dot(bm.T.astype(BF16), decayed)
        y = y + xs * dskip_ref[:, g * gw:(g + 1) * gw]
        z = z_ref[:, g * gw:(g + 1) * gw]
        yz = y * (z * _sigmoid(z))
        o_ref[:, g * gw:(g + 1) * gw] = (yz * lax.rsqrt(jnp.mean(yz * yz, axis=-1, keepdims=True) + EPS)
                                        * ng_ref[:, g * gw:(g + 1) * gw])


def _ssd(xbc, z, gd, dtt, conv_w, conv_b, a_log, d_skip, norm_g, b, s):
    t = xbc.shape[0]
    l = SSD_CHUNK
    nc = s // l
    gw = SSD_WIDTH // SSD_GROUPS
    xcols = gw // LANES
    b_blk0 = SSD_WIDTH // LANES
    c_blk0 = b_blk0 + SSD_GROUPS * SSD_STATE // LANES
    conv_b2 = conv_b[None, :]
    a_grp = a_log.reshape(SSD_GROUPS, SSD_REP)
    alane = jnp.zeros((SSD_GROUPS, 1, LANES), F32).at[:, 0, 16:16 + SSD_REP].set(a_grp)
    asub = jnp.zeros((SSD_GROUPS, SUBLANES, LANES), F32).at[:, :SSD_REP, :].set(
        jnp.broadcast_to(a_grp[:, :, None], (SSD_GROUPS, SSD_REP, LANES)))
    dskip = jnp.repeat(d_skip, SSD_HEAD_DIM)[None, :]
    ng = norm_g[None, :]
    ii = np.arange(l)
    tril = jnp.asarray(ii[:, None] >= ii[None, :], BF16)
    edt = np.zeros((LANES, gw), np.float32)
    eseg = np.zeros((LANES, SSD_REP * l), np.float32)
    for r in range(SSD_REP):
        edt[16 + r, r * SSD_HEAD_DIM:(r + 1) * SSD_HEAD_DIM] = 1.0
        eseg[16 + r, r * l:(r + 1) * l] = 1.0
    edt, eseg = jnp.asarray(edt, BF16), jnp.asarray(eseg, BF16)

    def rowblk(width):
        return pl.BlockSpec((l, width), lambda bi, c: (bi * nc + c, 0))

    def const(shape):
        return pl.BlockSpec(shape, lambda bi, c: (0,) * len(shape))

    return pl.pallas_call(
        _ssd_kernel,
        name="ssd",
        grid=(b, nc),
        in_specs=[rowblk(SSD_CONV_DIM), const((SSD_CONV, SSD_CONV_DIM)), const((1, SSD_CONV_DIM)),
                  rowblk(SSD_WIDTH), rowblk(SSD_GROUPS * LANES),
                  pl.BlockSpec((SSD_GROUPS * SUBLANES, l), lambda bi, c: (0, bi * nc + c)),
                  const((SSD_GROUPS, 1, LANES)), const((SSD_GROUPS, SUBLANES, LANES)),
                  const((1, SSD_WIDTH)), const((1, SSD_WIDTH)),
                  const((l, l)), const((LANES, gw)), const((LANES, SSD_REP * l))],
        out_specs=rowblk(SSD_WIDTH),
        out_shape=jax.ShapeDtypeStruct((t, SSD_WIDTH), F32),
        scratch_shapes=[pltpu.VMEM((SUBLANES, SSD_CONV_DIM), F32), pltpu.VMEM((l + SUBLANES, SSD_CONV_DIM), F32),
                        pltpu.VMEM((SSD_GROUPS, SSD_STATE, gw), F32)],
        compiler_params=_cparams(("parallel", "arbitrary")),
    )(xbc, conv_w, conv_b2, z, gd, dtt, alane, asub, dskip, ng, tril, edt, eseg)


def _rms(v, g):
    return v * lax.rsqrt(jnp.mean(v * v, axis=-1, keepdims=True) + EPS) * g


def _out_proj_kernel(x_ref, on_ref, os_ref, ng_ref, w1_ref, w2_ref, fg_ref, x1_ref, xnt_ref):
    onn = _rms(on_ref[...], ng_ref[...]).astype(BF16)
    x1 = x_ref[...] + _dot(onn, w1_ref[...]) + _dot(os_ref[...].astype(BF16), w2_ref[...])
    x1_ref[...] = x1
    xnt_ref[...] = _rms(x1, fg_ref[...]).T.astype(BF16)


def _out_proj(xt, o_nsa, o_ssd, nsa_g, w_out, ffn_g):
    t = xt.shape[0]
    tm = min(TOK_TILE, t)

    def tok(width):
        return pl.BlockSpec((tm, width), lambda i: (i, 0))

    def full(shape):
        return pl.BlockSpec(shape, lambda i: (0, 0))

    wb = w_out.astype(BF16)
    return pl.pallas_call(
        _out_proj_kernel,
        name="out_proj",
        grid=(t // tm,),
        in_specs=[tok(D_MODEL), tok(NSA_WIDTH), tok(SSD_WIDTH), full((1, NSA_WIDTH)),
                  full((NSA_WIDTH, D_MODEL)), full((SSD_WIDTH, D_MODEL)), full((1, D_MODEL))],
        out_specs=[tok(D_MODEL), pl.BlockSpec((D_MODEL, tm), lambda i: (0, i))],
        out_shape=[jax.ShapeDtypeStruct((t, D_MODEL), F32), jax.ShapeDtypeStruct((D_MODEL, t), BF16)],
        compiler_params=_cparams(("parallel",)),
    )(xt, o_nsa, o_ssd, nsa_g[None, :], wb[:NSA_WIDTH], wb[NSA_WIDTH:], ffn_g[None, :])


def _top_sorted(e, k):
    rows = lax.broadcasted_iota(jnp.int32, (k, e.shape[1]), 0)
    out = jnp.zeros((k, e.shape[1]), F32)
    cur = e
    for j in range(k):
        mk = jnp.max(cur, axis=0, keepdims=True)
        out = jnp.where(rows == j, jnp.maximum(mk, 0.0), out)
        target = jnp.where(mk >= 0.0, mk, 2.0)
        cur = jnp.where(cur == target, -float(j + 1), cur)
    rank = jnp.where(cur < 0.0, -1.0 - cur, float(k))
    return out, rank


def _pair_candidates(a16, b16):
    row8 = lax.broadcasted_iota(jnp.int32, (SUBLANES, a16.shape[1]), 0)
    groups = [a16[0:1] * b16[0:SUBLANES], a16[0:1] * b16[SUBLANES:], a16[1:2] * b16[0:SUBLANES]]
    for a in range(2, SUBLANES):
        groups.append(jnp.where(row8 < PEER_TOPK // (a + 1), a16[a:a + 1] * b16[0:SUBLANES], -1.0))
    groups.append(a16[SUBLANES:] * b16[0:1])
    return jnp.concatenate(groups, axis=0)


def _peer_front_kernel(xnt_ref, wq_ref, keys_ref, av_ref, lc_ref, bv_ref, rb_ref):
    xt = xnt_ref[...]
    tm = xt.shape[1]
    row8 = lax.broadcasted_iota(jnp.int32, (SUBLANES, tm), 0)
    for h in range(PEER_HEADS):
        qh = _dot(wq_ref[h * PEER_KEY_DIM:(h + 1) * PEER_KEY_DIM, :], xt)
        s1 = _dot(keys_ref[h, 0], qh[:PEER_HALF].astype(BF16))
        s2 = _dot(keys_ref[h, 1], qh[PEER_HALF:].astype(BF16))
        e1 = jnp.exp(s1 - jnp.max(s1, axis=0, keepdims=True))
        e2 = jnp.exp(s2 - jnp.max(s2, axis=0, keepdims=True))
        a16, ra = _top_sorted(e1, PEER_TOPK)
        b16, rb = _top_sorted(e2, PEER_TOPK)
        cand = _pair_candidates(a16, b16)
        cur = cand
        zsum = jnp.zeros((1, tm), F32)
        for _ in range(PEER_TOPK):
            tau = jnp.max(cur, axis=0, keepdims=True)
            zsum = zsum + tau
            cur = jnp.where(cur == tau, -1.0, cur)
        picked = jnp.where(cand >= tau, 1.0, 0.0)

        def count(g):
            return jnp.sum(picked[g * SUBLANES:(g + 1) * SUBLANES], axis=0, keepdims=True)

        lens = [count(0) + count(1)] + [count(a + 1) for a in range(1, SUBLANES)]
        len_lo = jnp.zeros((SUBLANES, tm), F32)
        for a, row in enumerate(lens):
            len_lo = jnp.where(row8 == a, row, len_lo)
        len16 = jnp.concatenate([len_lo, picked[(SUBLANES + 1) * SUBLANES:]], axis=0)
        pack = 2 * SUBLANES
        ra_b = ra.astype(BF16)
        lc = jnp.zeros(ra.shape, BF16)
        for a in range(PEER_TOPK):
            len_a = jnp.tile(jnp.broadcast_to(len16[a:a + 1], (pack, tm)).astype(BF16), (PEER_N_KEYS // pack, 1))
            lc = jnp.where(ra_b == float(a), len_a, lc)
        rows = slice(h * PEER_N_KEYS, (h + 1) * PEER_N_KEYS)
        av_ref[rows, :] = e1
        lc_ref[rows, :] = lc.astype(F32)
        bv_ref[rows, :] = (e2 * (1.0 / zsum)).astype(BF16)
        rb_ref[rows, :] = rb.astype(BF16)


def _peer_front(xnt, w_q, keys):
    t = xnt.shape[1]
    tm = min(PEER_FRONT_TOK, t)
    nk = PEER_HEADS * PEER_N_KEYS
    return pl.pallas_call(
        _peer_front_kernel,
        name="peer_front",
        grid=(t // tm,),
        in_specs=[pl.BlockSpec((D_MODEL, tm), lambda i: (0, i)),
                  pl.BlockSpec((PEER_HEADS * PEER_KEY_DIM, D_MODEL), lambda i: (0, 0)),
                  pl.BlockSpec((PEER_HEADS, 2, PEER_N_KEYS, PEER_HALF), lambda i: (0, 0, 0, 0))],
        out_specs=[pl.BlockSpec((nk, tm), lambda i: (0, i))] * 4,
        out_shape=[jax.ShapeDtypeStruct((nk, t), F32), jax.ShapeDtypeStruct((nk, t), F32),
                   jax.ShapeDtypeStruct((nk, t), BF16), jax.ShapeDtypeStruct((nk, t), BF16)],
        compiler_params=_cparams(("parallel",)),
    )(xnt, w_q.T.astype(BF16), keys.astype(BF16))


def _row_bf16(ref, row, n_rows):
    pack = 2 * SUBLANES
    tile = jnp.broadcast_to(ref[pl.ds(row, 1), :], (pack, ref.shape[1])).astype(BF16)
    return jnp.tile(tile, (n_rows // pack, 1))


def _peer_dense_kernel(xnt_ref, u_ref, vt_ref, av_ref, lc_ref, bv_ref, rb_ref, o_ref, hid_ref, act_ref):
    j = pl.program_id(1)
    n_sub = u_ref.shape[0] // PEER_N_KEYS

    @pl.when(j == 0)
    def _():
        o_ref[...] = jnp.zeros(o_ref.shape, F32)

    def gate_of(cl):
        e1 = j * n_sub + cl
        gate = jnp.zeros((PEER_N_KEYS, xnt_ref.shape[1]), BF16)
        for h in range(PEER_HEADS):
            rows = slice(h * PEER_N_KEYS, (h + 1) * PEER_N_KEYS)
            lc_row = _row_bf16(lc_ref, h * PEER_N_KEYS + e1, PEER_N_KEYS)
            av_row = _row_bf16(av_ref, h * PEER_N_KEYS + e1, PEER_N_KEYS)
            bv = bv_ref[rows, :]
            gate = gate + jnp.where(rb_ref[rows, :] < lc_row, bv, jnp.zeros_like(bv)) * av_row
        return gate

    piece = 2 * PEER_N_KEYS
    pack = 2 * SUBLANES
    for p in range(u_ref.shape[0] // piece):
        g0, g1 = gate_of(2 * p), gate_of(2 * p + 1)
        zero = jnp.tile(g0[0:pack, 0:LANES] * 0.0, (piece // pack, u_ref.shape[1] // LANES))
        hid = _dot(u_ref[p * piece:(p + 1) * piece, :] + zero, xnt_ref[...])
        act_ref[p * piece:p * piece + PEER_N_KEYS, :] = _gelu_sigmoid(hid[:PEER_N_KEYS]).astype(BF16) * g0
        act_ref[p * piece + PEER_N_KEYS:(p + 1) * piece, :] = _gelu_sigmoid(hid[PEER_N_KEYS:]).astype(BF16) * g1
    o_ref[...] += _dot(vt_ref[...], act_ref[...])


def _peer_dense(xnt, u, vt, av, lc, bv, rb):
    t = xnt.shape[1]
    tm = min(PEER_TOK, t)
    ne = u.shape[0]
    te = PEER_EXP
    tok = pl.BlockSpec((PEER_HEADS * PEER_N_KEYS, tm), lambda i, j: (0, i))
    return pl.pallas_call(
        _peer_dense_kernel,
        name="peer_dense",
        grid=(t // tm, ne // te),
        in_specs=[pl.BlockSpec((D_MODEL, tm), lambda i, j: (0, i)),
                  pl.BlockSpec((te, D_MODEL), lambda i, j: (j, 0)),
                  pl.BlockSpec((D_MODEL, te), lambda i, j: (0, j)),
                  tok, tok, tok, tok],
        out_specs=pl.BlockSpec((D_MODEL, tm), lambda i, j: (0, i)),
        out_shape=jax.ShapeDtypeStruct((D_MODEL, t), F32),
        scratch_shapes=[pltpu.VMEM((te, tm), F32), pltpu.VMEM((te, tm), BF16)],
        compiler_params=_cparams(("parallel", "arbitrary")),
    )(xnt, u, vt, av, lc, bv, rb)


def _final_kernel(x1_ref, pt_ref, g_ref, o_ref):
    o_ref[...] = _rms(x1_ref[...] + pt_ref[...].T, g_ref[...])


def _residual_norm(x1, peer_t, g):
    t = x1.shape[0]
    tm = min(TOK_TILE, t)
    return pl.pallas_call(
        _final_kernel,
        name="residual_norm",
        grid=(t // tm,),
        in_specs=[pl.BlockSpec((tm, D_MODEL), lambda i: (i, 0)), pl.BlockSpec((D_MODEL, tm), lambda i: (0, i)),
                  pl.BlockSpec((1, D_MODEL), lambda i: (0, 0))],
        out_specs=pl.BlockSpec((tm, D_MODEL), lambda i: (i, 0)),
        out_shape=jax.ShapeDtypeStruct((t, D_MODEL), F32),
        compiler_params=_cparams(("parallel",)),
    )(x1, peer_t, g[None, :])


def kernel(x, attn_norm_g, w_in, cmp_pos_k, cmp_w1_k, cmp_b1_k, cmp_w2_k, cmp_pos_v, cmp_w1_v, cmp_b1_v, cmp_w2_v,
           conv_w, conv_b, dt_bias, a_log, d_skip, ssd_norm_g, nsa_norm_g, w_out, ffn_norm_g, peer_w_q, peer_keys,
           peer_u, peer_v, final_norm_g):
    b, s, d = x.shape
    xt = x.reshape(b * s, d)
    assert attn_norm_g.shape[0] == 1, "single-layer block"
    for layer in range(1):
        q, kc, vc, ks, vs, kw, vw, gd, dtt, z, xbc = _in_proj(xt, attn_norm_g[layer], w_in[layer], dt_bias[layer], s)
        kcmp = _compress(kc, cmp_pos_k[layer], cmp_w1_k[layer], cmp_b1_k[layer], cmp_w2_k[layer], b, s)
        vcmp = _compress(vc, cmp_pos_v[layer], cmp_w1_v[layer], cmp_b1_v[layer], cmp_w2_v[layer], b, s)
        o_nsa = _nsa(q, kcmp, vcmp, ks, vs, kw, vw, gd, b, s)
        o_ssd = _ssd(xbc, z, gd, dtt, conv_w[layer], conv_b[layer], a_log[layer], d_skip[layer],
                     ssd_norm_g[layer], b, s)
        x1, xnt = _out_proj(xt, o_nsa, o_ssd, nsa_norm_g[layer], w_out[layer], ffn_norm_g[layer])
        av, lc, bv, rb = _peer_front(xnt, peer_w_q[layer], peer_keys[layer])
        peer_t = _peer_dense(xnt, peer_u[layer].astype(BF16), peer_v[layer].T.astype(BF16), av, lc, bv, rb)
    return _residual_norm(x1, peer_t, final_norm_g).reshape(b, s, d)
```

```python
import functools

import numpy as np
import jax
import jax.numpy as jnp
from jax import lax
from jax.experimental import pallas as pl
from jax.experimental.pallas import tpu as pltpu

F32 = jnp.float32
BF16 = jnp.bfloat16

EPS = 1e-6
D_MODEL = 1024
NSA_HEADS = 8
NSA_KV_GROUPS = 2
NSA_REP = NSA_HEADS // NSA_KV_GROUPS
HEAD_DIM = 64
NSA_WIDTH = NSA_HEADS * HEAD_DIM
KV_WIDTH = NSA_KV_GROUPS * HEAD_DIM
CMP_STRIDE = 16
CMP_BLOCK = 32
CMP_HIDDEN = 256
SEL_BLOCK = 64
SEL_TOP_N = 16
WINDOW = 512
ROPE_DIM = HEAD_DIM // 4
ROPE_THETA = 500000.0
SSD_HEADS = 8
SSD_HEAD_DIM = 64
SSD_WIDTH = SSD_HEADS * SSD_HEAD_DIM
SSD_GROUPS = 2
SSD_REP = SSD_HEADS // SSD_GROUPS
SSD_STATE = 128
SSD_CONV = 4
SSD_CHUNK = 128
SSD_CONV_DIM = SSD_WIDTH + 2 * SSD_GROUPS * SSD_STATE
PEER_HEADS = 8
PEER_N_KEYS = 128
PEER_N_EXPERTS = PEER_N_KEYS * PEER_N_KEYS
PEER_KEY_DIM = 256
PEER_HALF = PEER_KEY_DIM // 2
PEER_TOPK = 16

LANES = 128
SUBLANES = 8
VMEM_LIMIT = 48 * 1024 * 1024
NEG_BIG = -1e30
SEL_NEG = -1e9

TOK_TILE = 512
Q_TILE = 2 * SEL_BLOCK
K_TILE = 2 * SEL_BLOCK
S_CHUNK = 4 * K_TILE
PEER_TOK = 512
PEER_EXP = 2048
PEER_FRONT_TOK = 256


def _cparams(sem, flags=None):
    return pltpu.CompilerParams(dimension_semantics=sem, vmem_limit_bytes=VMEM_LIMIT, flags=flags)


def _dot(a, b):
    return jnp.dot(a, b, preferred_element_type=F32)


def _dot_nt(a, b):
    return lax.dot_general(a, b, (((1,), (1,)), ((), ())), preferred_element_type=F32)


def _split3(a):
    a1 = a.astype(BF16)
    r1 = a - a1.astype(F32)
    a2 = r1.astype(BF16)
    a3 = (r1 - a2.astype(F32)).astype(BF16)
    return a1, a2, a3


def _dot_exact_lhs(a, b01):
    a1, a2, a3 = _split3(a)
    return _dot(a1, b01) + _dot(a2, b01) + _dot(a3, b01)


def _dot_nt_exact_lhs(a, b01):
    a1, a2, a3 = _split3(a)
    return _dot_nt(a1, b01) + _dot_nt(a2, b01) + _dot_nt(a3, b01)


def _dot_exact_rhs(a01, b):
    b1, b2, b3 = _split3(b)
    return _dot(a01, b1) + _dot(a01, b2) + _dot(a01, b3)


def _dot_nt_exact_rhs(a01, b):
    b1, b2, b3 = _split3(b)
    return _dot_nt(a01, b1) + _dot_nt(a01, b2) + _dot_nt(a01, b3)


def _softplus(x):
    return jnp.maximum(x, 0.0) + jnp.log(1.0 + jnp.exp(-jnp.abs(x)))


def _sigmoid(x):
    return 1.0 / (1.0 + jnp.exp(-x))


def _gelu_tanh(x):
    return 0.5 * x * (1.0 + jnp.tanh(0.7978845608028654 * (x + 0.044715 * (x * x * x))))


def _gelu_sigmoid(x):
    c1 = -2.0 * 0.7978845608028654 * 1.4426950408889634
    u = x * (c1 + (c1 * 0.044715) * (x * x))
    return x / (1.0 + jnp.exp2(u))


def _rope128(p, cos, sa, sb):
    return p * cos + pltpu.roll(p, LANES - ROPE_DIM // 2, 1) * sa + pltpu.roll(p, ROPE_DIM // 2, 1) * sb


def _in_proj_kernel(seq_tiles, x_ref, g_ref, w_ref, wdt_ref, cos_ref, sa_ref, sb_ref, gdb_ref, dtb_ref,
                    q_ref, kc_ref, vc_ref, ks_ref, vs_ref, kw_ref, vw_ref, gd_ref, dtt_ref, z_ref, xbc_ref):
    tm = x_ref.shape[0]
    x = x_ref[...]
    h = x * lax.rsqrt(jnp.mean(x * x, axis=-1, keepdims=True) + EPS) * g_ref[...]
    hb = h.astype(BF16)
    pos0 = pl.multiple_of((pl.program_id(0) % seq_tiles) * tm, tm)
    cos = cos_ref[pl.ds(pos0, tm), :]
    sa = sa_ref[pl.ds(pos0, tm), :]
    sb = sb_ref[pl.ds(pos0, tm), :]
    lane = lax.broadcasted_iota(jnp.int32, (tm, LANES), 1)
    low = lane < HEAD_DIM

    def proj(c0, width):
        return _dot(hb, w_ref[:, c0:c0 + width])

    def halves(p):
        return p, pltpu.roll(p, HEAD_DIM, 1)

    pq = proj(0, NSA_WIDTH)
    for s in range(NSA_WIDTH // LANES):
        srcs = halves(_rope128(pq[:, s * LANES:(s + 1) * LANES], cos, sa, sb) * (HEAD_DIM ** -0.5))
        for half in range(2):
            head = 2 * s + half
            q_ref[:, head * LANES:(head + 1) * LANES] = jnp.where(low, srcs[half], 0.0).astype(BF16)
    c0 = NSA_WIDTH
    kc_ref[...] = _rope128(proj(c0, LANES), cos, sa, sb).astype(BF16)
    vc_ref[...] = proj(c0 + LANES, LANES).astype(BF16)
    c0 += 2 * LANES
    blk_of_row = (pos0 + lax.broadcasted_iota(jnp.int32, (tm, LANES), 0)) // SEL_BLOCK
    onehot = jnp.where(lane - HEAD_DIM == blk_of_row, 1.0, 0.0)
    for ref, rope, fill in ((ks_ref, True, onehot), (vs_ref, False, 1.0), (kw_ref, True, 0.0), (vw_ref, False, 1.0)):
        p = proj(c0, LANES)
        if rope:
            p = _rope128(p, cos, sa, sb)
        for g, src in enumerate(halves(p)):
            ref[:, g * LANES:(g + 1) * LANES] = jnp.where(low, src, fill).astype(BF16)
        c0 += LANES
    for s in range(2):
        p = proj(c0, LANES) + gdb_ref[:, s * LANES:(s + 1) * LANES]
        act = jnp.where(lane < 3 * NSA_REP, _sigmoid(p),
                        jnp.where((lane >= 16) & (lane < 16 + SSD_REP), _softplus(p), 0.0))
        gd_ref[:, s * LANES:(s + 1) * LANES] = act
        c0 += LANES
    z_ref[...] = proj(c0, SSD_WIDTH)
    c0 += SSD_WIDTH
    xbc_ref[...] = proj(c0, SSD_CONV_DIM)
    dtt_ref[...] = _softplus(_dot_nt(wdt_ref[...], hb) + dtb_ref[:, 0:1])


def _rope_tables(s):
    half = ROPE_DIM // 2
    inv = jnp.power(ROPE_THETA, -jnp.arange(half, dtype=F32) * 2.0 / ROPE_DIM)
    ang = jnp.arange(s).astype(F32)[:, None] * inv[None, :]
    cos, sin = jnp.cos(ang), jnp.sin(ang)
    zeros = jnp.zeros((s, HEAD_DIM - ROPE_DIM), F32)
    cos64 = jnp.concatenate([cos, cos, jnp.ones((s, HEAD_DIM - ROPE_DIM), F32)], axis=1)
    sa64 = jnp.concatenate([-sin, jnp.zeros_like(sin), zeros], axis=1)
    sb64 = jnp.concatenate([jnp.zeros_like(sin), sin, zeros], axis=1)
    return tuple(jnp.concatenate([t, t], axis=1) for t in (cos64, sa64, sb64))


def _in_proj(xt, attn_g, w_in, dt_bias, s):
    t = xt.shape[0]
    tm = min(TOK_TILE, s)
    o_gl = NSA_WIDTH + 6 * KV_WIDTH
    o_z = o_gl + 3 * NSA_HEADS
    o_xbc = o_z + SSD_WIDTH
    o_dt = o_xbc + SSD_CONV_DIM
    gd_cols, gd_bias = [], []
    for g in range(NSA_KV_GROUPS):
        gates = w_in[:, o_gl + 3 * NSA_REP * g:o_gl + 3 * NSA_REP * (g + 1)]
        dts = w_in[:, o_dt + SSD_REP * g:o_dt + SSD_REP * (g + 1)]
        gd_cols += [gates, jnp.zeros((D_MODEL, 16 - 3 * NSA_REP), F32), dts,
                    jnp.zeros((D_MODEL, LANES - 16 - SSD_REP), F32)]
        gd_bias += [jnp.zeros((16,), F32), dt_bias[SSD_REP * g:SSD_REP * (g + 1)],
                    jnp.zeros((LANES - 16 - SSD_REP,), F32)]
    w_main = jnp.concatenate([w_in[:, :o_gl]] + gd_cols + [w_in[:, o_z:o_dt]], axis=1).astype(BF16)
    gdb = jnp.concatenate(gd_bias)[None, :]
    wdt_rows, dtb_rows = [], []
    for g in range(SSD_GROUPS):
        wdt_rows += [w_in[:, o_dt + SSD_REP * g:o_dt + SSD_REP * (g + 1)].T,
                     jnp.zeros((SUBLANES - SSD_REP, D_MODEL), F32)]
        dtb_rows += [dt_bias[SSD_REP * g:SSD_REP * (g + 1)], jnp.zeros((SUBLANES - SSD_REP,), F32)]
    wdt = jnp.concatenate(wdt_rows, axis=0).astype(BF16)
    dtb = jnp.broadcast_to(jnp.concatenate(dtb_rows)[:, None], (2 * SUBLANES, LANES))
    cos, sa, sb = _rope_tables(s)
    n_main = w_main.shape[1]

    def full(shape):
        return pl.BlockSpec(shape, lambda i: (0, 0))

    def tok(width):
        return pl.BlockSpec((tm, width), lambda i: (i, 0))

    kv_widths = [LANES, LANES] + [NSA_KV_GROUPS * LANES] * 4
    out_shapes = ([jax.ShapeDtypeStruct((t, NSA_HEADS * LANES), BF16)]
                  + [jax.ShapeDtypeStruct((t, w), BF16) for w in kv_widths]
                  + [jax.ShapeDtypeStruct((t, 2 * LANES), F32),
                     jax.ShapeDtypeStruct((2 * SUBLANES, t), F32),
                     jax.ShapeDtypeStruct((t, SSD_WIDTH), F32),
                     jax.ShapeDtypeStruct((t, SSD_CONV_DIM), F32)])
    out_specs = ([tok(NSA_HEADS * LANES)] + [tok(w) for w in kv_widths]
                 + [tok(2 * LANES), pl.BlockSpec((2 * SUBLANES, tm), lambda i: (0, i)),
                    tok(SSD_WIDTH), tok(SSD_CONV_DIM)])
    return pl.pallas_call(
        functools.partial(_in_proj_kernel, s // tm),
        name="in_proj",
        grid=(t // tm,),
        in_specs=[tok(D_MODEL), full((1, D_MODEL)), full((D_MODEL, n_main)), full((2 * SUBLANES, D_MODEL)),
                  full((s, LANES)), full((s, LANES)), full((s, LANES)), full((1, 2 * LANES)),
                  full((2 * SUBLANES, LANES))],
        out_specs=out_specs,
        out_shape=out_shapes,
        compiler_params=_cparams(("parallel",)),
    )(xt, attn_g[None, :], w_main, wdt, cos, sa, sb, gdb, dtb)


def _compress_kernel(kv_ref, w1a_ref, w1b_ref, pos_ref, w1_ref, b1_ref, w2_ref, out_ref):
    kv = kv_ref[0]
    bias = _dot(pos_ref[...], w1_ref[...])[0:1, :] + b1_ref[...]
    for g in range(NSA_KV_GROUPS):
        first = _dot(kv, w1a_ref[g])
        second = _dot(kv, w1b_ref[g])
        nxt = pltpu.roll(second, second.shape[0] - 1, 0)
        hid = _gelu_tanh(first + nxt + bias)
        out_ref[0, :, g * LANES:(g + 1) * LANES] = _dot(hid.astype(BF16), w2_ref[...]).astype(BF16)


def _compress(kv, pos_emb, w1, b1, w2, b, s):
    nch = s // CMP_STRIDE
    kvf = kv.reshape(b, nch, CMP_STRIDE * LANES)
    w1r = w1.reshape(CMP_BLOCK, HEAD_DIM, CMP_HIDDEN)
    zeros = jnp.zeros((CMP_STRIDE, HEAD_DIM, CMP_HIDDEN), F32)

    def expand(w_half, g):
        parts = [w_half, zeros] if g == 0 else [zeros, w_half]
        return jnp.concatenate(parts, axis=1).reshape(CMP_STRIDE * LANES, CMP_HIDDEN)

    w1a = jnp.stack([expand(w1r[:CMP_STRIDE], g) for g in range(NSA_KV_GROUPS)]).astype(BF16)
    w1b = jnp.stack([expand(w1r[CMP_STRIDE:], g) for g in range(NSA_KV_GROUPS)]).astype(BF16)
    w2e = jnp.concatenate([w2, jnp.zeros((CMP_HIDDEN, LANES - HEAD_DIM), F32)], axis=1).astype(BF16)
    pos = jnp.zeros((SUBLANES, CMP_BLOCK * HEAD_DIM), F32).at[0].set(pos_emb.reshape(-1)).astype(BF16)
    return pl.pallas_call(
        _compress_kernel,
        name="compress",
        grid=(b,),
        in_specs=[pl.BlockSpec((1, nch, CMP_STRIDE * LANES), lambda i: (i, 0, 0)),
                  pl.BlockSpec((2, CMP_STRIDE * LANES, CMP_HIDDEN), lambda i: (0, 0, 0)),
                  pl.BlockSpec((2, CMP_STRIDE * LANES, CMP_HIDDEN), lambda i: (0, 0, 0)),
                  pl.BlockSpec((SUBLANES, CMP_BLOCK * HEAD_DIM), lambda i: (0, 0)),
                  pl.BlockSpec((CMP_BLOCK * HEAD_DIM, CMP_HIDDEN), lambda i: (0, 0)),
                  pl.BlockSpec((1, CMP_HIDDEN), lambda i: (0, 0)),
                  pl.BlockSpec((CMP_HIDDEN, LANES), lambda i: (0, 0))],
        out_specs=pl.BlockSpec((1, nch, NSA_KV_GROUPS * LANES), lambda i: (i, 0, 0)),
        out_shape=jax.ShapeDtypeStruct((b, nch, NSA_KV_GROUPS * LANES), BF16),
        compiler_params=_cparams(("parallel",)),
    )(kvf, w1a, w1b, pos, w1.astype(BF16), b1[None, :], w2e)


def _lane_tiles(a):
    return [a[:, k * LANES:(k + 1) * LANES] for k in range(a.shape[1] // LANES)]


def _max_tiles(macc, s):
    for tile in _lane_tiles(s):
        macc = jnp.maximum(macc, tile)
    return macc


def _exp_tiles(s, m_b):
    return jnp.concatenate([jnp.exp(tile - m_b) for tile in _lane_tiles(s)], axis=1).astype(BF16)


def _normalize(acc):
    return acc / pltpu.roll(acc, HEAD_DIM, 1)


def _block_rank(imp):
    nblk, tq = imp.shape
    sub = lax.broadcasted_iota(jnp.int32, (SUBLANES, tq), 0)
    groups = [imp[SUBLANES * v:SUBLANES * (v + 1)] for v in range(nblk // SUBLANES)]
    cnt = [jnp.zeros((SUBLANES, tq), F32) for _ in groups]
    for j in range(nblk):
        row = imp[j:j + 1, :]
        for v, grp in enumerate(groups):
            if SUBLANES * v > j:
                ahead = row >= grp
            elif SUBLANES * (v + 1) <= j:
                ahead = row > grp
            else:
                ahead = (row > grp) | ((row == grp) & (sub > j - SUBLANES * v))
            cnt[v] = cnt[v] + jnp.where(ahead, 1.0, 0.0)
    return jnp.concatenate(cnt, axis=0)


def _nsa_pair_kernel(q_ref, kc_ref, vc_ref, ks_ref, vs_ref, kw_ref, vw_ref, gd_ref, ovl_ref, o_ref, s_ref):
    i = pl.program_id(1)
    tq = Q_TILE
    rows = NSA_REP * tq
    ncmp = kc_ref.shape[1]
    nblk = ovl_ref.shape[0]
    groups = range(NSA_KV_GROUPS)
    t_row = i * tq + lax.broadcasted_iota(jnp.int32, (rows, 1), 0) % tq

    def lanes(g):
        return slice(g * LANES, (g + 1) * LANES)

    qs = [jnp.concatenate([q_ref[:, (g * NSA_REP + r) * LANES:(g * NSA_REP + r + 1) * LANES]
                           for r in range(NSA_REP)], axis=0) for g in groups]

    cmp_end = lax.broadcasted_iota(jnp.int32, (rows, ncmp), 1) * CMP_STRIDE + (CMP_BLOCK - 1)
    mask_c = cmp_end <= t_row
    p_cs, o_cmps = [], []
    for g in groups:
        s_c = jnp.where(mask_c, _dot_nt(qs[g], kc_ref[0, :, lanes(g)]), NEG_BIG)
        e_c = jnp.exp(s_c - jnp.max(s_c, axis=-1, keepdims=True))
        p_c = jnp.where(mask_c, e_c / jnp.sum(e_c, axis=-1, keepdims=True), 0.0)
        p_cs.append(p_c)
        o_cmps.append(_dot(p_c.astype(BF16), vc_ref[0, :, lanes(g)]))

    blk = lax.broadcasted_iota(jnp.int32, (nblk, tq), 0)
    cur = (i * tq + lax.broadcasted_iota(jnp.int32, (nblk, tq), 1)) // SEL_BLOCK
    forced = (blk == 0) | (blk == cur) | (blk == cur - 1)
    valid = blk <= cur
    eye = jnp.where(lax.broadcasted_iota(jnp.int32, (tq, tq), 0) == lax.broadcasted_iota(jnp.int32, (tq, tq), 1),
                    1.0, 0.0).astype(BF16)
    ones_lo = jnp.ones((HEAD_DIM, tq), F32)
    q2s = []
    for g in groups:
        p_sum = p_cs[g][0:tq]
        for r in range(1, NSA_REP):
            p_sum = p_sum + p_cs[g][r * tq:(r + 1) * tq]
        imp = _dot_nt_exact_rhs(ovl_ref[...], p_sum)
        imp = jnp.where(forced, 1e9, jnp.where(valid, imp, -1.0))
        sel_t = jnp.where((_block_rank(imp) < float(min(SEL_TOP_N, nblk))) & valid, 1.0, 0.0)
        pad = [jnp.ones((HEAD_DIM - nblk, tq), F32)] if nblk < HEAD_DIM else []
        sel_pad = jnp.concatenate([ones_lo, sel_t] + pad, axis=0).astype(BF16)
        negb = ((_dot_nt(eye, sel_pad) - 1.0) * (-SEL_NEG)).astype(BF16)
        q2s.append(qs[g] + jnp.concatenate([negb] * NSA_REP, axis=0))

    jd = (i * tq) // K_TILE
    nfull = jd // (S_CHUNK // K_TILE)
    lane_c = lax.broadcasted_iota(jnp.int32, (rows, S_CHUNK), 1)

    def scores(g, c):
        k0 = pl.multiple_of(c * S_CHUNK, S_CHUNK)
        return _dot_nt(q2s[g], ks_ref[pl.ds(k0, S_CHUNK), lanes(g)])

    def pass_max(c, maccs):
        out = []
        for g in groups:
            s = scores(g, c)
            s_ref[g, c] = s
            out.append(_max_tiles(maccs[g], s))
        return tuple(out)

    neg = jnp.full((rows, LANES), NEG_BIG, F32)
    maccs = lax.fori_loop(0, nfull, pass_max, (neg,) * NSA_KV_GROUPS)
    causal = nfull * S_CHUNK + lane_c <= t_row
    s_lasts = [jnp.where(causal, scores(g, nfull), NEG_BIG) for g in groups]
    m_bs = [jnp.broadcast_to(jnp.max(_max_tiles(maccs[g], s_lasts[g]), axis=-1, keepdims=True), (rows, LANES))
            for g in groups]

    def pass_sum(c, accs):
        k0 = pl.multiple_of(c * S_CHUNK, S_CHUNK)
        return tuple(accs[g] + _dot(_exp_tiles(s_ref[g, c], m_bs[g]), vs_ref[pl.ds(k0, S_CHUNK), lanes(g)])
                     for g in groups)

    zero = jnp.zeros((rows, LANES), F32)
    accs = lax.fori_loop(0, nfull, pass_sum, (zero,) * NSA_KV_GROUPS)
    k_last = pl.multiple_of(nfull * S_CHUNK, S_CHUNK)
    o_sels = [_normalize(accs[g] + _dot(_exp_tiles(s_lasts[g], m_bs[g]), vs_ref[pl.ds(k_last, S_CHUNK), lanes(g)]))
              for g in groups]

    n_wk = WINDOW + K_TILE
    k0 = pl.multiple_of(jnp.maximum(jd - WINDOW // K_TILE, 0) * K_TILE, K_TILE)
    diff = t_row - (k0 + lax.broadcasted_iota(jnp.int32, (rows, n_wk), 1))
    in_win = (diff >= 0) & (diff < WINDOW)
    o_wins = []
    for g in groups:
        s_w = jnp.where(in_win, _dot_nt(q2s[g], kw_ref[pl.ds(k0, n_wk), lanes(g)]), NEG_BIG)
        m_w = jnp.max(_max_tiles(neg, s_w), axis=-1, keepdims=True)
        p_w = _exp_tiles(s_w, jnp.broadcast_to(m_w, (rows, LANES)))
        o_wins.append(_normalize(_dot(p_w, vw_ref[pl.ds(k0, n_wk), lanes(g)])))

    low = lax.broadcasted_iota(jnp.int32, (tq, LANES), 1) < HEAD_DIM
    for g in groups:
        gd = gd_ref[:, lanes(g)]
        heads = []
        for r in range(NSA_REP):
            rs = slice(r * tq, (r + 1) * tq)
            heads.append(gd[:, 3 * r:3 * r + 1] * o_cmps[g][rs] + gd[:, 3 * r + 1:3 * r + 2] * o_sels[g][rs]
                         + gd[:, 3 * r + 2:3 * r + 3] * o_wins[g][rs])
        for sidx in range(NSA_REP // 2):
            slab = g * (NSA_REP // 2) + sidx
            o_ref[:, slab * LANES:(slab + 1) * LANES] = jnp.where(
                low, heads[2 * sidx], pltpu.roll(heads[2 * sidx + 1], HEAD_DIM, 1))


def _nsa(q, kcmp, vcmp, ks, vs, kw, vw, gd, b, s):
    t = q.shape[0]
    nq = s // Q_TILE
    ncmp = s // CMP_STRIDE
    nblk = s // SEL_BLOCK
    cs = np.arange(ncmp) * CMP_STRIDE
    ss = np.arange(nblk) * SEL_BLOCK
    overlap = (cs[:, None] < ss[None, :] + SEL_BLOCK) & (cs[:, None] + CMP_BLOCK > ss[None, :])
    overlap[ncmp - 1, :] = False
    ovl_t = jnp.asarray(overlap.T, BF16)

    assert nblk <= HEAD_DIM and s % S_CHUNK == 0 and s >= WINDOW + K_TILE

    ng = NSA_KV_GROUPS

    def seq_spec():
        return pl.BlockSpec((s, ng * LANES), lambda bi, i: (bi, 0))

    return pl.pallas_call(
        _nsa_pair_kernel,
        name="nsa",
        grid=(b, nq),
        in_specs=[pl.BlockSpec((Q_TILE, NSA_HEADS * LANES), lambda bi, i: (bi * nq + i, 0)),
                  pl.BlockSpec((1, ncmp, ng * LANES), lambda bi, i: (bi, 0, 0)),
                  pl.BlockSpec((1, ncmp, ng * LANES), lambda bi, i: (bi, 0, 0)),
                  seq_spec(), seq_spec(), seq_spec(), seq_spec(),
                  pl.BlockSpec((Q_TILE, ng * LANES), lambda bi, i: (bi * nq + i, 0)),
                  pl.BlockSpec((nblk, ncmp), lambda bi, i: (0, 0))],
        out_specs=pl.BlockSpec((Q_TILE, NSA_WIDTH), lambda bi, i: (bi * nq + i, 0)),
        out_shape=jax.ShapeDtypeStruct((t, NSA_WIDTH), F32),
        scratch_shapes=[pltpu.VMEM((ng, s // S_CHUNK, NSA_REP * Q_TILE, S_CHUNK), F32)],
        compiler_params=_cparams(("parallel", "arbitrary")),
    )(q, kcmp, vcmp, ks, vs, kw, vw, gd, ovl_t)


def _conv_silu(cur_ref, ext_ref, tail_ref, w_ref, b_ref):
    l = cur_ref.shape[0]
    cur = cur_ref[...]
    ext_ref[0:SUBLANES, :] = tail_ref[...]
    ext_ref[SUBLANES:, :] = cur
    tail_ref[...] = cur[l - SUBLANES:, :]
    acc = jnp.zeros(cur.shape, F32) + b_ref[...]
    for k in range(SSD_CONV):
        off = SUBLANES - (SSD_CONV - 1) + k
        acc = acc + ext_ref[off:off + l, :] * w_ref[k:k + 1, :]
    return acc * _sigmoid(acc)


def _ssd_kernel(xbc_ref, w_ref, b_ref, z_ref, gd_ref, dtt_ref, alane_ref, asub_ref, dskip_ref, ng_ref,
                tril_ref, edt_ref, eseg_ref, o_ref, tail_ref, ext_ref, st_ref):
    c = pl.program_id(1)
    l = xbc_ref.shape[0]
    gw = SSD_WIDTH // SSD_GROUPS
    b_col0 = SSD_WIDTH
    c_col0 = SSD_WIDTH + SSD_GROUPS * SSD_STATE

    @pl.when(c == 0)
    def _():
        tail_ref[...] = jnp.zeros(tail_ref.shape, F32)
        st_ref[...] = jnp.zeros(st_ref.shape, F32)

    conv = _conv_silu(xbc_ref, ext_ref, tail_ref, w_ref, b_ref)
    tril = tril_ref[...]
    causal = lax.broadcasted_iota(jnp.int32, (l, l), 0) >= lax.broadcasted_iota(jnp.int32, (l, l), 1)
    head_of_lane = lax.broadcasted_iota(jnp.int32, (l, gw), 1) // SSD_HEAD_DIM
    for g in range(SSD_GROUPS):
        xs = conv[:, g * gw:(g + 1) * gw]
        bm = conv[:, b_col0 + g * SSD_STATE:b_col0 + (g + 1) * SSD_STATE]
        cm = conv[:, c_col0 + g * SSD_STATE:c_col0 + (g + 1) * SSD_STATE]
        gd = gd_ref[:, g * LANES:(g + 1) * LANES]
        adt_c = gd * (-jnp.exp(alane_ref[g]))
        acum_c = _dot_exact_rhs(tril, adt_c)
        dt_full = _dot_exact_lhs(gd, edt_ref[...])
        ac_full = _dot_exact_lhs(acum_c, edt_ref[...])
        ac_seg = _dot_exact_lhs(acum_c, eseg_ref[...])
        adt_t = dtt_ref[g * SUBLANES:(g + 1) * SUBLANES, :] * (-jnp.exp(asub_ref[g]))
        acum_t = _dot_nt_exact_lhs(adt_t, tril)

        xdt = xs * dt_full
        cmb = cm.astype(BF16)
        cb = _dot_nt(cmb, bm.astype(BF16))
        y = jnp.zeros(xs.shape, F32)
        for r in range(SSD_REP):
            seg = jnp.exp(jnp.where(causal, ac_seg[:, r * l:(r + 1) * l] - acum_t[r:r + 1, :], NEG_BIG))
            x_r = jnp.where(head_of_lane == r, xdt, 0.0).astype(BF16)
            y = y + _dot((cb * seg).astype(BF16), x_r)
        ac_last = ac_full[l - 1:l, :]
        state = st_ref[g]
        y = y + _dot(cmb, state.astype(BF16)) * jnp.exp(ac_full)
        decayed = (xdt * jnp.exp(ac_last - ac_full)).astype(BF16)
        st_ref[g] = state * jnp.exp(ac_last) + _dot(bm.T.astype(BF16), decayed)
        y = y + xs * dskip_ref[:, g * gw:(g + 1) * gw]
        z = z_ref[:, g * gw:(g + 1) * gw]
        yz = y * (z * _sigmoid(z))
        o_ref[:, g * gw:(g + 1) * gw] = (yz * lax.rsqrt(jnp.mean(yz * yz, axis=-1, keepdims=True) + EPS)
                                        * ng_ref[:, g * gw:(g + 1) * gw])


def _ssd(xbc, z, gd, dtt, conv_w, conv_b, a_log, d_skip, norm_g, b, s):
    t = xbc.shape[0]
    l = SSD_CHUNK
    nc = s // l
    gw = SSD_WIDTH // SSD_GROUPS
    xcols = gw // LANES
    b_blk0 = SSD_WIDTH // LANES
    c_blk0 = b_blk0 + SSD_GROUPS * SSD_STATE // LANES
    conv_b2 = conv_b[None, :]
    a_grp = a_log.reshape(SSD_GROUPS, SSD_REP)
    alane = jnp.zeros((SSD_GROUPS, 1, LANES), F32).at[:, 0, 16:16 + SSD_REP].set(a_grp)
    asub = jnp.zeros((SSD_GROUPS, SUBLANES, LANES), F32).at[:, :SSD_REP, :].set(
        jnp.broadcast_to(a_grp[:, :, None], (SSD_GROUPS, SSD_REP, LANES)))
    dskip = jnp.repeat(d_skip, SSD_HEAD_DIM)[None, :]
    ng = norm_g[None, :]
    ii = np.arange(l)
    tril = jnp.asarray(ii[:, None] >= ii[None, :], BF16)
    edt = np.zeros((LANES, gw), np.float32)
    eseg = np.zeros((LANES, SSD_REP * l), np.float32)
    for r in range(SSD_REP):
        edt[16 + r, r * SSD_HEAD_DIM:(r + 1) * SSD_HEAD_DIM] = 1.0
        eseg[16 + r, r * l:(r + 1) * l] = 1.0
    edt, eseg = jnp.asarray(edt, BF16), jnp.asarray(eseg, BF16)

    def rowblk(width):
        return pl.BlockSpec((l, width), lambda bi, c: (bi * nc + c, 0))

    def const(shape):
        return pl.BlockSpec(shape, lambda bi, c: (0,) * len(shape))

    return pl.pallas_call(
        _ssd_kernel,
        name="ssd",
        grid=(b, nc),
        in_specs=[rowblk(SSD_CONV_DIM), const((SSD_CONV, SSD_CONV_DIM)), const((1, SSD_CONV_DIM)),
                  rowblk(SSD_WIDTH), rowblk(SSD_GROUPS * LANES),
                  pl.BlockSpec((SSD_GROUPS * SUBLANES, l), lambda bi, c: (0, bi * nc + c)),
                  const((SSD_GROUPS, 1, LANES)), const((SSD_GROUPS, SUBLANES, LANES)),
                  const((1, SSD_WIDTH)), const((1, SSD_WIDTH)),
                  const((l, l)), const((LANES, gw)), const((LANES, SSD_REP * l))],
        out_specs=rowblk(SSD_WIDTH),
        out_shape=jax.ShapeDtypeStruct((t, SSD_WIDTH), F32),
        scratch_shapes=[pltpu.VMEM((SUBLANES, SSD_CONV_DIM), F32), pltpu.VMEM((l + SUBLANES, SSD_CONV_DIM), F32),
                        pltpu.VMEM((SSD_GROUPS, SSD_STATE, gw), F32)],
        compiler_params=_cparams(("parallel", "arbitrary")),
    )(xbc, conv_w, conv_b2, z, gd, dtt, alane, asub, dskip, ng, tril, edt, eseg)


def _rms(v, g):
    return v * lax.rsqrt(jnp.mean(v * v, axis=-1, keepdims=True) + EPS) * g


def _out_proj_kernel(x_ref, on_ref, os_ref, ng_ref, w1_ref, w2_ref, fg_ref, x1_ref, xnt_ref):
    onn = _rms(on_ref[...], ng_ref[...]).astype(BF16)
    x1 = x_ref[...] + _dot(onn, w1_ref[...]) + _dot(os_ref[...].astype(BF16), w2_ref[...])
    x1_ref[...] = x1
    xnt_ref[...] = _rms(x1, fg_ref[...]).T.astype(BF16)


def _out_proj(xt, o_nsa, o_ssd, nsa_g, w_out, ffn_g):
    t = xt.shape[0]
    tm = min(TOK_TILE, t)

    def tok(width):
        return pl.BlockSpec((tm, width), lambda i: (i, 0))

    def full(shape):
        return pl.BlockSpec(shape, lambda i: (0, 0))

    wb = w_out.astype(BF16)
    return pl.pallas_call(
        _out_proj_kernel,
        name="out_proj",
        grid=(t // tm,),
        in_specs=[tok(D_MODEL), tok(NSA_WIDTH), tok(SSD_WIDTH), full((1, NSA_WIDTH)),
                  full((NSA_WIDTH, D_MODEL)), full((SSD_WIDTH, D_MODEL)), full((1, D_MODEL))],
        out_specs=[tok(D_MODEL), pl.BlockSpec((D_MODEL, tm), lambda i: (0, i))],
        out_shape=[jax.ShapeDtypeStruct((t, D_MODEL), F32), jax.ShapeDtypeStruct((D_MODEL, t), BF16)],
        compiler_params=_cparams(("parallel",)),
    )(xt, o_nsa, o_ssd, nsa_g[None, :], wb[:NSA_WIDTH], wb[NSA_WIDTH:], ffn_g[None, :])


def _top_sorted(e, k):
    rows = lax.broadcasted_iota(jnp.int32, (k, e.shape[1]), 0)
    out = jnp.zeros((k, e.shape[1]), F32)
    cur = e
    for j in range(k):
        mk = jnp.max(cur, axis=0, keepdims=True)
        out = jnp.where(rows == j, jnp.maximum(mk, 0.0), out)
        target = jnp.where(mk >= 0.0, mk, 2.0)
        cur = jnp.where(cur == target, -float(j + 1), cur)
    rank = jnp.where(cur < 0.0, -1.0 - cur, float(k))
    return out, rank


def _pair_candidates(a16, b16):
    row8 = lax.broadcasted_iota(jnp.int32, (SUBLANES, a16.shape[1]), 0)
    groups = [a16[0:1] * b16[0:SUBLANES], a16[0:1] * b16[SUBLANES:], a16[1:2] * b16[0:SUBLANES]]
    for a in range(2, SUBLANES):
        groups.append(jnp.where(row8 < PEER_TOPK // (a + 1), a16[a:a + 1] * b16[0:SUBLANES], -1.0))
    groups.append(a16[SUBLANES:] * b16[0:1])
    return jnp.concatenate(groups, axis=0)


def _peer_front_kernel(xnt_ref, wq_ref, keys_ref, av_ref, lc_ref, bv_ref, rb_ref):
    xt = xnt_ref[...]
    tm = xt.shape[1]
    row8 = lax.broadcasted_iota(jnp.int32, (SUBLANES, tm), 0)
    for h in range(PEER_HEADS):
        qh = _dot(wq_ref[h * PEER_KEY_DIM:(h + 1) * PEER_KEY_DIM, :], xt)
        s1 = _dot(keys_ref[h, 0], qh[:PEER_HALF].astype(BF16))
        s2 = _dot(keys_ref[h, 1], qh[PEER_HALF:].astype(BF16))
        e1 = jnp.exp(s1 - jnp.max(s1, axis=0, keepdims=True))
        e2 = jnp.exp(s2 - jnp.max(s2, axis=0, keepdims=True))
        a16, ra = _top_sorted(e1, PEER_TOPK)
        b16, rb = _top_sorted(e2, PEER_TOPK)
        cand = _pair_candidates(a16, b16)
        cur = cand
        zsum = jnp.zeros((1, tm), F32)
        for _ in range(PEER_TOPK):
            tau = jnp.max(cur, axis=0, keepdims=True)
            zsum = zsum + tau
            cur = jnp.where(cur == tau, -1.0, cur)
        picked = jnp.where(cand >= tau, 1.0, 0.0)

        def count(g):
            return jnp.sum(picked[g * SUBLANES:(g + 1) * SUBLANES], axis=0, keepdims=True)

        lens = [count(0) + count(1)] + [count(a + 1) for a in range(1, SUBLANES)]
        len_lo = jnp.zeros((SUBLANES, tm), F32)
        for a, row in enumerate(lens):
            len_lo = jnp.where(row8 == a, row, len_lo)
        len16 = jnp.concatenate([len_lo, picked[(SUBLANES + 1) * SUBLANES:]], axis=0)
        pack = 2 * SUBLANES
        ra_b = ra.astype(BF16)
        lc = jnp.zeros(ra.shape, BF16)
        for a in range(PEER_TOPK):
            len_a = jnp.tile(jnp.broadcast_to(len16[a:a + 1], (pack, tm)).astype(BF16), (PEER_N_KEYS // pack, 1))
            lc = jnp.where(ra_b == float(a), len_a, lc)
        rows = slice(h * PEER_N_KEYS, (h + 1) * PEER_N_KEYS)
        av_ref[rows, :] = e1
        lc_ref[rows, :] = lc.astype(F32)
        bv_ref[rows, :] = (e2 * (1.0 / zsum)).astype(BF16)
        rb_ref[rows, :] = rb.astype(BF16)


def _peer_front(xnt, w_q, keys):
    t = xnt.shape[1]
    tm = min(PEER_FRONT_TOK, t)
    nk = PEER_HEADS * PEER_N_KEYS
    return pl.pallas_call(
        _peer_front_kernel,
        name="peer_front",
        grid=(t // tm,),
        in_specs=[pl.BlockSpec((D_MODEL, tm), lambda i: (0, i)),
                  pl.BlockSpec((PEER_HEADS * PEER_KEY_DIM, D_MODEL), lambda i: (0, 0)),
                  pl.BlockSpec((PEER_HEADS, 2, PEER_N_KEYS, PEER_HALF), lambda i: (0, 0, 0, 0))],
        out_specs=[pl.BlockSpec((nk, tm), lambda i: (0, i))] * 4,
        out_shape=[jax.ShapeDtypeStruct((nk, t), F32), jax.ShapeDtypeStruct((nk, t), F32),
                   jax.ShapeDtypeStruct((nk, t), BF16), jax.ShapeDtypeStruct((nk, t), BF16)],
        compiler_params=_cparams(("parallel",)),
    )(xnt, w_q.T.astype(BF16), keys.astype(BF16))


def _row_bf16(ref, row, n_rows):
    pack = 2 * SUBLANES
    tile = jnp.broadcast_to(ref[pl.ds(row, 1), :], (pack, ref.shape[1])).astype(BF16)
    return jnp.tile(tile, (n_rows // pack, 1))


def _peer_dense_kernel(xnt_ref, u_ref, vt_ref, av_ref, lc_ref, bv_ref, rb_ref, o_ref, hid_ref, act_ref):
    j = pl.program_id(1)
    n_sub = u_ref.shape[0] // PEER_N_KEYS

    @pl.when(j == 0)
    def _():
        o_ref[...] = jnp.zeros(o_ref.shape, F32)

    def gate_of(cl):
        e1 = j * n_sub + cl
        gate = jnp.zeros((PEER_N_KEYS, xnt_ref.shape[1]), BF16)
        for h in range(PEER_HEADS):
            rows = slice(h * PEER_N_KEYS, (h + 1) * PEER_N_KEYS)
            lc_row = _row_bf16(lc_ref, h * PEER_N_KEYS + e1, PEER_N_KEYS)
            av_row = _row_bf16(av_ref, h * PEER_N_KEYS + e1, PEER_N_KEYS)
            bv = bv_ref[rows, :]
            gate = gate + jnp.where(rb_ref[rows, :] < lc_row, bv, jnp.zeros_like(bv)) * av_row
        return gate

    piece = 2 * PEER_N_KEYS
    pack = 2 * SUBLANES
    for p in range(u_ref.shape[0] // piece):
        g0, g1 = gate_of(2 * p), gate_of(2 * p + 1)
        zero = jnp.tile(g0[0:pack, 0:LANES] * 0.0, (piece // pack, u_ref.shape[1] // LANES))
        hid = _dot(u_ref[p * piece:(p + 1) * piece, :] + zero, xnt_ref[...])
        act_ref[p * piece:p * piece + PEER_N_KEYS, :] = _gelu_sigmoid(hid[:PEER_N_KEYS].astype(BF16)) * g0
        act_ref[p * piece + PEER_N_KEYS:(p + 1) * piece, :] = _gelu_sigmoid(hid[PEER_N_KEYS:].astype(BF16)) * g1
    o_ref[...] += _dot(vt_ref[...], act_ref[...])


def _peer_dense(xnt, u, vt, av, lc, bv, rb):
    t = xnt.shape[1]
    tm = min(PEER_TOK, t)
    ne = u.shape[0]
    te = PEER_EXP
    tok = pl.BlockSpec((PEER_HEADS * PEER_N_KEYS, tm), lambda i, j: (0, i))
    return pl.pallas_call(
        _peer_dense_kernel,
        name="peer_dense",
        grid=(t // tm, ne // te),
        in_specs=[pl.BlockSpec((D_MODEL, tm), lambda i, j: (0, i)),
                  pl.BlockSpec((te, D_MODEL), lambda i, j: (j, 0)),
                  pl.BlockSpec((D_MODEL, te), lambda i, j: (0, j)),
                  tok, tok, tok, tok],
        out_specs=pl.BlockSpec((D_MODEL, tm), lambda i, j: (0, i)),
        out_shape=jax.ShapeDtypeStruct((D_MODEL, t), F32),
        scratch_shapes=[pltpu.VMEM((te, tm), F32), pltpu.VMEM((te, tm), BF16)],
        compiler_params=_cparams(("parallel", "arbitrary")),
    )(xnt, u, vt, av, lc, bv, rb)


def _final_kernel(x1_ref, pt_ref, g_ref, o_ref):
    o_ref[...] = _rms(x1_ref[...] + pt_ref[...].T, g_ref[...])


def _residual_norm(x1, peer_t, g):
    t = x1.shape[0]
    tm = min(TOK_TILE, t)
    return pl.pallas_call(
        _final_kernel,
        name="residual_norm",
        grid=(t // tm,),
        in_specs=[pl.BlockSpec((tm, D_MODEL), lambda i: (i, 0)), pl.BlockSpec((D_MODEL, tm), lambda i: (0, i)),
                  pl.BlockSpec((1, D_MODEL), lambda i: (0, 0))],
        out_specs=pl.BlockSpec((tm, D_MODEL), lambda i: (i, 0)),
        out_shape=jax.ShapeDtypeStruct((t, D_MODEL), F32),
        compiler_params=_cparams(("parallel",)),
    )(x1, peer_t, g[None, :])


def kernel(x, attn_norm_g, w_in, cmp_pos_k, cmp_w1_k, cmp_b1_k, cmp_w2_k, cmp_pos_v, cmp_w1_v, cmp_b1_v, cmp_w2_v,
           conv_w, conv_b, dt_bias, a_log, d_skip, ssd_norm_g, nsa_norm_g, w_out, ffn_norm_g, peer_w_q, peer_keys,
           peer_u, peer_v, final_norm_g):
    b, s, d = x.shape
    xt = x.reshape(b * s, d)
    assert attn_norm_g.shape[0] == 1, "single-layer block"
    for layer in range(1):
        q, kc, vc, ks, vs, kw, vw, gd, dtt, z, xbc = _in_proj(xt, attn_norm_g[layer], w_in[layer], dt_bias[layer], s)
        kcmp = _compress(kc, cmp_pos_k[layer], cmp_w1_k[layer], cmp_b1_k[layer], cmp_w2_k[layer], b, s)
        vcmp = _compress(vc, cmp_pos_v[layer], cmp_w1_v[layer], cmp_b1_v[layer], cmp_w2_v[layer], b, s)
        o_nsa = _nsa(q, kcmp, vcmp, ks, vs, kw, vw, gd, b, s)
        o_ssd = _ssd(xbc, z, gd, dtt, conv_w[layer], conv_b[layer], a_log[layer], d_skip[layer],
                     ssd_norm_g[layer], b, s)
        x1, xnt = _out_proj(xt, o_nsa, o_ssd, nsa_norm_g[layer], w_out[layer], ffn_norm_g[layer])
        av, lc, bv, rb = _peer_front(xnt, peer_w_q[layer], peer_keys[layer])
        peer_t = _peer_dense(xnt, peer_u[layer].astype(BF16), peer_v[layer].T.astype(BF16), av, lc, bv, rb)
    return _residual_norm(x1, peer_t, final_norm_g).reshape(b, s, d)
```

```python
import functools

import numpy as np
import jax
import jax.numpy as jnp
from jax import lax
from jax.experimental import pallas as pl
from jax.experimental.pallas import tpu as pltpu

F32 = jnp.float32
BF16 = jnp.bfloat16

EPS = 1e-6
D_MODEL = 1024
NSA_HEADS = 8
NSA_KV_GROUPS = 2
NSA_REP = NSA_HEADS // NSA_KV_GROUPS
HEAD_DIM = 64
NSA_WIDTH = NSA_HEADS * HEAD_DIM
KV_WIDTH = NSA_KV_GROUPS * HEAD_DIM
CMP_STRIDE = 16
CMP_BLOCK = 32
CMP_HIDDEN = 256
SEL_BLOCK = 64
SEL_TOP_N = 16
WINDOW = 512
ROPE_DIM = HEAD_DIM // 4
ROPE_THETA = 500000.0
SSD_HEADS = 8
SSD_HEAD_DIM = 64
SSD_WIDTH = SSD_HEADS * SSD_HEAD_DIM
SSD_GROUPS = 2
SSD_REP = SSD_HEADS // SSD_GROUPS
SSD_STATE = 128
SSD_CONV = 4
SSD_CHUNK = 128
SSD_CONV_DIM = SSD_WIDTH + 2 * SSD_GROUPS * SSD_STATE
PEER_HEADS = 8
PEER_N_KEYS = 128
PEER_KEY_DIM = 256
PEER_HALF = PEER_KEY_DIM // 2
PEER_TOPK = 16

LANES = 128
SUBLANES = 8
VMEM_LIMIT = 48 * 1024 * 1024
NEG_BIG = -1e30
SEL_NEG = -1e9

TOK_TILE = 512
Q_TILE = 2 * SEL_BLOCK
K_TILE = 2 * SEL_BLOCK
S_CHUNK = 4 * K_TILE
PEER_TOK = 512
PEER_EXP = 2048
PEER_FRONT_TOK = 512


def _cparams(sem):
    return pltpu.CompilerParams(dimension_semantics=sem, vmem_limit_bytes=VMEM_LIMIT)


def _dot(a, b):
    return jnp.dot(a, b, preferred_element_type=F32)


def _dot_nt(a, b):
    return lax.dot_general(a, b, (((1,), (1,)), ((), ())), preferred_element_type=F32)


def _split3(a):
    a1 = a.astype(BF16)
    r1 = a - a1.astype(F32)
    a2 = r1.astype(BF16)
    a3 = (r1 - a2.astype(F32)).astype(BF16)
    return a1, a2, a3


def _dot_exact_lhs(a, b01):
    a1, a2, a3 = _split3(a)
    return _dot(a1, b01) + _dot(a2, b01) + _dot(a3, b01)


def _dot_nt_exact_lhs(a, b01):
    a1, a2, a3 = _split3(a)
    return _dot_nt(a1, b01) + _dot_nt(a2, b01) + _dot_nt(a3, b01)


def _dot_exact_rhs(a01, b):
    b1, b2, b3 = _split3(b)
    return _dot(a01, b1) + _dot(a01, b2) + _dot(a01, b3)


def _dot_nt_exact_rhs(a01, b):
    b1, b2, b3 = _split3(b)
    return _dot_nt(a01, b1) + _dot_nt(a01, b2) + _dot_nt(a01, b3)


def _softplus(x):
    return jnp.maximum(x, 0.0) + jnp.log(1.0 + jnp.exp(-jnp.abs(x)))


def _sigmoid(x):
    return 1.0 / (1.0 + jnp.exp(-x))


def _gelu_tanh(x):
    return 0.5 * x * (1.0 + jnp.tanh(0.7978845608028654 * (x + 0.044715 * (x * x * x))))


def _gelu_sigmoid(x):
    c1 = -2.0 * 0.7978845608028654 * 1.4426950408889634
    u = x * (c1 + (c1 * 0.044715) * (x * x))
    return x / (1.0 + jnp.exp2(u))


def _rope128(p, cos, sa, sb):
    return p * cos + pltpu.roll(p, LANES - ROPE_DIM // 2, 1) * sa + pltpu.roll(p, ROPE_DIM // 2, 1) * sb


def _in_proj_kernel(seq_tiles, x_ref, g_ref, w_ref, wdt_ref, cos_ref, sa_ref, sb_ref, gdb_ref, dtb_ref,
                    q_ref, kc_ref, vc_ref, ks_ref, vs_ref, kw_ref, vw_ref, gd_ref, dtt_ref, z_ref, xbc_ref):
    tm = x_ref.shape[0]
    x = x_ref[...]
    h = x * lax.rsqrt(jnp.mean(x * x, axis=-1, keepdims=True) + EPS) * g_ref[...]
    hb = h.astype(BF16)
    pos0 = pl.multiple_of((pl.program_id(0) % seq_tiles) * tm, tm)
    cos = cos_ref[pl.ds(pos0, tm), :]
    sa = sa_ref[pl.ds(pos0, tm), :]
    sb = sb_ref[pl.ds(pos0, tm), :]
    lane = lax.broadcasted_iota(jnp.int32, (tm, LANES), 1)
    low = lane < HEAD_DIM

    def proj(c0, width):
        return _dot(hb, w_ref[:, c0:c0 + width])

    def halves(p):
        return p, pltpu.roll(p, HEAD_DIM, 1)

    pq = proj(0, NSA_WIDTH)
    for s in range(NSA_WIDTH // LANES):
        srcs = halves(_rope128(pq[:, s * LANES:(s + 1) * LANES], cos, sa, sb) * (HEAD_DIM ** -0.5))
        for half in range(2):
            head = 2 * s + half
            q_ref[:, head * LANES:(head + 1) * LANES] = jnp.where(low, srcs[half], 0.0).astype(BF16)
    c0 = NSA_WIDTH
    kc_ref[...] = _rope128(proj(c0, LANES), cos, sa, sb).astype(BF16)
    vc_ref[...] = proj(c0 + LANES, LANES).astype(BF16)
    c0 += 2 * LANES
    blk_of_row = (pos0 + lax.broadcasted_iota(jnp.int32, (tm, LANES), 0)) // SEL_BLOCK
    onehot = jnp.where(lane - HEAD_DIM == blk_of_row, 1.0, 0.0)
    for ref, rope, fill in ((ks_ref, True, onehot), (vs_ref, False, 1.0), (kw_ref, True, 0.0), (vw_ref, False, 1.0)):
        p = proj(c0, LANES)
        if rope:
            p = _rope128(p, cos, sa, sb)
        for g, src in enumerate(halves(p)):
            ref[:, g * LANES:(g + 1) * LANES] = jnp.where(low, src, fill).astype(BF16)
        c0 += LANES
    for s in range(2):
        p = proj(c0, LANES) + gdb_ref[:, s * LANES:(s + 1) * LANES]
        act = jnp.where(lane < 3 * NSA_REP, _sigmoid(p),
                        jnp.where((lane >= 16) & (lane < 16 + SSD_REP), _softplus(p), 0.0))
        gd_ref[:, s * LANES:(s + 1) * LANES] = act
        c0 += LANES
    z_ref[...] = proj(c0, SSD_WIDTH)
    c0 += SSD_WIDTH
    xbc_ref[...] = proj(c0, SSD_CONV_DIM)
    dtt_ref[...] = _softplus(_dot_nt(wdt_ref[...], hb) + dtb_ref[:, 0:1])


def _rope_tables(s):
    half = ROPE_DIM // 2
    inv = jnp.power(ROPE_THETA, -jnp.arange(half, dtype=F32) * 2.0 / ROPE_DIM)
    ang = jnp.arange(s).astype(F32)[:, None] * inv[None, :]
    cos, sin = jnp.cos(ang), jnp.sin(ang)
    zeros = jnp.zeros((s, HEAD_DIM - ROPE_DIM), F32)
    cos64 = jnp.concatenate([cos, cos, jnp.ones((s, HEAD_DIM - ROPE_DIM), F32)], axis=1)
    sa64 = jnp.concatenate([-sin, jnp.zeros_like(sin), zeros], axis=1)
    sb64 = jnp.concatenate([jnp.zeros_like(sin), sin, zeros], axis=1)
    return tuple(jnp.concatenate([t, t], axis=1) for t in (cos64, sa64, sb64))


def _in_proj(xt, attn_g, w_in, dt_bias, s):
    t = xt.shape[0]
    tm = min(TOK_TILE, s)
    o_gl = NSA_WIDTH + 6 * KV_WIDTH
    o_z = o_gl + 3 * NSA_HEADS
    o_xbc = o_z + SSD_WIDTH
    o_dt = o_xbc + SSD_CONV_DIM
    gd_cols, gd_bias = [], []
    for g in range(NSA_KV_GROUPS):
        gates = w_in[:, o_gl + 3 * NSA_REP * g:o_gl + 3 * NSA_REP * (g + 1)]
        dts = w_in[:, o_dt + SSD_REP * g:o_dt + SSD_REP * (g + 1)]
        gd_cols += [gates, jnp.zeros((D_MODEL, 16 - 3 * NSA_REP), F32), dts,
                    jnp.zeros((D_MODEL, LANES - 16 - SSD_REP), F32)]
        gd_bias += [jnp.zeros((16,), F32), dt_bias[SSD_REP * g:SSD_REP * (g + 1)],
                    jnp.zeros((LANES - 16 - SSD_REP,), F32)]
    w_main = jnp.concatenate([w_in[:, :o_gl]] + gd_cols + [w_in[:, o_z:o_dt]], axis=1).astype(BF16)
    gdb = jnp.concatenate(gd_bias)[None, :]
    wdt_rows, dtb_rows = [], []
    for g in range(SSD_GROUPS):
        wdt_rows += [w_in[:, o_dt + SSD_REP * g:o_dt + SSD_REP * (g + 1)].T,
                     jnp.zeros((SUBLANES - SSD_REP, D_MODEL), F32)]
        dtb_rows += [dt_bias[SSD_REP * g:SSD_REP * (g + 1)], jnp.zeros((SUBLANES - SSD_REP,), F32)]
    wdt = jnp.concatenate(wdt_rows, axis=0).astype(BF16)
    dtb = jnp.broadcast_to(jnp.concatenate(dtb_rows)[:, None], (2 * SUBLANES, LANES))
    cos, sa, sb = _rope_tables(s)
    n_main = w_main.shape[1]

    def full(shape):
        return pl.BlockSpec(shape, lambda i: (0, 0))

    def tok(width):
        return pl.BlockSpec((tm, width), lambda i: (i, 0))

    kv_widths = [LANES, LANES] + [NSA_KV_GROUPS * LANES] * 4
    out_shapes = ([jax.ShapeDtypeStruct((t, NSA_HEADS * LANES), BF16)]
                  + [jax.ShapeDtypeStruct((t, w), BF16) for w in kv_widths]
                  + [jax.ShapeDtypeStruct((t, 2 * LANES), F32),
                     jax.ShapeDtypeStruct((2 * SUBLANES, t), F32),
                     jax.ShapeDtypeStruct((t, SSD_WIDTH), F32),
                     jax.ShapeDtypeStruct((t, SSD_CONV_DIM), F32)])
    out_specs = ([tok(NSA_HEADS * LANES)] + [tok(w) for w in kv_widths]
                 + [tok(2 * LANES), pl.BlockSpec((2 * SUBLANES, tm), lambda i: (0, i)),
                    tok(SSD_WIDTH), tok(SSD_CONV_DIM)])
    return pl.pallas_call(
        functools.partial(_in_proj_kernel, s // tm),
        name="in_proj",
        grid=(t // tm,),
        in_specs=[tok(D_MODEL), full((1, D_MODEL)), full((D_MODEL, n_main)), full((2 * SUBLANES, D_MODEL)),
                  full((s, LANES)), full((s, LANES)), full((s, LANES)), full((1, 2 * LANES)),
                  full((2 * SUBLANES, LANES))],
        out_specs=out_specs,
        out_shape=out_shapes,
        compiler_params=_cparams(("parallel",)),
    )(xt, attn_g[None, :], w_main, wdt, cos, sa, sb, gdb, dtb)


def _compress_kernel(kv_ref, w1a_ref, w1b_ref, pos_ref, w1_ref, b1_ref, w2_ref, out_ref):
    kv = kv_ref[0]
    bias = _dot(pos_ref[...], w1_ref[...])[0:1, :] + b1_ref[...]
    for g in range(NSA_KV_GROUPS):
        first = _dot(kv, w1a_ref[g])
        second = _dot(kv, w1b_ref[g])
        nxt = pltpu.roll(second, second.shape[0] - 1, 0)
        hid = _gelu_tanh(first + nxt + bias)
        out_ref[0, :, g * LANES:(g + 1) * LANES] = _dot(hid.astype(BF16), w2_ref[...]).astype(BF16)


def _compress(kv, pos_emb, w1, b1, w2, b, s):
    nch = s // CMP_STRIDE
    kvf = kv.reshape(b, nch, CMP_STRIDE * LANES)
    w1r = w1.reshape(CMP_BLOCK, HEAD_DIM, CMP_HIDDEN)
    zeros = jnp.zeros((CMP_STRIDE, HEAD_DIM, CMP_HIDDEN), F32)

    def expand(w_half, g):
        parts = [w_half, zeros] if g == 0 else [zeros, w_half]
        return jnp.concatenate(parts, axis=1).reshape(CMP_STRIDE * LANES, CMP_HIDDEN)

    w1a = jnp.stack([expand(w1r[:CMP_STRIDE], g) for g in range(NSA_KV_GROUPS)]).astype(BF16)
    w1b = jnp.stack([expand(w1r[CMP_STRIDE:], g) for g in range(NSA_KV_GROUPS)]).astype(BF16)
    w2e = jnp.concatenate([w2, jnp.zeros((CMP_HIDDEN, LANES - HEAD_DIM), F32)], axis=1).astype(BF16)
    pos = jnp.zeros((SUBLANES, CMP_BLOCK * HEAD_DIM), F32).at[0].set(pos_emb.reshape(-1)).astype(BF16)
    return pl.pallas_call(
        _compress_kernel,
        name="compress",
        grid=(b,),
        in_specs=[pl.BlockSpec((1, nch, CMP_STRIDE * LANES), lambda i: (i, 0, 0)),
                  pl.BlockSpec((2, CMP_STRIDE * LANES, CMP_HIDDEN), lambda i: (0, 0, 0)),
                  pl.BlockSpec((2, CMP_STRIDE * LANES, CMP_HIDDEN), lambda i: (0, 0, 0)),
                  pl.BlockSpec((SUBLANES, CMP_BLOCK * HEAD_DIM), lambda i: (0, 0)),
                  pl.BlockSpec((CMP_BLOCK * HEAD_DIM, CMP_HIDDEN), lambda i: (0, 0)),
                  pl.BlockSpec((1, CMP_HIDDEN), lambda i: (0, 0)),
                  pl.BlockSpec((CMP_HIDDEN, LANES), lambda i: (0, 0))],
        out_specs=pl.BlockSpec((1, nch, NSA_KV_GROUPS * LANES), lambda i: (i, 0, 0)),
        out_shape=jax.ShapeDtypeStruct((b, nch, NSA_KV_GROUPS * LANES), BF16),
        compiler_params=_cparams(("parallel",)),
    )(kvf, w1a, w1b, pos, w1.astype(BF16), b1[None, :], w2e)


def _lane_tiles(a):
    return [a[:, k * LANES:(k + 1) * LANES] for k in range(a.shape[1] // LANES)]


def _max_tiles(macc, s):
    for tile in _lane_tiles(s):
        macc = jnp.maximum(macc, tile)
    return macc


def _exp_tiles(s, m_b):
    return jnp.concatenate([jnp.exp(tile - m_b) for tile in _lane_tiles(s)], axis=1).astype(BF16)


def _normalize(acc):
    return acc / pltpu.roll(acc, HEAD_DIM, 1)


def _block_rank(imp):
    nblk, tq = imp.shape
    sub = lax.broadcasted_iota(jnp.int32, (SUBLANES, tq), 0)
    groups = [imp[SUBLANES * v:SUBLANES * (v + 1)] for v in range(nblk // SUBLANES)]
    cnt = [jnp.zeros((SUBLANES, tq), F32) for _ in groups]
    for j in range(nblk):
        row = imp[j:j + 1, :]
        for v, grp in enumerate(groups):
            if SUBLANES * v > j:
                ahead = row >= grp
            elif SUBLANES * (v + 1) <= j:
                ahead = row > grp
            else:
                ahead = (row > grp) | ((row == grp) & (sub > j - SUBLANES * v))
            cnt[v] = cnt[v] + jnp.where(ahead, 1.0, 0.0)
    return jnp.concatenate(cnt, axis=0)


def _nsa_pair_kernel(q_ref, kc_ref, vc_ref, ks_ref, vs_ref, kw_ref, vw_ref, gd_ref, ovl_ref, o_ref, s_ref):
    i = pl.program_id(1)
    tq = Q_TILE
    rows = NSA_REP * tq
    ncmp = kc_ref.shape[1]
    nblk = ovl_ref.shape[0]
    groups = range(NSA_KV_GROUPS)
    t_row = i * tq + lax.broadcasted_iota(jnp.int32, (rows, 1), 0) % tq

    def lanes(g):
        return slice(g * LANES, (g + 1) * LANES)

    qs = [jnp.concatenate([q_ref[:, (g * NSA_REP + r) * LANES:(g * NSA_REP + r + 1) * LANES]
                           for r in range(NSA_REP)], axis=0) for g in groups]

    jd = (i * tq) // K_TILE
    neg = jnp.full((rows, LANES), NEG_BIG, F32)
    n_wk = WINDOW + K_TILE
    k0 = pl.multiple_of(jnp.maximum(jd - WINDOW // K_TILE, 0) * K_TILE, K_TILE)
    diff = t_row - (k0 + lax.broadcasted_iota(jnp.int32, (rows, n_wk), 1))
    in_win = (diff >= 0) & (diff < WINDOW)
    o_wins = []
    for g in groups:
        s_w = jnp.where(in_win, _dot_nt(qs[g], kw_ref[pl.ds(k0, n_wk), lanes(g)]), NEG_BIG)
        m_w = jnp.max(_max_tiles(neg, s_w), axis=-1, keepdims=True)
        p_w = _exp_tiles(s_w, jnp.broadcast_to(m_w, (rows, LANES)))
        o_wins.append(_normalize(_dot(p_w, vw_ref[pl.ds(k0, n_wk), lanes(g)])))

    cmp_end = lax.broadcasted_iota(jnp.int32, (rows, ncmp), 1) * CMP_STRIDE + (CMP_BLOCK - 1)
    mask_c = cmp_end <= t_row
    p_cs, o_cmps = [], []
    for g in groups:
        s_c = jnp.where(mask_c, _dot_nt(qs[g], kc_ref[0, :, lanes(g)]), NEG_BIG)
        e_c = jnp.exp(s_c - jnp.max(s_c, axis=-1, keepdims=True))
        p_c = jnp.where(mask_c, e_c / jnp.sum(e_c, axis=-1, keepdims=True), 0.0)
        p_cs.append(p_c)
        o_cmps.append(_dot(p_c.astype(BF16), vc_ref[0, :, lanes(g)]))

    blk = lax.broadcasted_iota(jnp.int32, (nblk, tq), 0)
    cur = (i * tq + lax.broadcasted_iota(jnp.int32, (nblk, tq), 1)) // SEL_BLOCK
    forced = (blk == 0) | (blk == cur) | (blk == cur - 1)
    valid = blk <= cur
    eye = jnp.where(lax.broadcasted_iota(jnp.int32, (tq, tq), 0) == lax.broadcasted_iota(jnp.int32, (tq, tq), 1),
                    1.0, 0.0).astype(BF16)
    ones_lo = jnp.ones((HEAD_DIM, tq), F32)
    q2s = []
    for g in groups:
        p_sum = p_cs[g][0:tq]
        for r in range(1, NSA_REP):
            p_sum = p_sum + p_cs[g][r * tq:(r + 1) * tq]
        imp = _dot_nt_exact_rhs(ovl_ref[...], p_sum)
        imp = jnp.where(forced, 1e9, jnp.where(valid, imp, -1.0))
        sel_t = jnp.where((_block_rank(imp) < float(min(SEL_TOP_N, nblk))) & valid, 1.0, 0.0)
        pad = [jnp.ones((HEAD_DIM - nblk, tq), F32)] if nblk < HEAD_DIM else []
        sel_pad = jnp.concatenate([ones_lo, sel_t] + pad, axis=0).astype(BF16)
        negb = ((_dot_nt(eye, sel_pad) - 1.0) * (-SEL_NEG)).astype(BF16)
        q2s.append(qs[g] + jnp.concatenate([negb] * NSA_REP, axis=0))

    nfull = jd // (S_CHUNK // K_TILE)
    lane_c = lax.broadcasted_iota(jnp.int32, (rows, S_CHUNK), 1)

    def scores(g, c):
        k0 = pl.multiple_of(c * S_CHUNK, S_CHUNK)
        return _dot_nt(q2s[g], ks_ref[pl.ds(k0, S_CHUNK), lanes(g)])

    def pass_max(c, maccs):
        out = []
        for g in groups:
            s = scores(g, c)
            s_ref[g, c] = s
            out.append(_max_tiles(maccs[g], s))
        return tuple(out)

    maccs = lax.fori_loop(0, nfull, pass_max, (neg,) * NSA_KV_GROUPS)
    causal = nfull * S_CHUNK + lane_c <= t_row
    s_lasts = [jnp.where(causal, scores(g, nfull), NEG_BIG) for g in groups]
    m_bs = [jnp.broadcast_to(jnp.max(_max_tiles(maccs[g], s_lasts[g]), axis=-1, keepdims=True), (rows, LANES))
            for g in groups]

    def pass_sum(c, accs):
        k0 = pl.multiple_of(c * S_CHUNK, S_CHUNK)
        return tuple(accs[g] + _dot(_exp_tiles(s_ref[g, c], m_bs[g]), vs_ref[pl.ds(k0, S_CHUNK), lanes(g)])
                     for g in groups)

    zero = jnp.zeros((rows, LANES), F32)
    accs = lax.fori_loop(0, nfull, pass_sum, (zero,) * NSA_KV_GROUPS)
    k_last = pl.multiple_of(nfull * S_CHUNK, S_CHUNK)
    o_sels = [_normalize(accs[g] + _dot(_exp_tiles(s_lasts[g], m_bs[g]), vs_ref[pl.ds(k_last, S_CHUNK), lanes(g)]))
              for g in groups]

    low = lax.broadcasted_iota(jnp.int32, (tq, LANES), 1) < HEAD_DIM
    for g in groups:
        gd = gd_ref[:, lanes(g)]
        heads = []
        for r in range(NSA_REP):
            rs = slice(r * tq, (r + 1) * tq)
            heads.append(gd[:, 3 * r:3 * r + 1] * o_cmps[g][rs] + gd[:, 3 * r + 1:3 * r + 2] * o_sels[g][rs]
                         + gd[:, 3 * r + 2:3 * r + 3] * o_wins[g][rs])
        for sidx in range(NSA_REP // 2):
            slab = g * (NSA_REP // 2) + sidx
            o_ref[:, slab * LANES:(slab + 1) * LANES] = jnp.where(
                low, heads[2 * sidx], pltpu.roll(heads[2 * sidx + 1], HEAD_DIM, 1))


def _nsa(q, kcmp, vcmp, ks, vs, kw, vw, gd, b, s):
    t = q.shape[0]
    nq = s // Q_TILE
    ncmp = s // CMP_STRIDE
    nblk = s // SEL_BLOCK
    cs = np.arange(ncmp) * CMP_STRIDE
    ss = np.arange(nblk) * SEL_BLOCK
    overlap = (cs[:, None] < ss[None, :] + SEL_BLOCK) & (cs[:, None] + CMP_BLOCK > ss[None, :])
    overlap[ncmp - 1, :] = False
    ovl_t = jnp.asarray(overlap.T, BF16)

    assert nblk <= HEAD_DIM and s % S_CHUNK == 0 and s >= WINDOW + K_TILE

    ng = NSA_KV_GROUPS

    def seq_spec():
        return pl.BlockSpec((s, ng * LANES), lambda bi, i: (bi, 0))

    return pl.pallas_call(
        _nsa_pair_kernel,
        name="nsa",
        grid=(b, nq),
        in_specs=[pl.BlockSpec((Q_TILE, NSA_HEADS * LANES), lambda bi, i: (bi * nq + i, 0)),
                  pl.BlockSpec((1, ncmp, ng * LANES), lambda bi, i: (bi, 0, 0)),
                  pl.BlockSpec((1, ncmp, ng * LANES), lambda bi, i: (bi, 0, 0)),
                  seq_spec(), seq_spec(), seq_spec(), seq_spec(),
                  pl.BlockSpec((Q_TILE, ng * LANES), lambda bi, i: (bi * nq + i, 0)),
                  pl.BlockSpec((nblk, ncmp), lambda bi, i: (0, 0))],
        out_specs=pl.BlockSpec((Q_TILE, NSA_WIDTH), lambda bi, i: (bi * nq + i, 0)),
        out_shape=jax.ShapeDtypeStruct((t, NSA_WIDTH), F32),
        scratch_shapes=[pltpu.VMEM((ng, s // S_CHUNK, NSA_REP * Q_TILE, S_CHUNK), F32)],
        compiler_params=_cparams(("parallel", "arbitrary")),
    )(q, kcmp, vcmp, ks, vs, kw, vw, gd, ovl_t)


def _conv_silu(cur_ref, ext_ref, tail_ref, w_ref, b_ref):
    l = cur_ref.shape[0]
    cur = cur_ref[...]
    ext_ref[0:SUBLANES, :] = tail_ref[...]
    ext_ref[SUBLANES:, :] = cur
    tail_ref[...] = cur[l - SUBLANES:, :]
    acc = jnp.zeros(cur.shape, F32) + b_ref[...]
    for k in range(SSD_CONV):
        off = SUBLANES - (SSD_CONV - 1) + k
        acc = acc + ext_ref[off:off + l, :] * w_ref[k:k + 1, :]
    return acc * _sigmoid(acc)


def _ssd_kernel(xbc_ref, w_ref, b_ref, z_ref, gd_ref, dtt_ref, alane_ref, asub_ref, dskip_ref, ng_ref,
                tril_ref, edt_ref, eseg_ref, o_ref, tail_ref, ext_ref, st_ref):
    c = pl.program_id(1)
    l = xbc_ref.shape[0]
    gw = SSD_WIDTH // SSD_GROUPS
    b_col0 = SSD_WIDTH
    c_col0 = SSD_WIDTH + SSD_GROUPS * SSD_STATE

    @pl.when(c == 0)
    def _():
        tail_ref[...] = jnp.zeros(tail_ref.shape, F32)
        st_ref[...] = jnp.zeros(st_ref.shape, F32)

    conv = _conv_silu(xbc_ref, ext_ref, tail_ref, w_ref, b_ref)
    tril = tril_ref[...]
    causal = lax.broadcasted_iota(jnp.int32, (l, l), 0) >= lax.broadcasted_iota(jnp.int32, (l, l), 1)
    head_of_lane = lax.broadcasted_iota(jnp.int32, (l, gw), 1) // SSD_HEAD_DIM
    for g in range(SSD_GROUPS):
        xs = conv[:, g * gw:(g + 1) * gw]
        bm = conv[:, b_col0 + g * SSD_STATE:b_col0 + (g + 1) * SSD_STATE]
        cm = conv[:, c_col0 + g * SSD_STATE:c_col0 + (g + 1) * SSD_STATE]
        gd = gd_ref[:, g * LANES:(g + 1) * LANES]
        adt_c = gd * (-jnp.exp(alane_ref[g]))
        acum_c = _dot_exact_rhs(tril, adt_c)
        dt_full = _dot_exact_lhs(gd, edt_ref[...])
        ac_full = _dot_exact_lhs(acum_c, edt_ref[...])
        ac_seg = _dot_exact_lhs(acum_c, eseg_ref[...])
        adt_t = dtt_ref[g * SUBLANES:(g + 1) * SUBLANES, :] * (-jnp.exp(asub_ref[g]))
        acum_t = _dot_nt_exact_lhs(adt_t, tril)

        xdt = xs * dt_full
        cmb = cm.astype(BF16)
        cb = _dot_nt(cmb, bm.astype(BF16))
        y = jnp.zeros(xs.shape, F32)
        for r in range(SSD_REP):
            seg = jnp.exp(jnp.where(causal, ac_seg[:, r * l:(r + 1) * l] - acum_t[r:r + 1, :], NEG_BIG))
            x_r = jnp.where(head_of_lane == r, xdt, 0.0).astype(BF16)
            y = y + _dot((cb * seg).astype(BF16), x_r)
        ac_last = ac_full[l - 1:l, :]
        state = st_ref[g]
        y = y + _dot(cmb, state.astype(BF16)) * jnp.exp(ac_full)
        decayed = (xdt * jnp.exp(ac_last - ac_full)).astype(BF16)
        st_ref[g] = state * jnp.exp(ac_last) + _dot(bm.T.astype(BF16), decayed)
        y = y + xs * dskip_ref[:, g * gw:(g + 1) * gw]
        z = z_ref[:, g * gw:(g + 1) * gw]
        yz = y * (z * _sigmoid(z))
        o_ref[:, g * gw:(g + 1) * gw] = (yz * lax.rsqrt(jnp.mean(yz * yz, axis=-1, keepdims=True) + EPS)
                                        * ng_ref[:, g * gw:(g + 1) * gw])


def _ssd(xbc, z, gd, dtt, conv_w, conv_b, a_log, d_skip, norm_g, b, s):
    t = xbc.shape[0]
    l = SSD_CHUNK
    nc = s // l
    gw = SSD_WIDTH // SSD_GROUPS
    xcols = gw // LANES
    b_blk0 = SSD_WIDTH // LANES
    c_blk0 = b_blk0 + SSD_GROUPS * SSD_STATE // LANES
    conv_b2 = conv_b[None, :]
    a_grp = a_log.reshape(SSD_GROUPS, SSD_REP)
    alane = jnp.zeros((SSD_GROUPS, 1, LANES), F32).at[:, 0, 16:16 + SSD_REP].set(a_grp)
    asub = jnp.zeros((SSD_GROUPS, SUBLANES, LANES), F32).at[:, :SSD_REP, :].set(
        jnp.broadcast_to(a_grp[:, :, None], (SSD_GROUPS, SSD_REP, LANES)))
    dskip = jnp.repeat(d_skip, SSD_HEAD_DIM)[None, :]
    ng = norm_g[None, :]
    ii = np.arange(l)
    tril = jnp.asarray(ii[:, None] >= ii[None, :], BF16)
    edt = np.zeros((LANES, gw), np.float32)
    eseg = np.zeros((LANES, SSD_REP * l), np.float32)
    for r in range(SSD_REP):
        edt[16 + r, r * SSD_HEAD_DIM:(r + 1) * SSD_HEAD_DIM] = 1.0
        eseg[16 + r, r * l:(r + 1) * l] = 1.0
    edt, eseg = jnp.asarray(edt, BF16), jnp.asarray(eseg, BF16)

    def rowblk(width):
        return pl.BlockSpec((l, width), lambda bi, c: (bi * nc + c, 0))

    def const(shape):
        return pl.BlockSpec(shape, lambda bi, c: (0,) * len(shape))

    return pl.pallas_call(
        _ssd_kernel,
        name="ssd",
        grid=(b, nc),
        in_specs=[rowblk(SSD_CONV_DIM), const((SSD_CONV, SSD_CONV_DIM)), const((1, SSD_CONV_DIM)),
                  rowblk(SSD_WIDTH), rowblk(SSD_GROUPS * LANES),
                  pl.BlockSpec((SSD_GROUPS * SUBLANES, l), lambda bi, c: (0, bi * nc + c)),
                  const((SSD_GROUPS, 1, LANES)), const((SSD_GROUPS, SUBLANES, LANES)),
                  const((1, SSD_WIDTH)), const((1, SSD_WIDTH)),
                  const((l, l)), const((LANES, gw)), const((LANES, SSD_REP * l))],
        out_specs=rowblk(SSD_WIDTH),
        out_shape=jax.ShapeDtypeStruct((t, SSD_WIDTH), F32),
        scratch_shapes=[pltpu.VMEM((SUBLANES, SSD_CONV_DIM), F32), pltpu.VMEM((l + SUBLANES, SSD_CONV_DIM), F32),
                        pltpu.VMEM((SSD_GROUPS, SSD_STATE, gw), F32)],
        compiler_params=_cparams(("parallel", "arbitrary")),
    )(xbc, conv_w, conv_b2, z, gd, dtt, alane, asub, dskip, ng, tril, edt, eseg)


def _rms(v, g):
    return v * lax.rsqrt(jnp.mean(v * v, axis=-1, keepdims=True) + EPS) * g


def _out_proj_kernel(x_ref, on_ref, os_ref, ng_ref, w1_ref, w2_ref, fg_ref, x1_ref, xnt_ref):
    onn = _rms(on_ref[...], ng_ref[...]).astype(BF16)
    x1 = x_ref[...] + _dot(onn, w1_ref[...]) + _dot(os_ref[...].astype(BF16), w2_ref[...])
    x1_ref[...] = x1
    xnt_ref[...] = _rms(x1, fg_ref[...]).T.astype(BF16)


def _out_proj(xt, o_nsa, o_ssd, nsa_g, w_out, ffn_g):
    t = xt.shape[0]
    tm = min(TOK_TILE, t)

    def tok(width):
        return pl.BlockSpec((tm, width), lambda i: (i, 0))

    def full(shape):
        return pl.BlockSpec(shape, lambda i: (0, 0))

    wb = w_out.astype(BF16)
    return pl.pallas_call(
        _out_proj_kernel,
        name="out_proj",
        grid=(t // tm,),
        in_specs=[tok(D_MODEL), tok(NSA_WIDTH), tok(SSD_WIDTH), full((1, NSA_WIDTH)),
                  full((NSA_WIDTH, D_MODEL)), full((SSD_WIDTH, D_MODEL)), full((1, D_MODEL))],
        out_specs=[tok(D_MODEL), pl.BlockSpec((D_MODEL, tm), lambda i: (0, i))],
        out_shape=[jax.ShapeDtypeStruct((t, D_MODEL), F32), jax.ShapeDtypeStruct((D_MODEL, t), BF16)],
        compiler_params=_cparams(("parallel",)),
    )(xt, o_nsa, o_ssd, nsa_g[None, :], wb[:NSA_WIDTH], wb[NSA_WIDTH:], ffn_g[None, :])


def _top_sorted(e, k):
    rows = lax.broadcasted_iota(jnp.int32, (k, e.shape[1]), 0)
    out = jnp.zeros((k, e.shape[1]), F32)
    cur = e
    for j in range(k):
        mk = jnp.max(cur, axis=0, keepdims=True)
        out = jnp.where(rows == j, jnp.maximum(mk, 0.0), out)
        target = jnp.where(mk >= 0.0, mk, 2.0)
        cur = jnp.where(cur == target, -float(j + 1), cur)
    rank = jnp.where(cur < 0.0, -1.0 - cur, float(k))
    return out, rank


def _pair_candidates(a16, b16):
    row8 = lax.broadcasted_iota(jnp.int32, (SUBLANES, a16.shape[1]), 0)
    groups = [a16[0:1] * b16[0:SUBLANES], a16[0:1] * b16[SUBLANES:], a16[1:2] * b16[0:SUBLANES]]
    for a in range(2, SUBLANES):
        groups.append(jnp.where(row8 < PEER_TOPK // (a + 1), a16[a:a + 1] * b16[0:SUBLANES], -1.0))
    groups.append(a16[SUBLANES:] * b16[0:1])
    return jnp.concatenate(groups, axis=0)


def _peer_front_kernel(xnt_ref, wq_ref, keys_ref, av_ref, lc_ref, bv_ref, rb_ref):
    xt = xnt_ref[...]
    tm = xt.shape[1]
    row8 = lax.broadcasted_iota(jnp.int32, (SUBLANES, tm), 0)
    for h in range(PEER_HEADS):
        qh = _dot(wq_ref[h * PEER_KEY_DIM:(h + 1) * PEER_KEY_DIM, :], xt)
        s1 = _dot(keys_ref[h, 0], qh[:PEER_HALF].astype(BF16))
        s2 = _dot(keys_ref[h, 1], qh[PEER_HALF:].astype(BF16))
        e1 = jnp.exp(s1 - jnp.max(s1, axis=0, keepdims=True))
        e2 = jnp.exp(s2 - jnp.max(s2, axis=0, keepdims=True))
        a16, ra = _top_sorted(e1, PEER_TOPK)
        b16, rb = _top_sorted(e2, PEER_TOPK)
        cand = _pair_candidates(a16, b16)
        cur = cand
        zsum = jnp.zeros((1, tm), F32)
        for _ in range(PEER_TOPK):
            tau = jnp.maximum(jnp.max(cur, axis=0, keepdims=True), 0.0)
            zsum = zsum + tau
            cur = jnp.where(cur == tau, -1.0, cur)
        picked = jnp.where(cand >= tau, 1.0, 0.0)

        def count(g):
            return jnp.sum(picked[g * SUBLANES:(g + 1) * SUBLANES], axis=0, keepdims=True)

        lens = [count(0) + count(1)] + [count(a + 1) for a in range(1, SUBLANES)]
        len_lo = jnp.zeros((SUBLANES, tm), F32)
        for a, row in enumerate(lens):
            len_lo = jnp.where(row8 == a, row, len_lo)
        len16 = jnp.concatenate([len_lo, picked[(SUBLANES + 1) * SUBLANES:]], axis=0)
        pack = 2 * SUBLANES
        ra_b = ra.astype(BF16)
        lc = jnp.zeros(ra.shape, BF16)
        for a in range(PEER_TOPK):
            len_a = jnp.tile(jnp.broadcast_to(len16[a:a + 1], (pack, tm)).astype(BF16), (PEER_N_KEYS // pack, 1))
            lc = jnp.where(ra_b == float(a), len_a, lc)
        rows = slice(h * PEER_N_KEYS, (h + 1) * PEER_N_KEYS)
        av_ref[rows, :] = e1
        lc_ref[rows, :] = lc.astype(F32)
        bv_ref[rows, :] = (e2 * (1.0 / zsum)).astype(BF16)
        rb_ref[rows, :] = rb.astype(BF16)


def _peer_front(xnt, w_q, keys):
    t = xnt.shape[1]
    tm = min(PEER_FRONT_TOK, t)
    nk = PEER_HEADS * PEER_N_KEYS
    return pl.pallas_call(
        _peer_front_kernel,
        name="peer_front",
        grid=(t // tm,),
        in_specs=[pl.BlockSpec((D_MODEL, tm), lambda i: (0, i)),
                  pl.BlockSpec((PEER_HEADS * PEER_KEY_DIM, D_MODEL), lambda i: (0, 0)),
                  pl.BlockSpec((PEER_HEADS, 2, PEER_N_KEYS, PEER_HALF), lambda i: (0, 0, 0, 0))],
        out_specs=[pl.BlockSpec((nk, tm), lambda i: (0, i))] * 4,
        out_shape=[jax.ShapeDtypeStruct((nk, t), F32), jax.ShapeDtypeStruct((nk, t), F32),
                   jax.ShapeDtypeStruct((nk, t), BF16), jax.ShapeDtypeStruct((nk, t), BF16)],
        compiler_params=_cparams(("parallel",)),
    )(xnt, w_q.T.astype(BF16), keys.astype(BF16))


def _row_bf16(ref, row, n_rows):
    pack = 2 * SUBLANES
    tile = jnp.broadcast_to(ref[pl.ds(row, 1), :], (pack, ref.shape[1])).astype(BF16)
    return jnp.tile(tile, (n_rows // pack, 1))


def _peer_dense_kernel(xnt_ref, u_ref, vt_ref, av_ref, lc_ref, bv_ref, rb_ref, o_ref, act_ref):
    j = pl.program_id(1)
    n_sub = u_ref.shape[0] // PEER_N_KEYS

    @pl.when(j == 0)
    def _():
        o_ref[...] = jnp.zeros(o_ref.shape, F32)

    def gate_of(cl):
        e1 = j * n_sub + cl
        gate = jnp.zeros((PEER_N_KEYS, xnt_ref.shape[1]), BF16)
        for h in range(PEER_HEADS):
            rows = slice(h * PEER_N_KEYS, (h + 1) * PEER_N_KEYS)
            lc_row = _row_bf16(lc_ref, h * PEER_N_KEYS + e1, PEER_N_KEYS)
            av_row = _row_bf16(av_ref, h * PEER_N_KEYS + e1, PEER_N_KEYS)
            bv = bv_ref[rows, :]
            gate = gate + jnp.where(rb_ref[rows, :] < lc_row, bv, jnp.zeros_like(bv)) * av_row
        return gate

    per_piece = 4
    piece = per_piece * PEER_N_KEYS
    pack = 2 * SUBLANES
    for p in range(u_ref.shape[0] // piece):
        gates = [gate_of(per_piece * p + k) for k in range(per_piece)]
        zero = jnp.tile(gates[0][0:pack, 0:LANES] * 0.0, (piece // pack, u_ref.shape[1] // LANES))
        hid = _dot(u_ref[p * piece:(p + 1) * piece, :] + zero, xnt_ref[...])
        for k in range(per_piece):
            r0 = p * piece + k * PEER_N_KEYS
            act_ref[r0:r0 + PEER_N_KEYS, :] = (
                _gelu_sigmoid(hid[k * PEER_N_KEYS:(k + 1) * PEER_N_KEYS].astype(BF16)) * gates[k])
    o_ref[...] += _dot(vt_ref[...], act_ref[...])


def _peer_dense(xnt, u, vt, av, lc, bv, rb):
    t = xnt.shape[1]
    tm = min(PEER_TOK, t)
    ne = u.shape[0]
    te = PEER_EXP
    tok = pl.BlockSpec((PEER_HEADS * PEER_N_KEYS, tm), lambda i, j: (0, i))
    return pl.pallas_call(
        _peer_dense_kernel,
        name="peer_dense",
        grid=(t // tm, ne // te),
        in_specs=[pl.BlockSpec((D_MODEL, tm), lambda i, j: (0, i)),
                  pl.BlockSpec((te, D_MODEL), lambda i, j: (j, 0)),
                  pl.BlockSpec((D_MODEL, te), lambda i, j: (0, j)),
                  tok, tok, tok, tok],
        out_specs=pl.BlockSpec((D_MODEL, tm), lambda i, j: (0, i)),
        out_shape=jax.ShapeDtypeStruct((D_MODEL, t), F32),
        scratch_shapes=[pltpu.VMEM((te, tm), BF16)],
        compiler_params=_cparams(("parallel", "arbitrary")),
    )(xnt, u, vt, av, lc, bv, rb)


def _final_kernel(x1_ref, pt_ref, g_ref, o_ref):
    o_ref[...] = _rms(x1_ref[...] + pt_ref[...].T, g_ref[...])


def _residual_norm(x1, peer_t, g):
    t = x1.shape[0]
    tm = min(TOK_TILE, t)
    return pl.pallas_call(
        _final_kernel,
        name="residual_norm",
        grid=(t // tm,),
        in_specs=[pl.BlockSpec((tm, D_MODEL), lambda i: (i, 0)), pl.BlockSpec((D_MODEL, tm), lambda i: (0, i)),
                  pl.BlockSpec((1, D_MODEL), lambda i: (0, 0))],
        out_specs=pl.BlockSpec((tm, D_MODEL), lambda i: (i, 0)),
        out_shape=jax.ShapeDtypeStruct((t, D_MODEL), F32),
        compiler_params=_cparams(("parallel",)),
    )(x1, peer_t, g[None, :])


def kernel(x, attn_norm_g, w_in, cmp_pos_k, cmp_w1_k, cmp_b1_k, cmp_w2_k, cmp_pos_v, cmp_w1_v, cmp_b1_v, cmp_w2_v,
           conv_w, conv_b, dt_bias, a_log, d_skip, ssd_norm_g, nsa_norm_g, w_out, ffn_norm_g, peer_w_q, peer_keys,
           peer_u, peer_v, final_norm_g):
    b, s, d = x.shape
    xt = x.reshape(b * s, d)
    assert attn_norm_g.shape[0] == 1, "single-layer block"
    for layer in range(1):
        q, kc, vc, ks, vs, kw, vw, gd, dtt, z, xbc = _in_proj(xt, attn_norm_g[layer], w_in[layer], dt_bias[layer], s)
        kcmp = _compress(kc, cmp_pos_k[layer], cmp_w1_k[layer], cmp_b1_k[layer], cmp_w2_k[layer], b, s)
        vcmp = _compress(vc, cmp_pos_v[layer], cmp_w1_v[layer], cmp_b1_v[layer], cmp_w2_v[layer], b, s)
        o_nsa = _nsa(q, kcmp, vcmp, ks, vs, kw, vw, gd, b, s)
        o_ssd = _ssd(xbc, z, gd, dtt, conv_w[layer], conv_b[layer], a_log[layer], d_skip[layer],
                     ssd_norm_g[layer], b, s)
        x1, xnt = _out_proj(xt, o_nsa, o_ssd, nsa_norm_g[layer], w_out[layer], ffn_norm_g[layer])
        av, lc, bv, rb = _peer_front(xnt, peer_w_q[layer], peer_keys[layer])
        peer_t = _peer_dense(xnt, peer_u[layer].astype(BF16), peer_v[layer].T.astype(BF16), av, lc, bv, rb)
    return _residual_norm(x1, peer_t, final_norm_g).reshape(b, s, d)
```

```python
import functools

import numpy as np
import jax
import jax.numpy as jnp
from jax import lax
from jax.experimental import pallas as pl
from jax.experimental.pallas import tpu as pltpu

F32 = jnp.float32
BF16 = jnp.bfloat16

EPS = 1e-6
D_MODEL = 1024
NSA_HEADS = 8
NSA_KV_GROUPS = 2
NSA_REP = NSA_HEADS // NSA_KV_GROUPS
HEAD_DIM = 64
NSA_WIDTH = NSA_HEADS * HEAD_DIM
KV_WIDTH = NSA_KV_GROUPS * HEAD_DIM
CMP_STRIDE = 16
CMP_BLOCK = 32
CMP_HIDDEN = 256
SEL_BLOCK = 64
SEL_TOP_N = 16
WINDOW = 512
ROPE_DIM = HEAD_DIM // 4
ROPE_THETA = 500000.0
SSD_HEADS = 8
SSD_HEAD_DIM = 64
SSD_WIDTH = SSD_HEADS * SSD_HEAD_DIM
SSD_GROUPS = 2
SSD_REP = SSD_HEADS // SSD_GROUPS
SSD_STATE = 128
SSD_CONV = 4
SSD_CHUNK = 128
SSD_CONV_DIM = SSD_WIDTH + 2 * SSD_GROUPS * SSD_STATE
PEER_HEADS = 8
PEER_N_KEYS = 128
PEER_KEY_DIM = 256
PEER_HALF = PEER_KEY_DIM // 2
PEER_TOPK = 16

LANES = 128
SUBLANES = 8
VMEM_LIMIT = 48 * 1024 * 1024
NEG_BIG = -1e30
SEL_NEG = -1e9

TOK_TILE = 512
Q_TILE = 2 * SEL_BLOCK
K_TILE = 2 * SEL_BLOCK
S_CHUNK = 8 * K_TILE
PEER_TOK = 512
PEER_EXP = 2048
PEER_FRONT_TOK = 512


def _cparams(sem):
    return pltpu.CompilerParams(dimension_semantics=sem, vmem_limit_bytes=VMEM_LIMIT)


def _dot(a, b):
    return jnp.dot(a, b, preferred_element_type=F32)


def _dot_nt(a, b):
    return lax.dot_general(a, b, (((1,), (1,)), ((), ())), preferred_element_type=F32)


def _split3(a):
    a1 = a.astype(BF16)
    r1 = a - a1.astype(F32)
    a2 = r1.astype(BF16)
    a3 = (r1 - a2.astype(F32)).astype(BF16)
    return a1, a2, a3


def _dot_exact_lhs(a, b01):
    a1, a2, a3 = _split3(a)
    return _dot(a1, b01) + _dot(a2, b01) + _dot(a3, b01)


def _dot_nt_exact_lhs(a, b01):
    a1, a2, a3 = _split3(a)
    return _dot_nt(a1, b01) + _dot_nt(a2, b01) + _dot_nt(a3, b01)


def _dot_exact_rhs(a01, b):
    b1, b2, b3 = _split3(b)
    return _dot(a01, b1) + _dot(a01, b2) + _dot(a01, b3)


def _dot_nt_exact_rhs(a01, b):
    b1, b2, b3 = _split3(b)
    return _dot_nt(a01, b1) + _dot_nt(a01, b2) + _dot_nt(a01, b3)


def _softplus(x):
    return jnp.maximum(x, 0.0) + jnp.log(1.0 + jnp.exp(-jnp.abs(x)))


def _sigmoid(x):
    return 1.0 / (1.0 + jnp.exp(-x))


def _gelu_tanh(x):
    return 0.5 * x * (1.0 + jnp.tanh(0.7978845608028654 * (x + 0.044715 * (x * x * x))))


def _gelu_sigmoid(x):
    c1 = -2.0 * 0.7978845608028654 * 1.4426950408889634
    u = x * (c1 + (c1 * 0.044715) * (x * x))
    return x / (1.0 + jnp.exp2(u))


def _rope128(p, cos, sa, sb):
    return p * cos + pltpu.roll(p, LANES - ROPE_DIM // 2, 1) * sa + pltpu.roll(p, ROPE_DIM // 2, 1) * sb


def _in_proj_kernel(seq_tiles, x_ref, g_ref, w_ref, wdt_ref, cos_ref, sa_ref, sb_ref, gdb_ref, dtb_ref,
                    q_ref, kc_ref, vc_ref, ks_ref, vs_ref, kw_ref, vw_ref, gd_ref, dtt_ref, z_ref, xbc_ref):
    tm = x_ref.shape[0]
    x = x_ref[...]
    h = x * lax.rsqrt(jnp.mean(x * x, axis=-1, keepdims=True) + EPS) * g_ref[...]
    hb = h.astype(BF16)
    pos0 = pl.multiple_of((pl.program_id(0) % seq_tiles) * tm, tm)
    cos = cos_ref[pl.ds(pos0, tm), :]
    sa = sa_ref[pl.ds(pos0, tm), :]
    sb = sb_ref[pl.ds(pos0, tm), :]
    lane = lax.broadcasted_iota(jnp.int32, (tm, LANES), 1)
    low = lane < HEAD_DIM

    def proj(c0, width):
        return _dot(hb, w_ref[:, c0:c0 + width])

    def halves(p):
        return p, pltpu.roll(p, HEAD_DIM, 1)

    pq = proj(0, NSA_WIDTH)
    for s in range(NSA_WIDTH // LANES):
        srcs = halves(_rope128(pq[:, s * LANES:(s + 1) * LANES], cos, sa, sb) * (HEAD_DIM ** -0.5))
        for half in range(2):
            head = 2 * s + half
            q_ref[:, head * LANES:(head + 1) * LANES] = jnp.where(low, srcs[half], 0.0).astype(BF16)
    c0 = NSA_WIDTH
    kc_ref[...] = _rope128(proj(c0, LANES), cos, sa, sb).astype(BF16)
    vc_ref[...] = proj(c0 + LANES, LANES).astype(BF16)
    c0 += 2 * LANES
    blk_of_row = (pos0 + lax.broadcasted_iota(jnp.int32, (tm, LANES), 0)) // SEL_BLOCK
    onehot = jnp.where(lane - HEAD_DIM == blk_of_row, 1.0, 0.0)
    for ref, rope, fill in ((ks_ref, True, onehot), (vs_ref, False, 1.0), (kw_ref, True, 0.0), (vw_ref, False, 1.0)):
        p = proj(c0, LANES)
        if rope:
            p = _rope128(p, cos, sa, sb)
        for g, src in enumerate(halves(p)):
            ref[:, g * LANES:(g + 1) * LANES] = jnp.where(low, src, fill).astype(BF16)
        c0 += LANES
    for s in range(2):
        p = proj(c0, LANES) + gdb_ref[:, s * LANES:(s + 1) * LANES]
        act = jnp.where(lane < 3 * NSA_REP, _sigmoid(p),
                        jnp.where((lane >= 16) & (lane < 16 + SSD_REP), _softplus(p), 0.0))
        gd_ref[:, s * LANES:(s + 1) * LANES] = act
        c0 += LANES
    z_ref[...] = proj(c0, SSD_WIDTH)
    c0 += SSD_WIDTH
    xbc_ref[...] = proj(c0, SSD_CONV_DIM)
    dtt_ref[...] = _softplus(_dot_nt(wdt_ref[...], hb) + dtb_ref[:, 0:1])


def _rope_tables(s):
    half = ROPE_DIM // 2
    inv = jnp.power(ROPE_THETA, -jnp.arange(half, dtype=F32) * 2.0 / ROPE_DIM)
    ang = jnp.arange(s).astype(F32)[:, None] * inv[None, :]
    cos, sin = jnp.cos(ang), jnp.sin(ang)
    zeros = jnp.zeros((s, HEAD_DIM - ROPE_DIM), F32)
    cos64 = jnp.concatenate([cos, cos, jnp.ones((s, HEAD_DIM - ROPE_DIM), F32)], axis=1)
    sa64 = jnp.concatenate([-sin, jnp.zeros_like(sin), zeros], axis=1)
    sb64 = jnp.concatenate([jnp.zeros_like(sin), sin, zeros], axis=1)
    return tuple(jnp.concatenate([t, t], axis=1) for t in (cos64, sa64, sb64))


def _in_proj(xt, attn_g, w_in, dt_bias, s):
    t = xt.shape[0]
    tm = min(TOK_TILE, s)
    o_gl = NSA_WIDTH + 6 * KV_WIDTH
    o_z = o_gl + 3 * NSA_HEADS
    o_xbc = o_z + SSD_WIDTH
    o_dt = o_xbc + SSD_CONV_DIM
    gd_cols, gd_bias = [], []
    for g in range(NSA_KV_GROUPS):
        gates = w_in[:, o_gl + 3 * NSA_REP * g:o_gl + 3 * NSA_REP * (g + 1)]
        dts = w_in[:, o_dt + SSD_REP * g:o_dt + SSD_REP * (g + 1)]
        gd_cols += [gates, jnp.zeros((D_MODEL, 16 - 3 * NSA_REP), F32), dts,
                    jnp.zeros((D_MODEL, LANES - 16 - SSD_REP), F32)]
        gd_bias += [jnp.zeros((16,), F32), dt_bias[SSD_REP * g:SSD_REP * (g + 1)],
                    jnp.zeros((LANES - 16 - SSD_REP,), F32)]
    w_main = jnp.concatenate([w_in[:, :o_gl]] + gd_cols + [w_in[:, o_z:o_dt]], axis=1).astype(BF16)
    gdb = jnp.concatenate(gd_bias)[None, :]
    wdt_rows, dtb_rows = [], []
    for g in range(SSD_GROUPS):
        wdt_rows += [w_in[:, o_dt + SSD_REP * g:o_dt + SSD_REP * (g + 1)].T,
                     jnp.zeros((SUBLANES - SSD_REP, D_MODEL), F32)]
        dtb_rows += [dt_bias[SSD_REP * g:SSD_REP * (g + 1)], jnp.zeros((SUBLANES - SSD_REP,), F32)]
    wdt = jnp.concatenate(wdt_rows, axis=0).astype(BF16)
    dtb = jnp.broadcast_to(jnp.concatenate(dtb_rows)[:, None], (2 * SUBLANES, LANES))
    cos, sa, sb = _rope_tables(s)
    n_main = w_main.shape[1]

    def full(shape):
        return pl.BlockSpec(shape, lambda i: (0, 0))

    def tok(width):
        return pl.BlockSpec((tm, width), lambda i: (i, 0))

    kv_widths = [LANES, LANES] + [NSA_KV_GROUPS * LANES] * 4
    out_shapes = ([jax.ShapeDtypeStruct((t, NSA_HEADS * LANES), BF16)]
                  + [jax.ShapeDtypeStruct((t, w), BF16) for w in kv_widths]
                  + [jax.ShapeDtypeStruct((t, 2 * LANES), F32),
                     jax.ShapeDtypeStruct((2 * SUBLANES, t), F32),
                     jax.ShapeDtypeStruct((t, SSD_WIDTH), F32),
                     jax.ShapeDtypeStruct((t, SSD_CONV_DIM), F32)])
    out_specs = ([tok(NSA_HEADS * LANES)] + [tok(w) for w in kv_widths]
                 + [tok(2 * LANES), pl.BlockSpec((2 * SUBLANES, tm), lambda i: (0, i)),
                    tok(SSD_WIDTH), tok(SSD_CONV_DIM)])
    return pl.pallas_call(
        functools.partial(_in_proj_kernel, s // tm),
        name="in_proj",
        grid=(t // tm,),
        in_specs=[tok(D_MODEL), full((1, D_MODEL)), full((D_MODEL, n_main)), full((2 * SUBLANES, D_MODEL)),
                  full((s, LANES)), full((s, LANES)), full((s, LANES)), full((1, 2 * LANES)),
                  full((2 * SUBLANES, LANES))],
        out_specs=out_specs,
        out_shape=out_shapes,
        compiler_params=_cparams(("parallel",)),
    )(xt, attn_g[None, :], w_main, wdt, cos, sa, sb, gdb, dtb)


def _compress_kernel(kv_ref, w1a_ref, w1b_ref, pos_ref, w1_ref, b1_ref, w2_ref, out_ref):
    kv = kv_ref[0]
    bias = _dot(pos_ref[...], w1_ref[...])[0:1, :] + b1_ref[...]
    for g in range(NSA_KV_GROUPS):
        first = _dot(kv, w1a_ref[g])
        second = _dot(kv, w1b_ref[g])
        nxt = pltpu.roll(second, second.shape[0] - 1, 0)
        hid = _gelu_tanh(first + nxt + bias)
        out_ref[0, :, g * LANES:(g + 1) * LANES] = _dot(hid.astype(BF16), w2_ref[...]).astype(BF16)


def _compress(kv, pos_emb, w1, b1, w2, b, s):
    nch = s // CMP_STRIDE
    kvf = kv.reshape(b, nch, CMP_STRIDE * LANES)
    w1r = w1.reshape(CMP_BLOCK, HEAD_DIM, CMP_HIDDEN)
    zeros = jnp.zeros((CMP_STRIDE, HEAD_DIM, CMP_HIDDEN), F32)

    def expand(w_half, g):
        parts = [w_half, zeros] if g == 0 else [zeros, w_half]
        return jnp.concatenate(parts, axis=1).reshape(CMP_STRIDE * LANES, CMP_HIDDEN)

    w1a = jnp.stack([expand(w1r[:CMP_STRIDE], g) for g in range(NSA_KV_GROUPS)]).astype(BF16)
    w1b = jnp.stack([expand(w1r[CMP_STRIDE:], g) for g in range(NSA_KV_GROUPS)]).astype(BF16)
    w2e = jnp.concatenate([w2, jnp.zeros((CMP_HIDDEN, LANES - HEAD_DIM), F32)], axis=1).astype(BF16)
    pos = jnp.zeros((SUBLANES, CMP_BLOCK * HEAD_DIM), F32).at[0].set(pos_emb.reshape(-1)).astype(BF16)
    return pl.pallas_call(
        _compress_kernel,
        name="compress",
        grid=(b,),
        in_specs=[pl.BlockSpec((1, nch, CMP_STRIDE * LANES), lambda i: (i, 0, 0)),
                  pl.BlockSpec((2, CMP_STRIDE * LANES, CMP_HIDDEN), lambda i: (0, 0, 0)),
                  pl.BlockSpec((2, CMP_STRIDE * LANES, CMP_HIDDEN), lambda i: (0, 0, 0)),
                  pl.BlockSpec((SUBLANES, CMP_BLOCK * HEAD_DIM), lambda i: (0, 0)),
                  pl.BlockSpec((CMP_BLOCK * HEAD_DIM, CMP_HIDDEN), lambda i: (0, 0)),
                  pl.BlockSpec((1, CMP_HIDDEN), lambda i: (0, 0)),
                  pl.BlockSpec((CMP_HIDDEN, LANES), lambda i: (0, 0))],
        out_specs=pl.BlockSpec((1, nch, NSA_KV_GROUPS * LANES), lambda i: (i, 0, 0)),
        out_shape=jax.ShapeDtypeStruct((b, nch, NSA_KV_GROUPS * LANES), BF16),
        compiler_params=_cparams(("parallel",)),
    )(kvf, w1a, w1b, pos, w1.astype(BF16), b1[None, :], w2e)


def _lane_tiles(a):
    return [a[:, k * LANES:(k + 1) * LANES] for k in range(a.shape[1] // LANES)]


def _max_tiles(macc, s):
    for tile in _lane_tiles(s):
        macc = jnp.maximum(macc, tile)
    return macc


def _exp_tiles(s, m_b):
    return jnp.concatenate([jnp.exp(tile - m_b) for tile in _lane_tiles(s)], axis=1).astype(BF16)


def _normalize(acc):
    return acc / pltpu.roll(acc, HEAD_DIM, 1)


def _block_rank(imp):
    nblk, tq = imp.shape
    sub = lax.broadcasted_iota(jnp.int32, (SUBLANES, tq), 0)
    groups = [imp[SUBLANES * v:SUBLANES * (v + 1)] for v in range(nblk // SUBLANES)]
    cnt = [jnp.zeros((SUBLANES, tq), F32) for _ in groups]
    for j in range(nblk):
        row = imp[j:j + 1, :]
        for v, grp in enumerate(groups):
            if SUBLANES * v > j:
                ahead = row >= grp
            elif SUBLANES * (v + 1) <= j:
                ahead = row > grp
            else:
                ahead = (row > grp) | ((row == grp) & (sub > j - SUBLANES * v))
            cnt[v] = cnt[v] + jnp.where(ahead, 1.0, 0.0)
    return jnp.concatenate(cnt, axis=0)


def _nsa_pair_kernel(q_ref, kc_ref, vc_ref, ks_ref, vs_ref, kw_ref, vw_ref, gd_ref, ovl_ref, o_ref, s_ref):
    i = pl.program_id(1)
    tq = Q_TILE
    rows = NSA_REP * tq
    ncmp = kc_ref.shape[1]
    nblk = ovl_ref.shape[0]
    groups = range(NSA_KV_GROUPS)
    t_row = i * tq + lax.broadcasted_iota(jnp.int32, (rows, 1), 0) % tq

    def lanes(g):
        return slice(g * LANES, (g + 1) * LANES)

    qs = [jnp.concatenate([q_ref[:, (g * NSA_REP + r) * LANES:(g * NSA_REP + r + 1) * LANES]
                           for r in range(NSA_REP)], axis=0) for g in groups]

    jd = (i * tq) // K_TILE
    neg = jnp.full((rows, LANES), NEG_BIG, F32)
    n_wk = WINDOW + K_TILE
    k0 = pl.multiple_of(jnp.maximum(jd - WINDOW // K_TILE, 0) * K_TILE, K_TILE)
    diff = t_row - (k0 + lax.broadcasted_iota(jnp.int32, (rows, n_wk), 1))
    in_win = (diff >= 0) & (diff < WINDOW)
    o_wins = []
    for g in groups:
        s_w = jnp.where(in_win, _dot_nt(qs[g], kw_ref[pl.ds(k0, n_wk), lanes(g)]), NEG_BIG)
        m_w = jnp.max(_max_tiles(neg, s_w), axis=-1, keepdims=True)
        p_w = _exp_tiles(s_w, jnp.broadcast_to(m_w, (rows, LANES)))
        o_wins.append(_normalize(_dot(p_w, vw_ref[pl.ds(k0, n_wk), lanes(g)])))

    cmp_end = lax.broadcasted_iota(jnp.int32, (rows, ncmp), 1) * CMP_STRIDE + (CMP_BLOCK - 1)
    mask_c = cmp_end <= t_row
    p_cs, o_cmps = [], []
    for g in groups:
        s_c = jnp.where(mask_c, _dot_nt(qs[g], kc_ref[0, :, lanes(g)]), NEG_BIG)
        e_c = jnp.exp(s_c - jnp.max(s_c, axis=-1, keepdims=True))
        p_c = jnp.where(mask_c, e_c / jnp.sum(e_c, axis=-1, keepdims=True), 0.0)
        p_cs.append(p_c)
        o_cmps.append(_dot(p_c.astype(BF16), vc_ref[0, :, lanes(g)]))

    blk = lax.broadcasted_iota(jnp.int32, (nblk, tq), 0)
    cur = (i * tq + lax.broadcasted_iota(jnp.int32, (nblk, tq), 1)) // SEL_BLOCK
    forced = (blk == 0) | (blk == cur) | (blk == cur - 1)
    valid = blk <= cur
    eye = jnp.where(lax.broadcasted_iota(jnp.int32, (tq, tq), 0) == lax.broadcasted_iota(jnp.int32, (tq, tq), 1),
                    1.0, 0.0).astype(BF16)
    ones_lo = jnp.ones((HEAD_DIM, tq), F32)
    q2s = []
    for g in groups:
        p_sum = p_cs[g][0:tq]
        for r in range(1, NSA_REP):
            p_sum = p_sum + p_cs[g][r * tq:(r + 1) * tq]
        imp = _dot_nt_exact_rhs(ovl_ref[...], p_sum)
        imp = jnp.where(forced, 1e9, jnp.where(valid, imp, -1.0))
        sel_t = jnp.where((_block_rank(imp) < float(min(SEL_TOP_N, nblk))) & valid, 1.0, 0.0)
        pad = [jnp.ones((HEAD_DIM - nblk, tq), F32)] if nblk < HEAD_DIM else []
        sel_pad = jnp.concatenate([ones_lo, sel_t] + pad, axis=0).astype(BF16)
        negb = ((_dot_nt(eye, sel_pad) - 1.0) * (-SEL_NEG)).astype(BF16)
        q2s.append(qs[g] + jnp.concatenate([negb] * NSA_REP, axis=0))

    nfull = jd // (S_CHUNK // K_TILE)
    lane_c = lax.broadcasted_iota(jnp.int32, (rows, S_CHUNK), 1)

    def scores(g, c):
        k0 = pl.multiple_of(c * S_CHUNK, S_CHUNK)
        return _dot_nt(q2s[g], ks_ref[pl.ds(k0, S_CHUNK), lanes(g)])

    def pass_max(c, maccs):
        out = []
        for g in groups:
            s = scores(g, c)
            s_ref[g, c] = s
            out.append(_max_tiles(maccs[g], s))
        return tuple(out)

    maccs = lax.fori_loop(0, nfull, pass_max, (neg,) * NSA_KV_GROUPS)
    causal = nfull * S_CHUNK + lane_c <= t_row
    s_lasts = [jnp.where(causal, scores(g, nfull), NEG_BIG) for g in groups]
    m_bs = [jnp.broadcast_to(jnp.max(_max_tiles(maccs[g], s_lasts[g]), axis=-1, keepdims=True), (rows, LANES))
            for g in groups]

    def pass_sum(c, accs):
        k0 = pl.multiple_of(c * S_CHUNK, S_CHUNK)
        return tuple(accs[g] + _dot(_exp_tiles(s_ref[g, c], m_bs[g]), vs_ref[pl.ds(k0, S_CHUNK), lanes(g)])
                     for g in groups)

    zero = jnp.zeros((rows, LANES), F32)
    accs = lax.fori_loop(0, nfull, pass_sum, (zero,) * NSA_KV_GROUPS)
    k_last = pl.multiple_of(nfull * S_CHUNK, S_CHUNK)
    o_sels = [_normalize(accs[g] + _dot(_exp_tiles(s_lasts[g], m_bs[g]), vs_ref[pl.ds(k_last, S_CHUNK), lanes(g)]))
              for g in groups]

    low = lax.broadcasted_iota(jnp.int32, (tq, LANES), 1) < HEAD_DIM
    for g in groups:
        gd = gd_ref[:, lanes(g)]
        heads = []
        for r in range(NSA_REP):
            rs = slice(r * tq, (r + 1) * tq)
            heads.append(gd[:, 3 * r:3 * r + 1] * o_cmps[g][rs] + gd[:, 3 * r + 1:3 * r + 2] * o_sels[g][rs]
                         + gd[:, 3 * r + 2:3 * r + 3] * o_wins[g][rs])
        for sidx in range(NSA_REP // 2):
            slab = g * (NSA_REP // 2) + sidx
            o_ref[:, slab * LANES:(slab + 1) * LANES] = jnp.where(
                low, heads[2 * sidx], pltpu.roll(heads[2 * sidx + 1], HEAD_DIM, 1))


def _nsa(q, kcmp, vcmp, ks, vs, kw, vw, gd, b, s):
    t = q.shape[0]
    nq = s // Q_TILE
    ncmp = s // CMP_STRIDE
    nblk = s // SEL_BLOCK
    cs = np.arange(ncmp) * CMP_STRIDE
    ss = np.arange(nblk) * SEL_BLOCK
    overlap = (cs[:, None] < ss[None, :] + SEL_BLOCK) & (cs[:, None] + CMP_BLOCK > ss[None, :])
    overlap[ncmp - 1, :] = False
    ovl_t = jnp.asarray(overlap.T, BF16)

    assert nblk <= HEAD_DIM and s % S_CHUNK == 0 and s >= WINDOW + K_TILE

    ng = NSA_KV_GROUPS

    def seq_spec():
        return pl.BlockSpec((s, ng * LANES), lambda bi, i: (bi, 0))

    return pl.pallas_call(
        _nsa_pair_kernel,
        name="nsa",
        grid=(b, nq),
        in_specs=[pl.BlockSpec((Q_TILE, NSA_HEADS * LANES), lambda bi, i: (bi * nq + i, 0)),
                  pl.BlockSpec((1, ncmp, ng * LANES), lambda bi, i: (bi, 0, 0)),
                  pl.BlockSpec((1, ncmp, ng * LANES), lambda bi, i: (bi, 0, 0)),
                  seq_spec(), seq_spec(), seq_spec(), seq_spec(),
                  pl.BlockSpec((Q_TILE, ng * LANES), lambda bi, i: (bi * nq + i, 0)),
                  pl.BlockSpec((nblk, ncmp), lambda bi, i: (0, 0))],
        out_specs=pl.BlockSpec((Q_TILE, NSA_WIDTH), lambda bi, i: (bi * nq + i, 0)),
        out_shape=jax.ShapeDtypeStruct((t, NSA_WIDTH), F32),
        scratch_shapes=[pltpu.VMEM((ng, s // S_CHUNK, NSA_REP * Q_TILE, S_CHUNK), F32)],
        compiler_params=_cparams(("parallel", "arbitrary")),
    )(q, kcmp, vcmp, ks, vs, kw, vw, gd, ovl_t)


def _conv_silu(cur_ref, ext_ref, tail_ref, w_ref, b_ref):
    l = cur_ref.shape[0]
    cur = cur_ref[...]
    ext_ref[0:SUBLANES, :] = tail_ref[...]
    ext_ref[SUBLANES:, :] = cur
    tail_ref[...] = cur[l - SUBLANES:, :]
    acc = jnp.zeros(cur.shape, F32) + b_ref[...]
    for k in range(SSD_CONV):
        off = SUBLANES - (SSD_CONV - 1) + k
        acc = acc + ext_ref[off:off + l, :] * w_ref[k:k + 1, :]
    return acc * _sigmoid(acc)


def _ssd_kernel(xbc_ref, w_ref, b_ref, z_ref, gd_ref, dtt_ref, alane_ref, asub_ref, dskip_ref, ng_ref,
                tril_ref, edt_ref, eseg_ref, o_ref, tail_ref, ext_ref, st_ref):
    c = pl.program_id(1)
    l = xbc_ref.shape[0]
    gw = SSD_WIDTH // SSD_GROUPS
    b_col0 = SSD_WIDTH
    c_col0 = SSD_WIDTH + SSD_GROUPS * SSD_STATE

    @pl.when(c == 0)
    def _():
        tail_ref[...] = jnp.zeros(tail_ref.shape, F32)
        st_ref[...] = jnp.zeros(st_ref.shape, F32)

    conv = _conv_silu(xbc_ref, ext_ref, tail_ref, w_ref, b_ref)
    tril = tril_ref[...]
    causal = lax.broadcasted_iota(jnp.int32, (l, l), 0) >= lax.broadcasted_iota(jnp.int32, (l, l), 1)
    head_of_lane = lax.broadcasted_iota(jnp.int32, (l, gw), 1) // SSD_HEAD_DIM
    for g in range(SSD_GROUPS):
        xs = conv[:, g * gw:(g + 1) * gw]
        bm = conv[:, b_col0 + g * SSD_STATE:b_col0 + (g + 1) * SSD_STATE]
        cm = conv[:, c_col0 + g * SSD_STATE:c_col0 + (g + 1) * SSD_STATE]
        gd = gd_ref[:, g * LANES:(g + 1) * LANES]
        adt_c = gd * (-jnp.exp(alane_ref[g]))
        acum_c = _dot_exact_rhs(tril, adt_c)
        dt_full = _dot_exact_lhs(gd, edt_ref[...])
        ac_full = _dot_exact_lhs(acum_c, edt_ref[...])
        ac_seg = _dot_exact_lhs(acum_c, eseg_ref[...])
        adt_t = dtt_ref[g * SUBLANES:(g + 1) * SUBLANES, :] * (-jnp.exp(asub_ref[g]))
        acum_t = _dot_nt_exact_lhs(adt_t, tril)

        xdt = xs * dt_full
        cmb = cm.astype(BF16)
        cb = _dot_nt(cmb, bm.astype(BF16))
        y = jnp.zeros(xs.shape, F32)
        for r in range(SSD_REP):
            seg = jnp.exp(jnp.where(causal, ac_seg[:, r * l:(r + 1) * l] - acum_t[r:r + 1, :], NEG_BIG))
            x_r = jnp.where(head_of_lane == r, xdt, 0.0).astype(BF16)
            y = y + _dot((cb * seg).astype(BF16), x_r)
        ac_last = ac_full[l - 1:l, :]
        state = st_ref[g]
        y = y + _dot(cmb, state.astype(BF16)) * jnp.exp(ac_full)
        decayed = (xdt * jnp.exp(ac_last - ac_full)).astype(BF16)
        st_ref[g] = state * jnp.exp(ac_last) + _dot(bm.T.astype(BF16), decayed)
        y = y + xs * dskip_ref[:, g * gw:(g + 1) * gw]
        z = z_ref[:, g * gw:(g + 1) * gw]
        yz = y * (z * _sigmoid(z))
        o_ref[:, g * gw:(g + 1) * gw] = (yz * lax.rsqrt(jnp.mean(yz * yz, axis=-1, keepdims=True) + EPS)
                                        * ng_ref[:, g * gw:(g + 1) * gw])


def _ssd(xbc, z, gd, dtt, conv_w, conv_b, a_log, d_skip, norm_g, b, s):
    t = xbc.shape[0]
    l = SSD_CHUNK
    nc = s // l
    gw = SSD_WIDTH // SSD_GROUPS
    xcols = gw // LANES
    b_blk0 = SSD_WIDTH // LANES
    c_blk0 = b_blk0 + SSD_GROUPS * SSD_STATE // LANES
    conv_b2 = conv_b[None, :]
    a_grp = a_log.reshape(SSD_GROUPS, SSD_REP)
    alane = jnp.zeros((SSD_GROUPS, 1, LANES), F32).at[:, 0, 16:16 + SSD_REP].set(a_grp)
    asub = jnp.zeros((SSD_GROUPS, SUBLANES, LANES), F32).at[:, :SSD_REP, :].set(
        jnp.broadcast_to(a_grp[:, :, None], (SSD_GROUPS, SSD_REP, LANES)))
    dskip = jnp.repeat(d_skip, SSD_HEAD_DIM)[None, :]
    ng = norm_g[None, :]
    ii = np.arange(l)
    tril = jnp.asarray(ii[:, None] >= ii[None, :], BF16)
    edt = np.zeros((LANES, gw), np.float32)
    eseg = np.zeros((LANES, SSD_REP * l), np.float32)
    for r in range(SSD_REP):
        edt[16 + r, r * SSD_HEAD_DIM:(r + 1) * SSD_HEAD_DIM] = 1.0
        eseg[16 + r, r * l:(r + 1) * l] = 1.0
    edt, eseg = jnp.asarray(edt, BF16), jnp.asarray(eseg, BF16)

    def rowblk(width):
        return pl.BlockSpec((l, width), lambda bi, c: (bi * nc + c, 0))

    def const(shape):
        return pl.BlockSpec(shape, lambda bi, c: (0,) * len(shape))

    return pl.pallas_call(
        _ssd_kernel,
        name="ssd",
        grid=(b, nc),
        in_specs=[rowblk(SSD_CONV_DIM), const((SSD_CONV, SSD_CONV_DIM)), const((1, SSD_CONV_DIM)),
                  rowblk(SSD_WIDTH), rowblk(SSD_GROUPS * LANES),
                  pl.BlockSpec((SSD_GROUPS * SUBLANES, l), lambda bi, c: (0, bi * nc + c)),
                  const((SSD_GROUPS, 1, LANES)), const((SSD_GROUPS, SUBLANES, LANES)),
                  const((1, SSD_WIDTH)), const((1, SSD_WIDTH)),
                  const((l, l)), const((LANES, gw)), const((LANES, SSD_REP * l))],
        out_specs=rowblk(SSD_WIDTH),
        out_shape=jax.ShapeDtypeStruct((t, SSD_WIDTH), F32),
        scratch_shapes=[pltpu.VMEM((SUBLANES, SSD_CONV_DIM), F32), pltpu.VMEM((l + SUBLANES, SSD_CONV_DIM), F32),
                        pltpu.VMEM((SSD_GROUPS, SSD_STATE, gw), F32)],
        compiler_params=_cparams(("parallel", "arbitrary")),
    )(xbc, conv_w, conv_b2, z, gd, dtt, alane, asub, dskip, ng, tril, edt, eseg)


def _rms(v, g):
    return v * lax.rsqrt(jnp.mean(v * v, axis=-1, keepdims=True) + EPS) * g


def _out_proj_kernel(x_ref, on_ref, os_ref, ng_ref, w1_ref, w2_ref, fg_ref, x1_ref, xnt_ref):
    onn = _rms(on_ref[...], ng_ref[...]).astype(BF16)
    x1 = x_ref[...] + _dot(onn, w1_ref[...]) + _dot(os_ref[...].astype(BF16), w2_ref[...])
    x1_ref[...] = x1
    xnt_ref[...] = _rms(x1, fg_ref[...]).T.astype(BF16)


def _out_proj(xt, o_nsa, o_ssd, nsa_g, w_out, ffn_g):
    t = xt.shape[0]
    tm = min(TOK_TILE, t)

    def tok(width):
        return pl.BlockSpec((tm, width), lambda i: (i, 0))

    def full(shape):
        return pl.BlockSpec(shape, lambda i: (0, 0))

    wb = w_out.astype(BF16)
    return pl.pallas_call(
        _out_proj_kernel,
        name="out_proj",
        grid=(t // tm,),
        in_specs=[tok(D_MODEL), tok(NSA_WIDTH), tok(SSD_WIDTH), full((1, NSA_WIDTH)),
                  full((NSA_WIDTH, D_MODEL)), full((SSD_WIDTH, D_MODEL)), full((1, D_MODEL))],
        out_specs=[tok(D_MODEL), pl.BlockSpec((D_MODEL, tm), lambda i: (0, i))],
        out_shape=[jax.ShapeDtypeStruct((t, D_MODEL), F32), jax.ShapeDtypeStruct((D_MODEL, t), BF16)],
        compiler_params=_cparams(("parallel",)),
    )(xt, o_nsa, o_ssd, nsa_g[None, :], wb[:NSA_WIDTH], wb[NSA_WIDTH:], ffn_g[None, :])


def _top_sorted(e, k):
    rows = lax.broadcasted_iota(jnp.int32, (k, e.shape[1]), 0)
    out = jnp.zeros((k, e.shape[1]), F32)
    cur = e
    for j in range(k):
        mk = jnp.max(cur, axis=0, keepdims=True)
        out = jnp.where(rows == j, jnp.maximum(mk, 0.0), out)
        target = jnp.where(mk >= 0.0, mk, 2.0)
        cur = jnp.where(cur == target, -float(j + 1), cur)
    rank = jnp.where(cur < 0.0, -1.0 - cur, float(k))
    return out, rank


def _pair_candidates(a16, b16):
    row8 = lax.broadcasted_iota(jnp.int32, (SUBLANES, a16.shape[1]), 0)
    groups = [a16[0:1] * b16[0:SUBLANES], a16[0:1] * b16[SUBLANES:], a16[1:2] * b16[0:SUBLANES]]
    for a in range(2, SUBLANES):
        groups.append(jnp.where(row8 < PEER_TOPK // (a + 1), a16[a:a + 1] * b16[0:SUBLANES], -1.0))
    groups.append(a16[SUBLANES:] * b16[0:1])
    return jnp.concatenate(groups, axis=0)


def _peer_front_kernel(xnt_ref, wq_ref, keys_ref, av_ref, lc_ref, bv_ref, rb_ref):
    xt = xnt_ref[...]
    tm = xt.shape[1]
    row8 = lax.broadcasted_iota(jnp.int32, (SUBLANES, tm), 0)
    for h in range(PEER_HEADS):
        qh = _dot(wq_ref[h * PEER_KEY_DIM:(h + 1) * PEER_KEY_DIM, :], xt)
        s1 = _dot(keys_ref[h, 0], qh[:PEER_HALF].astype(BF16))
        s2 = _dot(keys_ref[h, 1], qh[PEER_HALF:].astype(BF16))
        e1 = jnp.exp(s1 - jnp.max(s1, axis=0, keepdims=True))
        e2 = jnp.exp(s2 - jnp.max(s2, axis=0, keepdims=True))
        a16, ra = _top_sorted(e1, PEER_TOPK)
        b16, rb = _top_sorted(e2, PEER_TOPK)
        cand = _pair_candidates(a16, b16)
        cur = cand
        zsum = jnp.zeros((1, tm), F32)
        for _ in range(PEER_TOPK):
            tau = jnp.maximum(jnp.max(cur, axis=0, keepdims=True), 0.0)
            zsum = zsum + tau
            cur = jnp.where(cur == tau, -1.0, cur)
        picked = jnp.where(cand >= tau, 1.0, 0.0)

        def count(g):
            return jnp.sum(picked[g * SUBLANES:(g + 1) * SUBLANES], axis=0, keepdims=True)

        lens = [count(0) + count(1)] + [count(a + 1) for a in range(1, SUBLANES)]
        len_lo = jnp.zeros((SUBLANES, tm), F32)
        for a, row in enumerate(lens):
            len_lo = jnp.where(row8 == a, row, len_lo)
        len16 = jnp.concatenate([len_lo, picked[(SUBLANES + 1) * SUBLANES:]], axis=0)
        pack = 2 * SUBLANES
        ra_b = ra.astype(BF16)
        lc = jnp.zeros(ra.shape, BF16)
        for a in range(PEER_TOPK):
            len_a = jnp.tile(jnp.broadcast_to(len16[a:a + 1], (pack, tm)).astype(BF16), (PEER_N_KEYS // pack, 1))
            lc = jnp.where(ra_b == float(a), len_a, lc)
        rows = slice(h * PEER_N_KEYS, (h + 1) * PEER_N_KEYS)
        av_ref[rows, :] = e1
        lc_ref[rows, :] = lc.astype(F32)
        bv_ref[rows, :] = (e2 * (1.0 / zsum)).astype(BF16)
        rb_ref[rows, :] = rb.astype(BF16)


def _peer_front(xnt, w_q, keys):
    t = xnt.shape[1]
    tm = min(PEER_FRONT_TOK, t)
    nk = PEER_HEADS * PEER_N_KEYS
    return pl.pallas_call(
        _peer_front_kernel,
        name="peer_front",
        grid=(t // tm,),
        in_specs=[pl.BlockSpec((D_MODEL, tm), lambda i: (0, i)),
                  pl.BlockSpec((PEER_HEADS * PEER_KEY_DIM, D_MODEL), lambda i: (0, 0)),
                  pl.BlockSpec((PEER_HEADS, 2, PEER_N_KEYS, PEER_HALF), lambda i: (0, 0, 0, 0))],
        out_specs=[pl.BlockSpec((nk, tm), lambda i: (0, i))] * 4,
        out_shape=[jax.ShapeDtypeStruct((nk, t), F32), jax.ShapeDtypeStruct((nk, t), F32),
                   jax.ShapeDtypeStruct((nk, t), BF16), jax.ShapeDtypeStruct((nk, t), BF16)],
        compiler_params=_cparams(("parallel",)),
    )(xnt, w_q.T.astype(BF16), keys.astype(BF16))


def _row_bf16(ref, row, n_rows):
    pack = 2 * SUBLANES
    tile = jnp.broadcast_to(ref[pl.ds(row, 1), :], (pack, ref.shape[1])).astype(BF16)
    return jnp.tile(tile, (n_rows // pack, 1))


def _peer_dense_kernel(xnt_ref, u_ref, vt_ref, av_ref, lc_ref, bv_ref, rb_ref, o_ref, act_ref):
    j = pl.program_id(1)
    n_sub = u_ref.shape[0] // PEER_N_KEYS

    @pl.when(j == 0)
    def _():
        o_ref[...] = jnp.zeros(o_ref.shape, F32)

    def gate_of(cl):
        e1 = j * n_sub + cl
        gate = jnp.zeros((PEER_N_KEYS, xnt_ref.shape[1]), BF16)
        for h in range(PEER_HEADS):
            rows = slice(h * PEER_N_KEYS, (h + 1) * PEER_N_KEYS)
            lc_row = _row_bf16(lc_ref, h * PEER_N_KEYS + e1, PEER_N_KEYS)
            av_row = _row_bf16(av_ref, h * PEER_N_KEYS + e1, PEER_N_KEYS)
            bv = bv_ref[rows, :]
            gate = gate + jnp.where(rb_ref[rows, :] < lc_row, bv, jnp.zeros_like(bv)) * av_row
        return gate

    per_piece = 4
    piece = per_piece * PEER_N_KEYS
    pack = 2 * SUBLANES
    for p in range(u_ref.shape[0] // piece):
        gates = [gate_of(per_piece * p + k) for k in range(per_piece)]
        zero = jnp.tile(gates[0][0:pack, 0:LANES] * 0.0, (piece // pack, u_ref.shape[1] // LANES))
        hid = _dot(u_ref[p * piece:(p + 1) * piece, :] + zero, xnt_ref[...])
        for k in range(per_piece):
            r0 = p * piece + k * PEER_N_KEYS
            act_ref[r0:r0 + PEER_N_KEYS, :] = (
                _gelu_sigmoid(hid[k * PEER_N_KEYS:(k + 1) * PEER_N_KEYS].astype(BF16)) * gates[k])
    o_ref[...] += _dot(vt_ref[...], act_ref[...])


def _peer_dense(xnt, u, vt, av, lc, bv, rb):
    t = xnt.shape[1]
    tm = min(PEER_TOK, t)
    ne = u.shape[0]
    te = PEER_EXP
    tok = pl.BlockSpec((PEER_HEADS * PEER_N_KEYS, tm), lambda i, j: (0, i))
    return pl.pallas_call(
        _peer_dense_kernel,
        name="peer_dense",
        grid=(t // tm, ne // te),
        in_specs=[pl.BlockSpec((D_MODEL, tm), lambda i, j: (0, i)),
                  pl.BlockSpec((te, D_MODEL), lambda i, j: (j, 0)),
                  pl.BlockSpec((D_MODEL, te), lambda i, j: (0, j)),
                  tok, tok, tok, tok],
        out_specs=pl.BlockSpec((D_MODEL, tm), lambda i, j: (0, i)),
        out_shape=jax.ShapeDtypeStruct((D_MODEL, t), F32),
        scratch_shapes=[pltpu.VMEM((te, tm), BF16)],
        compiler_params=_cparams(("parallel", "arbitrary")),
    )(xnt, u, vt, av, lc, bv, rb)


def _final_kernel(x1_ref, pt_ref, g_ref, o_ref):
    o_ref[...] = _rms(x1_ref[...] + pt_ref[...].T, g_ref[...])


def _residual_norm(x1, peer_t, g):
    t = x1.shape[0]
    tm = min(TOK_TILE, t)
    return pl.pallas_call(
        _final_kernel,
        name="residual_norm",
        grid=(t // tm,),
        in_specs=[pl.BlockSpec((tm, D_MODEL), lambda i: (i, 0)), pl.BlockSpec((D_MODEL, tm), lambda i: (0, i)),
                  pl.BlockSpec((1, D_MODEL), lambda i: (0, 0))],
        out_specs=pl.BlockSpec((tm, D_MODEL), lambda i: (i, 0)),
        out_shape=jax.ShapeDtypeStruct((t, D_MODEL), F32),
        compiler_params=_cparams(("parallel",)),
    )(x1, peer_t, g[None, :])


def kernel(x, attn_norm_g, w_in, cmp_pos_k, cmp_w1_k, cmp_b1_k, cmp_w2_k, cmp_pos_v, cmp_w1_v, cmp_b1_v, cmp_w2_v,
           conv_w, conv_b, dt_bias, a_log, d_skip, ssd_norm_g, nsa_norm_g, w_out, ffn_norm_g, peer_w_q, peer_keys,
           peer_u, peer_v, final_norm_g):
    b, s, d = x.shape
    xt = x.reshape(b * s, d)
    assert attn_norm_g.shape[0] == 1, "single-layer block"
    for layer in range(1):
        q, kc, vc, ks, vs, kw, vw, gd, dtt, z, xbc = _in_proj(xt, attn_norm_g[layer], w_in[layer], dt_bias[layer], s)
        kcmp = _compress(kc, cmp_pos_k[layer], cmp_w1_k[layer], cmp_b1_k[layer], cmp_w2_k[layer], b, s)
        vcmp = _compress(vc, cmp_pos_v[layer], cmp_w1_v[layer], cmp_b1_v[layer], cmp_w2_v[layer], b, s)
        o_nsa = _nsa(q, kcmp, vcmp, ks, vs, kw, vw, gd, b, s)
        o_ssd = _ssd(xbc, z, gd, dtt, conv_w[layer], conv_b[layer], a_log[layer], d_skip[layer],
                     ssd_norm_g[layer], b, s)
        x1, xnt = _out_proj(xt, o_nsa, o_ssd, nsa_norm_g[layer], w_out[layer], ffn_norm_g[layer])
        av, lc, bv, rb = _peer_front(xnt, peer_w_q[layer], peer_keys[layer])
        peer_t = _peer_dense(xnt, peer_u[layer].astype(BF16), peer_v[layer].T.astype(BF16), av, lc, bv, rb)
    return _residual_norm(x1, peer_t, final_norm_g).reshape(b, s, d)
```

```python
import functools

import numpy as np
import jax
import jax.numpy as jnp
from jax import lax
from jax.experimental import pallas as pl
from jax.experimental.pallas import tpu as pltpu

F32 = jnp.float32
BF16 = jnp.bfloat16

EPS = 1e-6
D_MODEL = 1024
NSA_HEADS = 8
NSA_KV_GROUPS = 2
NSA_REP = NSA_HEADS // NSA_KV_GROUPS
HEAD_DIM = 64
NSA_WIDTH = NSA_HEADS * HEAD_DIM
KV_WIDTH = NSA_KV_GROUPS * HEAD_DIM
CMP_STRIDE = 16
CMP_BLOCK = 32
CMP_HIDDEN = 256
SEL_BLOCK = 64
SEL_TOP_N = 16
WINDOW = 512
ROPE_DIM = HEAD_DIM // 4
ROPE_THETA = 500000.0
SSD_HEADS = 8
SSD_HEAD_DIM = 64
SSD_WIDTH = SSD_HEADS * SSD_HEAD_DIM
SSD_GROUPS = 2
SSD_REP = SSD_HEADS // SSD_GROUPS
SSD_STATE = 128
SSD_CONV = 4
SSD_CHUNK = 128
SSD_CONV_DIM = SSD_WIDTH + 2 * SSD_GROUPS * SSD_STATE
PEER_HEADS = 8
PEER_N_KEYS = 128
PEER_KEY_DIM = 256
PEER_HALF = PEER_KEY_DIM // 2
PEER_TOPK = 16

LANES = 128
SUBLANES = 8
VMEM_LIMIT = 48 * 1024 * 1024
NEG_BIG = -1e30
SEL_NEG = -1e9

TOK_TILE = 512
Q_TILE = 2 * SEL_BLOCK
K_TILE = 2 * SEL_BLOCK
S_CHUNK = 8 * K_TILE
PEER_TOK = 512
PEER_EXP = 2048
PEER_FRONT_TOK = 512


def _cparams(sem):
    return pltpu.CompilerParams(dimension_semantics=sem, vmem_limit_bytes=VMEM_LIMIT)


def _dot(a, b):
    return jnp.dot(a, b, preferred_element_type=F32)


def _dot_nt(a, b):
    return lax.dot_general(a, b, (((1,), (1,)), ((), ())), preferred_element_type=F32)


def _split3(a):
    a1 = a.astype(BF16)
    r1 = a - a1.astype(F32)
    a2 = r1.astype(BF16)
    a3 = (r1 - a2.astype(F32)).astype(BF16)
    return a1, a2, a3


def _dot_exact_lhs(a, b01):
    a1, a2, a3 = _split3(a)
    return _dot(a1, b01) + _dot(a2, b01) + _dot(a3, b01)


def _dot_nt_exact_lhs(a, b01):
    a1, a2, a3 = _split3(a)
    return _dot_nt(a1, b01) + _dot_nt(a2, b01) + _dot_nt(a3, b01)


def _dot_exact_rhs(a01, b):
    b1, b2, b3 = _split3(b)
    return _dot(a01, b1) + _dot(a01, b2) + _dot(a01, b3)


def _dot_nt_exact_rhs(a01, b):
    b1, b2, b3 = _split3(b)
    return _dot_nt(a01, b1) + _dot_nt(a01, b2) + _dot_nt(a01, b3)


def _softplus(x):
    return jnp.maximum(x, 0.0) + jnp.log(1.0 + jnp.exp(-jnp.abs(x)))


def _sigmoid(x):
    return 1.0 / (1.0 + jnp.exp(-x))


def _gelu_tanh(x):
    return 0.5 * x * (1.0 + jnp.tanh(0.7978845608028654 * (x + 0.044715 * (x * x * x))))


def _gelu_sigmoid(x):
    c1 = -2.0 * 0.7978845608028654 * 1.4426950408889634
    u = x * (c1 + (c1 * 0.044715) * (x * x))
    return x / (1.0 + jnp.exp2(u))


def _rope128(p, cos, sa, sb):
    return p * cos + pltpu.roll(p, LANES - ROPE_DIM // 2, 1) * sa + pltpu.roll(p, ROPE_DIM // 2, 1) * sb


def _in_proj_kernel(seq_tiles, x_ref, g_ref, w_ref, wdt_ref, cos_ref, sa_ref, sb_ref, gdb_ref, dtb_ref,
                    q_ref, kc_ref, vc_ref, ks_ref, vs_ref, kw_ref, vw_ref, gd_ref, dtt_ref, z_ref, xbc_ref):
    tm = x_ref.shape[0]
    x = x_ref[...]
    h = x * lax.rsqrt(jnp.mean(x * x, axis=-1, keepdims=True) + EPS) * g_ref[...]
    hb = h.astype(BF16)
    pos0 = pl.multiple_of((pl.program_id(0) % seq_tiles) * tm, tm)
    cos = cos_ref[pl.ds(pos0, tm), :]
    sa = sa_ref[pl.ds(pos0, tm), :]
    sb = sb_ref[pl.ds(pos0, tm), :]
    lane = lax.broadcasted_iota(jnp.int32, (tm, LANES), 1)
    low = lane < HEAD_DIM

    def proj(c0, width):
        return _dot(hb, w_ref[:, c0:c0 + width])

    def halves(p):
        return p, pltpu.roll(p, HEAD_DIM, 1)

    pq = proj(0, NSA_WIDTH)
    for s in range(NSA_WIDTH // LANES):
        srcs = halves(_rope128(pq[:, s * LANES:(s + 1) * LANES], cos, sa, sb) * (HEAD_DIM ** -0.5))
        for half in range(2):
            head = 2 * s + half
            q_ref[:, head * LANES:(head + 1) * LANES] = jnp.where(low, srcs[half], 0.0).astype(BF16)
    c0 = NSA_WIDTH
    kc_ref[...] = _rope128(proj(c0, LANES), cos, sa, sb).astype(BF16)
    vc_ref[...] = proj(c0 + LANES, LANES).astype(BF16)
    c0 += 2 * LANES
    blk_of_row = (pos0 + lax.broadcasted_iota(jnp.int32, (tm, LANES), 0)) // SEL_BLOCK
    onehot = jnp.where(lane - HEAD_DIM == blk_of_row, 1.0, 0.0)
    for ref, rope, fill in ((ks_ref, True, onehot), (vs_ref, False, 1.0), (kw_ref, True, 0.0), (vw_ref, False, 1.0)):
        p = proj(c0, LANES)
        if rope:
            p = _rope128(p, cos, sa, sb)
        for g, src in enumerate(halves(p)):
            ref[:, g * LANES:(g + 1) * LANES] = jnp.where(low, src, fill).astype(BF16)
        c0 += LANES
    for s in range(2):
        p = proj(c0, LANES) + gdb_ref[:, s * LANES:(s + 1) * LANES]
        act = jnp.where(lane < 3 * NSA_REP, _sigmoid(p),
                        jnp.where((lane >= 16) & (lane < 16 + SSD_REP), _softplus(p), 0.0))
        gd_ref[:, s * LANES:(s + 1) * LANES] = act
        c0 += LANES
    z_ref[...] = proj(c0, SSD_WIDTH).astype(BF16)
    c0 += SSD_WIDTH
    xbc_ref[...] = proj(c0, SSD_CONV_DIM).astype(BF16)
    dtt_ref[...] = _softplus(_dot_nt(wdt_ref[...], hb) + dtb_ref[:, 0:1])


def _rope_tables(s):
    half = ROPE_DIM // 2
    inv = jnp.power(ROPE_THETA, -jnp.arange(half, dtype=F32) * 2.0 / ROPE_DIM)
    ang = jnp.arange(s).astype(F32)[:, None] * inv[None, :]
    cos, sin = jnp.cos(ang), jnp.sin(ang)
    zeros = jnp.zeros((s, HEAD_DIM - ROPE_DIM), F32)
    cos64 = jnp.concatenate([cos, cos, jnp.ones((s, HEAD_DIM - ROPE_DIM), F32)], axis=1)
    sa64 = jnp.concatenate([-sin, jnp.zeros_like(sin), zeros], axis=1)
    sb64 = jnp.concatenate([jnp.zeros_like(sin), sin, zeros], axis=1)
    return tuple(jnp.concatenate([t, t], axis=1) for t in (cos64, sa64, sb64))


def _in_proj(xt, attn_g, w_in, dt_bias, s):
    t = xt.shape[0]
    tm = min(TOK_TILE, s)
    o_gl = NSA_WIDTH + 6 * KV_WIDTH
    o_z = o_gl + 3 * NSA_HEADS
    o_xbc = o_z + SSD_WIDTH
    o_dt = o_xbc + SSD_CONV_DIM
    gd_cols, gd_bias = [], []
    for g in range(NSA_KV_GROUPS):
        gates = w_in[:, o_gl + 3 * NSA_REP * g:o_gl + 3 * NSA_REP * (g + 1)]
        dts = w_in[:, o_dt + SSD_REP * g:o_dt + SSD_REP * (g + 1)]
        gd_cols += [gates, jnp.zeros((D_MODEL, 16 - 3 * NSA_REP), F32), dts,
                    jnp.zeros((D_MODEL, LANES - 16 - SSD_REP), F32)]
        gd_bias += [jnp.zeros((16,), F32), dt_bias[SSD_REP * g:SSD_REP * (g + 1)],
                    jnp.zeros((LANES - 16 - SSD_REP,), F32)]
    w_main = jnp.concatenate([w_in[:, :o_gl]] + gd_cols + [w_in[:, o_z:o_dt]], axis=1).astype(BF16)
    gdb = jnp.concatenate(gd_bias)[None, :]
    wdt_rows, dtb_rows = [], []
    for g in range(SSD_GROUPS):
        wdt_rows += [w_in[:, o_dt + SSD_REP * g:o_dt + SSD_REP * (g + 1)].T,
                     jnp.zeros((SUBLANES - SSD_REP, D_MODEL), F32)]
        dtb_rows += [dt_bias[SSD_REP * g:SSD_REP * (g + 1)], jnp.zeros((SUBLANES - SSD_REP,), F32)]
    wdt = jnp.concatenate(wdt_rows, axis=0).astype(BF16)
    dtb = jnp.broadcast_to(jnp.concatenate(dtb_rows)[:, None], (2 * SUBLANES, LANES))
    cos, sa, sb = _rope_tables(s)
    n_main = w_main.shape[1]

    def full(shape):
        return pl.BlockSpec(shape, lambda i: (0, 0))

    def tok(width):
        return pl.BlockSpec((tm, width), lambda i: (i, 0))

    kv_widths = [LANES, LANES] + [NSA_KV_GROUPS * LANES] * 4
    out_shapes = ([jax.ShapeDtypeStruct((t, NSA_HEADS * LANES), BF16)]
                  + [jax.ShapeDtypeStruct((t, w), BF16) for w in kv_widths]
                  + [jax.ShapeDtypeStruct((t, 2 * LANES), F32),
                     jax.ShapeDtypeStruct((2 * SUBLANES, t), F32),
                     jax.ShapeDtypeStruct((t, SSD_WIDTH), BF16),
                     jax.ShapeDtypeStruct((t, SSD_CONV_DIM), BF16)])
    out_specs = ([tok(NSA_HEADS * LANES)] + [tok(w) for w in kv_widths]
                 + [tok(2 * LANES), pl.BlockSpec((2 * SUBLANES, tm), lambda i: (0, i)),
                    tok(SSD_WIDTH), tok(SSD_CONV_DIM)])
    return pl.pallas_call(
        functools.partial(_in_proj_kernel, s // tm),
        name="in_proj",
        grid=(t // tm,),
        in_specs=[tok(D_MODEL), full((1, D_MODEL)), full((D_MODEL, n_main)), full((2 * SUBLANES, D_MODEL)),
                  full((s, LANES)), full((s, LANES)), full((s, LANES)), full((1, 2 * LANES)),
                  full((2 * SUBLANES, LANES))],
        out_specs=out_specs,
        out_shape=out_shapes,
        compiler_params=_cparams(("parallel",)),
    )(xt, attn_g[None, :], w_main, wdt, cos, sa, sb, gdb, dtb)


def _compress_kernel(kv_ref, w1a_ref, w1b_ref, pos_ref, w1_ref, b1_ref, w2_ref, out_ref):
    kv = kv_ref[0]
    bias = _dot(pos_ref[...], w1_ref[...])[0:1, :] + b1_ref[...]
    for g in range(NSA_KV_GROUPS):
        first = _dot(kv, w1a_ref[g])
        second = _dot(kv, w1b_ref[g])
        nxt = pltpu.roll(second, second.shape[0] - 1, 0)
        hid = _gelu_tanh(first + nxt + bias)
        out_ref[0, :, g * LANES:(g + 1) * LANES] = _dot(hid.astype(BF16), w2_ref[...]).astype(BF16)


def _compress(kv, pos_emb, w1, b1, w2, b, s):
    nch = s // CMP_STRIDE
    kvf = kv.reshape(b, nch, CMP_STRIDE * LANES)
    w1r = w1.reshape(CMP_BLOCK, HEAD_DIM, CMP_HIDDEN)
    zeros = jnp.zeros((CMP_STRIDE, HEAD_DIM, CMP_HIDDEN), F32)

    def expand(w_half, g):
        parts = [w_half, zeros] if g == 0 else [zeros, w_half]
        return jnp.concatenate(parts, axis=1).reshape(CMP_STRIDE * LANES, CMP_HIDDEN)

    w1a = jnp.stack([expand(w1r[:CMP_STRIDE], g) for g in range(NSA_KV_GROUPS)]).astype(BF16)
    w1b = jnp.stack([expand(w1r[CMP_STRIDE:], g) for g in range(NSA_KV_GROUPS)]).astype(BF16)
    w2e = jnp.concatenate([w2, jnp.zeros((CMP_HIDDEN, LANES - HEAD_DIM), F32)], axis=1).astype(BF16)
    pos = jnp.zeros((SUBLANES, CMP_BLOCK * HEAD_DIM), F32).at[0].set(pos_emb.reshape(-1)).astype(BF16)
    return pl.pallas_call(
        _compress_kernel,
        name="compress",
        grid=(b,),
        in_specs=[pl.BlockSpec((1, nch, CMP_STRIDE * LANES), lambda i: (i, 0, 0)),
                  pl.BlockSpec((2, CMP_STRIDE * LANES, CMP_HIDDEN), lambda i: (0, 0, 0)),
                  pl.BlockSpec((2, CMP_STRIDE * LANES, CMP_HIDDEN), lambda i: (0, 0, 0)),
                  pl.BlockSpec((SUBLANES, CMP_BLOCK * HEAD_DIM), lambda i: (0, 0)),
                  pl.BlockSpec((CMP_BLOCK * HEAD_DIM, CMP_HIDDEN), lambda i: (0, 0)),
                  pl.BlockSpec((1, CMP_HIDDEN), lambda i: (0, 0)),
                  pl.BlockSpec((CMP_HIDDEN, LANES), lambda i: (0, 0))],
        out_specs=pl.BlockSpec((1, nch, NSA_KV_GROUPS * LANES), lambda i: (i, 0, 0)),
        out_shape=jax.ShapeDtypeStruct((b, nch, NSA_KV_GROUPS * LANES), BF16),
        compiler_params=_cparams(("parallel",)),
    )(kvf, w1a, w1b, pos, w1.astype(BF16), b1[None, :], w2e)


def _lane_tiles(a):
    return [a[:, k * LANES:(k + 1) * LANES] for k in range(a.shape[1] // LANES)]


def _max_tiles(macc, s):
    for tile in _lane_tiles(s):
        macc = jnp.maximum(macc, tile)
    return macc


def _exp_tiles(s, m_b):
    return jnp.concatenate([jnp.exp(tile - m_b) for tile in _lane_tiles(s)], axis=1).astype(BF16)


def _normalize(acc):
    return acc / pltpu.roll(acc, HEAD_DIM, 1)


def _block_rank(imp):
    nblk, tq = imp.shape
    sub = lax.broadcasted_iota(jnp.int32, (SUBLANES, tq), 0)
    groups = [imp[SUBLANES * v:SUBLANES * (v + 1)] for v in range(nblk // SUBLANES)]
    cnt = [jnp.zeros((SUBLANES, tq), F32) for _ in groups]
    for j in range(nblk):
        row = imp[j:j + 1, :]
        for v, grp in enumerate(groups):
            if SUBLANES * v > j:
                ahead = row >= grp
            elif SUBLANES * (v + 1) <= j:
                ahead = row > grp
            else:
                ahead = (row > grp) | ((row == grp) & (sub > j - SUBLANES * v))
            cnt[v] = cnt[v] + jnp.where(ahead, 1.0, 0.0)
    return jnp.concatenate(cnt, axis=0)


def _nsa_pair_kernel(q_ref, kc_ref, vc_ref, ks_ref, vs_ref, kw_ref, vw_ref, gd_ref, ovl_ref, o_ref, s_ref):
    i = pl.program_id(1)
    tq = Q_TILE
    rows = NSA_REP * tq
    ncmp = kc_ref.shape[1]
    nblk = ovl_ref.shape[0]
    groups = range(NSA_KV_GROUPS)
    t_row = i * tq + lax.broadcasted_iota(jnp.int32, (rows, 1), 0) % tq

    def lanes(g):
        return slice(g * LANES, (g + 1) * LANES)

    qs = [jnp.concatenate([q_ref[:, (g * NSA_REP + r) * LANES:(g * NSA_REP + r + 1) * LANES]
                           for r in range(NSA_REP)], axis=0) for g in groups]

    jd = (i * tq) // K_TILE
    neg = jnp.full((rows, LANES), NEG_BIG, F32)
    n_wk = WINDOW + K_TILE
    k0 = pl.multiple_of(jnp.maximum(jd - WINDOW // K_TILE, 0) * K_TILE, K_TILE)
    diff = t_row - (k0 + lax.broadcasted_iota(jnp.int32, (rows, n_wk), 1))
    in_win = (diff >= 0) & (diff < WINDOW)
    o_wins = []
    for g in groups:
        s_w = jnp.where(in_win, _dot_nt(qs[g], kw_ref[pl.ds(k0, n_wk), lanes(g)]), NEG_BIG)
        m_w = jnp.max(_max_tiles(neg, s_w), axis=-1, keepdims=True)
        p_w = _exp_tiles(s_w, jnp.broadcast_to(m_w, (rows, LANES)))
        o_wins.append(_normalize(_dot(p_w, vw_ref[pl.ds(k0, n_wk), lanes(g)])))

    cmp_end = lax.broadcasted_iota(jnp.int32, (rows, ncmp), 1) * CMP_STRIDE + (CMP_BLOCK - 1)
    mask_c = cmp_end <= t_row
    p_cs, o_cmps = [], []
    for g in groups:
        s_c = jnp.where(mask_c, _dot_nt(qs[g], kc_ref[0, :, lanes(g)]), NEG_BIG)
        e_c = jnp.exp(s_c - jnp.max(s_c, axis=-1, keepdims=True))
        p_c = jnp.where(mask_c, e_c / jnp.sum(e_c, axis=-1, keepdims=True), 0.0)
        p_cs.append(p_c)
        o_cmps.append(_dot(p_c.astype(BF16), vc_ref[0, :, lanes(g)]))

    blk = lax.broadcasted_iota(jnp.int32, (nblk, tq), 0)
    cur = (i * tq + lax.broadcasted_iota(jnp.int32, (nblk, tq), 1)) // SEL_BLOCK
    forced = (blk == 0) | (blk == cur) | (blk == cur - 1)
    valid = blk <= cur
    eye = jnp.where(lax.broadcasted_iota(jnp.int32, (tq, tq), 0) == lax.broadcasted_iota(jnp.int32, (tq, tq), 1),
                    1.0, 0.0).astype(BF16)
    ones_lo = jnp.ones((HEAD_DIM, tq), F32)
    q2s = []
    for g in groups:
        p_sum = p_cs[g][0:tq]
        for r in range(1, NSA_REP):
            p_sum = p_sum + p_cs[g][r * tq:(r + 1) * tq]
        imp = _dot_nt_exact_rhs(ovl_ref[...], p_sum)
        imp = jnp.where(forced, 1e9, jnp.where(valid, imp, -1.0))
        sel_t = jnp.where((_block_rank(imp) < float(min(SEL_TOP_N, nblk))) & valid, 1.0, 0.0)
        pad = [jnp.ones((HEAD_DIM - nblk, tq), F32)] if nblk < HEAD_DIM else []
        sel_pad = jnp.concatenate([ones_lo, sel_t] + pad, axis=0).astype(BF16)
        negb = ((_dot_nt(eye, sel_pad) - 1.0) * (-SEL_NEG)).astype(BF16)
        q2s.append(qs[g] + jnp.concatenate([negb] * NSA_REP, axis=0))

    nfull = jd // (S_CHUNK // K_TILE)
    lane_c = lax.broadcasted_iota(jnp.int32, (rows, S_CHUNK), 1)

    def scores(g, c):
        k0 = pl.multiple_of(c * S_CHUNK, S_CHUNK)
        return _dot_nt(q2s[g], ks_ref[pl.ds(k0, S_CHUNK), lanes(g)])

    def pass_max(c, maccs):
        out = []
        for g in groups:
            s = scores(g, c)
            s_ref[g, c] = s
            out.append(_max_tiles(maccs[g], s))
        return tuple(out)

    maccs = lax.fori_loop(0, nfull, pass_max, (neg,) * NSA_KV_GROUPS)
    causal = nfull * S_CHUNK + lane_c <= t_row
    s_lasts = [jnp.where(causal, scores(g, nfull), NEG_BIG) for g in groups]
    m_bs = [jnp.broadcast_to(jnp.max(_max_tiles(maccs[g], s_lasts[g]), axis=-1, keepdims=True), (rows, LANES))
            for g in groups]

    def pass_sum(c, accs):
        k0 = pl.multiple_of(c * S_CHUNK, S_CHUNK)
        return tuple(accs[g] + _dot(_exp_tiles(s_ref[g, c], m_bs[g]), vs_ref[pl.ds(k0, S_CHUNK), lanes(g)])
                     for g in groups)

    zero = jnp.zeros((rows, LANES), F32)
    accs = lax.fori_loop(0, nfull, pass_sum, (zero,) * NSA_KV_GROUPS)
    k_last = pl.multiple_of(nfull * S_CHUNK, S_CHUNK)
    o_sels = [_normalize(accs[g] + _dot(_exp_tiles(s_lasts[g], m_bs[g]), vs_ref[pl.ds(k_last, S_CHUNK), lanes(g)]))
              for g in groups]

    low = lax.broadcasted_iota(jnp.int32, (tq, LANES), 1) < HEAD_DIM
    for g in groups:
        gd = gd_ref[:, lanes(g)]
        heads = []
        for r in range(NSA_REP):
            rs = slice(r * tq, (r + 1) * tq)
            heads.append(gd[:, 3 * r:3 * r + 1] * o_cmps[g][rs] + gd[:, 3 * r + 1:3 * r + 2] * o_sels[g][rs]
                         + gd[:, 3 * r + 2:3 * r + 3] * o_wins[g][rs])
        for sidx in range(NSA_REP // 2):
            slab = g * (NSA_REP // 2) + sidx
            o_ref[:, slab * LANES:(slab + 1) * LANES] = jnp.where(
                low, heads[2 * sidx], pltpu.roll(heads[2 * sidx + 1], HEAD_DIM, 1))


def _nsa(q, kcmp, vcmp, ks, vs, kw, vw, gd, b, s):
    t = q.shape[0]
    nq = s // Q_TILE
    ncmp = s // CMP_STRIDE
    nblk = s // SEL_BLOCK
    cs = np.arange(ncmp) * CMP_STRIDE
    ss = np.arange(nblk) * SEL_BLOCK
    overlap = (cs[:, None] < ss[None, :] + SEL_BLOCK) & (cs[:, None] + CMP_BLOCK > ss[None, :])
    overlap[ncmp - 1, :] = False
    ovl_t = jnp.asarray(overlap.T, BF16)

    assert nblk <= HEAD_DIM and s % S_CHUNK == 0 and s >= WINDOW + K_TILE

    ng = NSA_KV_GROUPS

    def seq_spec():
        return pl.BlockSpec((s, ng * LANES), lambda bi, i: (bi, 0))

    return pl.pallas_call(
        _nsa_pair_kernel,
        name="nsa",
        grid=(b, nq),
        in_specs=[pl.BlockSpec((Q_TILE, NSA_HEADS * LANES), lambda bi, i: (bi * nq + i, 0)),
                  pl.BlockSpec((1, ncmp, ng * LANES), lambda bi, i: (bi, 0, 0)),
                  pl.BlockSpec((1, ncmp, ng * LANES), lambda bi, i: (bi, 0, 0)),
                  seq_spec(), seq_spec(), seq_spec(), seq_spec(),
                  pl.BlockSpec((Q_TILE, ng * LANES), lambda bi, i: (bi * nq + i, 0)),
                  pl.BlockSpec((nblk, ncmp), lambda bi, i: (0, 0))],
        out_specs=pl.BlockSpec((Q_TILE, NSA_WIDTH), lambda bi, i: (bi * nq + i, 0)),
        out_shape=jax.ShapeDtypeStruct((t, NSA_WIDTH), F32),
        scratch_shapes=[pltpu.VMEM((ng, s // S_CHUNK, NSA_REP * Q_TILE, S_CHUNK), F32)],
        compiler_params=_cparams(("parallel", "arbitrary")),
    )(q, kcmp, vcmp, ks, vs, kw, vw, gd, ovl_t)


def _conv_silu(cur_ref, ext_ref, tail_ref, w_ref, b_ref):
    l = cur_ref.shape[0]
    cur = cur_ref[...].astype(F32)
    ext_ref[0:SUBLANES, :] = tail_ref[...]
    ext_ref[SUBLANES:, :] = cur
    tail_ref[...] = cur[l - SUBLANES:, :]
    acc = jnp.zeros(cur.shape, F32) + b_ref[...]
    for k in range(SSD_CONV):
        off = SUBLANES - (SSD_CONV - 1) + k
        acc = acc + ext_ref[off:off + l, :] * w_ref[k:k + 1, :]
    return acc * _sigmoid(acc)


def _ssd_kernel(xbc_ref, w_ref, b_ref, z_ref, gd_ref, dtt_ref, alane_ref, asub_ref, dskip_ref, ng_ref,
                tril_ref, edt_ref, eseg_ref, o_ref, tail_ref, ext_ref, st_ref):
    c = pl.program_id(1)
    l = xbc_ref.shape[0]
    gw = SSD_WIDTH // SSD_GROUPS
    b_col0 = SSD_WIDTH
    c_col0 = SSD_WIDTH + SSD_GROUPS * SSD_STATE

    @pl.when(c == 0)
    def _():
        tail_ref[...] = jnp.zeros(tail_ref.shape, F32)
        st_ref[...] = jnp.zeros(st_ref.shape, F32)

    conv = _conv_silu(xbc_ref, ext_ref, tail_ref, w_ref, b_ref)
    tril = tril_ref[...]
    causal = lax.broadcasted_iota(jnp.int32, (l, l), 0) >= lax.broadcasted_iota(jnp.int32, (l, l), 1)
    head_of_lane = lax.broadcasted_iota(jnp.int32, (l, gw), 1) // SSD_HEAD_DIM
    for g in range(SSD_GROUPS):
        xs = conv[:, g * gw:(g + 1) * gw]
        bm = conv[:, b_col0 + g * SSD_STATE:b_col0 + (g + 1) * SSD_STATE]
        cm = conv[:, c_col0 + g * SSD_STATE:c_col0 + (g + 1) * SSD_STATE]
        gd = gd_ref[:, g * LANES:(g + 1) * LANES]
        adt_c = gd * (-jnp.exp(alane_ref[g]))
        acum_c = _dot_exact_rhs(tril, adt_c)
        dt_full = _dot_exact_lhs(gd, edt_ref[...])
        ac_full = _dot_exact_lhs(acum_c, edt_ref[...])
        ac_seg = _dot_exact_lhs(acum_c, eseg_ref[...])
        adt_t = dtt_ref[g * SUBLANES:(g + 1) * SUBLANES, :] * (-jnp.exp(asub_ref[g]))
        acum_t = _dot_nt_exact_lhs(adt_t, tril)

        xdt = xs * dt_full
        cmb = cm.astype(BF16)
        cb = _dot_nt(cmb, bm.astype(BF16))
        y = jnp.zeros(xs.shape, F32)
        for r in range(SSD_REP):
            seg = jnp.exp(jnp.where(causal, ac_seg[:, r * l:(r + 1) * l] - acum_t[r:r + 1, :], NEG_BIG))
            x_r = jnp.where(head_of_lane == r, xdt, 0.0).astype(BF16)
            y = y + _dot((cb * seg).astype(BF16), x_r)
        ac_last = ac_full[l - 1:l, :]
        state = st_ref[g]
        y = y + _dot(cmb, state.astype(BF16)) * jnp.exp(ac_full)
        decayed = (xdt * jnp.exp(ac_last - ac_full)).astype(BF16)
        st_ref[g] = state * jnp.exp(ac_last) + _dot(bm.T.astype(BF16), decayed)
        y = y + xs * dskip_ref[:, g * gw:(g + 1) * gw]
        z = z_ref[:, g * gw:(g + 1) * gw].astype(F32)
        yz = y * (z * _sigmoid(z))
        o_ref[:, g * gw:(g + 1) * gw] = (yz * lax.rsqrt(jnp.mean(yz * yz, axis=-1, keepdims=True) + EPS)
                                        * ng_ref[:, g * gw:(g + 1) * gw])


def _ssd(xbc, z, gd, dtt, conv_w, conv_b, a_log, d_skip, norm_g, b, s):
    t = xbc.shape[0]
    l = SSD_CHUNK
    nc = s // l
    gw = SSD_WIDTH // SSD_GROUPS
    xcols = gw // LANES
    b_blk0 = SSD_WIDTH // LANES
    c_blk0 = b_blk0 + SSD_GROUPS * SSD_STATE // LANES
    conv_b2 = conv_b[None, :]
    a_grp = a_log.reshape(SSD_GROUPS, SSD_REP)
    alane = jnp.zeros((SSD_GROUPS, 1, LANES), F32).at[:, 0, 16:16 + SSD_REP].set(a_grp)
    asub = jnp.zeros((SSD_GROUPS, SUBLANES, LANES), F32).at[:, :SSD_REP, :].set(
        jnp.broadcast_to(a_grp[:, :, None], (SSD_GROUPS, SSD_REP, LANES)))
    dskip = jnp.repeat(d_skip, SSD_HEAD_DIM)[None, :]
    ng = norm_g[None, :]
    ii = np.arange(l)
    tril = jnp.asarray(ii[:, None] >= ii[None, :], BF16)
    edt = np.zeros((LANES, gw), np.float32)
    eseg = np.zeros((LANES, SSD_REP * l), np.float32)
    for r in range(SSD_REP):
        edt[16 + r, r * SSD_HEAD_DIM:(r + 1) * SSD_HEAD_DIM] = 1.0
        eseg[16 + r, r * l:(r + 1) * l] = 1.0
    edt, eseg = jnp.asarray(edt, BF16), jnp.asarray(eseg, BF16)

    def rowblk(width):
        return pl.BlockSpec((l, width), lambda bi, c: (bi * nc + c, 0))

    def const(shape):
        return pl.BlockSpec(shape, lambda bi, c: (0,) * len(shape))

    return pl.pallas_call(
        _ssd_kernel,
        name="ssd",
        grid=(b, nc),
        in_specs=[rowblk(SSD_CONV_DIM), const((SSD_CONV, SSD_CONV_DIM)), const((1, SSD_CONV_DIM)),
                  rowblk(SSD_WIDTH), rowblk(SSD_GROUPS * LANES),
                  pl.BlockSpec((SSD_GROUPS * SUBLANES, l), lambda bi, c: (0, bi * nc + c)),
                  const((SSD_GROUPS, 1, LANES)), const((SSD_GROUPS, SUBLANES, LANES)),
                  const((1, SSD_WIDTH)), const((1, SSD_WIDTH)),
                  const((l, l)), const((LANES, gw)), const((LANES, SSD_REP * l))],
        out_specs=rowblk(SSD_WIDTH),
        out_shape=jax.ShapeDtypeStruct((t, SSD_WIDTH), F32),
        scratch_shapes=[pltpu.VMEM((SUBLANES, SSD_CONV_DIM), F32), pltpu.VMEM((l + SUBLANES, SSD_CONV_DIM), F32),
                        pltpu.VMEM((SSD_GROUPS, SSD_STATE, gw), F32)],
        compiler_params=_cparams(("parallel", "arbitrary")),
    )(xbc, conv_w, conv_b2, z, gd, dtt, alane, asub, dskip, ng, tril, edt, eseg)


def _rms(v, g):
    return v * lax.rsqrt(jnp.mean(v * v, axis=-1, keepdims=True) + EPS) * g


def _out_proj_kernel(x_ref, on_ref, os_ref, ng_ref, w1_ref, w2_ref, fg_ref, x1_ref, xnt_ref):
    onn = _rms(on_ref[...], ng_ref[...]).astype(BF16)
    x1 = x_ref[...] + _dot(onn, w1_ref[...]) + _dot(os_ref[...].astype(BF16), w2_ref[...])
    x1_ref[...] = x1
    xnt_ref[...] = _rms(x1, fg_ref[...]).T.astype(BF16)


def _out_proj(xt, o_nsa, o_ssd, nsa_g, w_out, ffn_g):
    t = xt.shape[0]
    tm = min(TOK_TILE, t)

    def tok(width):
        return pl.BlockSpec((tm, width), lambda i: (i, 0))

    def full(shape):
        return pl.BlockSpec(shape, lambda i: (0, 0))

    wb = w_out.astype(BF16)
    return pl.pallas_call(
        _out_proj_kernel,
        name="out_proj",
        grid=(t // tm,),
        in_specs=[tok(D_MODEL), tok(NSA_WIDTH), tok(SSD_WIDTH), full((1, NSA_WIDTH)),
                  full((NSA_WIDTH, D_MODEL)), full((SSD_WIDTH, D_MODEL)), full((1, D_MODEL))],
        out_specs=[tok(D_MODEL), pl.BlockSpec((D_MODEL, tm), lambda i: (0, i))],
        out_shape=[jax.ShapeDtypeStruct((t, D_MODEL), F32), jax.ShapeDtypeStruct((D_MODEL, t), BF16)],
        compiler_params=_cparams(("parallel",)),
    )(xt, o_nsa, o_ssd, nsa_g[None, :], wb[:NSA_WIDTH], wb[NSA_WIDTH:], ffn_g[None, :])


def _top_sorted(e, k):
    rows = lax.broadcasted_iota(jnp.int32, (k, e.shape[1]), 0)
    out = jnp.zeros((k, e.shape[1]), F32)
    cur = e
    for j in range(k):
        mk = jnp.max(cur, axis=0, keepdims=True)
        out = jnp.where(rows == j, jnp.maximum(mk, 0.0), out)
        target = jnp.where(mk >= 0.0, mk, 2.0)
        cur = jnp.where(cur == target, -float(j + 1), cur)
    rank = jnp.where(cur < 0.0, -1.0 - cur, float(k))
    return out, rank


def _pair_candidates(a16, b16):
    row8 = lax.broadcasted_iota(jnp.int32, (SUBLANES, a16.shape[1]), 0)
    groups = [a16[0:1] * b16[0:SUBLANES], a16[0:1] * b16[SUBLANES:], a16[1:2] * b16[0:SUBLANES]]
    for a in range(2, SUBLANES):
        groups.append(jnp.where(row8 < PEER_TOPK // (a + 1), a16[a:a + 1] * b16[0:SUBLANES], -1.0))
    groups.append(a16[SUBLANES:] * b16[0:1])
    return jnp.concatenate(groups, axis=0)


def _peer_front_kernel(xnt_ref, wq_ref, keys_ref, av_ref, lc_ref, bv_ref, rb_ref):
    xt = xnt_ref[...]
    tm = xt.shape[1]
    row8 = lax.broadcasted_iota(jnp.int32, (SUBLANES, tm), 0)
    for h in range(PEER_HEADS):
        qh = _dot(wq_ref[h * PEER_KEY_DIM:(h + 1) * PEER_KEY_DIM, :], xt)
        s1 = _dot(keys_ref[h, 0], qh[:PEER_HALF].astype(BF16))
        s2 = _dot(keys_ref[h, 1], qh[PEER_HALF:].astype(BF16))
        e1 = jnp.exp(s1 - jnp.max(s1, axis=0, keepdims=True))
        e2 = jnp.exp(s2 - jnp.max(s2, axis=0, keepdims=True))
        a16, ra = _top_sorted(e1, PEER_TOPK)
        b16, rb = _top_sorted(e2, PEER_TOPK)
        cand = _pair_candidates(a16, b16)
        cur = cand
        zsum = jnp.zeros((1, tm), F32)
        for _ in range(PEER_TOPK):
            tau = jnp.maximum(jnp.max(cur, axis=0, keepdims=True), 0.0)
            zsum = zsum + tau
            cur = jnp.where(cur == tau, -1.0, cur)
        picked = jnp.where(cand >= tau, 1.0, 0.0)

        def count(g):
            return jnp.sum(picked[g * SUBLANES:(g + 1) * SUBLANES], axis=0, keepdims=True)

        lens = [count(0) + count(1)] + [count(a + 1) for a in range(1, SUBLANES)]
        len_lo = jnp.zeros((SUBLANES, tm), F32)
        for a, row in enumerate(lens):
            len_lo = jnp.where(row8 == a, row, len_lo)
        len16 = jnp.concatenate([len_lo, picked[(SUBLANES + 1) * SUBLANES:]], axis=0)
        pack = 2 * SUBLANES
        ra_b = ra.astype(BF16)
        lc = jnp.zeros(ra.shape, BF16)
        for a in range(PEER_TOPK):
            len_a = jnp.tile(jnp.broadcast_to(len16[a:a + 1], (pack, tm)).astype(BF16), (PEER_N_KEYS // pack, 1))
            lc = jnp.where(ra_b == float(a), len_a, lc)
        rows = slice(h * PEER_N_KEYS, (h + 1) * PEER_N_KEYS)
        av_ref[rows, :] = e1
        lc_ref[rows, :] = lc.astype(F32)
        bv_ref[rows, :] = (e2 * (1.0 / zsum)).astype(BF16)
        rb_ref[rows, :] = rb.astype(BF16)


def _peer_front(xnt, w_q, keys):
    t = xnt.shape[1]
    tm = min(PEER_FRONT_TOK, t)
    nk = PEER_HEADS * PEER_N_KEYS
    return pl.pallas_call(
        _peer_front_kernel,
        name="peer_front",
        grid=(t // tm,),
        in_specs=[pl.BlockSpec((D_MODEL, tm), lambda i: (0, i)),
                  pl.BlockSpec((PEER_HEADS * PEER_KEY_DIM, D_MODEL), lambda i: (0, 0)),
                  pl.BlockSpec((PEER_HEADS, 2, PEER_N_KEYS, PEER_HALF), lambda i: (0, 0, 0, 0))],
        out_specs=[pl.BlockSpec((nk, tm), lambda i: (0, i))] * 4,
        out_shape=[jax.ShapeDtypeStruct((nk, t), F32), jax.ShapeDtypeStruct((nk, t), F32),
                   jax.ShapeDtypeStruct((nk, t), BF16), jax.ShapeDtypeStruct((nk, t), BF16)],
        compiler_params=_cparams(("parallel",)),
    )(xnt, w_q.T.astype(BF16), keys.astype(BF16))


def _row_bf16(ref, row, n_rows):
    pack = 2 * SUBLANES
    tile = jnp.broadcast_to(ref[pl.ds(row, 1), :], (pack, ref.shape[1])).astype(BF16)
    return jnp.tile(tile, (n_rows // pack, 1))


def _peer_dense_kernel(xnt_ref, u_ref, vt_ref, av_ref, lc_ref, bv_ref, rb_ref, o_ref, act_ref):
    j = pl.program_id(1)
    n_sub = u_ref.shape[0] // PEER_N_KEYS

    @pl.when(j == 0)
    def _():
        o_ref[...] = jnp.zeros(o_ref.shape, F32)

    def gate_of(cl):
        e1 = j * n_sub + cl
        gate = jnp.zeros((PEER_N_KEYS, xnt_ref.shape[1]), BF16)
        for h in range(PEER_HEADS):
            rows = slice(h * PEER_N_KEYS, (h + 1) * PEER_N_KEYS)
            lc_row = _row_bf16(lc_ref, h * PEER_N_KEYS + e1, PEER_N_KEYS)
            av_row = _row_bf16(av_ref, h * PEER_N_KEYS + e1, PEER_N_KEYS)
            bv = bv_ref[rows, :]
            gate = gate + jnp.where(rb_ref[rows, :] < lc_row, bv, jnp.zeros_like(bv)) * av_row
        return gate

    per_piece = 4
    piece = per_piece * PEER_N_KEYS
    pack = 2 * SUBLANES
    for p in range(u_ref.shape[0] // piece):
        gates = [gate_of(per_piece * p + k) for k in range(per_piece)]
        zero = jnp.tile(gates[0][0:pack, 0:LANES] * 0.0, (piece // pack, u_ref.shape[1] // LANES))
        hid = _dot(u_ref[p * piece:(p + 1) * piece, :] + zero, xnt_ref[...])
        for k in range(per_piece):
            r0 = p * piece + k * PEER_N_KEYS
            act_ref[r0:r0 + PEER_N_KEYS, :] = (
                _gelu_sigmoid(hid[k * PEER_N_KEYS:(k + 1) * PEER_N_KEYS].astype(BF16)) * gates[k])
    o_ref[...] += _dot(vt_ref[...], act_ref[...])


def _peer_dense(xnt, u, vt, av, lc, bv, rb):
    t = xnt.shape[1]
    tm = min(PEER_TOK, t)
    ne = u.shape[0]
    te = PEER_EXP
    tok = pl.BlockSpec((PEER_HEADS * PEER_N_KEYS, tm), lambda i, j: (0, i))
    return pl.pallas_call(
        _peer_dense_kernel,
        name="peer_dense",
        grid=(t // tm, ne // te),
        in_specs=[pl.BlockSpec((D_MODEL, tm), lambda i, j: (0, i)),
                  pl.BlockSpec((te, D_MODEL), lambda i, j: (j, 0)),
                  pl.BlockSpec((D_MODEL, te), lambda i, j: (0, j)),
                  tok, tok, tok, tok],
        out_specs=pl.BlockSpec((D_MODEL, tm), lambda i, j: (0, i)),
        out_shape=jax.ShapeDtypeStruct((D_MODEL, t), F32),
        scratch_shapes=[pltpu.VMEM((te, tm), BF16)],
        compiler_params=_cparams(("parallel", "arbitrary")),
    )(xnt, u, vt, av, lc, bv, rb)


def _final_kernel(x1_ref, pt_ref, g_ref, o_ref):
    o_ref[...] = _rms(x1_ref[...] + pt_ref[...].T, g_ref[...])


def _residual_norm(x1, peer_t, g):
    t = x1.shape[0]
    tm = min(TOK_TILE, t)
    return pl.pallas_call(
        _final_kernel,
        name="residual_norm",
        grid=(t // tm,),
        in_specs=[pl.BlockSpec((tm, D_MODEL), lambda i: (i, 0)), pl.BlockSpec((D_MODEL, tm), lambda i: (0, i)),
                  pl.BlockSpec((1, D_MODEL), lambda i: (0, 0))],
        out_specs=pl.BlockSpec((tm, D_MODEL), lambda i: (i, 0)),
        out_shape=jax.ShapeDtypeStruct((t, D_MODEL), F32),
        compiler_params=_cparams(("parallel",)),
    )(x1, peer_t, g[None, :])


def kernel(x, attn_norm_g, w_in, cmp_pos_k, cmp_w1_k, cmp_b1_k, cmp_w2_k, cmp_pos_v, cmp_w1_v, cmp_b1_v, cmp_w2_v,
           conv_w, conv_b, dt_bias, a_log, d_skip, ssd_norm_g, nsa_norm_g, w_out, ffn_norm_g, peer_w_q, peer_keys,
           peer_u, peer_v, final_norm_g):
    b, s, d = x.shape
    xt = x.reshape(b * s, d)
    assert attn_norm_g.shape[0] == 1, "single-layer block"
    for layer in range(1):
        q, kc, vc, ks, vs, kw, vw, gd, dtt, z, xbc = _in_proj(xt, attn_norm_g[layer], w_in[layer], dt_bias[layer], s)
        kcmp = _compress(kc, cmp_pos_k[layer], cmp_w1_k[layer], cmp_b1_k[layer], cmp_w2_k[layer], b, s)
        vcmp = _compress(vc, cmp_pos_v[layer], cmp_w1_v[layer], cmp_b1_v[layer], cmp_w2_v[layer], b, s)
        o_nsa = _nsa(q, kcmp, vcmp, ks, vs, kw, vw, gd, b, s)
        o_ssd = _ssd(xbc, z, gd, dtt, conv_w[layer], conv_b[layer], a_log[layer], d_skip[layer],
                     ssd_norm_g[layer], b, s)
        x1, xnt = _out_proj(xt, o_nsa, o_ssd, nsa_norm_g[layer], w_out[layer], ffn_norm_g[layer])
        av, lc, bv, rb = _peer_front(xnt, peer_w_q[layer], peer_keys[layer])
        peer_t = _peer_dense(xnt, peer_u[layer].astype(BF16), peer_v[layer].T.astype(BF16), av, lc, bv, rb)
    return _residual_norm(x1, peer_t, final_norm_g).reshape(b, s, d)
```

```python
import functools

import numpy as np
import jax
import jax.numpy as jnp
from jax import lax
from jax.experimental import pallas as pl
from jax.experimental.pallas import tpu as pltpu

F32 = jnp.float32
BF16 = jnp.bfloat16

EPS = 1e-6
D_MODEL = 1024
NSA_HEADS = 8
NSA_KV_GROUPS = 2
NSA_REP = NSA_HEADS // NSA_KV_GROUPS
HEAD_DIM = 64
NSA_WIDTH = NSA_HEADS * HEAD_DIM
KV_WIDTH = NSA_KV_GROUPS * HEAD_DIM
CMP_STRIDE = 16
CMP_BLOCK = 32
CMP_HIDDEN = 256
SEL_BLOCK = 64
SEL_TOP_N = 16
WINDOW = 512
ROPE_DIM = HEAD_DIM // 4
ROPE_THETA = 500000.0
SSD_HEADS = 8
SSD_HEAD_DIM = 64
SSD_WIDTH = SSD_HEADS * SSD_HEAD_DIM
SSD_GROUPS = 2
SSD_REP = SSD_HEADS // SSD_GROUPS
SSD_STATE = 128
SSD_CONV = 4
SSD_CHUNK = 128
SSD_CONV_DIM = SSD_WIDTH + 2 * SSD_GROUPS * SSD_STATE
PEER_HEADS = 8
PEER_N_KEYS = 128
PEER_KEY_DIM = 256
PEER_HALF = PEER_KEY_DIM // 2
PEER_TOPK = 16

LANES = 128
SUBLANES = 8
VMEM_LIMIT = 48 * 1024 * 1024
NEG_BIG = -1e30
SEL_NEG = -1e9

TOK_TILE = 512
Q_TILE = 2 * SEL_BLOCK
K_TILE = 2 * SEL_BLOCK
S_CHUNK = 8 * K_TILE
PEER_TOK = 512
PEER_EXP = 2048
PEER_FRONT_TOK = 512


def _cparams(sem):
    return pltpu.CompilerParams(dimension_semantics=sem, vmem_limit_bytes=VMEM_LIMIT)


def _dot(a, b):
    return jnp.dot(a, b, preferred_element_type=F32)


def _dot_nt(a, b):
    return lax.dot_general(a, b, (((1,), (1,)), ((), ())), preferred_element_type=F32)


def _split3(a):
    a1 = a.astype(BF16)
    r1 = a - a1.astype(F32)
    a2 = r1.astype(BF16)
    a3 = (r1 - a2.astype(F32)).astype(BF16)
    return a1, a2, a3


def _dot_exact_lhs(a, b01):
    a1, a2, a3 = _split3(a)
    return _dot(a1, b01) + _dot(a2, b01) + _dot(a3, b01)


def _dot_nt_exact_lhs(a, b01):
    a1, a2, a3 = _split3(a)
    return _dot_nt(a1, b01) + _dot_nt(a2, b01) + _dot_nt(a3, b01)


def _dot_exact_rhs(a01, b):
    b1, b2, b3 = _split3(b)
    return _dot(a01, b1) + _dot(a01, b2) + _dot(a01, b3)


def _dot_nt_exact_rhs(a01, b):
    b1, b2, b3 = _split3(b)
    return _dot_nt(a01, b1) + _dot_nt(a01, b2) + _dot_nt(a01, b3)


def _softplus(x):
    return jnp.maximum(x, 0.0) + jnp.log(1.0 + jnp.exp(-jnp.abs(x)))


def _sigmoid(x):
    return 1.0 / (1.0 + jnp.exp(-x))


def _gelu_tanh(x):
    return 0.5 * x * (1.0 + jnp.tanh(0.7978845608028654 * (x + 0.044715 * (x * x * x))))


def _gelu_sigmoid(x):
    c1 = -2.0 * 0.7978845608028654 * 1.4426950408889634
    u = x * (c1 + (c1 * 0.044715) * (x * x))
    return x / (1.0 + jnp.exp2(u))


def _rope128(p, cos, sa, sb):
    return p * cos + pltpu.roll(p, LANES - ROPE_DIM // 2, 1) * sa + pltpu.roll(p, ROPE_DIM // 2, 1) * sb


def _in_proj_kernel(seq_tiles, x_ref, g_ref, w_ref, wdt_ref, cos_ref, sa_ref, sb_ref, gdb_ref, dtb_ref,
                    q_ref, kc_ref, vc_ref, ks_ref, vs_ref, kw_ref, vw_ref, gd_ref, dtt_ref, z_ref, xbc_ref):
    tm = x_ref.shape[0]
    x = x_ref[...]
    h = x * lax.rsqrt(jnp.mean(x * x, axis=-1, keepdims=True) + EPS) * g_ref[...]
    hb = h.astype(BF16)
    pos0 = pl.multiple_of((pl.program_id(0) % seq_tiles) * tm, tm)
    cos = cos_ref[pl.ds(pos0, tm), :]
    sa = sa_ref[pl.ds(pos0, tm), :]
    sb = sb_ref[pl.ds(pos0, tm), :]
    lane = lax.broadcasted_iota(jnp.int32, (tm, LANES), 1)
    low = lane < HEAD_DIM

    def proj(c0, width):
        return _dot(hb, w_ref[:, c0:c0 + width])

    def halves(p):
        return p, pltpu.roll(p, HEAD_DIM, 1)

    pq = proj(0, NSA_WIDTH)
    for s in range(NSA_WIDTH // LANES):
        srcs = halves(_rope128(pq[:, s * LANES:(s + 1) * LANES], cos, sa, sb) * (HEAD_DIM ** -0.5))
        for half in range(2):
            head = 2 * s + half
            q_ref[:, head * LANES:(head + 1) * LANES] = jnp.where(low, srcs[half], 0.0).astype(BF16)
    c0 = NSA_WIDTH
    kc_ref[...] = _rope128(proj(c0, LANES), cos, sa, sb).astype(BF16)
    vc_ref[...] = proj(c0 + LANES, LANES).astype(BF16)
    c0 += 2 * LANES
    blk_of_row = (pos0 + lax.broadcasted_iota(jnp.int32, (tm, LANES), 0)) // SEL_BLOCK
    onehot = jnp.where(lane - HEAD_DIM == blk_of_row, 1.0, 0.0)
    for ref, rope, fill in ((ks_ref, True, onehot), (vs_ref, False, 1.0), (kw_ref, True, 0.0), (vw_ref, False, 1.0)):
        p = proj(c0, LANES)
        if rope:
            p = _rope128(p, cos, sa, sb)
        for g, src in enumerate(halves(p)):
            ref[:, g * LANES:(g + 1) * LANES] = jnp.where(low, src, fill).astype(BF16)
        c0 += LANES
    for s in range(2):
        p = proj(c0, LANES) + gdb_ref[:, s * LANES:(s + 1) * LANES]
        act = jnp.where(lane < 3 * NSA_REP, _sigmoid(p),
                        jnp.where((lane >= 16) & (lane < 16 + SSD_REP), _softplus(p), 0.0))
        gd_ref[:, s * LANES:(s + 1) * LANES] = act
        c0 += LANES
    z_ref[...] = proj(c0, SSD_WIDTH)
    c0 += SSD_WIDTH
    xbc_ref[...] = proj(c0, SSD_CONV_DIM)
    dtt_ref[...] = _softplus(_dot_nt(wdt_ref[...], hb) + dtb_ref[:, 0:1])


def _rope_tables(s):
    half = ROPE_DIM // 2
    inv = jnp.power(ROPE_THETA, -jnp.arange(half, dtype=F32) * 2.0 / ROPE_DIM)
    ang = jnp.arange(s).astype(F32)[:, None] * inv[None, :]
    cos, sin = jnp.cos(ang), jnp.sin(ang)
    zeros = jnp.zeros((s, HEAD_DIM - ROPE_DIM), F32)
    cos64 = jnp.concatenate([cos, cos, jnp.ones((s, HEAD_DIM - ROPE_DIM), F32)], axis=1)
    sa64 = jnp.concatenate([-sin, jnp.zeros_like(sin), zeros], axis=1)
    sb64 = jnp.concatenate([jnp.zeros_like(sin), sin, zeros], axis=1)
    return tuple(jnp.concatenate([t, t], axis=1) for t in (cos64, sa64, sb64))


def _in_proj(xt, attn_g, w_in, dt_bias, s):
    t = xt.shape[0]
    tm = min(TOK_TILE, s)
    o_gl = NSA_WIDTH + 6 * KV_WIDTH
    o_z = o_gl + 3 * NSA_HEADS
    o_xbc = o_z + SSD_WIDTH
    o_dt = o_xbc + SSD_CONV_DIM
    gd_cols, gd_bias = [], []
    for g in range(NSA_KV_GROUPS):
        gates = w_in[:, o_gl + 3 * NSA_REP * g:o_gl + 3 * NSA_REP * (g + 1)]
        dts = w_in[:, o_dt + SSD_REP * g:o_dt + SSD_REP * (g + 1)]
        gd_cols += [gates, jnp.zeros((D_MODEL, 16 - 3 * NSA_REP), F32), dts,
                    jnp.zeros((D_MODEL, LANES - 16 - SSD_REP), F32)]
        gd_bias += [jnp.zeros((16,), F32), dt_bias[SSD_REP * g:SSD_REP * (g + 1)],
                    jnp.zeros((LANES - 16 - SSD_REP,), F32)]
    w_main = jnp.concatenate([w_in[:, :o_gl]] + gd_cols + [w_in[:, o_z:o_dt]], axis=1).astype(BF16)
    gdb = jnp.concatenate(gd_bias)[None, :]
    wdt_rows, dtb_rows = [], []
    for g in range(SSD_GROUPS):
        wdt_rows += [w_in[:, o_dt + SSD_REP * g:o_dt + SSD_REP * (g + 1)].T,
                     jnp.zeros((SUBLANES - SSD_REP, D_MODEL), F32)]
        dtb_rows += [dt_bias[SSD_REP * g:SSD_REP * (g + 1)], jnp.zeros((SUBLANES - SSD_REP,), F32)]
    wdt = jnp.concatenate(wdt_rows, axis=0).astype(BF16)
    dtb = jnp.broadcast_to(jnp.concatenate(dtb_rows)[:, None], (2 * SUBLANES, LANES))
    cos, sa, sb = _rope_tables(s)
    n_main = w_main.shape[1]

    def full(shape):
        return pl.BlockSpec(shape, lambda i: (0, 0))

    def tok(width):
        return pl.BlockSpec((tm, width), lambda i: (i, 0))

    kv_widths = [LANES, LANES] + [NSA_KV_GROUPS * LANES] * 4
    out_shapes = ([jax.ShapeDtypeStruct((t, NSA_HEADS * LANES), BF16)]
                  + [jax.ShapeDtypeStruct((t, w), BF16) for w in kv_widths]
                  + [jax.ShapeDtypeStruct((t, 2 * LANES), F32),
                     jax.ShapeDtypeStruct((2 * SUBLANES, t), F32),
                     jax.ShapeDtypeStruct((t, SSD_WIDTH), F32),
                     jax.ShapeDtypeStruct((t, SSD_CONV_DIM), F32)])
    out_specs = ([tok(NSA_HEADS * LANES)] + [tok(w) for w in kv_widths]
                 + [tok(2 * LANES), pl.BlockSpec((2 * SUBLANES, tm), lambda i: (0, i)),
                    tok(SSD_WIDTH), tok(SSD_CONV_DIM)])
    return pl.pallas_call(
        functools.partial(_in_proj_kernel, s // tm),
        name="in_proj",
        grid=(t // tm,),
        in_specs=[tok(D_MODEL), full((1, D_MODEL)), full((D_MODEL, n_main)), full((2 * SUBLANES, D_MODEL)),
                  full((s, LANES)), full((s, LANES)), full((s, LANES)), full((1, 2 * LANES)),
                  full((2 * SUBLANES, LANES))],
        out_specs=out_specs,
        out_shape=out_shapes,
        compiler_params=_cparams(("parallel",)),
    )(xt, attn_g[None, :], w_main, wdt, cos, sa, sb, gdb, dtb)


def _compress_kernel(kv_ref, w1a_ref, w1b_ref, pos_ref, w1_ref, b1_ref, w2_ref, out_ref):
    kv = kv_ref[0]
    bias = _dot(pos_ref[...], w1_ref[...])[0:1, :] + b1_ref[...]
    for g in range(NSA_KV_GROUPS):
        first = _dot(kv, w1a_ref[g])
        second = _dot(kv, w1b_ref[g])
        nxt = pltpu.roll(second, second.shape[0] - 1, 0)
        hid = _gelu_tanh(first + nxt + bias)
        out_ref[0, :, g * LANES:(g + 1) * LANES] = _dot(hid.astype(BF16), w2_ref[...]).astype(BF16)


def _compress(kv, pos_emb, w1, b1, w2, b, s):
    nch = s // CMP_STRIDE
    kvf = kv.reshape(b, nch, CMP_STRIDE * LANES)
    w1r = w1.reshape(CMP_BLOCK, HEAD_DIM, CMP_HIDDEN)
    zeros = jnp.zeros((CMP_STRIDE, HEAD_DIM, CMP_HIDDEN), F32)

    def expand(w_half, g):
        parts = [w_half, zeros] if g == 0 else [zeros, w_half]
        return jnp.concatenate(parts, axis=1).reshape(CMP_STRIDE * LANES, CMP_HIDDEN)

    w1a = jnp.stack([expand(w1r[:CMP_STRIDE], g) for g in range(NSA_KV_GROUPS)]).astype(BF16)
    w1b = jnp.stack([expand(w1r[CMP_STRIDE:], g) for g in range(NSA_KV_GROUPS)]).astype(BF16)
    w2e = jnp.concatenate([w2, jnp.zeros((CMP_HIDDEN, LANES - HEAD_DIM), F32)], axis=1).astype(BF16)
    pos = jnp.zeros((SUBLANES, CMP_BLOCK * HEAD_DIM), F32).at[0].set(pos_emb.reshape(-1)).astype(BF16)
    return pl.pallas_call(
        _compress_kernel,
        name="compress",
        grid=(b,),
        in_specs=[pl.BlockSpec((1, nch, CMP_STRIDE * LANES), lambda i: (i, 0, 0)),
                  pl.BlockSpec((2, CMP_STRIDE * LANES, CMP_HIDDEN), lambda i: (0, 0, 0)),
                  pl.BlockSpec((2, CMP_STRIDE * LANES, CMP_HIDDEN), lambda i: (0, 0, 0)),
                  pl.BlockSpec((SUBLANES, CMP_BLOCK * HEAD_DIM), lambda i: (0, 0)),
                  pl.BlockSpec((CMP_BLOCK * HEAD_DIM, CMP_HIDDEN), lambda i: (0, 0)),
                  pl.BlockSpec((1, CMP_HIDDEN), lambda i: (0, 0)),
                  pl.BlockSpec((CMP_HIDDEN, LANES), lambda i: (0, 0))],
        out_specs=pl.BlockSpec((1, nch, NSA_KV_GROUPS * LANES), lambda i: (i, 0, 0)),
        out_shape=jax.ShapeDtypeStruct((b, nch, NSA_KV_GROUPS * LANES), BF16),
        compiler_params=_cparams(("parallel",)),
    )(kvf, w1a, w1b, pos, w1.astype(BF16), b1[None, :], w2e)


def _lane_tiles(a):
    return [a[:, k * LANES:(k + 1) * LANES] for k in range(a.shape[1] // LANES)]


def _max_tiles(macc, s):
    for tile in _lane_tiles(s):
        macc = jnp.maximum(macc, tile)
    return macc


def _exp_tiles(s, m_b):
    return jnp.concatenate([jnp.exp(tile - m_b) for tile in _lane_tiles(s)], axis=1).astype(BF16)


def _normalize(acc):
    return acc / pltpu.roll(acc, HEAD_DIM, 1)


def _block_rank(imp):
    nblk, tq = imp.shape
    sub = lax.broadcasted_iota(jnp.int32, (SUBLANES, tq), 0)
    groups = [imp[SUBLANES * v:SUBLANES * (v + 1)] for v in range(nblk // SUBLANES)]
    cnt = [jnp.zeros((SUBLANES, tq), F32) for _ in groups]
    for j in range(nblk):
        row = imp[j:j + 1, :]
        for v, grp in enumerate(groups):
            if SUBLANES * v > j:
                ahead = row >= grp
            elif SUBLANES * (v + 1) <= j:
                ahead = row > grp
            else:
                ahead = (row > grp) | ((row == grp) & (sub > j - SUBLANES * v))
            cnt[v] = cnt[v] + jnp.where(ahead, 1.0, 0.0)
    return jnp.concatenate(cnt, axis=0)


def _nsa_pair_kernel(q_ref, kc_ref, vc_ref, ks_ref, vs_ref, kw_ref, vw_ref, gd_ref, ovl_ref, o_ref, s_ref):
    i = pl.program_id(1)
    tq = Q_TILE
    rows = NSA_REP * tq
    ncmp = kc_ref.shape[1]
    nblk = ovl_ref.shape[0]
    groups = range(NSA_KV_GROUPS)
    t_row = i * tq + lax.broadcasted_iota(jnp.int32, (rows, 1), 0) % tq

    def lanes(g):
        return slice(g * LANES, (g + 1) * LANES)

    qs = [jnp.concatenate([q_ref[:, (g * NSA_REP + r) * LANES:(g * NSA_REP + r + 1) * LANES]
                           for r in range(NSA_REP)], axis=0) for g in groups]

    jd = (i * tq) // K_TILE
    neg = jnp.full((rows, LANES), NEG_BIG, F32)
    n_wk = WINDOW + K_TILE
    k0 = pl.multiple_of(jnp.maximum(jd - WINDOW // K_TILE, 0) * K_TILE, K_TILE)
    diff = t_row - (k0 + lax.broadcasted_iota(jnp.int32, (rows, n_wk), 1))
    in_win = (diff >= 0) & (diff < WINDOW)
    o_wins = []
    for g in groups:
        s_w = jnp.where(in_win, _dot_nt(qs[g], kw_ref[pl.ds(k0, n_wk), lanes(g)]), NEG_BIG)
        m_w = jnp.max(_max_tiles(neg, s_w), axis=-1, keepdims=True)
        p_w = _exp_tiles(s_w, jnp.broadcast_to(m_w, (rows, LANES)))
        o_wins.append(_normalize(_dot(p_w, vw_ref[pl.ds(k0, n_wk), lanes(g)])))

    cmp_end = lax.broadcasted_iota(jnp.int32, (rows, ncmp), 1) * CMP_STRIDE + (CMP_BLOCK - 1)
    mask_c = cmp_end <= t_row
    p_cs, o_cmps = [], []
    for g in groups:
        s_c = jnp.where(mask_c, _dot_nt(qs[g], kc_ref[0, :, lanes(g)]), NEG_BIG)
        e_c = jnp.exp(s_c - jnp.max(s_c, axis=-1, keepdims=True))
        p_c = jnp.where(mask_c, e_c / jnp.sum(e_c, axis=-1, keepdims=True), 0.0)
        p_cs.append(p_c)
        o_cmps.append(_dot(p_c.astype(BF16), vc_ref[0, :, lanes(g)]))

    blk = lax.broadcasted_iota(jnp.int32, (nblk, tq), 0)
    cur = (i * tq + lax.broadcasted_iota(jnp.int32, (nblk, tq), 1)) // SEL_BLOCK
    forced = (blk == 0) | (blk == cur) | (blk == cur - 1)
    valid = blk <= cur
    eye = jnp.where(lax.broadcasted_iota(jnp.int32, (tq, tq), 0) == lax.broadcasted_iota(jnp.int32, (tq, tq), 1),
                    1.0, 0.0).astype(BF16)
    ones_lo = jnp.ones((HEAD_DIM, tq), F32)
    q2s = []
    for g in groups:
        p_sum = p_cs[g][0:tq]
        for r in range(1, NSA_REP):
            p_sum = p_sum + p_cs[g][r * tq:(r + 1) * tq]
        imp = _dot_nt_exact_rhs(ovl_ref[...], p_sum)
        imp = jnp.where(forced, 1e9, jnp.where(valid, imp, -1.0))
        sel_t = jnp.where((_block_rank(imp) < float(min(SEL_TOP_N, nblk))) & valid, 1.0, 0.0)
        pad = [jnp.ones((HEAD_DIM - nblk, tq), F32)] if nblk < HEAD_DIM else []
        sel_pad = jnp.concatenate([ones_lo, sel_t] + pad, axis=0).astype(BF16)
        negb = ((_dot_nt(eye, sel_pad) - 1.0) * (-SEL_NEG)).astype(BF16)
        q2s.append(qs[g] + jnp.concatenate([negb] * NSA_REP, axis=0))

    nfull = jd // (S_CHUNK // K_TILE)
    lane_c = lax.broadcasted_iota(jnp.int32, (rows, S_CHUNK), 1)

    def scores(g, c):
        k0 = pl.multiple_of(c * S_CHUNK, S_CHUNK)
        return _dot_nt(q2s[g], ks_ref[pl.ds(k0, S_CHUNK), lanes(g)])

    def pass_max(c, maccs):
        out = []
        for g in groups:
            s = scores(g, c)
            s_ref[g, c] = s
            out.append(_max_tiles(maccs[g], s))
        return tuple(out)

    maccs = lax.fori_loop(0, nfull, pass_max, (neg,) * NSA_KV_GROUPS)
    causal = nfull * S_CHUNK + lane_c <= t_row
    s_lasts = [jnp.where(causal, scores(g, nfull), NEG_BIG) for g in groups]
    m_bs = [jnp.broadcast_to(jnp.max(_max_tiles(maccs[g], s_lasts[g]), axis=-1, keepdims=True), (rows, LANES))
            for g in groups]

    def pass_sum(c, accs):
        k0 = pl.multiple_of(c * S_CHUNK, S_CHUNK)
        return tuple(accs[g] + _dot(_exp_tiles(s_ref[g, c], m_bs[g]), vs_ref[pl.ds(k0, S_CHUNK), lanes(g)])
                     for g in groups)

    zero = jnp.zeros((rows, LANES), F32)
    accs = lax.fori_loop(0, nfull, pass_sum, (zero,) * NSA_KV_GROUPS)
    k_last = pl.multiple_of(nfull * S_CHUNK, S_CHUNK)
    o_sels = [_normalize(accs[g] + _dot(_exp_tiles(s_lasts[g], m_bs[g]), vs_ref[pl.ds(k_last, S_CHUNK), lanes(g)]))
              for g in groups]

    low = lax.broadcasted_iota(jnp.int32, (tq, LANES), 1) < HEAD_DIM
    for g in groups:
        gd = gd_ref[:, lanes(g)]
        heads = []
        for r in range(NSA_REP):
            rs = slice(r * tq, (r + 1) * tq)
            heads.append(gd[:, 3 * r:3 * r + 1] * o_cmps[g][rs] + gd[:, 3 * r + 1:3 * r + 2] * o_sels[g][rs]
                         + gd[:, 3 * r + 2:3 * r + 3] * o_wins[g][rs])
        for sidx in range(NSA_REP // 2):
            slab = g * (NSA_REP // 2) + sidx
            o_ref[:, slab * LANES:(slab + 1) * LANES] = jnp.where(
                low, heads[2 * sidx], pltpu.roll(heads[2 * sidx + 1], HEAD_DIM, 1))


def _nsa(q, kcmp, vcmp, ks, vs, kw, vw, gd, b, s):
    t = q.shape[0]
    nq = s // Q_TILE
    ncmp = s // CMP_STRIDE
    nblk = s // SEL_BLOCK
    cs = np.arange(ncmp) * CMP_STRIDE
    ss = np.arange(nblk) * SEL_BLOCK
    overlap = (cs[:, None] < ss[None, :] + SEL_BLOCK) & (cs[:, None] + CMP_BLOCK > ss[None, :])
    overlap[ncmp - 1, :] = False
    ovl_t = jnp.asarray(overlap.T, BF16)

    assert nblk <= HEAD_DIM and s % S_CHUNK == 0 and s >= WINDOW + K_TILE

    ng = NSA_KV_GROUPS

    def seq_spec():
        return pl.BlockSpec((s, ng * LANES), lambda bi, i: (bi, 0))

    return pl.pallas_call(
        _nsa_pair_kernel,
        name="nsa",
        grid=(b, nq),
        in_specs=[pl.BlockSpec((Q_TILE, NSA_HEADS * LANES), lambda bi, i: (bi * nq + i, 0)),
                  pl.BlockSpec((1, ncmp, ng * LANES), lambda bi, i: (bi, 0, 0)),
                  pl.BlockSpec((1, ncmp, ng * LANES), lambda bi, i: (bi, 0, 0)),
                  seq_spec(), seq_spec(), seq_spec(), seq_spec(),
                  pl.BlockSpec((Q_TILE, ng * LANES), lambda bi, i: (bi * nq + i, 0)),
                  pl.BlockSpec((nblk, ncmp), lambda bi, i: (0, 0))],
        out_specs=pl.BlockSpec((Q_TILE, NSA_WIDTH), lambda bi, i: (bi * nq + i, 0)),
        out_shape=jax.ShapeDtypeStruct((t, NSA_WIDTH), F32),
        scratch_shapes=[pltpu.VMEM((ng, s // S_CHUNK, NSA_REP * Q_TILE, S_CHUNK), F32)],
        compiler_params=_cparams(("parallel", "arbitrary")),
    )(q, kcmp, vcmp, ks, vs, kw, vw, gd, ovl_t)


def _conv_silu(cur_ref, ext_ref, tail_ref, w_ref, b_ref):
    l = cur_ref.shape[0]
    cur = cur_ref[...]
    ext_ref[0:SUBLANES, :] = tail_ref[...]
    ext_ref[SUBLANES:, :] = cur
    tail_ref[...] = cur[l - SUBLANES:, :]
    acc = jnp.zeros(cur.shape, F32) + b_ref[...]
    for k in range(SSD_CONV):
        off = SUBLANES - (SSD_CONV - 1) + k
        acc = acc + ext_ref[off:off + l, :] * w_ref[k:k + 1, :]
    return acc * _sigmoid(acc)


def _ssd_kernel(xbc_ref, w_ref, b_ref, z_ref, gd_ref, dtt_ref, alane_ref, asub_ref, dskip_ref, ng_ref,
                tril_ref, edt_ref, eseg_ref, o_ref, tail_ref, ext_ref, st_ref):
    c = pl.program_id(1)
    l = SSD_CHUNK
    gw = SSD_WIDTH // SSD_GROUPS
    b_col0 = SSD_WIDTH
    c_col0 = SSD_WIDTH + SSD_GROUPS * SSD_STATE

    @pl.when(c == 0)
    def _():
        tail_ref[...] = jnp.zeros(tail_ref.shape, F32)
        st_ref[...] = jnp.zeros(st_ref.shape, F32)

    conv_all = _conv_silu(xbc_ref, ext_ref, tail_ref, w_ref, b_ref)
    tril = tril_ref[...]
    causal = lax.broadcasted_iota(jnp.int32, (l, l), 0) >= lax.broadcasted_iota(jnp.int32, (l, l), 1)
    head_of_lane = lax.broadcasted_iota(jnp.int32, (l, gw), 1) // SSD_HEAD_DIM
    for hf, g in [(hf, g) for hf in range(xbc_ref.shape[0] // l) for g in range(SSD_GROUPS)]:
        rs = slice(hf * l, (hf + 1) * l)
        conv = conv_all[rs]
        xs = conv[:, g * gw:(g + 1) * gw]
        bm = conv[:, b_col0 + g * SSD_STATE:b_col0 + (g + 1) * SSD_STATE]
        cm = conv[:, c_col0 + g * SSD_STATE:c_col0 + (g + 1) * SSD_STATE]
        gd = gd_ref[rs, g * LANES:(g + 1) * LANES]
        adt_c = gd * (-jnp.exp(alane_ref[g]))
        acum_c = _dot_exact_rhs(tril, adt_c)
        dt_full = _dot_exact_lhs(gd, edt_ref[...])
        ac_full = _dot_exact_lhs(acum_c, edt_ref[...])
        ac_seg = _dot_exact_lhs(acum_c, eseg_ref[...])
        adt_t = dtt_ref[g * SUBLANES:(g + 1) * SUBLANES, rs] * (-jnp.exp(asub_ref[g]))
        acum_t = _dot_nt_exact_lhs(adt_t, tril)

        xdt = xs * dt_full
        cmb = cm.astype(BF16)
        cb = _dot_nt(cmb, bm.astype(BF16))
        y = jnp.zeros(xs.shape, F32)
        for r in range(SSD_REP):
            seg = jnp.exp(jnp.where(causal, ac_seg[:, r * l:(r + 1) * l] - acum_t[r:r + 1, :], NEG_BIG))
            x_r = jnp.where(head_of_lane == r, xdt, 0.0).astype(BF16)
            y = y + _dot((cb * seg).astype(BF16), x_r)
        ac_last = ac_full[l - 1:l, :]
        state = st_ref[g]
        y = y + _dot(cmb, state.astype(BF16)) * jnp.exp(ac_full)
        decayed = (xdt * jnp.exp(ac_last - ac_full)).astype(BF16)
        st_ref[g] = state * jnp.exp(ac_last) + _dot(bm.T.astype(BF16), decayed)
        y = y + xs * dskip_ref[:, g * gw:(g + 1) * gw]
        z = z_ref[rs, g * gw:(g + 1) * gw]
        yz = y * (z * _sigmoid(z))
        o_ref[rs, g * gw:(g + 1) * gw] = (yz * lax.rsqrt(jnp.mean(yz * yz, axis=-1, keepdims=True) + EPS)
                                        * ng_ref[:, g * gw:(g + 1) * gw])


def _ssd(xbc, z, gd, dtt, conv_w, conv_b, a_log, d_skip, norm_g, b, s):
    t = xbc.shape[0]
    l = SSD_CHUNK
    rows = 2 * l
    nc = s // rows
    gw = SSD_WIDTH // SSD_GROUPS
    xcols = gw // LANES
    b_blk0 = SSD_WIDTH // LANES
    c_blk0 = b_blk0 + SSD_GROUPS * SSD_STATE // LANES
    conv_b2 = conv_b[None, :]
    a_grp = a_log.reshape(SSD_GROUPS, SSD_REP)
    alane = jnp.zeros((SSD_GROUPS, 1, LANES), F32).at[:, 0, 16:16 + SSD_REP].set(a_grp)
    asub = jnp.zeros((SSD_GROUPS, SUBLANES, LANES), F32).at[:, :SSD_REP, :].set(
        jnp.broadcast_to(a_grp[:, :, None], (SSD_GROUPS, SSD_REP, LANES)))
    dskip = jnp.repeat(d_skip, SSD_HEAD_DIM)[None, :]
    ng = norm_g[None, :]
    ii = np.arange(l)
    tril = jnp.asarray(ii[:, None] >= ii[None, :], BF16)
    edt = np.zeros((LANES, gw), np.float32)
    eseg = np.zeros((LANES, SSD_REP * l), np.float32)
    for r in range(SSD_REP):
        edt[16 + r, r * SSD_HEAD_DIM:(r + 1) * SSD_HEAD_DIM] = 1.0
        eseg[16 + r, r * l:(r + 1) * l] = 1.0
    edt, eseg = jnp.asarray(edt, BF16), jnp.asarray(eseg, BF16)

    def rowblk(width):
        return pl.BlockSpec((rows, width), lambda bi, c: (bi * nc + c, 0))

    def const(shape):
        return pl.BlockSpec(shape, lambda bi, c: (0,) * len(shape))

    return pl.pallas_call(
        _ssd_kernel,
        name="ssd",
        grid=(b, nc),
        in_specs=[rowblk(SSD_CONV_DIM), const((SSD_CONV, SSD_CONV_DIM)), const((1, SSD_CONV_DIM)),
                  rowblk(SSD_WIDTH), rowblk(SSD_GROUPS * LANES),
                  pl.BlockSpec((SSD_GROUPS * SUBLANES, rows), lambda bi, c: (0, bi * nc + c)),
                  const((SSD_GROUPS, 1, LANES)), const((SSD_GROUPS, SUBLANES, LANES)),
                  const((1, SSD_WIDTH)), const((1, SSD_WIDTH)),
                  const((l, l)), const((LANES, gw)), const((LANES, SSD_REP * l))],
        out_specs=rowblk(SSD_WIDTH),
        out_shape=jax.ShapeDtypeStruct((t, SSD_WIDTH), F32),
        scratch_shapes=[pltpu.VMEM((SUBLANES, SSD_CONV_DIM), F32), pltpu.VMEM((rows + SUBLANES, SSD_CONV_DIM), F32),
                        pltpu.VMEM((SSD_GROUPS, SSD_STATE, gw), F32)],
        compiler_params=_cparams(("parallel", "arbitrary")),
    )(xbc, conv_w, conv_b2, z, gd, dtt, alane, asub, dskip, ng, tril, edt, eseg)


def _rms(v, g):
    return v * lax.rsqrt(jnp.mean(v * v, axis=-1, keepdims=True) + EPS) * g


def _out_proj_kernel(x_ref, on_ref, os_ref, ng_ref, w1_ref, w2_ref, fg_ref, x1_ref, xnt_ref):
    onn = _rms(on_ref[...], ng_ref[...]).astype(BF16)
    x1 = x_ref[...] + _dot(onn, w1_ref[...]) + _dot(os_ref[...].astype(BF16), w2_ref[...])
    x1_ref[...] = x1
    xnt_ref[...] = _rms(x1, fg_ref[...]).T.astype(BF16)


def _out_proj(xt, o_nsa, o_ssd, nsa_g, w_out, ffn_g):
    t = xt.shape[0]
    tm = min(TOK_TILE, t)

    def tok(width):
        return pl.BlockSpec((tm, width), lambda i: (i, 0))

    def full(shape):
        return pl.BlockSpec(shape, lambda i: (0, 0))

    wb = w_out.astype(BF16)
    return pl.pallas_call(
        _out_proj_kernel,
        name="out_proj",
        grid=(t // tm,),
        in_specs=[tok(D_MODEL), tok(NSA_WIDTH), tok(SSD_WIDTH), full((1, NSA_WIDTH)),
                  full((NSA_WIDTH, D_MODEL)), full((SSD_WIDTH, D_MODEL)), full((1, D_MODEL))],
        out_specs=[tok(D_MODEL), pl.BlockSpec((D_MODEL, tm), lambda i: (0, i))],
        out_shape=[jax.ShapeDtypeStruct((t, D_MODEL), F32), jax.ShapeDtypeStruct((D_MODEL, t), BF16)],
        compiler_params=_cparams(("parallel",)),
    )(xt, o_nsa, o_ssd, nsa_g[None, :], wb[:NSA_WIDTH], wb[NSA_WIDTH:], ffn_g[None, :])


def _top_sorted(e, k):
    rows = lax.broadcasted_iota(jnp.int32, (k, e.shape[1]), 0)
    out = jnp.zeros((k, e.shape[1]), F32)
    cur = e
    for j in range(k):
        mk = jnp.max(cur, axis=0, keepdims=True)
        out = jnp.where(rows == j, jnp.maximum(mk, 0.0), out)
        target = jnp.where(mk >= 0.0, mk, 2.0)
        cur = jnp.where(cur == target, -float(j + 1), cur)
    rank = jnp.where(cur < 0.0, -1.0 - cur, float(k))
    return out, rank


def _pair_candidates(a16, b16):
    row8 = lax.broadcasted_iota(jnp.int32, (SUBLANES, a16.shape[1]), 0)
    groups = [a16[0:1] * b16[0:SUBLANES], a16[0:1] * b16[SUBLANES:], a16[1:2] * b16[0:SUBLANES]]
    for a in range(2, SUBLANES):
        groups.append(jnp.where(row8 < PEER_TOPK // (a + 1), a16[a:a + 1] * b16[0:SUBLANES], -1.0))
    groups.append(a16[SUBLANES:] * b16[0:1])
    return jnp.concatenate(groups, axis=0)


def _peer_front_kernel(xnt_ref, wq_ref, keys_ref, av_ref, lc_ref, bv_ref, rb_ref):
    xt = xnt_ref[...]
    tm = xt.shape[1]
    row8 = lax.broadcasted_iota(jnp.int32, (SUBLANES, tm), 0)
    for h in range(PEER_HEADS):
        qh = _dot(wq_ref[h * PEER_KEY_DIM:(h + 1) * PEER_KEY_DIM, :], xt)
        s1 = _dot(keys_ref[h, 0], qh[:PEER_HALF].astype(BF16))
        s2 = _dot(keys_ref[h, 1], qh[PEER_HALF:].astype(BF16))
        e1 = jnp.exp(s1 - jnp.max(s1, axis=0, keepdims=True))
        e2 = jnp.exp(s2 - jnp.max(s2, axis=0, keepdims=True))
        a16, ra = _top_sorted(e1, PEER_TOPK)
        b16, rb = _top_sorted(e2, PEER_TOPK)
        cand = _pair_candidates(a16, b16)
        cur = cand
        zsum = jnp.zeros((1, tm), F32)
        for _ in range(PEER_TOPK):
            tau = jnp.maximum(jnp.max(cur, axis=0, keepdims=True), 0.0)
            zsum = zsum + tau
            cur = jnp.where(cur == tau, -1.0, cur)
        picked = jnp.where(cand >= tau, 1.0, 0.0)

        def count(g):
            return jnp.sum(picked[g * SUBLANES:(g + 1) * SUBLANES], axis=0, keepdims=True)

        lens = [count(0) + count(1)] + [count(a + 1) for a in range(1, SUBLANES)]
        len_lo = jnp.zeros((SUBLANES, tm), F32)
        for a, row in enumerate(lens):
            len_lo = jnp.where(row8 == a, row, len_lo)
        len16 = jnp.concatenate([len_lo, picked[(SUBLANES + 1) * SUBLANES:]], axis=0)
        pack = 2 * SUBLANES
        ra_b = ra.astype(BF16)
        lc = jnp.zeros(ra.shape, BF16)
        for a in range(PEER_TOPK):
            len_a = jnp.tile(jnp.broadcast_to(len16[a:a + 1], (pack, tm)).astype(BF16), (PEER_N_KEYS // pack, 1))
            lc = jnp.where(ra_b == float(a), len_a, lc)
        rows = slice(h * PEER_N_KEYS, (h + 1) * PEER_N_KEYS)
        av_ref[rows, :] = e1
        lc_ref[rows, :] = lc.astype(F32)
        bv_ref[rows, :] = (e2 * (1.0 / zsum)).astype(BF16)
        rb_ref[rows, :] = rb.astype(BF16)


def _peer_front(xnt, w_q, keys):
    t = xnt.shape[1]
    tm = min(PEER_FRONT_TOK, t)
    nk = PEER_HEADS * PEER_N_KEYS
    return pl.pallas_call(
        _peer_front_kernel,
        name="peer_front",
        grid=(t // tm,),
        in_specs=[pl.BlockSpec((D_MODEL, tm), lambda i: (0, i)),
                  pl.BlockSpec((PEER_HEADS * PEER_KEY_DIM, D_MODEL), lambda i: (0, 0)),
                  pl.BlockSpec((PEER_HEADS, 2, PEER_N_KEYS, PEER_HALF), lambda i: (0, 0, 0, 0))],
        out_specs=[pl.BlockSpec((nk, tm), lambda i: (0, i))] * 4,
        out_shape=[jax.ShapeDtypeStruct((nk, t), F32), jax.ShapeDtypeStruct((nk, t), F32),
                   jax.ShapeDtypeStruct((nk, t), BF16), jax.ShapeDtypeStruct((nk, t), BF16)],
        compiler_params=_cparams(("parallel",)),
    )(xnt, w_q.T.astype(BF16), keys.astype(BF16))


def _row_bf16(ref, row, n_rows):
    pack = 2 * SUBLANES
    tile = jnp.broadcast_to(ref[pl.ds(row, 1), :], (pack, ref.shape[1])).astype(BF16)
    return jnp.tile(tile, (n_rows // pack, 1))


def _peer_dense_kernel(xnt_ref, u_ref, vt_ref, av_ref, lc_ref, bv_ref, rb_ref, o_ref, act_ref):
    j = pl.program_id(1)
    n_sub = u_ref.shape[0] // PEER_N_KEYS

    @pl.when(j == 0)
    def _():
        o_ref[...] = jnp.zeros(o_ref.shape, F32)

    def gate_of(cl):
        e1 = j * n_sub + cl
        gate = jnp.zeros((PEER_N_KEYS, xnt_ref.shape[1]), BF16)
        for h in range(PEER_HEADS):
            rows = slice(h * PEER_N_KEYS, (h + 1) * PEER_N_KEYS)
            lc_row = _row_bf16(lc_ref, h * PEER_N_KEYS + e1, PEER_N_KEYS)
            av_row = _row_bf16(av_ref, h * PEER_N_KEYS + e1, PEER_N_KEYS)
            bv = bv_ref[rows, :]
            gate = gate + jnp.where(rb_ref[rows, :] < lc_row, bv, jnp.zeros_like(bv)) * av_row
        return gate

    per_piece = 4
    piece = per_piece * PEER_N_KEYS
    pack = 2 * SUBLANES
    for p in range(u_ref.shape[0] // piece):
        gates = [gate_of(per_piece * p + k) for k in range(per_piece)]
        zero = jnp.tile(gates[0][0:pack, 0:LANES] * 0.0, (piece // pack, u_ref.shape[1] // LANES))
        hid = _dot(u_ref[p * piece:(p + 1) * piece, :] + zero, xnt_ref[...])
        for k in range(per_piece):
            r0 = p * piece + k * PEER_N_KEYS
            act_ref[r0:r0 + PEER_N_KEYS, :] = (
                _gelu_sigmoid(hid[k * PEER_N_KEYS:(k + 1) * PEER_N_KEYS].astype(BF16)) * gates[k])
    o_ref[...] += _dot(vt_ref[...], act_ref[...])


def _peer_dense(xnt, u, vt, av, lc, bv, rb):
    t = xnt.shape[1]
    tm = min(PEER_TOK, t)
    ne = u.shape[0]
    te = PEER_EXP
    tok = pl.BlockSpec((PEER_HEADS * PEER_N_KEYS, tm), lambda i, j: (0, i))
    return pl.pallas_call(
        _peer_dense_kernel,
        name="peer_dense",
        grid=(t // tm, ne // te),
        in_specs=[pl.BlockSpec((D_MODEL, tm), lambda i, j: (0, i)),
                  pl.BlockSpec((te, D_MODEL), lambda i, j: (j, 0)),
                  pl.BlockSpec((D_MODEL, te), lambda i, j: (0, j)),
                  tok, tok, tok, tok],
        out_specs=pl.BlockSpec((D_MODEL, tm), lambda i, j: (0, i)),
        out_shape=jax.ShapeDtypeStruct((D_MODEL, t), F32),
        scratch_shapes=[pltpu.VMEM((te, tm), BF16)],
        compiler_params=_cparams(("parallel", "arbitrary")),
    )(xnt, u, vt, av, lc, bv, rb)


def _final_kernel(x1_ref, pt_ref, g_ref, o_ref):
    o_ref[...] = _rms(x1_ref[...] + pt_ref[...].T, g_ref[...])


def _residual_norm(x1, peer_t, g):
    t = x1.shape[0]
    tm = min(TOK_TILE, t)
    return pl.pallas_call(
        _final_kernel,
        name="residual_norm",
        grid=(t // tm,),
        in_specs=[pl.BlockSpec((tm, D_MODEL), lambda i: (i, 0)), pl.BlockSpec((D_MODEL, tm), lambda i: (0, i)),
                  pl.BlockSpec((1, D_MODEL), lambda i: (0, 0))],
        out_specs=pl.BlockSpec((tm, D_MODEL), lambda i: (i, 0)),
        out_shape=jax.ShapeDtypeStruct((t, D_MODEL), F32),
        compiler_params=_cparams(("parallel",)),
    )(x1, peer_t, g[None, :])


def kernel(x, attn_norm_g, w_in, cmp_pos_k, cmp_w1_k, cmp_b1_k, cmp_w2_k, cmp_pos_v, cmp_w1_v, cmp_b1_v, cmp_w2_v,
           conv_w, conv_b, dt_bias, a_log, d_skip, ssd_norm_g, nsa_norm_g, w_out, ffn_norm_g, peer_w_q, peer_keys,
           peer_u, peer_v, final_norm_g):
    b, s, d = x.shape
    xt = x.reshape(b * s, d)
    assert attn_norm_g.shape[0] == 1, "single-layer block"
    for layer in range(1):
        q, kc, vc, ks, vs, kw, vw, gd, dtt, z, xbc = _in_proj(xt, attn_norm_g[layer], w_in[layer], dt_bias[layer], s)
        kcmp = _compress(kc, cmp_pos_k[layer], cmp_w1_k[layer], cmp_b1_k[layer], cmp_w2_k[layer], b, s)
        vcmp = _compress(vc, cmp_pos_v[layer], cmp_w1_v[layer], cmp_b1_v[layer], cmp_w2_v[layer], b, s)
        o_nsa = _nsa(q, kcmp, vcmp, ks, vs, kw, vw, gd, b, s)
        o_ssd = _ssd(xbc, z, gd, dtt, conv_w[layer], conv_b[layer], a_log[layer], d_skip[layer],
                     ssd_norm_g[layer], b, s)
        x1, xnt = _out_proj(xt, o_nsa, o_ssd, nsa_norm_g[layer], w_out[layer], ffn_norm_g[layer])
        av, lc, bv, rb = _peer_front(xnt, peer_w_q[layer], peer_keys[layer])
        peer_t = _peer_dense(xnt, peer_u[layer].astype(BF16), peer_v[layer].T.astype(BF16), av, lc, bv, rb)
    return _residual_norm(x1, peer_t, final_norm_g).reshape(b, s, d)
```
